```python
import jax
import jax.numpy as jnp
from jax import lax
import numpy as np

D_MODEL = 1024
BATCH = 8
SEQ = 2048
DEPTH = 4
DEC_BATCH = 128
DEC_SEQ = 1
PAST_LEN = 16384
PAGE_SIZE = 128

N_MIXERS = 4
N_A = (DEPTH + 3) // 4
N_B = (DEPTH + 2) // 4
N_C = (DEPTH + 1) // 4
N_D = DEPTH // 4
D_FF = 4 * D_MODEL
NORM_EPS = 1e-6

RWKV_HEAD = 64
RWKV_H = D_MODEL // RWKV_HEAD
RWKV_LORA_W = 64
RWKV_LORA_A = 64
RWKV_LORA_G = 160
RWKV_GN_EPS = 64e-5

GLA_H = 4
GLA_DK_TOT = D_MODEL // 2
GLA_DV_TOT = D_MODEL
GLA_DK = GLA_DK_TOT // GLA_H
GLA_DV = GLA_DV_TOT // GLA_H
GLA_LR = 16
GLA_NORMALIZER = 16.0
GLA_CHUNK = 64
GLA_NORM_EPS = 1e-5

CONV_W = 3

POOL_WINDOWS = (2, 4, 8, 16)
POOL_G = D_MODEL // len(POOL_WINDOWS)
POOL_BUF = max(POOL_WINDOWS) - 1

kernel_name = 'hybrid_rwkv7_gla_conv_pool_decoder_step'


def rmsnorm(x, g, eps):
    xf = x.astype(jnp.float32)
    y = xf * lax.rsqrt(jnp.mean(xf * xf, axis=-1, keepdims=True) + eps)
    return (y * g.astype(jnp.float32)).astype(x.dtype)


def sqrelu_mlp(x, w_up, w_down):
    return jnp.square(jax.nn.relu(x @ w_up)) @ w_down


def rwkv7_mix(u, shift_buf, S0, mu, w_rkv, w0, w1, w2, a0, a1, a2, g1, g2, k_k, k_a, r_k, ln_w, ln_b, wo):
    B, L, D = u.shape
    f32 = jnp.float32
    prev = jnp.concatenate([shift_buf[:, None, :].astype(u.dtype), u[:, :-1]], axis=1)
    xm = u[None] + (prev - u)[None] * mu[:, None, None, :].astype(u.dtype)
    xr, xw, xk, xv, xa, xg = xm[0], xm[1], xm[2], xm[3], xm[4], xm[5]
    rkv = jnp.einsum('nbld,nde->nble', jnp.stack([xr, xk, xv]), w_rkv)
    r = rkv[0].astype(f32)
    k = rkv[1].astype(f32)
    v = rkv[2].astype(f32)
    w_log = -jax.nn.softplus(-(w0 + jnp.tanh(xw @ w1) @ w2).astype(f32)) - 0.5
    decay = jnp.exp(-jnp.exp(w_log))
    a = jax.nn.sigmoid((a0 + (xa @ a1) @ a2).astype(f32))
    g = (jax.nn.sigmoid(xg @ g1) @ g2).astype(f32)

    def heads(t):
        return t.reshape(B, L, RWKV_H, RWKV_HEAD)

    kk = heads(k * k_k.astype(f32))
    kk = kk / jnp.maximum(jnp.sqrt(jnp.sum(kk * kk, axis=-1, keepdims=True)), 1e-12)
    k = k * (1.0 + (a - 1.0) * k_a.astype(f32))
    r_h, k_h, v_h, w_h, a_h = heads(r), heads(k), heads(v), heads(decay), heads(a)
    b_h = kk * a_h
    seq = tuple(jnp.moveaxis(t, 1, 0) for t in (r_h, w_h, k_h, v_h, -kk, b_h))

    def step(S, inp):
        r_t, w_t, k_t, v_t, a_t, b_t = inp
        sa = jnp.einsum('bhvk,bhk->bhv', S, a_t)
        S = S * w_t[:, :, None, :] + sa[..., None] * b_t[:, :, None, :] + v_t[..., None] * k_t[:, :, None, :]
        return S, jnp.einsum('bhvk,bhk->bhv', S, r_t)

    S_fin, y = lax.scan(step, S0.astype(f32), seq)
    y = jnp.moveaxis(y, 0, 1)
    mean = jnp.mean(y, axis=-1, keepdims=True)
    var = jnp.mean(jnp.square(y - mean), axis=-1, keepdims=True)
    yn = ((y - mean) * lax.rsqrt(var + RWKV_GN_EPS)).reshape(B, L, D) * ln_w.astype(f32) + ln_b.astype(f32)
    bonus = (jnp.sum(r_h * k_h * r_k.astype(f32), axis=-1, keepdims=True) * v_h).reshape(B, L, D)
    out = ((yn + bonus) * g).astype(u.dtype) @ wo
    return out, S_fin.astype(S0.dtype), u[:, -1]


def gla_chunked(q, k, v, gk, S0):
    B, L, H, DK = q.shape
    DV = v.shape[-1]
    C = min(GLA_CHUNK, L)
    n = -(-L // C)
    pad = n * C - L
    f32 = jnp.float32

    def blocks(t):
        t = jnp.pad(t.astype(f32), ((0, 0), (0, pad), (0, 0), (0, 0)))
        return jnp.moveaxis(t.reshape(B, n, C, H, t.shape[-1]), 1, 0)

    mask = jnp.tril(jnp.ones((C, C), dtype=bool))[None, :, :, None, None]

    def step(S, inp):
        qc, kc, vc, gc = inp
        b = jnp.cumsum(gc, axis=1)
        o_inter = jnp.einsum('bchk,bhkv->bchv', qc * jnp.exp(b), S)
        diff = b[:, :, None] - b[:, None, :]
        dec = jnp.exp(jnp.where(mask, diff, -jnp.inf))
        att = jnp.einsum('btshk,bthk,bshk->bhts', dec, qc, kc)
        o_intra = jnp.einsum('bhts,bshv->bthv', att, vc)
        b_last = b[:, -1]
        S = jnp.exp(b_last)[..., None] * S + jnp.einsum('bshk,bshv->bhkv', kc * jnp.exp(b_last[:, None] - b), vc)
        return S, o_inter + o_intra

    S_fin, o = lax.scan(step, S0.astype(f32), (blocks(q), blocks(k), blocks(v), blocks(gk)))
    o = jnp.moveaxis(o, 0, 1).reshape(B, n * C, H, DV)[:, :L]
    return o, S_fin


def gla_mix(u, S0, w_in, w_gk2, b_gk, norm_w, wo):
    B, L, D = u.shape
    z = u @ w_in
    q, k, v, g, gl = jnp.split(z, [GLA_DK_TOT, 2 * GLA_DK_TOT, 2 * GLA_DK_TOT + GLA_DV_TOT, 2 * GLA_DK_TOT + 2 * GLA_DV_TOT], axis=-1)
    gk = jax.nn.log_sigmoid((gl @ w_gk2 + b_gk).astype(jnp.float32)) / GLA_NORMALIZER
    q = q.reshape(B, L, GLA_H, GLA_DK) * (GLA_DK ** -0.5)
    k = k.reshape(B, L, GLA_H, GLA_DK)
    v = v.reshape(B, L, GLA_H, GLA_DV)
    gk = gk.reshape(B, L, GLA_H, GLA_DK)
    o, S_fin = gla_chunked(q, k, v, gk, S0)
    o = rmsnorm(o, norm_w, GLA_NORM_EPS) * jax.nn.silu(g.reshape(B, L, GLA_H, GLA_DV).astype(jnp.float32))
    out = o.reshape(B, L, GLA_DV_TOT).astype(u.dtype) @ wo
    return out, S_fin.astype(S0.dtype)


def conv_mix(u, buf, w_in, w_conv, wo):
    L = u.shape[1]
    gB, gC, h = jnp.split(u @ w_in, 3, axis=-1)
    zc = jnp.concatenate([buf.astype(u.dtype), gC * h], axis=1)
    conv = sum(w_conv[j] * zc[:, j:j + L] for j in range(CONV_W))
    out = (gB * conv) @ wo
    return out, zc[:, -(CONV_W - 1):]


def pool_mix(u, buf, pos0, pool_w, scale):
    B, L, D = u.shape
    zc = jnp.concatenate([buf.astype(u.dtype), u], axis=1)
    zf = zc.astype(jnp.float32)
    cs = jnp.concatenate([jnp.zeros((B, 1, D), jnp.float32), jnp.cumsum(zf, axis=1)], axis=1)
    pos = pos0 + jnp.arange(L)
    uf = u.astype(jnp.float32)
    ys = []
    for gi, w in enumerate(POOL_WINDOWS):
        sl = slice(gi * POOL_G, (gi + 1) * POOL_G)
        s = cs[:, POOL_BUF + 1:POOL_BUF + 1 + L, sl] - cs[:, POOL_BUF + 1 - w:POOL_BUF + 1 - w + L, sl]
        cnt = jnp.minimum(w, pos + 1).astype(jnp.float32)
        d = s / cnt[None, :, None] - uf[:, :, sl]
        ys.append(d.astype(u.dtype) @ pool_w[gi])
    out = jnp.concatenate(ys, axis=-1) * scale
    return out, zc[:, -POOL_BUF:]


def setup_inputs(seed: int = 0) -> dict:
    key = jax.random.key(seed)
    ks = iter(jax.random.split(key, 64))
    D = D_MODEL

    def nrm(shape, scale):
        return scale * jax.random.normal(next(ks), shape, jnp.float32)

    def uni(shape, lo, hi):
        return jax.random.uniform(next(ks), shape, jnp.float32, lo, hi)

    inp = {}
    inp['x_prompt'] = nrm((BATCH, SEQ, D), 1.0)
    inp['x_sample'] = nrm((DEC_BATCH, DEC_SEQ, D), 1.0)
    inp['state_rwkv_wkv'] = nrm((N_A, DEC_BATCH, RWKV_H, RWKV_HEAD, RWKV_HEAD), 0.3)
    inp['state_rwkv_shift'] = nrm((N_A, DEC_BATCH, D), 1.0)
    inp['state_gla'] = nrm((N_B, DEC_BATCH, GLA_H, GLA_DK, GLA_DV), 0.3)
    inp['state_conv'] = nrm((N_C, DEC_BATCH, CONV_W - 1, D), 1.0)
    inp['state_pool'] = nrm((N_D, DEC_BATCH, POOL_BUF, D), 1.0)
    inp['norm_mix'] = 1.0 + nrm((DEPTH, D), 0.05)
    inp['norm_ffn'] = 1.0 + nrm((DEPTH, D), 0.05)
    inp['norm_final'] = 1.0 + nrm((D,), 0.05)
    inp['ffn_up'] = nrm((DEPTH, D, D_FF), D ** -0.5)
    inp['ffn_down'] = nrm((DEPTH, D_FF, D), 0.5 * D_FF ** -0.5)
    inp['rwkv_mu'] = uni((N_A, 6, D), 0.0, 1.0)
    inp['rwkv_w_rkv'] = nrm((N_A, 3, D, D), D ** -0.5)
    inp['rwkv_w0'] = uni((N_A, D), -6.0, 1.0)
    inp['rwkv_w1'] = nrm((N_A, D, RWKV_LORA_W), D ** -0.5)
    inp['rwkv_w2'] = nrm((N_A, RWKV_LORA_W, D), 0.1 * RWKV_LORA_W ** -0.5)
    inp['rwkv_a0'] = nrm((N_A, D), 0.1)
    inp['rwkv_a1'] = nrm((N_A, D, RWKV_LORA_A), D ** -0.5)
    inp['rwkv_a2'] = nrm((N_A, RWKV_LORA_A, D), 0.1 * RWKV_LORA_A ** -0.5)
    inp['rwkv_g1'] = nrm((N_A, D, RWKV_LORA_G), D ** -0.5)
    inp['rwkv_g2'] = nrm((N_A, RWKV_LORA_G, D), RWKV_LORA_G ** -0.5)
    inp['rwkv_k_k'] = 0.85 + nrm((N_A, D), 0.05)
    inp['rwkv_k_a'] = 1.0 + nrm((N_A, D), 0.05)
    inp['rwkv_r_k'] = nrm((N_A, RWKV_H, RWKV_HEAD), 0.1)
    inp['rwkv_ln_w'] = 1.0 + nrm((N_A, D), 0.05)
    inp['rwkv_ln_b'] = nrm((N_A, D), 0.02)
    inp['rwkv_wo'] = nrm((N_A, D, D), 0.5 * D ** -0.5)
    inp['gla_w_in'] = nrm((N_B, D, 2 * GLA_DK_TOT + 2 * GLA_DV_TOT + GLA_LR), D ** -0.5)
    inp['gla_w_gk2'] = nrm((N_B, GLA_LR, GLA_DK_TOT), GLA_LR ** -0.5)
    inp['gla_b_gk'] = nrm((N_B, GLA_DK_TOT), 0.1)
    inp['gla_norm'] = 1.0 + nrm((N_B, GLA_DV), 0.05)
    inp['gla_wo'] = nrm((N_B, GLA_DV_TOT, D), 0.5 * GLA_DV_TOT ** -0.5)
    inp['conv_w_in'] = nrm((N_C, D, 3 * D), D ** -0.5)
    inp['conv_w'] = nrm((N_C, CONV_W, D), 0.5)
    inp['conv_wo'] = nrm((N_C, D, D), 0.5 * D ** -0.5)
    inp['pool_w'] = nrm((N_D, len(POOL_WINDOWS), POOL_G, POOL_G), POOL_G ** -0.5)
    inp['pool_scale'] = uni((N_D, D), 0.5, 1.5)
    return inp


def reference(x_prompt, x_sample, state_rwkv_wkv, state_rwkv_shift, state_gla, state_conv, state_pool,
              norm_mix, norm_ffn, norm_final, ffn_up, ffn_down,
              rwkv_mu, rwkv_w_rkv, rwkv_w0, rwkv_w1, rwkv_w2, rwkv_a0, rwkv_a1, rwkv_a2, rwkv_g1, rwkv_g2,
              rwkv_k_k, rwkv_k_a, rwkv_r_k, rwkv_ln_w, rwkv_ln_b, rwkv_wo,
              gla_w_in, gla_w_gk2, gla_b_gk, gla_norm, gla_wo,
              conv_w_in, conv_w, conv_wo,
              pool_w, pool_scale):

    def trunk(h, st_wkv, st_shift, st_gla, st_conv, st_pool, pos0):
        n_wkv, n_shift, n_gla, n_conv, n_pool = [], [], [], [], []
        for i in range(DEPTH):
            j = i // N_MIXERS
            kind = i % N_MIXERS
            u = rmsnorm(h, norm_mix[i], NORM_EPS)
            if kind == 0:
                out, s_new, sh_new = rwkv7_mix(u, st_shift[j], st_wkv[j], rwkv_mu[j], rwkv_w_rkv[j], rwkv_w0[j],
                                               rwkv_w1[j], rwkv_w2[j], rwkv_a0[j], rwkv_a1[j], rwkv_a2[j],
                                               rwkv_g1[j], rwkv_g2[j], rwkv_k_k[j], rwkv_k_a[j], rwkv_r_k[j],
                                               rwkv_ln_w[j], rwkv_ln_b[j], rwkv_wo[j])
                n_wkv.append(s_new)
                n_shift.append(sh_new.astype(st_shift.dtype))
            elif kind == 1:
                out, s_new = gla_mix(u, st_gla[j], gla_w_in[j], gla_w_gk2[j], gla_b_gk[j], gla_norm[j], gla_wo[j])
                n_gla.append(s_new)
            elif kind == 2:
                out, c_new = conv_mix(u, st_conv[j], conv_w_in[j], conv_w[j], conv_wo[j])
                n_conv.append(c_new.astype(st_conv.dtype))
            else:
                out, p_new = pool_mix(u, st_pool[j], pos0, pool_w[j], pool_scale[j])
                n_pool.append(p_new.astype(st_pool.dtype))
            h = h + out.astype(h.dtype)
            h = h + sqrelu_mlp(rmsnorm(h, norm_ffn[i], NORM_EPS), ffn_up[i], ffn_down[i]).astype(h.dtype)
        y = rmsnorm(h, norm_final, NORM_EPS)
        return y, jnp.stack(n_wkv), jnp.stack(n_shift), jnp.stack(n_gla), jnp.stack(n_conv), jnp.stack(n_pool)

    dt = x_prompt.dtype
    y_p, wkv_p, sh_p, gla_p, conv_p, pool_p = trunk(
        x_prompt,
        jnp.zeros((N_A, BATCH, RWKV_H, RWKV_HEAD, RWKV_HEAD), dt),
        jnp.zeros((N_A, BATCH, D_MODEL), dt),
        jnp.zeros((N_B, BATCH, GLA_H, GLA_DK, GLA_DV), dt),
        jnp.zeros((N_C, BATCH, CONV_W - 1, D_MODEL), dt),
        jnp.zeros((N_D, BATCH, POOL_BUF, D_MODEL), dt),
        0)
    y_s, wkv_s, sh_s, gla_s, conv_s, pool_s = trunk(
        x_sample, state_rwkv_wkv, state_rwkv_shift, state_gla, state_conv, state_pool, PAST_LEN)
    return (y_p, y_s, wkv_p, wkv_s, sh_p, sh_s, gla_p, gla_s, conv_p, conv_s, pool_p, pool_s)
```

```python
import functools

import jax
import jax.numpy as jnp
from jax import lax
from jax.experimental import pallas as pl
from jax.experimental.pallas import tpu as pltpu

F32 = jnp.float32
BF16 = jnp.bfloat16

D_MODEL = 1024
D_FF = 4 * D_MODEL
NORM_EPS = 1e-6
PAST_LEN = 16384

RWKV_HEAD = 64
RWKV_H = D_MODEL // RWKV_HEAD
RWKV_GN_EPS = 64e-5

GLA_H = 4
GLA_DK_TOT = D_MODEL // 2
GLA_DV_TOT = D_MODEL
GLA_DK = GLA_DK_TOT // GLA_H
GLA_DV = GLA_DV_TOT // GLA_H
GLA_LR = 16
GLA_NORMALIZER = 16.0
GLA_NORM_EPS = 1e-5

CONV_W = 3
POOL_WINDOWS = (2, 4, 8, 16)
POOL_G = D_MODEL // len(POOL_WINDOWS)
POOL_BUF = max(POOL_WINDOWS) - 1

LANES = 128
SUBLANES = 8
VMEM_LIMIT_BYTES = 56 * 1024 * 1024

CHUNK = 64
FFN_COLS = 1024
HI = lax.Precision.HIGHEST


def _const_spec(shape):
    nd = len(shape)
    return pl.BlockSpec(shape, lambda *_: (0,) * nd, pipeline_mode=pl.Buffered(1))


def _params(sem):
    return pltpu.CompilerParams(dimension_semantics=sem, vmem_limit_bytes=VMEM_LIMIT_BYTES)


def _rms(x, g, eps):
    return x * lax.rsqrt(jnp.mean(x * x, axis=-1, keepdims=True) + eps) * g


def _bdot(a, w):
    return jnp.dot(a.astype(BF16), w, preferred_element_type=F32)


def _mm(a, b, prec=None):
    return jnp.dot(a, b, preferred_element_type=F32, precision=prec)


def _nt(a, b, prec=None):
    return lax.dot_general(a, b, (((1,), (1,)), ((), ())), preferred_element_type=F32, precision=prec)


def _tn(a, b, prec=None):
    return lax.dot_general(a, b, (((0,), (0,)), ((), ())), preferred_element_type=F32, precision=prec)


def _iota(shape, dim):
    return lax.broadcasted_iota(jnp.int32, shape, dim)


def _group_sum(x, group):
    shift = group.bit_length() - 1
    bd = (jnp.right_shift(_iota((LANES, LANES), 0), shift)
          == jnp.right_shift(_iota((LANES, LANES), 1), shift)).astype(BF16)
    hi = x.astype(BF16)
    lo = (x - hi.astype(F32)).astype(BF16)
    cols = []
    for c in range(x.shape[1] // LANES):
        sl = slice(c * LANES, (c + 1) * LANES)
        cols.append(_mm(hi[:, sl], bd) + _mm(lo[:, sl], bd))
    return jnp.concatenate(cols, axis=1)


def _softplus(x):
    return jnp.maximum(x, 0.0) + jnp.log(1.0 + jnp.exp(-jnp.abs(x)))


def _row_tile(m, want):
    t = min(want, m)
    assert m % t == 0, (m, t)
    return t


def _ffn_kernel(h_ref, g_ref, wu_ref, wd_ref, *rest, final):
    if final:
        gf_ref, o_ref = rest
    else:
        (o_ref,) = rest
    h = h_ref[...]
    x = _rms(h, g_ref[...], NORM_EPS).astype(BF16)
    acc = h
    for c in range(D_FF // FFN_COLS):
        sl = slice(c * FFN_COLS, (c + 1) * FFN_COLS)
        a = jnp.dot(x, wu_ref[:, sl], preferred_element_type=F32)
        a = jnp.square(jnp.maximum(a, 0.0)).astype(BF16)
        acc = acc + jnp.dot(a, wd_ref[sl, :], preferred_element_type=F32)
    if final:
        acc = _rms(acc, gf_ref[...], NORM_EPS)
    o_ref[...] = acc


def _ffn(h, g, wu, wd, final_g=None, tm=512):
    m = h.shape[0]
    tm = _row_tile(m, tm)
    row = pl.BlockSpec((tm, D_MODEL), lambda i: (i, 0))
    ins = [h, g.reshape(1, D_MODEL), wu, wd]
    specs = [row, _const_spec((1, D_MODEL)), _const_spec(wu.shape), _const_spec(wd.shape)]
    if final_g is not None:
        ins.append(final_g.reshape(1, D_MODEL))
        specs.append(_const_spec((1, D_MODEL)))
    return pl.pallas_call(
        functools.partial(_ffn_kernel, final=final_g is not None),
        grid=(m // tm,),
        in_specs=specs,
        out_specs=row,
        out_shape=jax.ShapeDtypeStruct((m, D_MODEL), F32),
        compiler_params=_params(("parallel",)),
        name="ffn",
    )(*ins)


def _rwkv_pre_math(u, prev, mu_ref, wrkv_ref, w0_ref, w1_ref, w2_ref, a0_ref, a1_ref, a2_ref,
                   g1_ref, g2_ref, kk_ref, ka_ref, rk_ref):
    dx = prev - u

    def mix(i):
        return (u + dx * mu_ref[i:i + 1, :]).astype(BF16)

    r = _bdot(mix(0), wrkv_ref[0])
    k = _bdot(mix(2), wrkv_ref[1])
    v = _bdot(mix(3), wrkv_ref[2])
    wl = w0_ref[...] + _bdot(jnp.tanh(_bdot(mix(1), w1_ref[...])), w2_ref[...])
    lw = -jnp.exp(-_softplus(-wl) - 0.5)
    a = jax.nn.sigmoid(a0_ref[...] + _bdot(_bdot(mix(4), a1_ref[...]), a2_ref[...]))
    g = _bdot(jax.nn.sigmoid(_bdot(mix(5), g1_ref[...])), g2_ref[...])
    kk = k * kk_ref[...]
    kk = kk / jnp.maximum(jnp.sqrt(_group_sum(kk * kk, RWKV_HEAD)), 1e-12)
    k = k * (1.0 + (a - 1.0) * ka_ref[...])
    bonus = _group_sum(r * k * rk_ref[...], RWKV_HEAD) * v
    return r, lw, k, v, kk, kk * a, g, bonus


_N_RWKV_W = 13


def _rwkv_pre_prompt_kernel(h_ref, nm_ref, *rest):
    w = rest[:_N_RWKV_W]
    outs = rest[_N_RWKV_W:_N_RWKV_W + 8]
    sh_ref, carry_ref = rest[_N_RWKV_W + 8:]
    tm = h_ref.shape[0]

    @pl.when(pl.program_id(1) == 0)
    def _():
        carry_ref[...] = jnp.zeros_like(carry_ref)

    u = _rms(h_ref[...], nm_ref[...], NORM_EPS)
    prev = jnp.where(_iota(u.shape, 0) == 0, carry_ref[0:1, :], pltpu.roll(u, 1, axis=0))
    last = u[tm - 1:tm, :]
    carry_ref[0:1, :] = last
    sh_ref[...] = last
    for o_ref, val in zip(outs, _rwkv_pre_math(u, prev, *w)):
        o_ref[...] = val


def _rwkv_pre_decode_kernel(h_ref, prev_ref, nm_ref, *rest):
    w = rest[:_N_RWKV_W]
    outs = rest[_N_RWKV_W:_N_RWKV_W + 8]
    (sh_ref,) = rest[_N_RWKV_W + 8:]
    u = _rms(h_ref[...], nm_ref[...], NORM_EPS)
    sh_ref[...] = u
    for o_ref, val in zip(outs, _rwkv_pre_math(u, prev_ref[...], *w)):
        o_ref[...] = val


def _rwkv_weights(p):
    row = lambda x: x.reshape(1, D_MODEL)
    return [p["mu"], p["w_rkv"].astype(BF16), row(p["w0"]), p["w1"].astype(BF16), p["w2"].astype(BF16),
            row(p["a0"]), p["a1"].astype(BF16), p["a2"].astype(BF16), p["g1"].astype(BF16),
            p["g2"].astype(BF16), row(p["k_k"]), row(p["k_a"]), row(p["r_k"])]


def _rwkv_pre_prompt(h, nm, p, tm=256):
    b, l, _ = h.shape
    tm = _row_tile(l, tm)
    ws = _rwkv_weights(p)
    tile = pl.BlockSpec((None, tm, D_MODEL), lambda i, j: (i, j, 0))
    outs = pl.pallas_call(
        _rwkv_pre_prompt_kernel,
        grid=(b, l // tm),
        in_specs=[tile, _const_spec((1, D_MODEL))] + [_const_spec(w.shape) for w in ws],
        out_specs=[tile] * 8 + [pl.BlockSpec((None, 1, D_MODEL), lambda i, j: (i, 0, 0))],
        out_shape=[jax.ShapeDtypeStruct((b, l, D_MODEL), F32)] * 8
        + [jax.ShapeDtypeStruct((b, 1, D_MODEL), F32)],
        scratch_shapes=[pltpu.VMEM((SUBLANES, D_MODEL), F32)],
        compiler_params=_params(("parallel", "arbitrary")),
        name="rwkv_pre_prompt",
    )(h, nm.reshape(1, D_MODEL), *ws)
    return outs[:8], outs[8].reshape(b, D_MODEL)


def _rwkv_pre_decode(h, prev, nm, p):
    m = h.shape[0]
    ws = _rwkv_weights(p)
    full = pl.BlockSpec((m, D_MODEL), lambda i: (0, 0))
    outs = pl.pallas_call(
        _rwkv_pre_decode_kernel,
        grid=(1,),
        in_specs=[full, full, _const_spec((1, D_MODEL))] + [_const_spec(w.shape) for w in ws],
        out_specs=[full] * 9,
        out_shape=[jax.ShapeDtypeStruct((m, D_MODEL), F32)] * 9,
        compiler_params=_params(("arbitrary",)),
        name="rwkv_pre_decode",
    )(h, prev, nm.reshape(1, D_MODEL), *ws)
    return outs[:8], outs[8]


def _rwkv_rec_prompt_kernel(r_ref, lw_ref, k_ref, v_ref, kk_ref, bb_ref, y_ref, so_ref, s_scr):
    tc = r_ref.shape[0]
    n_pairs = D_MODEL // LANES
    c2 = 2 * CHUNK

    @pl.when(pl.program_id(1) == 0)
    def _():
        s_scr[...] = jnp.zeros_like(s_scr)

    ltri = (_iota((CHUNK, CHUNK), 0) >= _iota((CHUNK, CHUNK), 1)).astype(F32)
    ri, ci = _iota((c2, c2), 0), _iota((c2, c2), 1)
    same = jnp.right_shift(ri, 6) == jnp.right_shift(ci, 6)
    strict = same & (ri > ci)
    incl = same & (ri >= ci)
    eye = (ri == ci).astype(F32)
    head0 = _iota((CHUNK, LANES), 1) < RWKV_HEAD

    def stack(z):
        return jnp.concatenate([jnp.where(head0, z, 0.0), jnp.where(head0, 0.0, z)], axis=0)

    def chunk(c, carry):
        rows = pl.ds(pl.multiple_of(c * CHUNK, CHUNK), CHUNK)
        lw = lw_ref[rows, :]
        cum = _mm(ltri, lw, HI)
        g_in = jnp.exp(cum)
        g_inv = jnp.exp(-cum)
        g_last = g_in[CHUNK - 1:CHUNK, :]
        rt = r_ref[rows, :] * g_in
        at = -kk_ref[rows, :] * jnp.exp(cum - lw)
        bt = bb_ref[rows, :] * g_inv
        kt = k_ref[rows, :] * g_inv
        v = v_ref[rows, :]
        for p in range(n_pairs):
            sl = slice(p * LANES, (p + 1) * LANES)
            a_s, r_s, b_s, k_s, v_s = (stack(z[:, sl]) for z in (at, rt, bt, kt, v))
            gl = g_last[:, sl]
            s0 = s_scr[p]
            a_ab = jnp.where(strict, _nt(a_s, b_s, HI), 0.0)
            a_ak = jnp.where(strict, _nt(a_s, k_s, HI), 0.0)
            a_rb = jnp.where(incl, _nt(r_s, b_s, HI), 0.0)
            a_rk = jnp.where(incl, _nt(r_s, k_s, HI), 0.0)
            inv = eye + a_ab
            pw = a_ab
            for _ in range(CHUNK.bit_length() - 2):
                pw = _mm(pw, pw, HI)
                inv = inv + _mm(inv, pw, HI)
            x = _nt(a_s, s0, HI) + _mm(a_ak, v_s, HI)
            u = _mm(inv, x, HI)
            y = _nt(r_s, s0, HI) + _mm(a_rb, u, HI) + _mm(a_rk, v_s, HI)
            y_ref[rows, sl] = y[:CHUNK] + y[CHUNK:]
            uv = jnp.concatenate([u, v_s], axis=0)
            bk = jnp.concatenate([b_s * gl, k_s * gl], axis=0)
            s_scr[p] = s0 * gl + _tn(uv, bk, HI)
        return carry

    lax.fori_loop(0, tc // CHUNK, chunk, 0)

    @pl.when(pl.program_id(1) == pl.num_programs(1) - 1)
    def _():
        for hd in range(RWKV_H):
            p, o = divmod(hd, 2)
            so_ref[hd] = s_scr[p][o * RWKV_HEAD:(o + 1) * RWKV_HEAD, o * RWKV_HEAD:(o + 1) * RWKV_HEAD]


def _rwkv_rec_prompt(r, lw, k, v, kk, bb, tc=256):
    b, l, _ = r.shape
    tc = _row_tile(l, tc)
    tile = pl.BlockSpec((None, tc, D_MODEL), lambda i, j: (i, j, 0))
    st = (RWKV_H, RWKV_HEAD, RWKV_HEAD)
    return pl.pallas_call(
        _rwkv_rec_prompt_kernel,
        grid=(b, l // tc),
        in_specs=[tile] * 6,
        out_specs=[tile, pl.BlockSpec((None,) + st, lambda i, j: (i, 0, 0, 0))],
        out_shape=[jax.ShapeDtypeStruct((b, l, D_MODEL), F32), jax.ShapeDtypeStruct((b,) + st, F32)],
        scratch_shapes=[pltpu.VMEM((D_MODEL // LANES, LANES, LANES), F32)],
        compiler_params=_params(("parallel", "arbitrary")),
        name="rwkv_rec_prompt",
    )(r, lw, k, v, kk, bb)


def _rwkv_rec_decode_kernel(r_ref, lw_ref, k_ref, v_ref, kk_ref, bb_ref, s_ref, y_ref, so_ref):
    eye = (_iota((RWKV_HEAD, RWKV_HEAD), 0) == _iota((RWKV_HEAD, RWKV_HEAD), 1)).astype(F32)[None]
    for hd in range(RWKV_H):
        hs = slice(hd, hd + 1)
        s0 = s_ref[:, hd]
        a = -kk_ref[:, hs, :]
        sa = jnp.sum(s0 * a, axis=-1, keepdims=True)
        v_col = jnp.sum(eye * v_ref[:, hs, :], axis=-1, keepdims=True)
        s1 = s0 * jnp.exp(lw_ref[:, hs, :]) + sa * bb_ref[:, hs, :] + v_col * k_ref[:, hs, :]
        so_ref[:, hd] = s1
        y_col = jnp.sum(s1 * r_ref[:, hs, :], axis=-1, keepdims=True)
        y_ref[:, hs, :] = jnp.sum(eye * y_col, axis=1, keepdims=True)


def _rwkv_rec_decode(r, lw, k, v, kk, bb, s, bblk=8):
    m = r.shape[0]
    vec = pl.BlockSpec((bblk, RWKV_H, RWKV_HEAD), lambda i: (i, 0, 0))
    st = pl.BlockSpec((bblk, RWKV_H, RWKV_HEAD, RWKV_HEAD), lambda i: (i, 0, 0, 0))
    hv = lambda x: x.reshape(m, RWKV_H, RWKV_HEAD)
    y, s_new = pl.pallas_call(
        _rwkv_rec_decode_kernel,
        grid=(m // bblk,),
        in_specs=[vec] * 6 + [st],
        out_specs=[vec, st],
        out_shape=[jax.ShapeDtypeStruct((m, RWKV_H, RWKV_HEAD), F32), jax.ShapeDtypeStruct(s.shape, F32)],
        compiler_params=_params(("parallel",)),
        name="rwkv_rec_decode",
    )(hv(r), hv(lw), hv(k), hv(v), hv(kk), hv(bb), s)
    return y.reshape(m, D_MODEL), s_new


def _rwkv_post_kernel(h_ref, y_ref, bo_ref, g_ref, lnw_ref, lnb_ref, wo_ref, o_ref):
    y = y_ref[...]
    d = y - _group_sum(y, RWKV_HEAD) * (1.0 / RWKV_HEAD)
    var = _group_sum(d * d, RWKV_HEAD) * (1.0 / RWKV_HEAD)
    yn = d * lax.rsqrt(var + RWKV_GN_EPS) * lnw_ref[...] + lnb_ref[...]
    o_ref[...] = h_ref[...] + _bdot((yn + bo_ref[...]) * g_ref[...], wo_ref[...])


def _rwkv_post(h, y, bonus, g, p, tm=512):
    m = h.shape[0]
    tm = _row_tile(m, tm)
    row = pl.BlockSpec((tm, D_MODEL), lambda i: (i, 0))
    vec = _const_spec((1, D_MODEL))
    return pl.pallas_call(
        _rwkv_post_kernel,
        grid=(m // tm,),
        in_specs=[row] * 4 + [vec, vec, _const_spec((D_MODEL, D_MODEL))],
        out_specs=row,
        out_shape=jax.ShapeDtypeStruct((m, D_MODEL), F32),
        compiler_params=_params(("parallel",)),
        name="rwkv_post",
    )(h, y, bonus, g, p["ln_w"].reshape(1, D_MODEL), p["ln_b"].reshape(1, D_MODEL), p["wo"].astype(BF16))


def _gla_pre_kernel(h_ref, nm_ref, win_ref, wgl_ref, wgk_ref, bgk_ref, q_ref, k_ref, v_ref, g_ref, gk_ref):
    u = _rms(h_ref[...], nm_ref[...], NORM_EPS).astype(BF16)
    z = jnp.dot(u, win_ref[...], preferred_element_type=F32)
    q_ref[...] = z[:, :GLA_DK_TOT] * (GLA_DK ** -0.5)
    k_ref[...] = z[:, GLA_DK_TOT:2 * GLA_DK_TOT]
    v_ref[...] = z[:, 2 * GLA_DK_TOT:2 * GLA_DK_TOT + GLA_DV_TOT]
    g_ref[...] = z[:, 2 * GLA_DK_TOT + GLA_DV_TOT:]
    gl = jnp.dot(u, wgl_ref[...], preferred_element_type=F32)
    pre = _bdot(gl, wgk_ref[...]) + bgk_ref[...]
    gk_ref[...] = -_softplus(-pre) * (1.0 / GLA_NORMALIZER)


def _gla_pre(h, nm, p, tm=512):
    m = h.shape[0]
    tm = _row_tile(m, tm)
    n_main = 2 * GLA_DK_TOT + 2 * GLA_DV_TOT
    w_in = p["w_in"].astype(BF16)
    w_main = w_in[:, :n_main]
    w_gl = jnp.pad(w_in[:, n_main:], ((0, 0), (0, LANES - GLA_LR)))
    w_gk = jnp.pad(p["w_gk2"].astype(BF16), ((0, LANES - GLA_LR), (0, 0)))
    row = lambda n: pl.BlockSpec((tm, n), lambda i: (i, 0))
    widths = (GLA_DK_TOT, GLA_DK_TOT, GLA_DV_TOT, GLA_DV_TOT, GLA_DK_TOT)
    return pl.pallas_call(
        _gla_pre_kernel,
        grid=(m // tm,),
        in_specs=[row(D_MODEL), _const_spec((1, D_MODEL)), _const_spec(w_main.shape), _const_spec(w_gl.shape),
                  _const_spec(w_gk.shape), _const_spec((1, GLA_DK_TOT))],
        out_specs=[row(n) for n in widths],
        out_shape=[jax.ShapeDtypeStruct((m, n), F32) for n in widths],
        compiler_params=_params(("parallel",)),
        name="gla_pre",
    )(h, nm.reshape(1, D_MODEL), w_main, w_gl, w_gk, p["b_gk"].reshape(1, GLA_DK_TOT))


def _gla_rec_prompt_kernel(q_ref, k_ref, v_ref, gk_ref, o_ref, so_ref, s_scr):
    tc = q_ref.shape[0]

    @pl.when(pl.program_id(1) == 0)
    def _():
        s_scr[...] = jnp.zeros_like(s_scr)

    ri, ci = _iota((CHUNK, CHUNK), 0), _iota((CHUNK, CHUNK), 1)
    incl = ri >= ci
    ltri = incl.astype(F32)

    def chunk(c, carry):
        rows = pl.ds(pl.multiple_of(c * CHUNK, CHUNK), CHUNK)
        cum = _mm(ltri, gk_ref[rows, :], HI)
        qe = q_ref[rows, :] * jnp.exp(cum)
        ke = k_ref[rows, :] * jnp.exp(-cum)
        g_last = jnp.exp(cum[CHUNK - 1:CHUNK, :])
        kh = ke * g_last
        v = v_ref[rows, :]
        for hd in range(GLA_H):
            ks = slice(hd * GLA_DK, (hd + 1) * GLA_DK)
            vs = slice(hd * GLA_DV, (hd + 1) * GLA_DV)
            st = s_scr[hd]
            att = jnp.where(incl, _nt(qe[:, ks], ke[:, ks], HI), 0.0)
            o_ref[rows, vs] = _nt(qe[:, ks], st, HI) + _mm(att, v[:, vs], HI)
            s_scr[hd] = st * g_last[:, ks] + _tn(v[:, vs], kh[:, ks], HI)
        return carry

    lax.fori_loop(0, tc // CHUNK, chunk, 0)

    @pl.when(pl.program_id(1) == pl.num_programs(1) - 1)
    def _():
        for hd in range(GLA_H):
            so_ref[hd] = s_scr[hd].T


def _gla_rec_prompt(q, k, v, gk, tc=256):
    b, l, _ = q.shape
    tc = _row_tile(l, tc)
    tile = lambda n: pl.BlockSpec((None, tc, n), lambda i, j: (i, j, 0))
    st = (GLA_H, GLA_DK, GLA_DV)
    return pl.pallas_call(
        _gla_rec_prompt_kernel,
        grid=(b, l // tc),
        in_specs=[tile(GLA_DK_TOT), tile(GLA_DK_TOT), tile(GLA_DV_TOT), tile(GLA_DK_TOT)],
        out_specs=[tile(GLA_DV_TOT), pl.BlockSpec((None,) + st, lambda i, j: (i, 0, 0, 0))],
        out_shape=[jax.ShapeDtypeStruct((b, l, GLA_DV_TOT), F32), jax.ShapeDtypeStruct((b,) + st, F32)],
        scratch_shapes=[pltpu.VMEM((GLA_H, GLA_DV, GLA_DK), F32)],
        compiler_params=_params(("parallel", "arbitrary")),
        name="gla_rec_prompt",
    )(q, k, v, gk)


def _gla_rec_decode_kernel(q_ref, k_ref, v_ref, gk_ref, s_ref, o_ref, so_ref):
    eye = (_iota((GLA_DK, GLA_DK), 0) == _iota((GLA_DK, GLA_DK), 1)).astype(F32)[None]

    def col(x):
        return jnp.sum(eye * x, axis=-1, keepdims=True)

    for hd in range(GLA_H):
        hs = slice(hd, hd + 1)
        s1 = s_ref[:, hd] * col(jnp.exp(gk_ref[:, hs, :])) + col(k_ref[:, hs, :]) * v_ref[:, hs, :]
        so_ref[:, hd] = s1
        o_ref[:, hs, :] = jnp.sum(col(q_ref[:, hs, :]) * s1, axis=1, keepdims=True)


def _gla_rec_decode(q, k, v, gk, s, bblk=8):
    m = q.shape[0]
    kvec = pl.BlockSpec((bblk, GLA_H, GLA_DK), lambda i: (i, 0, 0))
    vvec = pl.BlockSpec((bblk, GLA_H, GLA_DV), lambda i: (i, 0, 0))
    st = pl.BlockSpec((bblk, GLA_H, GLA_DK, GLA_DV), lambda i: (i, 0, 0, 0))
    hk = lambda x: x.reshape(m, GLA_H, GLA_DK)
    o, s_new = pl.pallas_call(
        _gla_rec_decode_kernel,
        grid=(m // bblk,),
        in_specs=[kvec, kvec, vvec, kvec, st],
        out_specs=[vvec, st],
        out_shape=[jax.ShapeDtypeStruct((m, GLA_H, GLA_DV), F32), jax.ShapeDtypeStruct(s.shape, F32)],
        compiler_params=_params(("parallel",)),
        name="gla_rec_decode",
    )(hk(q), hk(k), v.reshape(m, GLA_H, GLA_DV), hk(gk), s)
    return o.reshape(m, GLA_DV_TOT), s_new


def _gla_post_kernel(h_ref, o_ref_in, g_ref, nw_ref, wo_ref, out_ref):
    o = o_ref_in[...]
    parts = []
    for hd in range(GLA_H):
        oh = o[:, hd * GLA_DV:(hd + 1) * GLA_DV]
        parts.append(oh * lax.rsqrt(jnp.mean(oh * oh, axis=-1, keepdims=True) + GLA_NORM_EPS))
    on = jnp.concatenate(parts, axis=1) * nw_ref[...]
    g = g_ref[...]
    out_ref[...] = h_ref[...] + _bdot(on * (g * jax.nn.sigmoid(g)), wo_ref[...])


def _gla_post(h, o, g, p, tm=512):
    m = h.shape[0]
    tm = _row_tile(m, tm)
    row = pl.BlockSpec((tm, D_MODEL), lambda i: (i, 0))
    return pl.pallas_call(
        _gla_post_kernel,
        grid=(m // tm,),
        in_specs=[row] * 3 + [_const_spec((1, D_MODEL)), _const_spec((D_MODEL, D_MODEL))],
        out_specs=row,
        out_shape=jax.ShapeDtypeStruct((m, D_MODEL), F32),
        compiler_params=_params(("parallel",)),
        name="gla_post",
    )(h, o, g, jnp.tile(p["norm"], GLA_H).reshape(1, D_MODEL), p["wo"].astype(BF16))


def _conv_gates(h_ref, nm_ref, win_ref):
    u = _rms(h_ref[...], nm_ref[...], NORM_EPS).astype(BF16)
    z = jnp.dot(u, win_ref[...], preferred_element_type=F32)
    return z[:, :D_MODEL], z[:, D_MODEL:2 * D_MODEL] * z[:, 2 * D_MODEL:]


def _conv_prompt_kernel(h_ref, nm_ref, win_ref, cw_ref, wo_ref, o_ref, st_ref, carry_ref):
    tm = h_ref.shape[0]

    @pl.when(pl.program_id(1) == 0)
    def _():
        carry_ref[...] = jnp.zeros_like(carry_ref)

    g_b, zc = _conv_gates(h_ref, nm_ref, win_ref)
    row = _iota(zc.shape, 0)
    z1 = jnp.where(row == 0, carry_ref[1:2, :], pltpu.roll(zc, 1, axis=0))
    z2 = jnp.where(row == 0, carry_ref[0:1, :], jnp.where(row == 1, carry_ref[1:2, :], pltpu.roll(zc, 2, axis=0)))
    conv = cw_ref[0:1, :] * z2 + cw_ref[1:2, :] * z1 + cw_ref[2:3, :] * zc
    o_ref[...] = h_ref[...] + _bdot(g_b * conv, wo_ref[...])
    tail = zc[tm - (CONV_W - 1):, :]
    carry_ref[0:CONV_W - 1, :] = tail
    st_ref[...] = tail


def _conv_prompt(h, nm, p, tm=256):
    b, l, _ = h.shape
    tm = _row_tile(l, tm)
    tile = pl.BlockSpec((None, tm, D_MODEL), lambda i, j: (i, j, 0))
    return pl.pallas_call(
        _conv_prompt_kernel,
        grid=(b, l // tm),
        in_specs=[tile, _const_spec((1, D_MODEL)), _const_spec((D_MODEL, 3 * D_MODEL)),
                  _const_spec((CONV_W, D_MODEL)), _const_spec((D_MODEL, D_MODEL))],
        out_specs=[tile, pl.BlockSpec((None, CONV_W - 1, D_MODEL), lambda i, j: (i, 0, 0))],
        out_shape=[jax.ShapeDtypeStruct((b, l, D_MODEL), F32),
                   jax.ShapeDtypeStruct((b, CONV_W - 1, D_MODEL), F32)],
        scratch_shapes=[pltpu.VMEM((SUBLANES, D_MODEL), F32)],
        compiler_params=_params(("parallel", "arbitrary")),
        name="conv_prompt",
    )(h, nm.reshape(1, D_MODEL), p["w_in"].astype(BF16), p["w"], p["wo"].astype(BF16))


def _conv_decode_kernel(h_ref, buf_ref, nm_ref, win_ref, cw_ref, wo_ref, o_ref, st_ref):
    g_b, zc = _conv_gates(h_ref, nm_ref, win_ref)
    z2 = buf_ref[:, 0, :]
    z1 = buf_ref[:, 1, :]
    conv = cw_ref[0:1, :] * z2 + cw_ref[1:2, :] * z1 + cw_ref[2:3, :] * zc
    o_ref[...] = h_ref[...] + _bdot(g_b * conv, wo_ref[...])
    st_ref[:, 0, :] = z1
    st_ref[:, 1, :] = zc


def _conv_decode(h, buf, nm, p):
    m = h.shape[0]
    full = pl.BlockSpec((m, D_MODEL), lambda i: (0, 0))
    bufs = pl.BlockSpec(buf.shape, lambda i: (0, 0, 0))
    return pl.pallas_call(
        _conv_decode_kernel,
        grid=(1,),
        in_specs=[full, bufs, _const_spec((1, D_MODEL)), _const_spec((D_MODEL, 3 * D_MODEL)),
                  _const_spec((CONV_W, D_MODEL)), _const_spec((D_MODEL, D_MODEL))],
        out_specs=[full, bufs],
        out_shape=[jax.ShapeDtypeStruct((m, D_MODEL), F32), jax.ShapeDtypeStruct(buf.shape, F32)],
        compiler_params=_params(("arbitrary",)),
        name="conv_decode",
    )(h, buf, nm.reshape(1, D_MODEL), p["w_in"].astype(BF16), p["w"], p["wo"].astype(BF16))


def _pool_project(d_groups, pw_ref, sc_ref):
    ys = [_bdot(d, pw_ref[gi]) for gi, d in enumerate(d_groups)]
    return jnp.concatenate(ys, axis=1) * sc_ref[...]


def _pool_prompt_kernel(h_ref, nm_ref, pw_ref, sc_ref, o_ref, st_ref, carry_ref):
    tm = h_ref.shape[0]
    hist = carry_ref.shape[0]
    j = pl.program_id(1)

    @pl.when(j == 0)
    def _():
        carry_ref[...] = jnp.zeros_like(carry_ref)

    u = _rms(h_ref[...], nm_ref[...], NORM_EPS)
    ext = jnp.concatenate([carry_ref[...], u], axis=0)
    pos = j * tm + _iota((tm, 1), 0)
    d_groups = []
    for gi, w in enumerate(POOL_WINDOWS):
        sl = slice(gi * POOL_G, (gi + 1) * POOL_G)
        s = ext[:, sl]
        span = 1
        while span < w:
            s = s + pltpu.roll(s, span, axis=0)
            span *= 2
        cnt = jnp.minimum(w, pos + 1).astype(F32)
        d_groups.append(s[hist:, :] / cnt - u[:, sl])
    o_ref[...] = h_ref[...] + _pool_project(d_groups, pw_ref, sc_ref)
    tail = u[tm - hist:, :]
    carry_ref[...] = tail
    st_ref[...] = tail


def _pool_prompt(h, nm, p, tm=256):
    b, l, _ = h.shape
    tm = _row_tile(l, tm)
    hist = POOL_BUF + 1
    tile = pl.BlockSpec((None, tm, D_MODEL), lambda i, j: (i, j, 0))
    out, st = pl.pallas_call(
        _pool_prompt_kernel,
        grid=(b, l // tm),
        in_specs=[tile, _const_spec((1, D_MODEL)), _const_spec(p["w"].shape), _const_spec((1, D_MODEL))],
        out_specs=[tile, pl.BlockSpec((None, hist, D_MODEL), lambda i, j: (i, 0, 0))],
        out_shape=[jax.ShapeDtypeStruct((b, l, D_MODEL), F32), jax.ShapeDtypeStruct((b, hist, D_MODEL), F32)],
        scratch_shapes=[pltpu.VMEM((hist, D_MODEL), F32)],
        compiler_params=_params(("parallel", "arbitrary")),
        name="pool_prompt",
    )(h, nm.reshape(1, D_MODEL), p["w"].astype(BF16), p["scale"].reshape(1, D_MODEL))
    return out, st[:, 1:]


def _pool_decode_kernel(h_ref, buf_ref, nm_ref, pw_ref, sc_ref, o_ref, st_ref):
    u = _rms(h_ref[...], nm_ref[...], NORM_EPS)
    d_groups = []
    for gi, w in enumerate(POOL_WINDOWS):
        sl = slice(gi * POOL_G, (gi + 1) * POOL_G)
        s = u[:, sl]
        for i in range(1, w):
            s = s + buf_ref[:, POOL_BUF - i, sl]
        cnt = float(min(w, PAST_LEN + 1))
        d_groups.append(s / cnt - u[:, sl])
    o_ref[...] = h_ref[...] + _pool_project(d_groups, pw_ref, sc_ref)
    st_ref[:, 0:POOL_BUF - 1, :] = buf_ref[:, 1:POOL_BUF, :]
    st_ref[:, POOL_BUF - 1, :] = u


def _pool_decode(h, buf, nm, p):
    m = h.shape[0]
    full = pl.BlockSpec((m, D_MODEL), lambda i: (0, 0))
    bufs = pl.BlockSpec(buf.shape, lambda i: (0, 0, 0))
    return pl.pallas_call(
        _pool_decode_kernel,
        grid=(1,),
        in_specs=[full, bufs, _const_spec((1, D_MODEL)), _const_spec(p["w"].shape), _const_spec((1, D_MODEL))],
        out_specs=[full, bufs],
        out_shape=[jax.ShapeDtypeStruct((m, D_MODEL), F32), jax.ShapeDtypeStruct(buf.shape, F32)],
        compiler_params=_params(("arbitrary",)),
        name="pool_decode",
    )(h, buf, nm.reshape(1, D_MODEL), p["w"].astype(BF16), p["scale"].reshape(1, D_MODEL))


def _trunk_prompt(x, nm, nf, nfin, wu, wd, rw, gl, cv, po):
    b, l, _ = x.shape
    flat = lambda t: t.reshape(b * l, t.shape[-1])
    seq = lambda t: t.reshape(b, l, t.shape[-1])

    (r, lw, k, v, kk, bb, g, bonus), shift = _rwkv_pre_prompt(x, nm[0], rw)
    y, wkv = _rwkv_rec_prompt(r, lw, k, v, kk, bb)
    h = _rwkv_post(flat(x), flat(y), flat(bonus), flat(g), rw)
    h = _ffn(h, nf[0], wu[0], wd[0])

    q, k, v, g, gk = _gla_pre(h, nm[1], gl)
    o, gla_s = _gla_rec_prompt(seq(q), seq(k), seq(v), seq(gk))
    h = _gla_post(h, flat(o), g, gl)
    h = _ffn(h, nf[1], wu[1], wd[1])

    h, conv_s = _conv_prompt(seq(h), nm[2], cv)
    h = _ffn(flat(h), nf[2], wu[2], wd[2])

    h, pool_s = _pool_prompt(seq(h), nm[3], po)
    y = _ffn(flat(h), nf[3], wu[3], wd[3], final_g=nfin)
    return seq(y), wkv[None], shift[None], gla_s[None], conv_s[None], pool_s[None]


def _trunk_decode(x, st_wkv, st_shift, st_gla, st_conv, st_pool, nm, nf, nfin, wu, wd, rw, gl, cv, po):
    m = x.shape[0]
    h = x.reshape(m, D_MODEL)

    (r, lw, k, v, kk, bb, g, bonus), shift = _rwkv_pre_decode(h, st_shift[0], nm[0], rw)
    y, wkv = _rwkv_rec_decode(r, lw, k, v, kk, bb, st_wkv[0])
    h = _rwkv_post(h, y, bonus, g, rw)
    h = _ffn(h, nf[0], wu[0], wd[0])

    q, k, v, g, gk = _gla_pre(h, nm[1], gl)
    o, gla_s = _gla_rec_decode(q, k, v, gk, st_gla[0])
    h = _gla_post(h, o, g, gl)
    h = _ffn(h, nf[1], wu[1], wd[1])

    h, conv_s = _conv_decode(h, st_conv[0], nm[2], cv)
    h = _ffn(h, nf[2], wu[2], wd[2])

    h, pool_s = _pool_decode(h, st_pool[0], nm[3], po)
    y = _ffn(h, nf[3], wu[3], wd[3], final_g=nfin)
    return y.reshape(m, 1, D_MODEL), wkv[None], shift[None], gla_s[None], conv_s[None], pool_s[None]


def kernel(x_prompt, x_sample, state_rwkv_wkv, state_rwkv_shift, state_gla, state_conv, state_pool, norm_mix, norm_ffn, norm_final, ffn_up, ffn_down, rwkv_mu, rwkv_w_rkv, rwkv_w0, rwkv_w1, rwkv_w2, rwkv_a0, rwkv_a1, rwkv_a2, rwkv_g1, rwkv_g2, rwkv_k_k, rwkv_k_a, rwkv_r_k, rwkv_ln_w, rwkv_ln_b, rwkv_wo, gla_w_in, gla_w_gk2, gla_b_gk, gla_norm, gla_wo, conv_w_in, conv_w, conv_wo, pool_w, pool_scale):
    assert x_prompt.shape[1] % CHUNK == 0 and x_sample.shape[1] == 1
    wu = ffn_up.astype(BF16)
    wd = ffn_down.astype(BF16)
    rw = dict(mu=rwkv_mu[0], w_rkv=rwkv_w_rkv[0], w0=rwkv_w0[0], w1=rwkv_w1[0], w2=rwkv_w2[0], a0=rwkv_a0[0],
              a1=rwkv_a1[0], a2=rwkv_a2[0], g1=rwkv_g1[0], g2=rwkv_g2[0], k_k=rwkv_k_k[0], k_a=rwkv_k_a[0],
              r_k=rwkv_r_k[0], ln_w=rwkv_ln_w[0], ln_b=rwkv_ln_b[0], wo=rwkv_wo[0])
    gl = dict(w_in=gla_w_in[0], w_gk2=gla_w_gk2[0], b_gk=gla_b_gk[0], norm=gla_norm[0], wo=gla_wo[0])
    cv = dict(w_in=conv_w_in[0], w=conv_w[0], wo=conv_wo[0])
    po = dict(w=pool_w[0], scale=pool_scale[0])
    shared = (norm_mix, norm_ffn, norm_final, wu, wd, rw, gl, cv, po)
    y_p, wkv_p, sh_p, gla_p, conv_p, pool_p = _trunk_prompt(x_prompt, *shared)
    y_s, wkv_s, sh_s, gla_s, conv_s, pool_s = _trunk_decode(
        x_sample, state_rwkv_wkv, state_rwkv_shift, state_gla, state_conv, state_pool, *shared)
    return (y_p, y_s, wkv_p, wkv_s, sh_p, sh_s, gla_p, gla_s, conv_p, conv_s, pool_p, pool_s)
```

```python
import functools

import jax
import jax.numpy as jnp
from jax import lax
from jax.experimental import pallas as pl
from jax.experimental.pallas import tpu as pltpu

F32 = jnp.float32
BF16 = jnp.bfloat16

D_MODEL = 1024
D_FF = 4 * D_MODEL
NORM_EPS = 1e-6
PAST_LEN = 16384

RWKV_HEAD = 64
RWKV_H = D_MODEL // RWKV_HEAD
RWKV_GN_EPS = 64e-5

GLA_H = 4
GLA_DK_TOT = D_MODEL // 2
GLA_DV_TOT = D_MODEL
GLA_DK = GLA_DK_TOT // GLA_H
GLA_DV = GLA_DV_TOT // GLA_H
GLA_LR = 16
GLA_NORMALIZER = 16.0
GLA_NORM_EPS = 1e-5

CONV_W = 3
POOL_WINDOWS = (2, 4, 8, 16)
POOL_G = D_MODEL // len(POOL_WINDOWS)
POOL_BUF = max(POOL_WINDOWS) - 1

LANES = 128
SUBLANES = 8
VMEM_LIMIT_BYTES = 56 * 1024 * 1024

CHUNK = 64
FFN_COLS = 1024
HI = lax.Precision.HIGHEST


def _const_spec(shape):
    nd = len(shape)
    return pl.BlockSpec(shape, lambda *_: (0,) * nd, pipeline_mode=pl.Buffered(1))


def _params(sem):
    return pltpu.CompilerParams(dimension_semantics=sem, vmem_limit_bytes=VMEM_LIMIT_BYTES)


def _rms(x, g, eps):
    return x * lax.rsqrt(jnp.mean(x * x, axis=-1, keepdims=True) + eps) * g


def _bdot(a, w):
    return jnp.dot(a.astype(BF16), w, preferred_element_type=F32)


def _mm(a, b, prec=None):
    return jnp.dot(a, b, preferred_element_type=F32, precision=prec)


def _nt(a, b, prec=None):
    return lax.dot_general(a, b, (((1,), (1,)), ((), ())), preferred_element_type=F32, precision=prec)


def _tn(a, b, prec=None):
    return lax.dot_general(a, b, (((0,), (0,)), ((), ())), preferred_element_type=F32, precision=prec)


def _iota(shape, dim):
    return lax.broadcasted_iota(jnp.int32, shape, dim)


def _group_sum(x, group):
    shift = group.bit_length() - 1
    bd = (jnp.right_shift(_iota((LANES, LANES), 0), shift)
          == jnp.right_shift(_iota((LANES, LANES), 1), shift)).astype(BF16)
    hi = x.astype(BF16)
    lo = (x - hi.astype(F32)).astype(BF16)
    cols = []
    for c in range(x.shape[1] // LANES):
        sl = slice(c * LANES, (c + 1) * LANES)
        cols.append(_mm(hi[:, sl], bd) + _mm(lo[:, sl], bd))
    return jnp.concatenate(cols, axis=1)


def _softplus(x):
    return jnp.maximum(x, 0.0) + jnp.log(1.0 + jnp.exp(-jnp.abs(x)))


def _row_tile(m, want):
    t = min(want, m)
    assert m % t == 0, (m, t)
    return t


def _ffn_kernel(h_ref, g_ref, wu_ref, wd_ref, *rest, final):
    if final:
        gf_ref, o_ref = rest
    else:
        (o_ref,) = rest
    h = h_ref[...]
    x = _rms(h, g_ref[...], NORM_EPS).astype(BF16)
    acc = h
    for c in range(D_FF // FFN_COLS):
        sl = slice(c * FFN_COLS, (c + 1) * FFN_COLS)
        a = jnp.dot(x, wu_ref[:, sl], preferred_element_type=F32)
        a = jnp.square(jnp.maximum(a, 0.0)).astype(BF16)
        acc = acc + jnp.dot(a, wd_ref[sl, :], preferred_element_type=F32)
    if final:
        acc = _rms(acc, gf_ref[...], NORM_EPS)
    o_ref[...] = acc


def _ffn(h, g, wu, wd, final_g=None, tm=512):
    m = h.shape[0]
    tm = _row_tile(m, tm)
    row = pl.BlockSpec((tm, D_MODEL), lambda i: (i, 0))
    ins = [h, g.reshape(1, D_MODEL), wu, wd]
    specs = [row, _const_spec((1, D_MODEL)), _const_spec(wu.shape), _const_spec(wd.shape)]
    if final_g is not None:
        ins.append(final_g.reshape(1, D_MODEL))
        specs.append(_const_spec((1, D_MODEL)))
    return pl.pallas_call(
        functools.partial(_ffn_kernel, final=final_g is not None),
        grid=(m // tm,),
        in_specs=specs,
        out_specs=row,
        out_shape=jax.ShapeDtypeStruct((m, D_MODEL), F32),
        compiler_params=_params(("parallel",)),
        name="ffn",
    )(*ins)


def _rwkv_pre_math(u, prev, mu_ref, wrkv_ref, w0_ref, w1_ref, w2_ref, a0_ref, a1_ref, a2_ref,
                   g1_ref, g2_ref, kk_ref, ka_ref, rk_ref):
    dx = prev - u

    def mix(i):
        return (u + dx * mu_ref[i:i + 1, :]).astype(BF16)

    r = _bdot(mix(0), wrkv_ref[0])
    k = _bdot(mix(2), wrkv_ref[1])
    v = _bdot(mix(3), wrkv_ref[2])
    wl = w0_ref[...] + _bdot(jnp.tanh(_bdot(mix(1), w1_ref[...])), w2_ref[...])
    lw = -jnp.exp(-_softplus(-wl) - 0.5)
    a = jax.nn.sigmoid(a0_ref[...] + _bdot(_bdot(mix(4), a1_ref[...]), a2_ref[...]))
    g = _bdot(jax.nn.sigmoid(_bdot(mix(5), g1_ref[...])), g2_ref[...])
    kk = k * kk_ref[...]
    kk = kk / jnp.maximum(jnp.sqrt(_group_sum(kk * kk, RWKV_HEAD)), 1e-12)
    k = k * (1.0 + (a - 1.0) * ka_ref[...])
    bonus = _group_sum(r * k * rk_ref[...], RWKV_HEAD) * v
    return r, lw, k, v, kk, kk * a, g, bonus


_N_RWKV_W = 13


def _rwkv_pre_prompt_kernel(h_ref, nm_ref, *rest):
    w = rest[:_N_RWKV_W]
    outs = rest[_N_RWKV_W:_N_RWKV_W + 8]
    sh_ref, carry_ref = rest[_N_RWKV_W + 8:]
    tm = h_ref.shape[0]

    @pl.when(pl.program_id(1) == 0)
    def _():
        carry_ref[...] = jnp.zeros_like(carry_ref)

    u = _rms(h_ref[...], nm_ref[...], NORM_EPS)
    prev = jnp.where(_iota(u.shape, 0) == 0, carry_ref[0:1, :], pltpu.roll(u, 1, axis=0))
    last = u[tm - 1:tm, :]
    carry_ref[0:1, :] = last
    sh_ref[...] = last
    for o_ref, val in zip(outs, _rwkv_pre_math(u, prev, *w)):
        o_ref[...] = val


def _rwkv_pre_decode_kernel(h_ref, prev_ref, nm_ref, *rest):
    w = rest[:_N_RWKV_W]
    outs = rest[_N_RWKV_W:_N_RWKV_W + 8]
    (sh_ref,) = rest[_N_RWKV_W + 8:]
    u = _rms(h_ref[...], nm_ref[...], NORM_EPS)
    sh_ref[...] = u
    for o_ref, val in zip(outs, _rwkv_pre_math(u, prev_ref[...], *w)):
        o_ref[...] = val


def _rwkv_weights(p):
    row = lambda x: x.reshape(1, D_MODEL)
    return [p["mu"], p["w_rkv"].astype(BF16), row(p["w0"]), p["w1"].astype(BF16), p["w2"].astype(BF16),
            row(p["a0"]), p["a1"].astype(BF16), p["a2"].astype(BF16), p["g1"].astype(BF16),
            p["g2"].astype(BF16), row(p["k_k"]), row(p["k_a"]), row(p["r_k"])]


def _rwkv_pre_prompt(h, nm, p, tm=256):
    b, l, _ = h.shape
    tm = _row_tile(l, tm)
    ws = _rwkv_weights(p)
    tile = pl.BlockSpec((None, tm, D_MODEL), lambda i, j: (i, j, 0))
    outs = pl.pallas_call(
        _rwkv_pre_prompt_kernel,
        grid=(b, l // tm),
        in_specs=[tile, _const_spec((1, D_MODEL))] + [_const_spec(w.shape) for w in ws],
        out_specs=[tile] * 8 + [pl.BlockSpec((None, 1, D_MODEL), lambda i, j: (i, 0, 0))],
        out_shape=[jax.ShapeDtypeStruct((b, l, D_MODEL), F32)] * 8
        + [jax.ShapeDtypeStruct((b, 1, D_MODEL), F32)],
        scratch_shapes=[pltpu.VMEM((SUBLANES, D_MODEL), F32)],
        compiler_params=_params(("parallel", "arbitrary")),
        name="rwkv_pre_prompt",
    )(h, nm.reshape(1, D_MODEL), *ws)
    return outs[:8], outs[8].reshape(b, D_MODEL)


def _rwkv_pre_decode(h, prev, nm, p):
    m = h.shape[0]
    ws = _rwkv_weights(p)
    full = pl.BlockSpec((m, D_MODEL), lambda i: (0, 0))
    outs = pl.pallas_call(
        _rwkv_pre_decode_kernel,
        grid=(1,),
        in_specs=[full, full, _const_spec((1, D_MODEL))] + [_const_spec(w.shape) for w in ws],
        out_specs=[full] * 9,
        out_shape=[jax.ShapeDtypeStruct((m, D_MODEL), F32)] * 9,
        compiler_params=_params(("arbitrary",)),
        name="rwkv_pre_decode",
    )(h, prev, nm.reshape(1, D_MODEL), *ws)
    return outs[:8], outs[8]


def _rwkv_rec_prompt_kernel(r_ref, lw_ref, k_ref, v_ref, kk_ref, bb_ref, y_ref, so_ref, s_scr):
    tc = r_ref.shape[0]
    n_pairs = D_MODEL // LANES
    c2 = 2 * CHUNK

    @pl.when(pl.program_id(1) == 0)
    def _():
        s_scr[...] = jnp.zeros_like(s_scr)

    ltri = (_iota((CHUNK, CHUNK), 0) >= _iota((CHUNK, CHUNK), 1)).astype(F32)
    ri, ci = _iota((c2, c2), 0), _iota((c2, c2), 1)
    same = jnp.right_shift(ri, 6) == jnp.right_shift(ci, 6)
    strict = same & (ri > ci)
    incl = same & (ri >= ci)
    eye = (ri == ci).astype(F32)
    head0 = _iota((CHUNK, LANES), 1) < RWKV_HEAD

    def stack(z):
        return jnp.concatenate([jnp.where(head0, z, 0.0), jnp.where(head0, 0.0, z)], axis=0)

    def chunk(c, carry):
        rows = pl.ds(pl.multiple_of(c * CHUNK, CHUNK), CHUNK)
        lw = lw_ref[rows, :]
        cum = _mm(ltri, lw, HI)
        g_in = jnp.exp(cum)
        g_inv = jnp.exp(-cum)
        g_last = g_in[CHUNK - 1:CHUNK, :]
        rt = r_ref[rows, :] * g_in
        at = -kk_ref[rows, :] * jnp.exp(cum - lw)
        bt = bb_ref[rows, :] * g_inv
        kt = k_ref[rows, :] * g_inv
        v = v_ref[rows, :]
        pairs = range(n_pairs)
        sls = [slice(p * LANES, (p + 1) * LANES) for p in pairs]
        ar = [jnp.concatenate([stack(at[:, sl]), stack(rt[:, sl])], axis=0).astype(BF16) for sl in sls]
        bk = [jnp.concatenate([stack(bt[:, sl]), stack(kt[:, sl])], axis=0) for sl in sls]
        v_b = [stack(v[:, sl]).astype(BF16) for sl in sls]
        gram = [_nt(ar[p], bk[p].astype(BF16)) for p in pairs]
        a_ab = [jnp.where(strict, gram[p][:c2, :c2], 0.0) for p in pairs]
        a_k = [jnp.concatenate([jnp.where(strict, gram[p][:c2, c2:], 0.0),
                                jnp.where(incl, gram[p][c2:, c2:], 0.0)], axis=0).astype(BF16) for p in pairs]
        a_rb = [jnp.where(incl, gram[p][c2:, :c2], 0.0).astype(BF16) for p in pairs]
        xs = [_nt(ar[p], s_scr[p].astype(BF16)) + _mm(a_k[p], v_b[p]) for p in pairs]
        inv = [eye + a_ab[p] for p in pairs]
        pw = [a_ab[p].astype(BF16) for p in pairs]
        for _ in range(CHUNK.bit_length() - 2):
            pw = [_mm(pw[p], pw[p]).astype(BF16) for p in pairs]
            inv = [inv[p] + _mm(inv[p].astype(BF16), pw[p]) for p in pairs]
        u_b = [_mm(inv[p].astype(BF16), xs[p][:c2].astype(BF16)).astype(BF16) for p in pairs]
        for p in pairs:
            y = xs[p][c2:] + _mm(a_rb[p], u_b[p])
            y_ref[rows, sls[p]] = y[:CHUNK] + y[CHUNK:]
        for p in pairs:
            gl = g_last[:, sls[p]]
            uv = jnp.concatenate([u_b[p], v_b[p]], axis=0)
            s_scr[p] = s_scr[p] * gl + _tn(uv, (bk[p] * gl).astype(BF16))
        return carry

    lax.fori_loop(0, tc // CHUNK, chunk, 0)

    @pl.when(pl.program_id(1) == pl.num_programs(1) - 1)
    def _():
        for hd in range(RWKV_H):
            p, o = divmod(hd, 2)
            so_ref[hd] = s_scr[p][o * RWKV_HEAD:(o + 1) * RWKV_HEAD, o * RWKV_HEAD:(o + 1) * RWKV_HEAD]


def _rwkv_rec_prompt(r, lw, k, v, kk, bb, tc=256):
    b, l, _ = r.shape
    tc = _row_tile(l, tc)
    tile = pl.BlockSpec((None, tc, D_MODEL), lambda i, j: (i, j, 0))
    st = (RWKV_H, RWKV_HEAD, RWKV_HEAD)
    return pl.pallas_call(
        _rwkv_rec_prompt_kernel,
        grid=(b, l // tc),
        in_specs=[tile] * 6,
        out_specs=[tile, pl.BlockSpec((None,) + st, lambda i, j: (i, 0, 0, 0))],
        out_shape=[jax.ShapeDtypeStruct((b, l, D_MODEL), F32), jax.ShapeDtypeStruct((b,) + st, F32)],
        scratch_shapes=[pltpu.VMEM((D_MODEL // LANES, LANES, LANES), F32)],
        compiler_params=_params(("parallel", "arbitrary")),
        name="rwkv_rec_prompt",
    )(r, lw, k, v, kk, bb)


def _rwkv_rec_decode_kernel(r_ref, lw_ref, k_ref, v_ref, kk_ref, bb_ref, s_ref, y_ref, so_ref):
    eye = (_iota((RWKV_HEAD, RWKV_HEAD), 0) == _iota((RWKV_HEAD, RWKV_HEAD), 1)).astype(F32)[None]
    for hd in range(RWKV_H):
        hs = slice(hd, hd + 1)
        s0 = s_ref[:, hd]
        a = -kk_ref[:, hs, :]
        sa = jnp.sum(s0 * a, axis=-1, keepdims=True)
        v_col = jnp.sum(eye * v_ref[:, hs, :], axis=-1, keepdims=True)
        s1 = s0 * jnp.exp(lw_ref[:, hs, :]) + sa * bb_ref[:, hs, :] + v_col * k_ref[:, hs, :]
        so_ref[:, hd] = s1
        y_col = jnp.sum(s1 * r_ref[:, hs, :], axis=-1, keepdims=True)
        y_ref[:, hs, :] = jnp.sum(eye * y_col, axis=1, keepdims=True)


def _rwkv_rec_decode(r, lw, k, v, kk, bb, s, bblk=8):
    m = r.shape[0]
    vec = pl.BlockSpec((bblk, RWKV_H, RWKV_HEAD), lambda i: (i, 0, 0))
    st = pl.BlockSpec((bblk, RWKV_H, RWKV_HEAD, RWKV_HEAD), lambda i: (i, 0, 0, 0))
    hv = lambda x: x.reshape(m, RWKV_H, RWKV_HEAD)
    y, s_new = pl.pallas_call(
        _rwkv_rec_decode_kernel,
        grid=(m // bblk,),
        in_specs=[vec] * 6 + [st],
        out_specs=[vec, st],
        out_shape=[jax.ShapeDtypeStruct((m, RWKV_H, RWKV_HEAD), F32), jax.ShapeDtypeStruct(s.shape, F32)],
        compiler_params=_params(("parallel",)),
        name="rwkv_rec_decode",
    )(hv(r), hv(lw), hv(k), hv(v), hv(kk), hv(bb), s)
    return y.reshape(m, D_MODEL), s_new


def _rwkv_post_kernel(h_ref, y_ref, bo_ref, g_ref, lnw_ref, lnb_ref, wo_ref, o_ref):
    y = y_ref[...]
    d = y - _group_sum(y, RWKV_HEAD) * (1.0 / RWKV_HEAD)
    var = _group_sum(d * d, RWKV_HEAD) * (1.0 / RWKV_HEAD)
    yn = d * lax.rsqrt(var + RWKV_GN_EPS) * lnw_ref[...] + lnb_ref[...]
    o_ref[...] = h_ref[...] + _bdot((yn + bo_ref[...]) * g_ref[...], wo_ref[...])


def _rwkv_post(h, y, bonus, g, p, tm=512):
    m = h.shape[0]
    tm = _row_tile(m, tm)
    row = pl.BlockSpec((tm, D_MODEL), lambda i: (i, 0))
    vec = _const_spec((1, D_MODEL))
    return pl.pallas_call(
        _rwkv_post_kernel,
        grid=(m // tm,),
        in_specs=[row] * 4 + [vec, vec, _const_spec((D_MODEL, D_MODEL))],
        out_specs=row,
        out_shape=jax.ShapeDtypeStruct((m, D_MODEL), F32),
        compiler_params=_params(("parallel",)),
        name="rwkv_post",
    )(h, y, bonus, g, p["ln_w"].reshape(1, D_MODEL), p["ln_b"].reshape(1, D_MODEL), p["wo"].astype(BF16))


def _gla_pre_kernel(h_ref, nm_ref, win_ref, wgl_ref, wgk_ref, bgk_ref, q_ref, k_ref, v_ref, g_ref, gk_ref):
    u = _rms(h_ref[...], nm_ref[...], NORM_EPS).astype(BF16)
    z = jnp.dot(u, win_ref[...], preferred_element_type=F32)
    q_ref[...] = z[:, :GLA_DK_TOT] * (GLA_DK ** -0.5)
    k_ref[...] = z[:, GLA_DK_TOT:2 * GLA_DK_TOT]
    v_ref[...] = z[:, 2 * GLA_DK_TOT:2 * GLA_DK_TOT + GLA_DV_TOT]
    g_ref[...] = z[:, 2 * GLA_DK_TOT + GLA_DV_TOT:]
    gl = jnp.dot(u, wgl_ref[...], preferred_element_type=F32)
    pre = _bdot(gl, wgk_ref[...]) + bgk_ref[...]
    gk_ref[...] = -_softplus(-pre) * (1.0 / GLA_NORMALIZER)


def _gla_pre(h, nm, p, tm=512):
    m = h.shape[0]
    tm = _row_tile(m, tm)
    n_main = 2 * GLA_DK_TOT + 2 * GLA_DV_TOT
    w_in = p["w_in"].astype(BF16)
    w_main = w_in[:, :n_main]
    w_gl = jnp.pad(w_in[:, n_main:], ((0, 0), (0, LANES - GLA_LR)))
    w_gk = jnp.pad(p["w_gk2"].astype(BF16), ((0, LANES - GLA_LR), (0, 0)))
    row = lambda n: pl.BlockSpec((tm, n), lambda i: (i, 0))
    widths = (GLA_DK_TOT, GLA_DK_TOT, GLA_DV_TOT, GLA_DV_TOT, GLA_DK_TOT)
    return pl.pallas_call(
        _gla_pre_kernel,
        grid=(m // tm,),
        in_specs=[row(D_MODEL), _const_spec((1, D_MODEL)), _const_spec(w_main.shape), _const_spec(w_gl.shape),
                  _const_spec(w_gk.shape), _const_spec((1, GLA_DK_TOT))],
        out_specs=[row(n) for n in widths],
        out_shape=[jax.ShapeDtypeStruct((m, n), F32) for n in widths],
        compiler_params=_params(("parallel",)),
        name="gla_pre",
    )(h, nm.reshape(1, D_MODEL), w_main, w_gl, w_gk, p["b_gk"].reshape(1, GLA_DK_TOT))


def _gla_rec_prompt_kernel(q_ref, k_ref, v_ref, gk_ref, o_ref, so_ref, s_scr):
    tc = q_ref.shape[0]

    @pl.when(pl.program_id(1) == 0)
    def _():
        s_scr[...] = jnp.zeros_like(s_scr)

    ri, ci = _iota((CHUNK, CHUNK), 0), _iota((CHUNK, CHUNK), 1)
    incl = ri >= ci
    ltri = incl.astype(F32)

    def chunk(c, carry):
        rows = pl.ds(pl.multiple_of(c * CHUNK, CHUNK), CHUNK)
        cum = _mm(ltri, gk_ref[rows, :], HI)
        qe = (q_ref[rows, :] * jnp.exp(cum)).astype(BF16)
        ke = k_ref[rows, :] * jnp.exp(-cum)
        g_last = jnp.exp(cum[CHUNK - 1:CHUNK, :])
        kh = (ke * g_last).astype(BF16)
        ke = ke.astype(BF16)
        v = v_ref[rows, :].astype(BF16)
        heads = range(GLA_H)
        ks = [slice(hd * GLA_DK, (hd + 1) * GLA_DK) for hd in heads]
        vs = [slice(hd * GLA_DV, (hd + 1) * GLA_DV) for hd in heads]
        att = [jnp.where(incl, _nt(qe[:, ks[hd]], ke[:, ks[hd]]), 0.0).astype(BF16) for hd in heads]
        o_inter = [_nt(qe[:, ks[hd]], s_scr[hd].astype(BF16)) for hd in heads]
        for hd in heads:
            o_ref[rows, vs[hd]] = o_inter[hd] + _mm(att[hd], v[:, vs[hd]])
        for hd in heads:
            s_scr[hd] = s_scr[hd] * g_last[:, ks[hd]] + _tn(v[:, vs[hd]], kh[:, ks[hd]])
        return carry

    lax.fori_loop(0, tc // CHUNK, chunk, 0)

    @pl.when(pl.program_id(1) == pl.num_programs(1) - 1)
    def _():
        for hd in range(GLA_H):
            so_ref[hd] = s_scr[hd].T


def _gla_rec_prompt(q, k, v, gk, tc=256):
    b, l, _ = q.shape
    tc = _row_tile(l, tc)
    tile = lambda n: pl.BlockSpec((None, tc, n), lambda i, j: (i, j, 0))
    st = (GLA_H, GLA_DK, GLA_DV)
    return pl.pallas_call(
        _gla_rec_prompt_kernel,
        grid=(b, l // tc),
        in_specs=[tile(GLA_DK_TOT), tile(GLA_DK_TOT), tile(GLA_DV_TOT), tile(GLA_DK_TOT)],
        out_specs=[tile(GLA_DV_TOT), pl.BlockSpec((None,) + st, lambda i, j: (i, 0, 0, 0))],
        out_shape=[jax.ShapeDtypeStruct((b, l, GLA_DV_TOT), F32), jax.ShapeDtypeStruct((b,) + st, F32)],
        scratch_shapes=[pltpu.VMEM((GLA_H, GLA_DV, GLA_DK), F32)],
        compiler_params=_params(("parallel", "arbitrary")),
        name="gla_rec_prompt",
    )(q, k, v, gk)


def _gla_rec_decode_kernel(q_ref, k_ref, v_ref, gk_ref, s_ref, o_ref, so_ref):
    eye = (_iota((GLA_DK, GLA_DK), 0) == _iota((GLA_DK, GLA_DK), 1)).astype(F32)[None]

    def col(x):
        return jnp.sum(eye * x, axis=-1, keepdims=True)

    for hd in range(GLA_H):
        hs = slice(hd, hd + 1)
        s1 = s_ref[:, hd] * col(jnp.exp(gk_ref[:, hs, :])) + col(k_ref[:, hs, :]) * v_ref[:, hs, :]
        so_ref[:, hd] = s1
        o_ref[:, hs, :] = jnp.sum(col(q_ref[:, hs, :]) * s1, axis=1, keepdims=True)


def _gla_rec_decode(q, k, v, gk, s, bblk=8):
    m = q.shape[0]
    kvec = pl.BlockSpec((bblk, GLA_H, GLA_DK), lambda i: (i, 0, 0))
    vvec = pl.BlockSpec((bblk, GLA_H, GLA_DV), lambda i: (i, 0, 0))
    st = pl.BlockSpec((bblk, GLA_H, GLA_DK, GLA_DV), lambda i: (i, 0, 0, 0))
    hk = lambda x: x.reshape(m, GLA_H, GLA_DK)
    o, s_new = pl.pallas_call(
        _gla_rec_decode_kernel,
        grid=(m // bblk,),
        in_specs=[kvec, kvec, vvec, kvec, st],
        out_specs=[vvec, st],
        out_shape=[jax.ShapeDtypeStruct((m, GLA_H, GLA_DV), F32), jax.ShapeDtypeStruct(s.shape, F32)],
        compiler_params=_params(("parallel",)),
        name="gla_rec_decode",
    )(hk(q), hk(k), v.reshape(m, GLA_H, GLA_DV), hk(gk), s)
    return o.reshape(m, GLA_DV_TOT), s_new


def _gla_post_kernel(h_ref, o_ref_in, g_ref, nw_ref, wo_ref, out_ref):
    o = o_ref_in[...]
    parts = []
    for hd in range(GLA_H):
        oh = o[:, hd * GLA_DV:(hd + 1) * GLA_DV]
        parts.append(oh * lax.rsqrt(jnp.mean(oh * oh, axis=-1, keepdims=True) + GLA_NORM_EPS))
    on = jnp.concatenate(parts, axis=1) * nw_ref[...]
    g = g_ref[...]
    out_ref[...] = h_ref[...] + _bdot(on * (g * jax.nn.sigmoid(g)), wo_ref[...])


def _gla_post(h, o, g, p, tm=512):
    m = h.shape[0]
    tm = _row_tile(m, tm)
    row = pl.BlockSpec((tm, D_MODEL), lambda i: (i, 0))
    return pl.pallas_call(
        _gla_post_kernel,
        grid=(m // tm,),
        in_specs=[row] * 3 + [_const_spec((1, D_MODEL)), _const_spec((D_MODEL, D_MODEL))],
        out_specs=row,
        out_shape=jax.ShapeDtypeStruct((m, D_MODEL), F32),
        compiler_params=_params(("parallel",)),
        name="gla_post",
    )(h, o, g, jnp.tile(p["norm"], GLA_H).reshape(1, D_MODEL), p["wo"].astype(BF16))


def _conv_gates(h_ref, nm_ref, win_ref):
    u = _rms(h_ref[...], nm_ref[...], NORM_EPS).astype(BF16)
    z = jnp.dot(u, win_ref[...], preferred_element_type=F32)
    return z[:, :D_MODEL], z[:, D_MODEL:2 * D_MODEL] * z[:, 2 * D_MODEL:]


def _conv_prompt_kernel(h_ref, nm_ref, win_ref, cw_ref, wo_ref, o_ref, st_ref, carry_ref):
    tm = h_ref.shape[0]

    @pl.when(pl.program_id(1) == 0)
    def _():
        carry_ref[...] = jnp.zeros_like(carry_ref)

    g_b, zc = _conv_gates(h_ref, nm_ref, win_ref)
    row = _iota(zc.shape, 0)
    z1 = jnp.where(row == 0, carry_ref[1:2, :], pltpu.roll(zc, 1, axis=0))
    z2 = jnp.where(row == 0, carry_ref[0:1, :], jnp.where(row == 1, carry_ref[1:2, :], pltpu.roll(zc, 2, axis=0)))
    conv = cw_ref[0:1, :] * z2 + cw_ref[1:2, :] * z1 + cw_ref[2:3, :] * zc
    o_ref[...] = h_ref[...] + _bdot(g_b * conv, wo_ref[...])
    tail = zc[tm - (CONV_W - 1):, :]
    carry_ref[0:CONV_W - 1, :] = tail
    st_ref[...] = tail


def _conv_prompt(h, nm, p, tm=256):
    b, l, _ = h.shape
    tm = _row_tile(l, tm)
    tile = pl.BlockSpec((None, tm, D_MODEL), lambda i, j: (i, j, 0))
    return pl.pallas_call(
        _conv_prompt_kernel,
        grid=(b, l // tm),
        in_specs=[tile, _const_spec((1, D_MODEL)), _const_spec((D_MODEL, 3 * D_MODEL)),
                  _const_spec((CONV_W, D_MODEL)), _const_spec((D_MODEL, D_MODEL))],
        out_specs=[tile, pl.BlockSpec((None, CONV_W - 1, D_MODEL), lambda i, j: (i, 0, 0))],
        out_shape=[jax.ShapeDtypeStruct((b, l, D_MODEL), F32),
                   jax.ShapeDtypeStruct((b, CONV_W - 1, D_MODEL), F32)],
        scratch_shapes=[pltpu.VMEM((SUBLANES, D_MODEL), F32)],
        compiler_params=_params(("parallel", "arbitrary")),
        name="conv_prompt",
    )(h, nm.reshape(1, D_MODEL), p["w_in"].astype(BF16), p["w"], p["wo"].astype(BF16))


def _conv_decode_kernel(h_ref, buf_ref, nm_ref, win_ref, cw_ref, wo_ref, o_ref, st_ref):
    g_b, zc = _conv_gates(h_ref, nm_ref, win_ref)
    z2 = buf_ref[:, 0, :]
    z1 = buf_ref[:, 1, :]
    conv = cw_ref[0:1, :] * z2 + cw_ref[1:2, :] * z1 + cw_ref[2:3, :] * zc
    o_ref[...] = h_ref[...] + _bdot(g_b * conv, wo_ref[...])
    st_ref[:, 0, :] = z1
    st_ref[:, 1, :] = zc


def _conv_decode(h, buf, nm, p):
    m = h.shape[0]
    full = pl.BlockSpec((m, D_MODEL), lambda i: (0, 0))
    bufs = pl.BlockSpec(buf.shape, lambda i: (0, 0, 0))
    return pl.pallas_call(
        _conv_decode_kernel,
        grid=(1,),
        in_specs=[full, bufs, _const_spec((1, D_MODEL)), _const_spec((D_MODEL, 3 * D_MODEL)),
                  _const_spec((CONV_W, D_MODEL)), _const_spec((D_MODEL, D_MODEL))],
        out_specs=[full, bufs],
        out_shape=[jax.ShapeDtypeStruct((m, D_MODEL), F32), jax.ShapeDtypeStruct(buf.shape, F32)],
        compiler_params=_params(("arbitrary",)),
        name="conv_decode",
    )(h, buf, nm.reshape(1, D_MODEL), p["w_in"].astype(BF16), p["w"], p["wo"].astype(BF16))


def _pool_project(d_groups, pw_ref, sc_ref):
    ys = [_bdot(d, pw_ref[gi]) for gi, d in enumerate(d_groups)]
    return jnp.concatenate(ys, axis=1) * sc_ref[...]


def _pool_prompt_kernel(h_ref, nm_ref, pw_ref, sc_ref, o_ref, st_ref, carry_ref):
    tm = h_ref.shape[0]
    hist = carry_ref.shape[0]
    j = pl.program_id(1)

    @pl.when(j == 0)
    def _():
        carry_ref[...] = jnp.zeros_like(carry_ref)

    u = _rms(h_ref[...], nm_ref[...], NORM_EPS)
    ext = jnp.concatenate([carry_ref[...], u], axis=0)
    pos = j * tm + _iota((tm, 1), 0)
    d_groups = []
    for gi, w in enumerate(POOL_WINDOWS):
        sl = slice(gi * POOL_G, (gi + 1) * POOL_G)
        s = ext[:, sl]
        span = 1
        while span < w:
            s = s + pltpu.roll(s, span, axis=0)
            span *= 2
        cnt = jnp.minimum(w, pos + 1).astype(F32)
        d_groups.append(s[hist:, :] / cnt - u[:, sl])
    o_ref[...] = h_ref[...] + _pool_project(d_groups, pw_ref, sc_ref)
    tail = u[tm - hist:, :]
    carry_ref[...] = tail
    st_ref[...] = tail


def _pool_prompt(h, nm, p, tm=256):
    b, l, _ = h.shape
    tm = _row_tile(l, tm)
    hist = POOL_BUF + 1
    tile = pl.BlockSpec((None, tm, D_MODEL), lambda i, j: (i, j, 0))
    out, st = pl.pallas_call(
        _pool_prompt_kernel,
        grid=(b, l // tm),
        in_specs=[tile, _const_spec((1, D_MODEL)), _const_spec(p["w"].shape), _const_spec((1, D_MODEL))],
        out_specs=[tile, pl.BlockSpec((None, hist, D_MODEL), lambda i, j: (i, 0, 0))],
        out_shape=[jax.ShapeDtypeStruct((b, l, D_MODEL), F32), jax.ShapeDtypeStruct((b, hist, D_MODEL), F32)],
        scratch_shapes=[pltpu.VMEM((hist, D_MODEL), F32)],
        compiler_params=_params(("parallel", "arbitrary")),
        name="pool_prompt",
    )(h, nm.reshape(1, D_MODEL), p["w"].astype(BF16), p["scale"].reshape(1, D_MODEL))
    return out, st[:, 1:]


def _pool_decode_kernel(h_ref, buf_ref, nm_ref, pw_ref, sc_ref, o_ref, st_ref):
    u = _rms(h_ref[...], nm_ref[...], NORM_EPS)
    d_groups = []
    for gi, w in enumerate(POOL_WINDOWS):
        sl = slice(gi * POOL_G, (gi + 1) * POOL_G)
        s = u[:, sl]
        for i in range(1, w):
            s = s + buf_ref[:, POOL_BUF - i, sl]
        cnt = float(min(w, PAST_LEN + 1))
        d_groups.append(s / cnt - u[:, sl])
    o_ref[...] = h_ref[...] + _pool_project(d_groups, pw_ref, sc_ref)
    st_ref[:, 0:POOL_BUF - 1, :] = buf_ref[:, 1:POOL_BUF, :]
    st_ref[:, POOL_BUF - 1, :] = u


def _pool_decode(h, buf, nm, p):
    m = h.shape[0]
    full = pl.BlockSpec((m, D_MODEL), lambda i: (0, 0))
    bufs = pl.BlockSpec(buf.shape, lambda i: (0, 0, 0))
    return pl.pallas_call(
        _pool_decode_kernel,
        grid=(1,),
        in_specs=[full, bufs, _const_spec((1, D_MODEL)), _const_spec(p["w"].shape), _const_spec((1, D_MODEL))],
        out_specs=[full, bufs],
        out_shape=[jax.ShapeDtypeStruct((m, D_MODEL), F32), jax.ShapeDtypeStruct(buf.shape, F32)],
        compiler_params=_params(("arbitrary",)),
        name="pool_decode",
    )(h, buf, nm.reshape(1, D_MODEL), p["w"].astype(BF16), p["scale"].reshape(1, D_MODEL))


def _trunk_prompt(x, nm, nf, nfin, wu, wd, rw, gl, cv, po):
    b, l, _ = x.shape
    flat = lambda t: t.reshape(b * l, t.shape[-1])
    seq = lambda t: t.reshape(b, l, t.shape[-1])

    (r, lw, k, v, kk, bb, g, bonus), shift = _rwkv_pre_prompt(x, nm[0], rw)
    y, wkv = _rwkv_rec_prompt(r, lw, k, v, kk, bb)
    h = _rwkv_post(flat(x), flat(y), flat(bonus), flat(g), rw)
    h = _ffn(h, nf[0], wu[0], wd[0])

    q, k, v, g, gk = _gla_pre(h, nm[1], gl)
    o, gla_s = _gla_rec_prompt(seq(q), seq(k), seq(v), seq(gk))
    h = _gla_post(h, flat(o), g, gl)
    h = _ffn(h, nf[1], wu[1], wd[1])

    h, conv_s = _conv_prompt(seq(h), nm[2], cv)
    h = _ffn(flat(h), nf[2], wu[2], wd[2])

    h, pool_s = _pool_prompt(seq(h), nm[3], po)
    y = _ffn(flat(h), nf[3], wu[3], wd[3], final_g=nfin)
    return seq(y), wkv[None], shift[None], gla_s[None], conv_s[None], pool_s[None]


def _trunk_decode(x, st_wkv, st_shift, st_gla, st_conv, st_pool, nm, nf, nfin, wu, wd, rw, gl, cv, po):
    m = x.shape[0]
    h = x.reshape(m, D_MODEL)

    (r, lw, k, v, kk, bb, g, bonus), shift = _rwkv_pre_decode(h, st_shift[0], nm[0], rw)
    y, wkv = _rwkv_rec_decode(r, lw, k, v, kk, bb, st_wkv[0])
    h = _rwkv_post(h, y, bonus, g, rw)
    h = _ffn(h, nf[0], wu[0], wd[0])

    q, k, v, g, gk = _gla_pre(h, nm[1], gl)
    o, gla_s = _gla_rec_decode(q, k, v, gk, st_gla[0])
    h = _gla_post(h, o, g, gl)
    h = _ffn(h, nf[1], wu[1], wd[1])

    h, conv_s = _conv_decode(h, st_conv[0], nm[2], cv)
    h = _ffn(h, nf[2], wu[2], wd[2])

    h, pool_s = _pool_decode(h, st_pool[0], nm[3], po)
    y = _ffn(h, nf[3], wu[3], wd[3], final_g=nfin)
    return y.reshape(m, 1, D_MODEL), wkv[None], shift[None], gla_s[None], conv_s[None], pool_s[None]


def kernel(x_prompt, x_sample, state_rwkv_wkv, state_rwkv_shift, state_gla, state_conv, state_pool, norm_mix, norm_ffn, norm_final, ffn_up, ffn_down, rwkv_mu, rwkv_w_rkv, rwkv_w0, rwkv_w1, rwkv_w2, rwkv_a0, rwkv_a1, rwkv_a2, rwkv_g1, rwkv_g2, rwkv_k_k, rwkv_k_a, rwkv_r_k, rwkv_ln_w, rwkv_ln_b, rwkv_wo, gla_w_in, gla_w_gk2, gla_b_gk, gla_norm, gla_wo, conv_w_in, conv_w, conv_wo, pool_w, pool_scale):
    assert x_prompt.shape[1] % CHUNK == 0 and x_sample.shape[1] == 1
    wu = ffn_up.astype(BF16)
    wd = ffn_down.astype(BF16)
    rw = dict(mu=rwkv_mu[0], w_rkv=rwkv_w_rkv[0], w0=rwkv_w0[0], w1=rwkv_w1[0], w2=rwkv_w2[0], a0=rwkv_a0[0],
              a1=rwkv_a1[0], a2=rwkv_a2[0], g1=rwkv_g1[0], g2=rwkv_g2[0], k_k=rwkv_k_k[0], k_a=rwkv_k_a[0],
              r_k=rwkv_r_k[0], ln_w=rwkv_ln_w[0], ln_b=rwkv_ln_b[0], wo=rwkv_wo[0])
    gl = dict(w_in=gla_w_in[0], w_gk2=gla_w_gk2[0], b_gk=gla_b_gk[0], norm=gla_norm[0], wo=gla_wo[0])
    cv = dict(w_in=conv_w_in[0], w=conv_w[0], wo=conv_wo[0])
    po = dict(w=pool_w[0], scale=pool_scale[0])
    shared = (norm_mix, norm_ffn, norm_final, wu, wd, rw, gl, cv, po)
    y_p, wkv_p, sh_p, gla_p, conv_p, pool_p = _trunk_prompt(x_prompt, *shared)
    y_s, wkv_s, sh_s, gla_s, conv_s, pool_s = _trunk_decode(
        x_sample, state_rwkv_wkv, state_rwkv_shift, state_gla, state_conv, state_pool, *shared)
    return (y_p, y_s, wkv_p, wkv_s, sh_p, sh_s, gla_p, gla_s, conv_p, conv_s, pool_p, pool_s)
```

```python
import functools

import jax
import jax.numpy as jnp
from jax import lax
from jax.experimental import pallas as pl
from jax.experimental.pallas import tpu as pltpu

F32 = jnp.float32
BF16 = jnp.bfloat16

D_MODEL = 1024
D_FF = 4 * D_MODEL
NORM_EPS = 1e-6
PAST_LEN = 16384

RWKV_HEAD = 64
RWKV_H = D_MODEL // RWKV_HEAD
RWKV_GN_EPS = 64e-5
RWKV_QUAD = 4

GLA_H = 4
GLA_DK_TOT = D_MODEL // 2
GLA_DV_TOT = D_MODEL
GLA_DK = GLA_DK_TOT // GLA_H
GLA_DV = GLA_DV_TOT // GLA_H
GLA_LR = 16
GLA_NORMALIZER = 16.0
GLA_NORM_EPS = 1e-5
GLA_SUB = 16

CONV_W = 3
POOL_WINDOWS = (2, 4, 8, 16)
POOL_G = D_MODEL // len(POOL_WINDOWS)
POOL_BUF = max(POOL_WINDOWS) - 1

LANES = 128
SUBLANES = 8
VMEM_LIMIT_BYTES = 56 * 1024 * 1024

CHUNK = 64
FFN_COLS = 1024
HI = lax.Precision.HIGHEST


def _const_spec(shape):
    nd = len(shape)
    return pl.BlockSpec(shape, lambda *_: (0,) * nd, pipeline_mode=pl.Buffered(1))


def _params(sem):
    return pltpu.CompilerParams(dimension_semantics=sem, vmem_limit_bytes=VMEM_LIMIT_BYTES)


def _rms(x, g, eps):
    return x * lax.rsqrt(jnp.mean(x * x, axis=-1, keepdims=True) + eps) * g


def _bdot(a, w):
    return jnp.dot(a.astype(BF16), w, preferred_element_type=F32)


def _mm(a, b, prec=None):
    return jnp.dot(a, b, preferred_element_type=F32, precision=prec)


def _nt(a, b, prec=None):
    return lax.dot_general(a, b, (((1,), (1,)), ((), ())), preferred_element_type=F32, precision=prec)


def _tn(a, b, prec=None):
    return lax.dot_general(a, b, (((0,), (0,)), ((), ())), preferred_element_type=F32, precision=prec)


def _iota(shape, dim):
    return lax.broadcasted_iota(jnp.int32, shape, dim)


def _group_sum(x, group):
    shift = group.bit_length() - 1
    bd = (jnp.right_shift(_iota((LANES, LANES), 0), shift)
          == jnp.right_shift(_iota((LANES, LANES), 1), shift)).astype(BF16)
    hi = x.astype(BF16)
    lo = (x - hi.astype(F32)).astype(BF16)
    cols = []
    for c in range(x.shape[1] // LANES):
        sl = slice(c * LANES, (c + 1) * LANES)
        cols.append(_mm(hi[:, sl], bd) + _mm(lo[:, sl], bd))
    return jnp.concatenate(cols, axis=1)


def _softplus(x):
    return jnp.maximum(x, 0.0) + jnp.log(1.0 + jnp.exp(-jnp.abs(x)))


def _row_tile(m, want):
    t = min(want, m)
    assert m % t == 0, (m, t)
    return t


def _ffn_kernel(h_ref, g_ref, wu_ref, wd_ref, *rest, final):
    if final:
        gf_ref, o_ref = rest
    else:
        (o_ref,) = rest
    h = h_ref[...]
    x = _rms(h, g_ref[...], NORM_EPS).astype(BF16)
    acc = h
    for c in range(D_FF // FFN_COLS):
        sl = slice(c * FFN_COLS, (c + 1) * FFN_COLS)
        a = jnp.dot(x, wu_ref[:, sl], preferred_element_type=F32)
        a = jnp.square(jnp.maximum(a, 0.0)).astype(BF16)
        acc = acc + jnp.dot(a, wd_ref[sl, :], preferred_element_type=F32)
    if final:
        acc = _rms(acc, gf_ref[...], NORM_EPS)
    o_ref[...] = acc


def _ffn(h, g, wu, wd, layer, final_g=None, tm=512):
    m = h.shape[0]
    tm = _row_tile(m, tm)
    row = pl.BlockSpec((tm, D_MODEL), lambda i: (i, 0))
    layer_spec = lambda shape: pl.BlockSpec((None,) + shape[1:], lambda i: (layer, 0, 0),
                                            pipeline_mode=pl.Buffered(1))
    ins = [h, g.reshape(1, D_MODEL), wu, wd]
    specs = [row, _const_spec((1, D_MODEL)), layer_spec(wu.shape), layer_spec(wd.shape)]
    if final_g is not None:
        ins.append(final_g.reshape(1, D_MODEL))
        specs.append(_const_spec((1, D_MODEL)))
    return pl.pallas_call(
        functools.partial(_ffn_kernel, final=final_g is not None),
        grid=(m // tm,),
        in_specs=specs,
        out_specs=row,
        out_shape=jax.ShapeDtypeStruct((m, D_MODEL), F32),
        compiler_params=_params(("parallel",)),
        name="ffn",
    )(*ins)


def _rwkv_pre_math(u, prev, mu_ref, wrkv_ref, w0_ref, w1_ref, w2_ref, a0_ref, a1_ref, a2_ref,
                   g1_ref, g2_ref, kk_ref, ka_ref, rk_ref):
    dx = prev - u

    def mix(i):
        return (u + dx * mu_ref[i:i + 1, :]).astype(BF16)

    r = _bdot(mix(0), wrkv_ref[0])
    k = _bdot(mix(2), wrkv_ref[1])
    v = _bdot(mix(3), wrkv_ref[2])
    wl = w0_ref[...] + _bdot(jnp.tanh(_bdot(mix(1), w1_ref[...])), w2_ref[...])
    lw = -jnp.exp(-_softplus(-wl) - 0.5)
    a = jax.nn.sigmoid(a0_ref[...] + _bdot(_bdot(mix(4), a1_ref[...]), a2_ref[...]))
    g = _bdot(jax.nn.sigmoid(_bdot(mix(5), g1_ref[...])), g2_ref[...])
    kk = k * kk_ref[...]
    kk = kk / jnp.maximum(jnp.sqrt(_group_sum(kk * kk, RWKV_HEAD)), 1e-12)
    k = k * (1.0 + (a - 1.0) * ka_ref[...])
    bonus = _group_sum(r * k * rk_ref[...], RWKV_HEAD) * v
    return r, lw, k, v, kk, kk * a, g, bonus


_N_RWKV_W = 13


def _rwkv_pre_prompt_kernel(h_ref, nm_ref, *rest):
    w = rest[:_N_RWKV_W]
    outs = rest[_N_RWKV_W:_N_RWKV_W + 8]
    sh_ref, carry_ref = rest[_N_RWKV_W + 8:]
    tm = h_ref.shape[0]

    @pl.when(pl.program_id(1) == 0)
    def _():
        carry_ref[...] = jnp.zeros_like(carry_ref)

    u = _rms(h_ref[...], nm_ref[...], NORM_EPS)
    prev = jnp.where(_iota(u.shape, 0) == 0, carry_ref[0:1, :], pltpu.roll(u, 1, axis=0))
    last = u[tm - 1:tm, :]
    carry_ref[0:1, :] = last
    sh_ref[...] = last
    for o_ref, val in zip(outs, _rwkv_pre_math(u, prev, *w)):
        o_ref[...] = val


def _rwkv_pre_decode_kernel(h_ref, prev_ref, nm_ref, *rest):
    w = rest[:_N_RWKV_W]
    outs = rest[_N_RWKV_W:_N_RWKV_W + 8]
    (sh_ref,) = rest[_N_RWKV_W + 8:]
    u = _rms(h_ref[...], nm_ref[...], NORM_EPS)
    sh_ref[...] = u
    for o_ref, val in zip(outs, _rwkv_pre_math(u, prev_ref[...], *w)):
        o_ref[...] = val


def _rwkv_weights(p):
    row = lambda x: x.reshape(1, D_MODEL)
    return [p["mu"], p["w_rkv"].astype(BF16), row(p["w0"]), p["w1"].astype(BF16), p["w2"].astype(BF16),
            row(p["a0"]), p["a1"].astype(BF16), p["a2"].astype(BF16), p["g1"].astype(BF16),
            p["g2"].astype(BF16), row(p["k_k"]), row(p["k_a"]), row(p["r_k"])]


def _rwkv_pre_prompt(h, nm, p, tm=256):
    b, l, _ = h.shape
    tm = _row_tile(l, tm)
    ws = _rwkv_weights(p)
    tile = pl.BlockSpec((None, tm, D_MODEL), lambda i, j: (i, j, 0))
    outs = pl.pallas_call(
        _rwkv_pre_prompt_kernel,
        grid=(b, l // tm),
        in_specs=[tile, _const_spec((1, D_MODEL))] + [_const_spec(w.shape) for w in ws],
        out_specs=[tile] * 8 + [pl.BlockSpec((None, 1, D_MODEL), lambda i, j: (i, 0, 0))],
        out_shape=[jax.ShapeDtypeStruct((b, l, D_MODEL), F32)] * 8
        + [jax.ShapeDtypeStruct((b, 1, D_MODEL), F32)],
        scratch_shapes=[pltpu.VMEM((SUBLANES, D_MODEL), F32)],
        compiler_params=_params(("parallel", "arbitrary")),
        name="rwkv_pre_prompt",
    )(h, nm.reshape(1, D_MODEL), *ws)
    return outs[:8], outs[8].reshape(b, D_MODEL)


def _rwkv_pre_decode(h, prev, nm, p):
    m = h.shape[0]
    ws = _rwkv_weights(p)
    full = pl.BlockSpec((m, D_MODEL), lambda i: (0, 0))
    outs = pl.pallas_call(
        _rwkv_pre_decode_kernel,
        grid=(1,),
        in_specs=[full, full, _const_spec((1, D_MODEL))] + [_const_spec(w.shape) for w in ws],
        out_specs=[full] * 9,
        out_shape=[jax.ShapeDtypeStruct((m, D_MODEL), F32)] * 9,
        compiler_params=_params(("arbitrary",)),
        name="rwkv_pre_decode",
    )(h, prev, nm.reshape(1, D_MODEL), *ws)
    return outs[:8], outs[8]


def _rwkv_rec_prompt_kernel(r_ref, lw_ref, k_ref, v_ref, kk_ref, bb_ref, y_ref, so_ref, s_scr):
    tc = r_ref.shape[0]
    qw = RWKV_QUAD * RWKV_HEAD
    n_quads = D_MODEL // qw
    assert RWKV_QUAD * CHUNK == qw

    @pl.when(pl.program_id(1) == 0)
    def _():
        s_scr[...] = jnp.zeros_like(s_scr)

    ltri = (_iota((CHUNK, CHUNK), 0) >= _iota((CHUNK, CHUNK), 1)).astype(F32)
    t_idx, s_idx = _iota((CHUNK, qw), 0), _iota((CHUNK, qw), 1) & (CHUNK - 1)
    strict = t_idx > s_idx
    incl = t_idx >= s_idx
    eye = (t_idx == s_idx).astype(F32)
    blocks = jnp.right_shift(_iota((qw, qw), 0), 6) == jnp.right_shift(_iota((qw, qw), 1), 6)

    def bdiag(z):
        zb = z.astype(BF16)
        return jnp.where(blocks, jnp.concatenate([zb] * RWKV_QUAD, axis=0), jnp.zeros((), BF16))

    quads = range(n_quads)
    sls = [slice(q * qw, (q + 1) * qw) for q in quads]
    cc = CHUNK

    n_chunks = tc // CHUNK
    items = [(c, q) for c in range(n_chunks) for q in quads]
    every = range(len(items))

    scaled = []
    for c in range(n_chunks):
        rows = slice(c * CHUNK, (c + 1) * CHUNK)
        lw = lw_ref[rows, :]
        cum = _mm(ltri, lw, HI)
        g_in = jnp.exp(cum)
        g_inv = jnp.exp(-cum)
        g_last = g_in[CHUNK - 1:CHUNK, :]
        scaled.append(dict(
            rt=r_ref[rows, :] * g_in, at=-kk_ref[rows, :] * jnp.exp(cum - lw),
            bt=bb_ref[rows, :] * g_inv, kt=k_ref[rows, :] * g_inv, v=v_ref[rows, :], g_last=g_last))

    def part(name, i):
        c, q = items[i]
        return scaled[c][name][:, sls[q]]

    ar = [jnp.concatenate([part("at", i), part("rt", i)], axis=0).astype(BF16) for i in every]
    bk4 = [jnp.concatenate([bdiag(part("bt", i)), bdiag(part("kt", i))], axis=0) for i in every]
    gram = [_nt(ar[i], bk4[i]) for i in every]
    a_ab = [jnp.where(strict, gram[i][:cc, :qw], 0.0) for i in every]
    a_k = [jnp.concatenate([jnp.where(strict, gram[i][:cc, qw:], 0.0),
                            jnp.where(incl, gram[i][cc:, qw:], 0.0)], axis=0).astype(BF16) for i in every]
    a_rb = [jnp.where(incl, gram[i][cc:, :qw], 0.0).astype(BF16) for i in every]
    akv = [_mm(a_k[i], bdiag(part("v", i))) for i in every]
    inv = [eye + a_ab[i] for i in every]
    pw = [_mm(a_ab[i].astype(BF16), bdiag(a_ab[i])) for i in every]
    for _ in range(1, CHUNK.bit_length() - 2):
        both = [_mm(jnp.concatenate([pw[i], inv[i]], axis=0).astype(BF16), bdiag(pw[i])) for i in every]
        inv = [inv[i] + both[i][cc:] for i in every]
        pw = [both[i][:cc] for i in every]
    inv = [(inv[i] + _mm(inv[i].astype(BF16), bdiag(pw[i]))).astype(BF16) for i in every]
    tt = [_mm(inv[i], jnp.concatenate([bdiag(part("at", i)), bdiag(akv[i][:cc])], axis=1)) for i in every]
    ta = [tt[i][:, :qw].astype(BF16) for i in every]
    tav = [tt[i][:, qw:] for i in every]
    bkg = [(jnp.concatenate([part("bt", i), part("kt", i)], axis=0) * part("g_last", i)).astype(BF16)
           for i in every]
    w = [jnp.where(blocks, _tn(ta[i], bkg[i][:cc]), 0.0).astype(BF16) for i in every]
    n = [jnp.where(blocks, _tn(jnp.concatenate([tav[i], part("v", i)], axis=0).astype(BF16), bkg[i]), 0.0)
         for i in every]

    s_in = []
    state = [s_scr[q] for q in quads]
    for c in range(n_chunks):
        s_b = [state[q].astype(BF16) for q in quads]
        s_in.extend(s_b)
        state = [state[q] * scaled[c]["g_last"][:, sls[q]] + _mm(s_b[q], w[c * n_quads + q]) + n[c * n_quads + q]
                 for q in quads]
    for q in quads:
        s_scr[q] = state[q]

    xs = [_nt(jnp.concatenate([ta[i], ar[i][cc:]], axis=0), s_in[i]) for i in every]
    u = [xs[i][:cc] + tav[i] for i in every]
    for i in every:
        c, q = items[i]
        y_ref[c * CHUNK:(c + 1) * CHUNK, sls[q]] = xs[i][cc:] + akv[i][cc:] + _mm(a_rb[i], bdiag(u[i]))

    @pl.when(pl.program_id(1) == pl.num_programs(1) - 1)
    def _():
        for hd in range(RWKV_H):
            q, o = divmod(hd, RWKV_QUAD)
            so_ref[hd] = s_scr[q][o * RWKV_HEAD:(o + 1) * RWKV_HEAD, o * RWKV_HEAD:(o + 1) * RWKV_HEAD]


def _rwkv_rec_prompt(r, lw, k, v, kk, bb, tc=256):
    b, l, _ = r.shape
    tc = _row_tile(l, tc)
    tile = pl.BlockSpec((None, tc, D_MODEL), lambda i, j: (i, j, 0))
    st = (RWKV_H, RWKV_HEAD, RWKV_HEAD)
    return pl.pallas_call(
        _rwkv_rec_prompt_kernel,
        grid=(b, l // tc),
        in_specs=[tile] * 6,
        out_specs=[tile, pl.BlockSpec((None,) + st, lambda i, j: (i, 0, 0, 0))],
        out_shape=[jax.ShapeDtypeStruct((b, l, D_MODEL), F32), jax.ShapeDtypeStruct((b,) + st, F32)],
        scratch_shapes=[pltpu.VMEM((RWKV_H // RWKV_QUAD, RWKV_QUAD * RWKV_HEAD, RWKV_QUAD * RWKV_HEAD), F32)],
        compiler_params=_params(("parallel", "arbitrary")),
        name="rwkv_rec_prompt",
    )(r, lw, k, v, kk, bb)


def _rwkv_rec_decode_kernel(r_ref, lw_ref, k_ref, v_ref, kk_ref, bb_ref, s_ref, y_ref, so_ref):
    eye = (_iota((RWKV_HEAD, RWKV_HEAD), 0) == _iota((RWKV_HEAD, RWKV_HEAD), 1)).astype(F32)[None]
    for hd in range(RWKV_H):
        hs = slice(hd, hd + 1)
        s0 = s_ref[:, hd]
        a = -kk_ref[:, hs, :]
        sa = jnp.sum(s0 * a, axis=-1, keepdims=True)
        v_col = jnp.sum(eye * v_ref[:, hs, :], axis=-1, keepdims=True)
        s1 = s0 * jnp.exp(lw_ref[:, hs, :]) + sa * bb_ref[:, hs, :] + v_col * k_ref[:, hs, :]
        so_ref[:, hd] = s1
        y_col = jnp.sum(s1 * r_ref[:, hs, :], axis=-1, keepdims=True)
        y_ref[:, hs, :] = jnp.sum(eye * y_col, axis=1, keepdims=True)


def _rwkv_rec_decode(r, lw, k, v, kk, bb, s, bblk=8):
    m = r.shape[0]
    vec = pl.BlockSpec((bblk, RWKV_H, RWKV_HEAD), lambda i: (i, 0, 0))
    st = pl.BlockSpec((bblk, RWKV_H, RWKV_HEAD, RWKV_HEAD), lambda i: (i, 0, 0, 0))
    hv = lambda x: x.reshape(m, RWKV_H, RWKV_HEAD)
    y, s_new = pl.pallas_call(
        _rwkv_rec_decode_kernel,
        grid=(m // bblk,),
        in_specs=[vec] * 6 + [st],
        out_specs=[vec, st],
        out_shape=[jax.ShapeDtypeStruct((m, RWKV_H, RWKV_HEAD), F32), jax.ShapeDtypeStruct(s.shape, F32)],
        compiler_params=_params(("parallel",)),
        name="rwkv_rec_decode",
    )(hv(r), hv(lw), hv(k), hv(v), hv(kk), hv(bb), s)
    return y.reshape(m, D_MODEL), s_new


def _rwkv_post_kernel(h_ref, y_ref, bo_ref, g_ref, lnw_ref, lnb_ref, wo_ref, o_ref):
    y = y_ref[...]
    d = y - _group_sum(y, RWKV_HEAD) * (1.0 / RWKV_HEAD)
    var = _group_sum(d * d, RWKV_HEAD) * (1.0 / RWKV_HEAD)
    yn = d * lax.rsqrt(var + RWKV_GN_EPS) * lnw_ref[...] + lnb_ref[...]
    o_ref[...] = h_ref[...] + _bdot((yn + bo_ref[...]) * g_ref[...], wo_ref[...])


def _rwkv_post(h, y, bonus, g, p, tm=512):
    m = h.shape[0]
    tm = _row_tile(m, tm)
    row = pl.BlockSpec((tm, D_MODEL), lambda i: (i, 0))
    vec = _const_spec((1, D_MODEL))
    return pl.pallas_call(
        _rwkv_post_kernel,
        grid=(m // tm,),
        in_specs=[row] * 4 + [vec, vec, _const_spec((D_MODEL, D_MODEL))],
        out_specs=row,
        out_shape=jax.ShapeDtypeStruct((m, D_MODEL), F32),
        compiler_params=_params(("parallel",)),
        name="rwkv_post",
    )(h, y, bonus, g, p["ln_w"].reshape(1, D_MODEL), p["ln_b"].reshape(1, D_MODEL), p["wo"].astype(BF16))


def _gla_pre_kernel(h_ref, nm_ref, win_ref, wgl_ref, wgk_ref, bgk_ref, q_ref, k_ref, v_ref, g_ref, gk_ref):
    u = _rms(h_ref[...], nm_ref[...], NORM_EPS).astype(BF16)
    z = jnp.dot(u, win_ref[...], preferred_element_type=F32)
    q_ref[...] = z[:, :GLA_DK_TOT] * (GLA_DK ** -0.5)
    k_ref[...] = z[:, GLA_DK_TOT:2 * GLA_DK_TOT]
    v_ref[...] = z[:, 2 * GLA_DK_TOT:2 * GLA_DK_TOT + GLA_DV_TOT]
    g_ref[...] = z[:, 2 * GLA_DK_TOT + GLA_DV_TOT:]
    gl = jnp.dot(u, wgl_ref[...], preferred_element_type=F32)
    pre = _bdot(gl, wgk_ref[...]) + bgk_ref[...]
    gk_ref[...] = -_softplus(-pre) * (1.0 / GLA_NORMALIZER)


def _gla_pre(h, nm, p, tm=512):
    m = h.shape[0]
    tm = _row_tile(m, tm)
    n_main = 2 * GLA_DK_TOT + 2 * GLA_DV_TOT
    w_in = p["w_in"].astype(BF16)
    w_main = w_in[:, :n_main]
    w_gl = jnp.pad(w_in[:, n_main:], ((0, 0), (0, LANES - GLA_LR)))
    w_gk = jnp.pad(p["w_gk2"].astype(BF16), ((0, LANES - GLA_LR), (0, 0)))
    row = lambda n: pl.BlockSpec((tm, n), lambda i: (i, 0))
    widths = (GLA_DK_TOT, GLA_DK_TOT, GLA_DV_TOT, GLA_DV_TOT, GLA_DK_TOT)
    return pl.pallas_call(
        _gla_pre_kernel,
        grid=(m // tm,),
        in_specs=[row(D_MODEL), _const_spec((1, D_MODEL)), _const_spec(w_main.shape), _const_spec(w_gl.shape),
                  _const_spec(w_gk.shape), _const_spec((1, GLA_DK_TOT))],
        out_specs=[row(n) for n in widths],
        out_shape=[jax.ShapeDtypeStruct((m, n), F32) for n in widths],
        compiler_params=_params(("parallel",)),
        name="gla_pre",
    )(h, nm.reshape(1, D_MODEL), w_main, w_gl, w_gk, p["b_gk"].reshape(1, GLA_DK_TOT))


def _gla_rec_prompt_kernel(q_ref, k_ref, v_ref, gk_ref, o_ref, so_ref, s_scr):
    tc = q_ref.shape[0]

    @pl.when(pl.program_id(1) == 0)
    def _():
        s_scr[...] = jnp.zeros_like(s_scr)

    ri, ci = _iota((CHUNK, CHUNK), 0), _iota((CHUNK, CHUNK), 1)
    incl = ri >= ci
    ltri = incl.astype(F32)
    n_sub = CHUNK // GLA_SUB
    sub_of_row = jnp.right_shift(_iota((CHUNK, 1), 0), GLA_SUB.bit_length() - 1)

    heads = range(GLA_H)
    ks = [slice(hd * GLA_DK, (hd + 1) * GLA_DK) for hd in heads]
    vs = [slice(hd * GLA_DV, (hd + 1) * GLA_DV) for hd in heads]
    n_chunks = tc // CHUNK
    items = [(c, hd) for c in range(n_chunks) for hd in heads]

    prep = []
    for c in range(n_chunks):
        rows = slice(c * CHUNK, (c + 1) * CHUNK)
        cum = _mm(ltri, gk_ref[rows, :], HI)
        q, k = q_ref[rows, :], k_ref[rows, :]
        last = cum[CHUNK - 1:CHUNK, :]
        refs = [jnp.zeros_like(last)] + [cum[i * GLA_SUB - 1:i * GLA_SUB, :] for i in range(1, n_sub)]
        ref_of_row = refs[0]
        for i in range(1, n_sub):
            ref_of_row = jnp.where(sub_of_row >= i, refs[i], ref_of_row)
        q_in = (q * jnp.exp(cum - ref_of_row)).astype(BF16)
        k_in = [(k * jnp.exp(jnp.where(sub_of_row <= i, refs[i] - cum, 0.0))).astype(BF16) for i in range(n_sub)]
        prep.append(dict(q_in=q_in, k_in=k_in, qe=(q * jnp.exp(cum)).astype(BF16),
                         kh=(k * jnp.exp(last - cum)).astype(BF16), g_last=jnp.exp(last),
                         v=v_ref[rows, :].astype(BF16)))

    def scores(c, hd):
        p = prep[c]
        blocks = [_nt(p["q_in"][i * GLA_SUB:(i + 1) * GLA_SUB, ks[hd]], p["k_in"][i][:, ks[hd]])
                  for i in range(n_sub)]
        return jnp.where(incl, jnp.concatenate(blocks, axis=0), 0.0).astype(BF16)

    att = [scores(c, hd) for c, hd in items]
    intra = [_mm(att[i], prep[c]["v"][:, vs[hd]]) for i, (c, hd) in enumerate(items)]
    kv = [_tn(prep[c]["v"][:, vs[hd]], prep[c]["kh"][:, ks[hd]]) for c, hd in items]

    s_in = []
    state = [s_scr[hd] for hd in heads]
    for c in range(n_chunks):
        s_in.extend(state[hd].astype(BF16) for hd in heads)
        state = [state[hd] * prep[c]["g_last"][:, ks[hd]] + kv[c * GLA_H + hd] for hd in heads]
    for hd in heads:
        s_scr[hd] = state[hd]

    for i, (c, hd) in enumerate(items):
        o_ref[c * CHUNK:(c + 1) * CHUNK, vs[hd]] = _nt(prep[c]["qe"][:, ks[hd]], s_in[i]) + intra[i]

    @pl.when(pl.program_id(1) == pl.num_programs(1) - 1)
    def _():
        for hd in range(GLA_H):
            so_ref[hd] = s_scr[hd].T


def _gla_rec_prompt(q, k, v, gk, tc=256):
    b, l, _ = q.shape
    tc = _row_tile(l, tc)
    tile = lambda n: pl.BlockSpec((None, tc, n), lambda i, j: (i, j, 0))
    st = (GLA_H, GLA_DK, GLA_DV)
    return pl.pallas_call(
        _gla_rec_prompt_kernel,
        grid=(b, l // tc),
        in_specs=[tile(GLA_DK_TOT), tile(GLA_DK_TOT), tile(GLA_DV_TOT), tile(GLA_DK_TOT)],
        out_specs=[tile(GLA_DV_TOT), pl.BlockSpec((None,) + st, lambda i, j: (i, 0, 0, 0))],
        out_shape=[jax.ShapeDtypeStruct((b, l, GLA_DV_TOT), F32), jax.ShapeDtypeStruct((b,) + st, F32)],
        scratch_shapes=[pltpu.VMEM((GLA_H, GLA_DV, GLA_DK), F32)],
        compiler_params=_params(("parallel", "arbitrary")),
        name="gla_rec_prompt",
    )(q, k, v, gk)


def _gla_rec_decode_kernel(q_ref, k_ref, v_ref, gk_ref, s_ref, o_ref, so_ref):
    eye = (_iota((GLA_DK, GLA_DK), 0) == _iota((GLA_DK, GLA_DK), 1)).astype(F32)[None]

    def col(x):
        return jnp.sum(eye * x, axis=-1, keepdims=True)

    for hd in range(GLA_H):
        hs = slice(hd, hd + 1)
        s1 = s_ref[:, hd] * col(jnp.exp(gk_ref[:, hs, :])) + col(k_ref[:, hs, :]) * v_ref[:, hs, :]
        so_ref[:, hd] = s1
        o_ref[:, hs, :] = jnp.sum(col(q_ref[:, hs, :]) * s1, axis=1, keepdims=True)


def _gla_rec_decode(q, k, v, gk, s, bblk=8):
    m = q.shape[0]
    kvec = pl.BlockSpec((bblk, GLA_H, GLA_DK), lambda i: (i, 0, 0))
    vvec = pl.BlockSpec((bblk, GLA_H, GLA_DV), lambda i: (i, 0, 0))
    st = pl.BlockSpec((bblk, GLA_H, GLA_DK, GLA_DV), lambda i: (i, 0, 0, 0))
    hk = lambda x: x.reshape(m, GLA_H, GLA_DK)
    o, s_new = pl.pallas_call(
        _gla_rec_decode_kernel,
        grid=(m // bblk,),
        in_specs=[kvec, kvec, vvec, kvec, st],
        out_specs=[vvec, st],
        out_shape=[jax.ShapeDtypeStruct((m, GLA_H, GLA_DV), F32), jax.ShapeDtypeStruct(s.shape, F32)],
        compiler_params=_params(("parallel",)),
        name="gla_rec_decode",
    )(hk(q), hk(k), v.reshape(m, GLA_H, GLA_DV), hk(gk), s)
    return o.reshape(m, GLA_DV_TOT), s_new


def _gla_post_kernel(h_ref, o_ref_in, g_ref, nw_ref, wo_ref, out_ref):
    o = o_ref_in[...]
    parts = []
    for hd in range(GLA_H):
        oh = o[:, hd * GLA_DV:(hd + 1) * GLA_DV]
        parts.append(oh * lax.rsqrt(jnp.mean(oh * oh, axis=-1, keepdims=True) + GLA_NORM_EPS))
    on = jnp.concatenate(parts, axis=1) * nw_ref[...]
    g = g_ref[...]
    out_ref[...] = h_ref[...] + _bdot(on * (g * jax.nn.sigmoid(g)), wo_ref[...])


def _gla_post(h, o, g, p, tm=512):
    m = h.shape[0]
    tm = _row_tile(m, tm)
    row = pl.BlockSpec((tm, D_MODEL), lambda i: (i, 0))
    return pl.pallas_call(
        _gla_post_kernel,
        grid=(m // tm,),
        in_specs=[row] * 3 + [_const_spec((1, D_MODEL)), _const_spec((D_MODEL, D_MODEL))],
        out_specs=row,
        out_shape=jax.ShapeDtypeStruct((m, D_MODEL), F32),
        compiler_params=_params(("parallel",)),
        name="gla_post",
    )(h, o, g, jnp.tile(p["norm"], GLA_H).reshape(1, D_MODEL), p["wo"].astype(BF16))


def _conv_gates(h_ref, nm_ref, win_ref):
    u = _rms(h_ref[...], nm_ref[...], NORM_EPS).astype(BF16)
    z = jnp.dot(u, win_ref[...], preferred_element_type=F32)
    return z[:, :D_MODEL], z[:, D_MODEL:2 * D_MODEL] * z[:, 2 * D_MODEL:]


def _conv_prompt_kernel(h_ref, nm_ref, win_ref, cw_ref, wo_ref, o_ref, st_ref, carry_ref):
    tm = h_ref.shape[0]

    @pl.when(pl.program_id(1) == 0)
    def _():
        carry_ref[...] = jnp.zeros_like(carry_ref)

    g_b, zc = _conv_gates(h_ref, nm_ref, win_ref)
    row = _iota(zc.shape, 0)
    z1 = jnp.where(row == 0, carry_ref[1:2, :], pltpu.roll(zc, 1, axis=0))
    z2 = jnp.where(row == 0, carry_ref[0:1, :], jnp.where(row == 1, carry_ref[1:2, :], pltpu.roll(zc, 2, axis=0)))
    conv = cw_ref[0:1, :] * z2 + cw_ref[1:2, :] * z1 + cw_ref[2:3, :] * zc
    o_ref[...] = h_ref[...] + _bdot(g_b * conv, wo_ref[...])
    tail = zc[tm - (CONV_W - 1):, :]
    carry_ref[0:CONV_W - 1, :] = tail
    st_ref[...] = tail


def _conv_prompt(h, nm, p, tm=256):
    b, l, _ = h.shape
    tm = _row_tile(l, tm)
    tile = pl.BlockSpec((None, tm, D_MODEL), lambda i, j: (i, j, 0))
    return pl.pallas_call(
        _conv_prompt_kernel,
        grid=(b, l // tm),
        in_specs=[tile, _const_spec((1, D_MODEL)), _const_spec((D_MODEL, 3 * D_MODEL)),
                  _const_spec((CONV_W, D_MODEL)), _const_spec((D_MODEL, D_MODEL))],
        out_specs=[tile, pl.BlockSpec((None, CONV_W - 1, D_MODEL), lambda i, j: (i, 0, 0))],
        out_shape=[jax.ShapeDtypeStruct((b, l, D_MODEL), F32),
                   jax.ShapeDtypeStruct((b, CONV_W - 1, D_MODEL), F32)],
        scratch_shapes=[pltpu.VMEM((SUBLANES, D_MODEL), F32)],
        compiler_params=_params(("parallel", "arbitrary")),
        name="conv_prompt",
    )(h, nm.reshape(1, D_MODEL), p["w_in"].astype(BF16), p["w"], p["wo"].astype(BF16))


def _conv_decode_kernel(h_ref, buf_ref, nm_ref, win_ref, cw_ref, wo_ref, o_ref, st_ref):
    g_b, zc = _conv_gates(h_ref, nm_ref, win_ref)
    z2 = buf_ref[:, 0, :]
    z1 = buf_ref[:, 1, :]
    conv = cw_ref[0:1, :] * z2 + cw_ref[1:2, :] * z1 + cw_ref[2:3, :] * zc
    o_ref[...] = h_ref[...] + _bdot(g_b * conv, wo_ref[...])
    st_ref[:, 0, :] = z1
    st_ref[:, 1, :] = zc


def _conv_decode(h, buf, nm, p):
    m = h.shape[0]
    full = pl.BlockSpec((m, D_MODEL), lambda i: (0, 0))
    bufs = pl.BlockSpec(buf.shape, lambda i: (0, 0, 0))
    return pl.pallas_call(
        _conv_decode_kernel,
        grid=(1,),
        in_specs=[full, bufs, _const_spec((1, D_MODEL)), _const_spec((D_MODEL, 3 * D_MODEL)),
                  _const_spec((CONV_W, D_MODEL)), _const_spec((D_MODEL, D_MODEL))],
        out_specs=[full, bufs],
        out_shape=[jax.ShapeDtypeStruct((m, D_MODEL), F32), jax.ShapeDtypeStruct(buf.shape, F32)],
        compiler_params=_params(("arbitrary",)),
        name="conv_decode",
    )(h, buf, nm.reshape(1, D_MODEL), p["w_in"].astype(BF16), p["w"], p["wo"].astype(BF16))


def _pool_project(d_groups, pw_ref, sc_ref):
    ys = [_bdot(d, pw_ref[gi]) for gi, d in enumerate(d_groups)]
    return jnp.concatenate(ys, axis=1) * sc_ref[...]


def _pool_prompt_kernel(h_ref, nm_ref, pw_ref, sc_ref, o_ref, st_ref, carry_ref):
    tm = h_ref.shape[0]
    hist = carry_ref.shape[0]
    j = pl.program_id(1)

    @pl.when(j == 0)
    def _():
        carry_ref[...] = jnp.zeros_like(carry_ref)

    u = _rms(h_ref[...], nm_ref[...], NORM_EPS)
    ext = jnp.concatenate([carry_ref[...], u], axis=0)
    pos = j * tm + _iota((tm, 1), 0)
    d_groups = []
    for gi, w in enumerate(POOL_WINDOWS):
        sl = slice(gi * POOL_G, (gi + 1) * POOL_G)
        s = ext[:, sl]
        span = 1
        while span < w:
            s = s + pltpu.roll(s, span, axis=0)
            span *= 2
        cnt = jnp.minimum(w, pos + 1).astype(F32)
        d_groups.append(s[hist:, :] / cnt - u[:, sl])
    o_ref[...] = h_ref[...] + _pool_project(d_groups, pw_ref, sc_ref)
    tail = u[tm - hist:, :]
    carry_ref[...] = tail
    st_ref[...] = tail


def _pool_prompt(h, nm, p, tm=256):
    b, l, _ = h.shape
    tm = _row_tile(l, tm)
    hist = POOL_BUF + 1
    tile = pl.BlockSpec((None, tm, D_MODEL), lambda i, j: (i, j, 0))
    out, st = pl.pallas_call(
        _pool_prompt_kernel,
        grid=(b, l // tm),
        in_specs=[tile, _const_spec((1, D_MODEL)), _const_spec(p["w"].shape), _const_spec((1, D_MODEL))],
        out_specs=[tile, pl.BlockSpec((None, hist, D_MODEL), lambda i, j: (i, 0, 0))],
        out_shape=[jax.ShapeDtypeStruct((b, l, D_MODEL), F32), jax.ShapeDtypeStruct((b, hist, D_MODEL), F32)],
        scratch_shapes=[pltpu.VMEM((hist, D_MODEL), F32)],
        compiler_params=_params(("parallel", "arbitrary")),
        name="pool_prompt",
    )(h, nm.reshape(1, D_MODEL), p["w"].astype(BF16), p["scale"].reshape(1, D_MODEL))
    return out, st[:, 1:]


def _pool_decode_kernel(h_ref, buf_ref, nm_ref, pw_ref, sc_ref, o_ref, st_ref):
    u = _rms(h_ref[...], nm_ref[...], NORM_EPS)
    d_groups = []
    for gi, w in enumerate(POOL_WINDOWS):
        sl = slice(gi * POOL_G, (gi + 1) * POOL_G)
        s = u[:, sl]
        for i in range(1, w):
            s = s + buf_ref[:, POOL_BUF - i, sl]
        cnt = float(min(w, PAST_LEN + 1))
        d_groups.append(s / cnt - u[:, sl])
    o_ref[...] = h_ref[...] + _pool_project(d_groups, pw_ref, sc_ref)
    st_ref[:, 0:POOL_BUF - 1, :] = buf_ref[:, 1:POOL_BUF, :]
    st_ref[:, POOL_BUF - 1, :] = u


def _pool_decode(h, buf, nm, p):
    m = h.shape[0]
    full = pl.BlockSpec((m, D_MODEL), lambda i: (0, 0))
    bufs = pl.BlockSpec(buf.shape, lambda i: (0, 0, 0))
    return pl.pallas_call(
        _pool_decode_kernel,
        grid=(1,),
        in_specs=[full, bufs, _const_spec((1, D_MODEL)), _const_spec(p["w"].shape), _const_spec((1, D_MODEL))],
        out_specs=[full, bufs],
        out_shape=[jax.ShapeDtypeStruct((m, D_MODEL), F32), jax.ShapeDtypeStruct(buf.shape, F32)],
        compiler_params=_params(("arbitrary",)),
        name="pool_decode",
    )(h, buf, nm.reshape(1, D_MODEL), p["w"].astype(BF16), p["scale"].reshape(1, D_MODEL))


def _trunk_prompt(x, nm, nf, nfin, wu, wd, rw, gl, cv, po):
    b, l, _ = x.shape
    flat = lambda t: t.reshape(b * l, t.shape[-1])
    seq = lambda t: t.reshape(b, l, t.shape[-1])

    (r, lw, k, v, kk, bb, g, bonus), shift = _rwkv_pre_prompt(x, nm[0], rw)
    y, wkv = _rwkv_rec_prompt(r, lw, k, v, kk, bb)
    h = _rwkv_post(flat(x), flat(y), flat(bonus), flat(g), rw)
    h = _ffn(h, nf[0], wu, wd, 0)

    q, k, v, g, gk = _gla_pre(h, nm[1], gl)
    o, gla_s = _gla_rec_prompt(seq(q), seq(k), seq(v), seq(gk))
    h = _gla_post(h, flat(o), g, gl)
    h = _ffn(h, nf[1], wu, wd, 1)

    h, conv_s = _conv_prompt(seq(h), nm[2], cv)
    h = _ffn(flat(h), nf[2], wu, wd, 2)

    h, pool_s = _pool_prompt(seq(h), nm[3], po)
    y = _ffn(flat(h), nf[3], wu, wd, 3, final_g=nfin)
    return seq(y), wkv[None], shift[None], gla_s[None], conv_s[None], pool_s[None]


def _trunk_decode(x, st_wkv, st_shift, st_gla, st_conv, st_pool, nm, nf, nfin, wu, wd, rw, gl, cv, po):
    m = x.shape[0]
    h = x.reshape(m, D_MODEL)

    (r, lw, k, v, kk, bb, g, bonus), shift = _rwkv_pre_decode(h, st_shift[0], nm[0], rw)
    y, wkv = _rwkv_rec_decode(r, lw, k, v, kk, bb, st_wkv[0])
    h = _rwkv_post(h, y, bonus, g, rw)
    h = _ffn(h, nf[0], wu, wd, 0)

    q, k, v, g, gk = _gla_pre(h, nm[1], gl)
    o, gla_s = _gla_rec_decode(q, k, v, gk, st_gla[0])
    h = _gla_post(h, o, g, gl)
    h = _ffn(h, nf[1], wu, wd, 1)

    h, conv_s = _conv_decode(h, st_conv[0], nm[2], cv)
    h = _ffn(h, nf[2], wu, wd, 2)

    h, pool_s = _pool_decode(h, st_pool[0], nm[3], po)
    y = _ffn(h, nf[3], wu, wd, 3, final_g=nfin)
    return y.reshape(m, 1, D_MODEL), wkv[None], shift[None], gla_s[None], conv_s[None], pool_s[None]


def kernel(x_prompt, x_sample, state_rwkv_wkv, state_rwkv_shift, state_gla, state_conv, state_pool, norm_mix, norm_ffn, norm_final, ffn_up, ffn_down, rwkv_mu, rwkv_w_rkv, rwkv_w0, rwkv_w1, rwkv_w2, rwkv_a0, rwkv_a1, rwkv_a2, rwkv_g1, rwkv_g2, rwkv_k_k, rwkv_k_a, rwkv_r_k, rwkv_ln_w, rwkv_ln_b, rwkv_wo, gla_w_in, gla_w_gk2, gla_b_gk, gla_norm, gla_wo, conv_w_in, conv_w, conv_wo, pool_w, pool_scale):
    assert x_prompt.shape[1] % CHUNK == 0 and x_sample.shape[1] == 1
    wu = ffn_up.astype(BF16)
    wd = ffn_down.astype(BF16)
    rw = dict(mu=rwkv_mu[0], w_rkv=rwkv_w_rkv[0], w0=rwkv_w0[0], w1=rwkv_w1[0], w2=rwkv_w2[0], a0=rwkv_a0[0],
              a1=rwkv_a1[0], a2=rwkv_a2[0], g1=rwkv_g1[0], g2=rwkv_g2[0], k_k=rwkv_k_k[0], k_a=rwkv_k_a[0],
              r_k=rwkv_r_k[0], ln_w=rwkv_ln_w[0], ln_b=rwkv_ln_b[0], wo=rwkv_wo[0])
    gl = dict(w_in=gla_w_in[0], w_gk2=gla_w_gk2[0], b_gk=gla_b_gk[0], norm=gla_norm[0], wo=gla_wo[0])
    cv = dict(w_in=conv_w_in[0], w=conv_w[0], wo=conv_wo[0])
    po = dict(w=pool_w[0], scale=pool_scale[0])
    shared = (norm_mix, norm_ffn, norm_final, wu, wd, rw, gl, cv, po)
    y_p, wkv_p, sh_p, gla_p, conv_p, pool_p = _trunk_prompt(x_prompt, *shared)
    y_s, wkv_s, sh_s, gla_s, conv_s, pool_s = _trunk_decode(
        x_sample, state_rwkv_wkv, state_rwkv_shift, state_gla, state_conv, state_pool, *shared)
    return (y_p, y_s, wkv_p, wkv_s, sh_p, sh_s, gla_p, gla_s, conv_p, conv_s, pool_p, pool_s)
```

```python
import functools
import math

import jax
import jax.numpy as jnp
from jax import lax
from jax.experimental import pallas as pl
from jax.experimental.pallas import tpu as pltpu

F32 = jnp.float32
BF16 = jnp.bfloat16

D_MODEL = 1024
D_FF = 4 * D_MODEL
NORM_EPS = 1e-6
PAST_LEN = 16384

RWKV_HEAD = 64
RWKV_H = D_MODEL // RWKV_HEAD
RWKV_GN_EPS = 64e-5
RWKV_QUAD = 4

GLA_H = 4
GLA_DK_TOT = D_MODEL // 2
GLA_DV_TOT = D_MODEL
GLA_DK = GLA_DK_TOT // GLA_H
GLA_DV = GLA_DV_TOT // GLA_H
GLA_LR = 16
GLA_NORMALIZER = 16.0
GLA_NORM_EPS = 1e-5
GLA_SUB = 16

CONV_W = 3
POOL_WINDOWS = (2, 4, 8, 16)
POOL_G = D_MODEL // len(POOL_WINDOWS)
POOL_BUF = max(POOL_WINDOWS) - 1

LANES = 128
SUBLANES = 8
MXU_DIM = 256
VMEM_LIMIT_BYTES = 56 * 1024 * 1024

CHUNK = 64
FFN_COLS = 1024
HI = lax.Precision.HIGHEST


def _const_spec(shape):
    nd = len(shape)
    return pl.BlockSpec(shape, lambda *_: (0,) * nd, pipeline_mode=pl.Buffered(1))


def _params(sem):
    return pltpu.CompilerParams(dimension_semantics=sem, vmem_limit_bytes=VMEM_LIMIT_BYTES)


def _rms(x, g, eps):
    return x * lax.rsqrt(jnp.mean(x * x, axis=-1, keepdims=True) + eps) * g


def _bdot(a, w):
    return jnp.dot(a.astype(BF16), w, preferred_element_type=F32)


def _mm(a, b, prec=None):
    return jnp.dot(a, b, preferred_element_type=F32, precision=prec)


def _nt(a, b, prec=None):
    return lax.dot_general(a, b, (((1,), (1,)), ((), ())), preferred_element_type=F32, precision=prec)


def _tn(a, b, prec=None):
    return lax.dot_general(a, b, (((0,), (0,)), ((), ())), preferred_element_type=F32, precision=prec)


def _iota(shape, dim):
    return lax.broadcasted_iota(jnp.int32, shape, dim)


def _group_sum(x, group):
    shift = group.bit_length() - 1
    bd = (jnp.right_shift(_iota((MXU_DIM, MXU_DIM), 0), shift)
          == jnp.right_shift(_iota((MXU_DIM, MXU_DIM), 1), shift)).astype(BF16)
    hi = x.astype(BF16)
    lo = (x - hi.astype(F32)).astype(BF16)
    cols = []
    for c in range(x.shape[1] // MXU_DIM):
        sl = slice(c * MXU_DIM, (c + 1) * MXU_DIM)
        cols.append(_mm(hi[:, sl], bd) + _mm(lo[:, sl], bd))
    return jnp.concatenate(cols, axis=1)


def _softplus(x):
    return jnp.maximum(x, 0.0) + jnp.log(1.0 + jnp.exp(-jnp.abs(x)))


def _row_tile(m, want):
    t = min(want, m)
    assert m % t == 0, (m, t)
    return t


def _ffn_kernel(h_ref, g_ref, wu_ref, wd_ref, *rest, final):
    if final:
        gf_ref, o_ref = rest
    else:
        (o_ref,) = rest
    h = h_ref[...]
    x = _rms(h, g_ref[...], NORM_EPS).astype(BF16)
    acc = h
    for c in range(D_FF // FFN_COLS):
        sl = slice(c * FFN_COLS, (c + 1) * FFN_COLS)
        a = jnp.dot(x, wu_ref[:, sl], preferred_element_type=F32)
        a = jnp.square(jnp.maximum(a, 0.0)).astype(BF16)
        acc = acc + jnp.dot(a, wd_ref[sl, :], preferred_element_type=F32)
    if final:
        acc = _rms(acc, gf_ref[...], NORM_EPS)
    o_ref[...] = acc


def _ffn(h, g, wu, wd, layer, final_g=None, tm=512):
    m = h.shape[0]
    tm = _row_tile(m, tm)
    row = pl.BlockSpec((tm, D_MODEL), lambda i: (i, 0))
    layer_spec = lambda shape: pl.BlockSpec((None,) + shape[1:], lambda i: (layer, 0, 0),
                                            pipeline_mode=pl.Buffered(1))
    ins = [h, g.reshape(1, D_MODEL), wu, wd]
    specs = [row, _const_spec((1, D_MODEL)), layer_spec(wu.shape), layer_spec(wd.shape)]
    if final_g is not None:
        ins.append(final_g.reshape(1, D_MODEL))
        specs.append(_const_spec((1, D_MODEL)))
    return pl.pallas_call(
        functools.partial(_ffn_kernel, final=final_g is not None),
        grid=(m // tm,),
        in_specs=specs,
        out_specs=row,
        out_shape=jax.ShapeDtypeStruct((m, D_MODEL), F32),
        compiler_params=_params(("parallel",)),
        name="ffn",
    )(*ins)


def _rwkv_pre_math(u, prev, mu_ref, wrkv_ref, w0_ref, w1_ref, w2_ref, a0_ref, a1_ref, a2_ref,
                   g1_ref, g2_ref, kk_ref, ka_ref, rk_ref):
    dx = prev - u

    def mix(i):
        return (u + dx * mu_ref[i:i + 1, :]).astype(BF16)

    r = _bdot(mix(0), wrkv_ref[0])
    k = _bdot(mix(2), wrkv_ref[1])
    v = _bdot(mix(3), wrkv_ref[2])
    wl = w0_ref[...] + _bdot(jnp.tanh(_bdot(mix(1), w1_ref[...])), w2_ref[...])
    lw = jax.nn.sigmoid(wl) * (-math.exp(-0.5))
    a = jax.nn.sigmoid(a0_ref[...] + _bdot(_bdot(mix(4), a1_ref[...]), a2_ref[...]))
    g = _bdot(jax.nn.sigmoid(_bdot(mix(5), g1_ref[...])), g2_ref[...])
    kk = k * kk_ref[...]
    kk = kk * lax.rsqrt(jnp.maximum(_group_sum(kk * kk, RWKV_HEAD), 1e-24))
    k = k * (1.0 + (a - 1.0) * ka_ref[...])
    bonus = _group_sum(r * k * rk_ref[...], RWKV_HEAD) * v
    return r, lw, k, v, kk, kk * a, g, bonus


_N_RWKV_W = 13


def _rwkv_pre_prompt_kernel(h_ref, nm_ref, *rest):
    w = rest[:_N_RWKV_W]
    outs = rest[_N_RWKV_W:_N_RWKV_W + 8]
    sh_ref, carry_ref = rest[_N_RWKV_W + 8:]
    tm = h_ref.shape[0]

    @pl.when(pl.program_id(1) == 0)
    def _():
        carry_ref[...] = jnp.zeros_like(carry_ref)

    u = _rms(h_ref[...], nm_ref[...], NORM_EPS)
    prev = jnp.where(_iota(u.shape, 0) == 0, carry_ref[0:1, :], pltpu.roll(u, 1, axis=0))
    last = u[tm - 1:tm, :]
    carry_ref[0:1, :] = last
    sh_ref[...] = last
    for o_ref, val in zip(outs, _rwkv_pre_math(u, prev, *w)):
        o_ref[...] = val


def _rwkv_pre_decode_kernel(h_ref, prev_ref, nm_ref, *rest):
    w = rest[:_N_RWKV_W]
    outs = rest[_N_RWKV_W:_N_RWKV_W + 8]
    (sh_ref,) = rest[_N_RWKV_W + 8:]
    u = _rms(h_ref[...], nm_ref[...], NORM_EPS)
    sh_ref[...] = u
    vals = _rwkv_pre_math(u, prev_ref[...], *w)
    for o_ref, val in zip(outs[:6], vals[:6]):
        o_ref[...] = val.T
    for o_ref, val in zip(outs[6:], vals[6:]):
        o_ref[...] = val


def _rwkv_weights(p):
    row = lambda x: x.reshape(1, D_MODEL)
    return [p["mu"], p["w_rkv"].astype(BF16), row(p["w0"]), p["w1"].astype(BF16), p["w2"].astype(BF16),
            row(p["a0"]), p["a1"].astype(BF16), p["a2"].astype(BF16), p["g1"].astype(BF16),
            p["g2"].astype(BF16), row(p["k_k"]), row(p["k_a"]), row(p["r_k"])]


def _rwkv_pre_prompt(h, nm, p, tm=256):
    b, l, _ = h.shape
    tm = _row_tile(l, tm)
    ws = _rwkv_weights(p)
    tile = pl.BlockSpec((None, tm, D_MODEL), lambda i, j: (i, j, 0))
    outs = pl.pallas_call(
        _rwkv_pre_prompt_kernel,
        grid=(b, l // tm),
        in_specs=[tile, _const_spec((1, D_MODEL))] + [_const_spec(w.shape) for w in ws],
        out_specs=[tile] * 8 + [pl.BlockSpec((None, 1, D_MODEL), lambda i, j: (i, 0, 0))],
        out_shape=[jax.ShapeDtypeStruct((b, l, D_MODEL), F32)] * 8
        + [jax.ShapeDtypeStruct((b, 1, D_MODEL), F32)],
        scratch_shapes=[pltpu.VMEM((SUBLANES, D_MODEL), F32)],
        compiler_params=_params(("parallel", "arbitrary")),
        name="rwkv_pre_prompt",
    )(h, nm.reshape(1, D_MODEL), *ws)
    return outs[:8], outs[8].reshape(b, D_MODEL)


def _rwkv_pre_decode(h, prev, nm, p):
    m = h.shape[0]
    ws = _rwkv_weights(p)
    full = pl.BlockSpec((m, D_MODEL), lambda i: (0, 0))
    full_t = pl.BlockSpec((D_MODEL, m), lambda i: (0, 0))
    outs = pl.pallas_call(
        _rwkv_pre_decode_kernel,
        grid=(1,),
        in_specs=[full, full, _const_spec((1, D_MODEL))] + [_const_spec(w.shape) for w in ws],
        out_specs=[full_t] * 6 + [full] * 3,
        out_shape=[jax.ShapeDtypeStruct((D_MODEL, m), F32)] * 6 + [jax.ShapeDtypeStruct((m, D_MODEL), F32)] * 3,
        compiler_params=_params(("arbitrary",)),
        name="rwkv_pre_decode",
    )(h, prev, nm.reshape(1, D_MODEL), *ws)
    return outs[:8], outs[8]


def _rwkv_rec_prompt_kernel(r_ref, lw_ref, k_ref, v_ref, kk_ref, bb_ref, y_ref, so_ref, s_scr):
    tc = r_ref.shape[0]
    qw = RWKV_QUAD * RWKV_HEAD
    n_quads = D_MODEL // qw
    assert RWKV_QUAD * CHUNK == qw

    @pl.when(pl.program_id(1) == 0)
    def _():
        s_scr[...] = jnp.zeros_like(s_scr)

    ltri = (_iota((CHUNK, CHUNK), 0) >= _iota((CHUNK, CHUNK), 1)).astype(F32)
    t_idx, s_idx = _iota((CHUNK, qw), 0), _iota((CHUNK, qw), 1) & (CHUNK - 1)
    strict = t_idx > s_idx
    incl = t_idx >= s_idx
    eye = (t_idx == s_idx).astype(F32)
    blocks = jnp.right_shift(_iota((qw, qw), 0), 6) == jnp.right_shift(_iota((qw, qw), 1), 6)

    def bdiag(z):
        zb = z.astype(BF16)
        return jnp.where(blocks, jnp.concatenate([zb] * RWKV_QUAD, axis=0), jnp.zeros((), BF16))

    quads = range(n_quads)
    sls = [slice(q * qw, (q + 1) * qw) for q in quads]
    cc = CHUNK

    n_chunks = tc // CHUNK
    items = [(c, q) for c in range(n_chunks) for q in quads]
    every = range(len(items))

    scaled = []
    for c in range(n_chunks):
        rows = slice(c * CHUNK, (c + 1) * CHUNK)
        lw = lw_ref[rows, :]
        cum = _mm(ltri, lw, HI)
        g_in = jnp.exp(cum)
        g_inv = jnp.exp(-cum)
        g_last = g_in[CHUNK - 1:CHUNK, :]
        scaled.append(dict(
            rt=r_ref[rows, :] * g_in, at=-kk_ref[rows, :] * jnp.exp(cum - lw),
            bt=bb_ref[rows, :] * g_inv, kt=k_ref[rows, :] * g_inv, v=v_ref[rows, :], g_last=g_last))

    def part(name, i):
        c, q = items[i]
        return scaled[c][name][:, sls[q]]

    ar = [jnp.concatenate([part("at", i), part("rt", i)], axis=0).astype(BF16) for i in every]
    bk4 = [jnp.concatenate([bdiag(part("bt", i)), bdiag(part("kt", i))], axis=0) for i in every]
    gram = [_nt(ar[i], bk4[i]) for i in every]
    a_ab = [jnp.where(strict, gram[i][:cc, :qw], 0.0) for i in every]
    a_k = [jnp.concatenate([jnp.where(strict, gram[i][:cc, qw:], 0.0),
                            jnp.where(incl, gram[i][cc:, qw:], 0.0)], axis=0).astype(BF16) for i in every]
    a_rb = [jnp.where(incl, gram[i][cc:, :qw], 0.0).astype(BF16) for i in every]
    akv = [_mm(a_k[i], bdiag(part("v", i))) for i in every]
    inv = [eye + a_ab[i] for i in every]
    pw = [_mm(a_ab[i].astype(BF16), bdiag(a_ab[i])) for i in every]
    for _ in range(1, CHUNK.bit_length() - 2):
        both = [_mm(jnp.concatenate([pw[i], inv[i]], axis=0).astype(BF16), bdiag(pw[i])) for i in every]
        inv = [inv[i] + both[i][cc:] for i in every]
        pw = [both[i][:cc] for i in every]
    inv = [(inv[i] + _mm(inv[i].astype(BF16), bdiag(pw[i]))).astype(BF16) for i in every]
    tt = [_mm(inv[i], jnp.concatenate([bdiag(part("at", i)), bdiag(akv[i][:cc])], axis=1)) for i in every]
    ta = [tt[i][:, :qw].astype(BF16) for i in every]
    tav = [tt[i][:, qw:] for i in every]
    bkg = [(jnp.concatenate([part("bt", i), part("kt", i)], axis=0) * part("g_last", i)).astype(BF16)
           for i in every]
    w = [jnp.where(blocks, _tn(ta[i], bkg[i][:cc]), 0.0).astype(BF16) for i in every]
    n = [jnp.where(blocks, _tn(jnp.concatenate([tav[i], part("v", i)], axis=0).astype(BF16), bkg[i]), 0.0)
         for i in every]

    s_in = []
    state = [s_scr[q] for q in quads]
    for c in range(n_chunks):
        s_b = [state[q].astype(BF16) for q in quads]
        s_in.extend(s_b)
        state = [state[q] * scaled[c]["g_last"][:, sls[q]] + _mm(s_b[q], w[c * n_quads + q]) + n[c * n_quads + q]
                 for q in quads]
    for q in quads:
        s_scr[q] = state[q]

    xs = [_nt(jnp.concatenate([ta[i], ar[i][cc:]], axis=0), s_in[i]) for i in every]
    u = [xs[i][:cc] + tav[i] for i in every]
    for i in every:
        c, q = items[i]
        y_ref[c * CHUNK:(c + 1) * CHUNK, sls[q]] = xs[i][cc:] + akv[i][cc:] + _mm(a_rb[i], bdiag(u[i]))

    @pl.when(pl.program_id(1) == pl.num_programs(1) - 1)
    def _():
        for hd in range(RWKV_H):
            q, o = divmod(hd, RWKV_QUAD)
            so_ref[hd] = s_scr[q][o * RWKV_HEAD:(o + 1) * RWKV_HEAD, o * RWKV_HEAD:(o + 1) * RWKV_HEAD]


def _rwkv_rec_prompt(r, lw, k, v, kk, bb, tc=256):
    b, l, _ = r.shape
    tc = _row_tile(l, tc)
    tile = pl.BlockSpec((None, tc, D_MODEL), lambda i, j: (i, j, 0))
    st = (RWKV_H, RWKV_HEAD, RWKV_HEAD)
    return pl.pallas_call(
        _rwkv_rec_prompt_kernel,
        grid=(b, l // tc),
        in_specs=[tile] * 6,
        out_specs=[tile, pl.BlockSpec((None,) + st, lambda i, j: (i, 0, 0, 0))],
        out_shape=[jax.ShapeDtypeStruct((b, l, D_MODEL), F32), jax.ShapeDtypeStruct((b,) + st, F32)],
        scratch_shapes=[pltpu.VMEM((RWKV_H // RWKV_QUAD, RWKV_QUAD * RWKV_HEAD, RWKV_QUAD * RWKV_HEAD), F32)],
        compiler_params=_params(("parallel", "arbitrary")),
        name="rwkv_rec_prompt",
    )(r, lw, k, v, kk, bb)


def _rwkv_rec_decode_kernel(r_ref, lw_ref, k_ref, v_ref, kk_ref, bb_ref, s_ref, y_ref, so_ref):
    w = jnp.exp(lw_ref[...])
    a = -kk_ref[...]
    b, k, r = bb_ref[...], k_ref[...], r_ref[...]

    def row(vi, carry):
        s0 = s_ref[vi]
        sa = jnp.sum(s0 * a, axis=0, keepdims=True)
        s1 = s0 * w + sa * b + v_ref[pl.ds(vi, 1), :] * k
        so_ref[vi] = s1
        y_ref[pl.ds(vi, 1), :] = jnp.sum(s1 * r, axis=0, keepdims=True)
        return carry

    lax.fori_loop(0, RWKV_HEAD, row, 0, unroll=8)


def _rwkv_rec_decode(r, lw, k, v, kk, bb, s):
    m = r.shape[1]
    vec = pl.BlockSpec((RWKV_HEAD, m), lambda i: (i, 0))
    st = pl.BlockSpec((None, RWKV_HEAD, RWKV_HEAD, m), lambda i: (i, 0, 0, 0))
    return pl.pallas_call(
        _rwkv_rec_decode_kernel,
        grid=(RWKV_H,),
        in_specs=[vec] * 6 + [st],
        out_specs=[vec, st],
        out_shape=[jax.ShapeDtypeStruct((D_MODEL, m), F32), jax.ShapeDtypeStruct(s.shape, F32)],
        compiler_params=_params(("parallel",)),
        name="rwkv_rec_decode",
    )(r, lw, k, v, kk, bb, s)


def _rwkv_post_kernel(h_ref, y_ref, bo_ref, g_ref, lnw_ref, lnb_ref, wo_ref, o_ref):
    y = y_ref[...]
    d = y - _group_sum(y, RWKV_HEAD) * (1.0 / RWKV_HEAD)
    var = _group_sum(d * d, RWKV_HEAD) * (1.0 / RWKV_HEAD)
    yn = d * lax.rsqrt(var + RWKV_GN_EPS) * lnw_ref[...] + lnb_ref[...]
    o_ref[...] = h_ref[...] + _bdot((yn + bo_ref[...]) * g_ref[...], wo_ref[...])


def _rwkv_post(h, y, bonus, g, p, tm=512):
    m = h.shape[0]
    tm = _row_tile(m, tm)
    row = pl.BlockSpec((tm, D_MODEL), lambda i: (i, 0))
    vec = _const_spec((1, D_MODEL))
    return pl.pallas_call(
        _rwkv_post_kernel,
        grid=(m // tm,),
        in_specs=[row] * 4 + [vec, vec, _const_spec((D_MODEL, D_MODEL))],
        out_specs=row,
        out_shape=jax.ShapeDtypeStruct((m, D_MODEL), F32),
        compiler_params=_params(("parallel",)),
        name="rwkv_post",
    )(h, y, bonus, g, p["ln_w"].reshape(1, D_MODEL), p["ln_b"].reshape(1, D_MODEL), p["wo"].astype(BF16))


def _gla_pre_kernel(h_ref, nm_ref, win_ref, wgl_ref, wgk_ref, bgk_ref, q_ref, k_ref, v_ref, g_ref, gk_ref):
    u = _rms(h_ref[...], nm_ref[...], NORM_EPS).astype(BF16)
    z = _nt(u, win_ref[...])
    q_ref[...] = z[:, :GLA_DK_TOT] * (GLA_DK ** -0.5)
    k_ref[...] = z[:, GLA_DK_TOT:2 * GLA_DK_TOT]
    v_ref[...] = z[:, 2 * GLA_DK_TOT:2 * GLA_DK_TOT + GLA_DV_TOT]
    g_ref[...] = z[:, 2 * GLA_DK_TOT + GLA_DV_TOT:]
    gl = jnp.dot(u, wgl_ref[...], preferred_element_type=F32)
    pre = _bdot(gl, wgk_ref[...]) + bgk_ref[...]
    gk_ref[...] = -_softplus(-pre) * (1.0 / GLA_NORMALIZER)


def _gla_pre(h, nm, p, tm=512):
    m = h.shape[0]
    tm = _row_tile(m, tm)
    n_main = 2 * GLA_DK_TOT + 2 * GLA_DV_TOT
    w_main =p["w_in"].T[:n_main].astype(BF16)
    w_gl = jnp.pad(p["w_in"][:, n_main:].astype(BF16), ((0, 0), (0, LANES - GLA_LR)))
    w_gk = jnp.pad(p["w_gk2"].astype(BF16), ((0, LANES - GLA_LR), (0, 0)))
    row = lambda n: pl.BlockSpec((tm, n), lambda i: (i, 0))
    widths = (GLA_DK_TOT, GLA_DK_TOT, GLA_DV_TOT, GLA_DV_TOT, GLA_DK_TOT)
    return pl.pallas_call(
        _gla_pre_kernel,
        grid=(m // tm,),
        in_specs=[row(D_MODEL), _const_spec((1, D_MODEL)), _const_spec(w_main.shape), _const_spec(w_gl.shape),
                  _const_spec(w_gk.shape), _const_spec((1, GLA_DK_TOT))],
        out_specs=[row(n) for n in widths],
        out_shape=[jax.ShapeDtypeStruct((m, n), F32) for n in widths],
        compiler_params=_params(("parallel",)),
        name="gla_pre",
    )(h, nm.reshape(1, D_MODEL), w_main, w_gl, w_gk, p["b_gk"].reshape(1, GLA_DK_TOT))


def _gla_rec_prompt_kernel(q_ref, k_ref, v_ref, gk_ref, o_ref, so_ref, s_scr):
    tc = q_ref.shape[0]

    @pl.when(pl.program_id(1) == 0)
    def _():
        s_scr[...] = jnp.zeros_like(s_scr)

    ri, ci = _iota((CHUNK, CHUNK), 0), _iota((CHUNK, CHUNK), 1)
    incl = ri >= ci
    ltri = incl.astype(F32)
    n_sub = CHUNK // GLA_SUB
    sub_of_row = jnp.right_shift(_iota((CHUNK, 1), 0), GLA_SUB.bit_length() - 1)

    heads = range(GLA_H)
    ks = [slice(hd * GLA_DK, (hd + 1) * GLA_DK) for hd in heads]
    vs = [slice(hd * GLA_DV, (hd + 1) * GLA_DV) for hd in heads]
    n_chunks = tc // CHUNK
    items = [(c, hd) for c in range(n_chunks) for hd in heads]

    prep = []
    for c in range(n_chunks):
        rows = slice(c * CHUNK, (c + 1) * CHUNK)
        cum = _mm(ltri, gk_ref[rows, :], HI)
        q, k = q_ref[rows, :], k_ref[rows, :]
        last = cum[CHUNK - 1:CHUNK, :]
        refs = [jnp.zeros_like(last)] + [cum[i * GLA_SUB - 1:i * GLA_SUB, :] for i in range(1, n_sub)]
        ref_of_row = refs[0]
        for i in range(1, n_sub):
            ref_of_row = jnp.where(sub_of_row >= i, refs[i], ref_of_row)
        q_in = (q * jnp.exp(cum - ref_of_row)).astype(BF16)
        k_in = [(k * jnp.exp(jnp.where(sub_of_row <= i, refs[i] - cum, 0.0))).astype(BF16) for i in range(n_sub)]
        prep.append(dict(q_in=q_in, k_in=k_in, qe=(q * jnp.exp(cum)).astype(BF16),
                         kh=(k * jnp.exp(last - cum)).astype(BF16), g_last=jnp.exp(last),
                         v=v_ref[rows, :].astype(BF16)))

    def scores(c, hd):
        p = prep[c]
        blocks = [_nt(p["q_in"][i * GLA_SUB:(i + 1) * GLA_SUB, ks[hd]], p["k_in"][i][:, ks[hd]])
                  for i in range(n_sub)]
        return jnp.where(incl, jnp.concatenate(blocks, axis=0), 0.0).astype(BF16)

    att = [scores(c, hd) for c, hd in items]
    intra = [_mm(att[i], prep[c]["v"][:, vs[hd]]) for i, (c, hd) in enumerate(items)]
    kv = [_tn(prep[c]["v"][:, vs[hd]], prep[c]["kh"][:, ks[hd]]) for c, hd in items]

    s_in = []
    state = [s_scr[hd] for hd in heads]
    for c in range(n_chunks):
        s_in.extend(state[hd].astype(BF16) for hd in heads)
        state = [state[hd] * prep[c]["g_last"][:, ks[hd]] + kv[c * GLA_H + hd] for hd in heads]
    for hd in heads:
        s_scr[hd] = state[hd]

    for i, (c, hd) in enumerate(items):
        o_ref[c * CHUNK:(c + 1) * CHUNK, vs[hd]] = _nt(prep[c]["qe"][:, ks[hd]], s_in[i]) + intra[i]

    @pl.when(pl.program_id(1) == pl.num_programs(1) - 1)
    def _():
        for hd in range(GLA_H):
            so_ref[hd] = s_scr[hd].T


def _gla_rec_prompt(q, k, v, gk, tc=256):
    b, l, _ = q.shape
    tc = _row_tile(l, tc)
    tile = lambda n: pl.BlockSpec((None, tc, n), lambda i, j: (i, j, 0))
    st = (GLA_H, GLA_DK, GLA_DV)
    return pl.pallas_call(
        _gla_rec_prompt_kernel,
        grid=(b, l // tc),
        in_specs=[tile(GLA_DK_TOT), tile(GLA_DK_TOT), tile(GLA_DV_TOT), tile(GLA_DK_TOT)],
        out_specs=[tile(GLA_DV_TOT), pl.BlockSpec((None,) + st, lambda i, j: (i, 0, 0, 0))],
        out_shape=[jax.ShapeDtypeStruct((b, l, GLA_DV_TOT), F32), jax.ShapeDtypeStruct((b,) + st, F32)],
        scratch_shapes=[pltpu.VMEM((GLA_H, GLA_DV, GLA_DK), F32)],
        compiler_params=_params(("parallel", "arbitrary")),
        name="gla_rec_prompt",
    )(q, k, v, gk)


def _gla_rec_decode_kernel(q_ref, k_ref, v_ref, gk_ref, s_ref, o_ref, so_ref):
    eye = (_iota((GLA_DK, GLA_DK), 0) == _iota((GLA_DK, GLA_DK), 1)).astype(F32)[None]

    def col(x):
        return jnp.sum(eye * x, axis=-1, keepdims=True)

    for hd in range(GLA_H):
        hs = slice(hd, hd + 1)
        s1 = s_ref[:, hd] * col(jnp.exp(gk_ref[:, hs, :])) + col(k_ref[:, hs, :]) * v_ref[:, hs, :]
        so_ref[:, hd] = s1
        o_ref[:, hs, :] = jnp.sum(col(q_ref[:, hs, :]) * s1, axis=1, keepdims=True)


def _gla_rec_decode(q, k, v, gk, s, bblk=8):
    m = q.shape[0]
    kvec = pl.BlockSpec((bblk, GLA_H, GLA_DK), lambda i: (i, 0, 0))
    vvec = pl.BlockSpec((bblk, GLA_H, GLA_DV), lambda i: (i, 0, 0))
    st = pl.BlockSpec((bblk, GLA_H, GLA_DK, GLA_DV), lambda i: (i, 0, 0, 0))
    hk = lambda x: x.reshape(m, GLA_H, GLA_DK)
    o, s_new = pl.pallas_call(
        _gla_rec_decode_kernel,
        grid=(m // bblk,),
        in_specs=[kvec, kvec, vvec, kvec, st],
        out_specs=[vvec, st],
        out_shape=[jax.ShapeDtypeStruct((m, GLA_H, GLA_DV), F32), jax.ShapeDtypeStruct(s.shape, F32)],
        compiler_params=_params(("parallel",)),
        name="gla_rec_decode",
    )(hk(q), hk(k), v.reshape(m, GLA_H, GLA_DV), hk(gk), s)
    return o.reshape(m, GLA_DV_TOT), s_new


def _gla_post_kernel(h_ref, o_ref_in, g_ref, nw_ref, wo_ref, out_ref):
    o = o_ref_in[...]
    parts = []
    for hd in range(GLA_H):
        oh = o[:, hd * GLA_DV:(hd + 1) * GLA_DV]
        parts.append(oh * lax.rsqrt(jnp.mean(oh * oh, axis=-1, keepdims=True) + GLA_NORM_EPS))
    on = jnp.concatenate(parts, axis=1) * nw_ref[...]
    g = g_ref[...]
    out_ref[...] = h_ref[...] + _bdot(on * (g * jax.nn.sigmoid(g)), wo_ref[...])


def _gla_post(h, o, g, p, tm=512):
    m = h.shape[0]
    tm = _row_tile(m, tm)
    row = pl.BlockSpec((tm, D_MODEL), lambda i: (i, 0))
    return pl.pallas_call(
        _gla_post_kernel,
        grid=(m // tm,),
        in_specs=[row] * 3 + [_const_spec((1, D_MODEL)), _const_spec((D_MODEL, D_MODEL))],
        out_specs=row,
        out_shape=jax.ShapeDtypeStruct((m, D_MODEL), F32),
        compiler_params=_params(("parallel",)),
        name="gla_post",
    )(h, o, g, jnp.tile(p["norm"], GLA_H).reshape(1, D_MODEL), p["wo"].astype(BF16))


def _conv_gates(h_ref, nm_ref, win_ref):
    u = _rms(h_ref[...], nm_ref[...], NORM_EPS).astype(BF16)
    z = jnp.dot(u, win_ref[...], preferred_element_type=F32)
    return z[:, :D_MODEL], z[:, D_MODEL:2 * D_MODEL] * z[:, 2 * D_MODEL:]


def _conv_prompt_kernel(h_ref, nm_ref, win_ref, cw_ref, wo_ref, o_ref, st_ref, carry_ref):
    tm = h_ref.shape[0]

    @pl.when(pl.program_id(1) == 0)
    def _():
        carry_ref[...] = jnp.zeros_like(carry_ref)

    g_b, zc = _conv_gates(h_ref, nm_ref, win_ref)
    row = _iota(zc.shape, 0)
    z1 = jnp.where(row == 0, carry_ref[1:2, :], pltpu.roll(zc, 1, axis=0))
    z2 = jnp.where(row == 0, carry_ref[0:1, :], jnp.where(row == 1, carry_ref[1:2, :], pltpu.roll(zc, 2, axis=0)))
    conv = cw_ref[0:1, :] * z2 + cw_ref[1:2, :] * z1 + cw_ref[2:3, :] * zc
    o_ref[...] = h_ref[...] + _bdot(g_b * conv, wo_ref[...])
    tail = zc[tm - (CONV_W - 1):, :]
    carry_ref[0:CONV_W - 1, :] = tail
    st_ref[...] = tail


def _conv_prompt(h, nm, p, tm=256):
    b, l, _ = h.shape
    tm = _row_tile(l, tm)
    tile = pl.BlockSpec((None, tm, D_MODEL), lambda i, j: (i, j, 0))
    return pl.pallas_call(
        _conv_prompt_kernel,
        grid=(b, l // tm),
        in_specs=[tile, _const_spec((1, D_MODEL)), _const_spec((D_MODEL, 3 * D_MODEL)),
                  _const_spec((CONV_W, D_MODEL)), _const_spec((D_MODEL, D_MODEL))],
        out_specs=[tile, pl.BlockSpec((None, CONV_W - 1, D_MODEL), lambda i, j: (i, 0, 0))],
        out_shape=[jax.ShapeDtypeStruct((b, l, D_MODEL), F32),
                   jax.ShapeDtypeStruct((b, CONV_W - 1, D_MODEL), F32)],
        scratch_shapes=[pltpu.VMEM((SUBLANES, D_MODEL), F32)],
        compiler_params=_params(("parallel", "arbitrary")),
        name="conv_prompt",
    )(h, nm.reshape(1, D_MODEL), p["w_in"].astype(BF16), p["w"], p["wo"].astype(BF16))


def _conv_decode_kernel(h_ref, buf_ref, nm_ref, win_ref, cw_ref, wo_ref, o_ref, st_ref):
    g_b, zc = _conv_gates(h_ref, nm_ref, win_ref)
    z2 = buf_ref[:, 0, :]
    z1 = buf_ref[:, 1, :]
    conv = cw_ref[0:1, :] * z2 + cw_ref[1:2, :] * z1 + cw_ref[2:3, :] * zc
    o_ref[...] = h_ref[...] + _bdot(g_b * conv, wo_ref[...])
    st_ref[:, 0, :] = z1
    st_ref[:, 1, :] = zc


def _conv_decode(h, buf, nm, p):
    m = h.shape[0]
    full = pl.BlockSpec((m, D_MODEL), lambda i: (0, 0))
    bufs = pl.BlockSpec(buf.shape, lambda i: (0, 0, 0))
    return pl.pallas_call(
        _conv_decode_kernel,
        grid=(1,),
        in_specs=[full, bufs, _const_spec((1, D_MODEL)), _const_spec((D_MODEL, 3 * D_MODEL)),
                  _const_spec((CONV_W, D_MODEL)), _const_spec((D_MODEL, D_MODEL))],
        out_specs=[full, bufs],
        out_shape=[jax.ShapeDtypeStruct((m, D_MODEL), F32), jax.ShapeDtypeStruct(buf.shape, F32)],
        compiler_params=_params(("arbitrary",)),
        name="conv_decode",
    )(h, buf, nm.reshape(1, D_MODEL), p["w_in"].astype(BF16), p["w"], p["wo"].astype(BF16))


def _pool_project(d_groups, pw_ref, sc_ref):
    ys = [_bdot(d, pw_ref[gi]) for gi, d in enumerate(d_groups)]
    return jnp.concatenate(ys, axis=1) * sc_ref[...]


def _pool_prompt_kernel(h_ref, nm_ref, pw_ref, sc_ref, o_ref, st_ref, carry_ref):
    tm = h_ref.shape[0]
    hist = carry_ref.shape[0]
    j = pl.program_id(1)

    @pl.when(j == 0)
    def _():
        carry_ref[...] = jnp.zeros_like(carry_ref)

    u = _rms(h_ref[...], nm_ref[...], NORM_EPS)
    ext = jnp.concatenate([carry_ref[...], u], axis=0)
    pos = j * tm + _iota((tm, 1), 0)
    d_groups = []
    for gi, w in enumerate(POOL_WINDOWS):
        sl = slice(gi * POOL_G, (gi + 1) * POOL_G)
        s = ext[:, sl]
        span = 1
        while span < w:
            s = s + pltpu.roll(s, span, axis=0)
            span *= 2
        cnt = jnp.minimum(w, pos + 1).astype(F32)
        d_groups.append(s[hist:, :] / cnt - u[:, sl])
    o_ref[...] = h_ref[...] + _pool_project(d_groups, pw_ref, sc_ref)
    tail = u[tm - hist:, :]
    carry_ref[...] = tail
    st_ref[...] = tail


def _pool_prompt(h, nm, p, tm=256):
    b, l, _ = h.shape
    tm = _row_tile(l, tm)
    hist = POOL_BUF + 1
    tile = pl.BlockSpec((None, tm, D_MODEL), lambda i, j: (i, j, 0))
    out, st = pl.pallas_call(
        _pool_prompt_kernel,
        grid=(b, l // tm),
        in_specs=[tile, _const_spec((1, D_MODEL)), _const_spec(p["w"].shape), _const_spec((1, D_MODEL))],
        out_specs=[tile, pl.BlockSpec((None, hist, D_MODEL), lambda i, j: (i, 0, 0))],
        out_shape=[jax.ShapeDtypeStruct((b, l, D_MODEL), F32), jax.ShapeDtypeStruct((b, hist, D_MODEL), F32)],
        scratch_shapes=[pltpu.VMEM((hist, D_MODEL), F32)],
        compiler_params=_params(("parallel", "arbitrary")),
        name="pool_prompt",
    )(h, nm.reshape(1, D_MODEL), p["w"].astype(BF16), p["scale"].reshape(1, D_MODEL))
    return out, st[:, 1:]


def _pool_decode_kernel(h_ref, buf_ref, nm_ref, pw_ref, sc_ref, o_ref, st_ref):
    u = _rms(h_ref[...], nm_ref[...], NORM_EPS)
    d_groups = []
    for gi, w in enumerate(POOL_WINDOWS):
        sl = slice(gi * POOL_G, (gi + 1) * POOL_G)
        s = u[:, sl]
        for i in range(1, w):
            s = s + buf_ref[:, POOL_BUF - i, sl]
        cnt = float(min(w, PAST_LEN + 1))
        d_groups.append(s / cnt - u[:, sl])
    o_ref[...] = h_ref[...] + _pool_project(d_groups, pw_ref, sc_ref)
    st_ref[:, 0:POOL_BUF - 1, :] = buf_ref[:, 1:POOL_BUF, :]
    st_ref[:, POOL_BUF - 1, :] = u


def _pool_decode(h, buf, nm, p):
    m = h.shape[0]
    full = pl.BlockSpec((m, D_MODEL), lambda i: (0, 0))
    bufs = pl.BlockSpec(buf.shape, lambda i: (0, 0, 0))
    return pl.pallas_call(
        _pool_decode_kernel,
        grid=(1,),
        in_specs=[full, bufs, _const_spec((1, D_MODEL)), _const_spec(p["w"].shape), _const_spec((1, D_MODEL))],
        out_specs=[full, bufs],
        out_shape=[jax.ShapeDtypeStruct((m, D_MODEL), F32), jax.ShapeDtypeStruct(buf.shape, F32)],
        compiler_params=_params(("arbitrary",)),
        name="pool_decode",
    )(h, buf, nm.reshape(1, D_MODEL), p["w"].astype(BF16), p["scale"].reshape(1, D_MODEL))


def _trunk_prompt(x, nm, nf, nfin, wu, wd, rw, gl, cv, po):
    b, l, _ = x.shape
    flat = lambda t: t.reshape(b * l, t.shape[-1])
    seq = lambda t: t.reshape(b, l, t.shape[-1])

    (r, lw, k, v, kk, bb, g, bonus), shift = _rwkv_pre_prompt(x, nm[0], rw)
    y, wkv = _rwkv_rec_prompt(r, lw, k, v, kk, bb)
    h = _rwkv_post(flat(x), flat(y), flat(bonus), flat(g), rw)
    h = _ffn(h, nf[0], wu, wd, 0)

    q, k, v, g, gk = _gla_pre(h, nm[1], gl)
    o, gla_s = _gla_rec_prompt(seq(q), seq(k), seq(v), seq(gk))
    h = _gla_post(h, flat(o), g, gl)
    h = _ffn(h, nf[1], wu, wd, 1)

    h, conv_s = _conv_prompt(seq(h), nm[2], cv)
    h = _ffn(flat(h), nf[2], wu, wd, 2)

    h, pool_s = _pool_prompt(seq(h), nm[3], po)
    y = _ffn(flat(h), nf[3], wu, wd, 3, final_g=nfin)
    return seq(y), wkv[None], shift[None], gla_s[None], conv_s[None], pool_s[None]


def _trunk_decode(x, st_wkv, st_shift, st_gla, st_conv, st_pool, nm, nf, nfin, wu, wd, rw, gl, cv, po):
    m = x.shape[0]
    h = x.reshape(m, D_MODEL)

    (r, lw, k, v, kk, bb, g, bonus), shift = _rwkv_pre_decode(h, st_shift[0], nm[0], rw)
    y_t, wkv_t = _rwkv_rec_decode(r, lw, k, v, kk, bb, jnp.transpose(st_wkv[0], (1, 2, 3, 0)))
    wkv = jnp.transpose(wkv_t, (3, 0, 1, 2))
    h = _rwkv_post(h, y_t.T, bonus, g, rw)
    h = _ffn(h, nf[0], wu, wd, 0)

    q, k, v, g, gk = _gla_pre(h, nm[1], gl)
    o, gla_s = _gla_rec_decode(q, k, v, gk, st_gla[0])
    h = _gla_post(h, o, g, gl)
    h = _ffn(h, nf[1], wu, wd, 1)

    h, conv_s = _conv_decode(h, st_conv[0], nm[2], cv)
    h = _ffn(h, nf[2], wu, wd, 2)

    h, pool_s = _pool_decode(h, st_pool[0], nm[3], po)
    y = _ffn(h, nf[3], wu, wd, 3, final_g=nfin)
    return y.reshape(m, 1, D_MODEL), wkv[None], shift[None], gla_s[None], conv_s[None], pool_s[None]


def kernel(x_prompt, x_sample, state_rwkv_wkv, state_rwkv_shift, state_gla, state_conv, state_pool, norm_mix, norm_ffn, norm_final, ffn_up, ffn_down, rwkv_mu, rwkv_w_rkv, rwkv_w0, rwkv_w1, rwkv_w2, rwkv_a0, rwkv_a1, rwkv_a2, rwkv_g1, rwkv_g2, rwkv_k_k, rwkv_k_a, rwkv_r_k, rwkv_ln_w, rwkv_ln_b, rwkv_wo, gla_w_in, gla_w_gk2, gla_b_gk, gla_norm, gla_wo, conv_w_in, conv_w, conv_wo, pool_w, pool_scale):
    assert x_prompt.shape[1] % CHUNK == 0 and x_sample.shape[1] == 1
    wu = ffn_up.astype(BF16)
    wd = ffn_down.astype(BF16)
    rw = dict(mu=rwkv_mu[0], w_rkv=rwkv_w_rkv[0], w0=rwkv_w0[0], w1=rwkv_w1[0], w2=rwkv_w2[0], a0=rwkv_a0[0],
              a1=rwkv_a1[0], a2=rwkv_a2[0], g1=rwkv_g1[0], g2=rwkv_g2[0], k_k=rwkv_k_k[0], k_a=rwkv_k_a[0],
              r_k=rwkv_r_k[0], ln_w=rwkv_ln_w[0], ln_b=rwkv_ln_b[0], wo=rwkv_wo[0])
    gl = dict(w_in=gla_w_in[0], w_gk2=gla_w_gk2[0], b_gk=gla_b_gk[0], norm=gla_norm[0], wo=gla_wo[0])
    cv = dict(w_in=conv_w_in[0], w=conv_w[0], wo=conv_wo[0])
    po = dict(w=pool_w[0], scale=pool_scale[0])
    shared = (norm_mix, norm_ffn, norm_final, wu, wd, rw, gl, cv, po)
    y_p, wkv_p, sh_p, gla_p, conv_p, pool_p = _trunk_prompt(x_prompt, *shared)
    y_s, wkv_s, sh_s, gla_s, conv_s, pool_s = _trunk_decode(
        x_sample, state_rwkv_wkv, state_rwkv_shift, state_gla, state_conv, state_pool, *shared)
    return (y_p, y_s, wkv_p, wkv_s, sh_p, sh_s, gla_p, gla_s, conv_p, conv_s, pool_p, pool_s)
```

```python
import functools
import math

import jax
import jax.numpy as jnp
from jax import lax
from jax.experimental import pallas as pl
from jax.experimental.pallas import tpu as pltpu

F32 = jnp.float32
BF16 = jnp.bfloat16

D_MODEL = 1024
D_FF = 4 * D_MODEL
NORM_EPS = 1e-6
PAST_LEN = 16384

RWKV_HEAD = 64
RWKV_H = D_MODEL // RWKV_HEAD
RWKV_GN_EPS = 64e-5
RWKV_QUAD = 4

GLA_H = 4
GLA_DK_TOT = D_MODEL // 2
GLA_DV_TOT = D_MODEL
GLA_DK = GLA_DK_TOT // GLA_H
GLA_DV = GLA_DV_TOT // GLA_H
GLA_LR = 16
GLA_NORMALIZER = 16.0
GLA_NORM_EPS = 1e-5
GLA_SUB = 16

CONV_W = 3
POOL_WINDOWS = (2, 4, 8, 16)
POOL_G = D_MODEL // len(POOL_WINDOWS)
POOL_BUF = max(POOL_WINDOWS) - 1

LANES = 128
SUBLANES = 8
MXU_DIM = 256
VMEM_LIMIT_BYTES = 56 * 1024 * 1024

CHUNK = 64
FFN_COLS = 1024
SUBTILES = 2
HI = lax.Precision.HIGHEST


def _const_spec(shape):
    nd = len(shape)
    return pl.BlockSpec(shape, lambda *_: (0,) * nd, pipeline_mode=pl.Buffered(1))


def _params(sem):
    return pltpu.CompilerParams(dimension_semantics=sem, vmem_limit_bytes=VMEM_LIMIT_BYTES)


def _rms(x, g, eps):
    return x * lax.rsqrt(jnp.mean(x * x, axis=-1, keepdims=True) + eps) * g


def _bdot(a, w):
    return jnp.dot(a.astype(BF16), w, preferred_element_type=F32)


def _mm(a, b, prec=None):
    return jnp.dot(a, b, preferred_element_type=F32, precision=prec)


def _nt(a, b, prec=None):
    return lax.dot_general(a, b, (((1,), (1,)), ((), ())), preferred_element_type=F32, precision=prec)


def _tn(a, b, prec=None):
    return lax.dot_general(a, b, (((0,), (0,)), ((), ())), preferred_element_type=F32, precision=prec)


def _iota(shape, dim):
    return lax.broadcasted_iota(jnp.int32, shape, dim)


def _group_sum(x, group):
    shift = group.bit_length() - 1
    bd = (jnp.right_shift(_iota((MXU_DIM, MXU_DIM), 0), shift)
          == jnp.right_shift(_iota((MXU_DIM, MXU_DIM), 1), shift)).astype(BF16)
    hi = x.astype(BF16)
    lo = (x - hi.astype(F32)).astype(BF16)
    cols = []
    for c in range(x.shape[1] // MXU_DIM):
        sl = slice(c * MXU_DIM, (c + 1) * MXU_DIM)
        cols.append(_mm(hi[:, sl], bd) + _mm(lo[:, sl], bd))
    return jnp.concatenate(cols, axis=1)


def _softplus(x):
    return jnp.maximum(x, 0.0) + jnp.log(1.0 + jnp.exp(-jnp.abs(x)))


def _row_tile(m, want):
    t = min(want, m)
    assert m % t == 0, (m, t)
    return t


def _mlp(h, g_ref, wu_ref, wd_ref, gf_ref=None):
    x = _rms(h, g_ref[...], NORM_EPS).astype(BF16)
    acc = h
    for c in range(D_FF // FFN_COLS):
        sl = slice(c * FFN_COLS, (c + 1) * FFN_COLS)
        a = jnp.dot(x, wu_ref[:, sl], preferred_element_type=F32)
        a = jnp.square(jnp.maximum(a, 0.0)).astype(BF16)
        acc = acc + jnp.dot(a, wd_ref[sl, :], preferred_element_type=F32)
    if gf_ref is not None:
        acc = _rms(acc, gf_ref[...], NORM_EPS)
    return acc


def _mlp_operands(nf, wu, wd, layer, final_g=None):
    layer_spec = lambda shape: pl.BlockSpec((None,) + shape[1:], lambda *_: (layer, 0, 0),
                                            pipeline_mode=pl.Buffered(1))
    ins = [nf.reshape(1, D_MODEL), wu, wd]
    specs = [_const_spec((1, D_MODEL)), layer_spec(wu.shape), layer_spec(wd.shape)]
    if final_g is not None:
        ins.append(final_g.reshape(1, D_MODEL))
        specs.append(_const_spec((1, D_MODEL)))
    return ins, specs


def _ffn_kernel(h_ref, g_ref, wu_ref, wd_ref, *rest):
    *gf_ref, o_ref = rest
    o_ref[...] = _mlp(h_ref[...], g_ref, wu_ref, wd_ref, *gf_ref)


def _ffn(h, g, wu, wd, layer, final_g=None, tm=512):
    m = h.shape[0]
    tm = _row_tile(m, tm)
    row = pl.BlockSpec((tm, D_MODEL), lambda i: (i, 0))
    mlp_ins, mlp_specs = _mlp_operands(g, wu, wd, layer, final_g)
    return pl.pallas_call(
        _ffn_kernel,
        grid=(m // tm,),
        in_specs=[row] + mlp_specs,
        out_specs=row,
        out_shape=jax.ShapeDtypeStruct((m, D_MODEL), F32),
        compiler_params=_params(("parallel",)),
        name="ffn",
    )(h, *mlp_ins)


def _rwkv_pre_math(u, prev, mu_ref, wrkv_ref, w0_ref, w1_ref, w2_ref, a0_ref, a1_ref, a2_ref,
                   g1_ref, g2_ref, kk_ref, ka_ref, rk_ref):
    dx = prev - u

    def mix(i):
        return (u + dx * mu_ref[i:i + 1, :]).astype(BF16)

    r = _bdot(mix(0), wrkv_ref[0])
    k = _bdot(mix(2), wrkv_ref[1])
    v = _bdot(mix(3), wrkv_ref[2])
    wl = w0_ref[...] + _bdot(jnp.tanh(_bdot(mix(1), w1_ref[...])), w2_ref[...])
    lw = jax.nn.sigmoid(wl) * (-math.exp(-0.5))
    a = jax.nn.sigmoid(a0_ref[...] + _bdot(_bdot(mix(4), a1_ref[...]), a2_ref[...]))
    g = _bdot(jax.nn.sigmoid(_bdot(mix(5), g1_ref[...])), g2_ref[...])
    kk = k * kk_ref[...]
    kk = kk * lax.rsqrt(jnp.maximum(_group_sum(kk * kk, RWKV_HEAD), 1e-24))
    k = k * (1.0 + (a - 1.0) * ka_ref[...])
    bonus = _group_sum(r * k * rk_ref[...], RWKV_HEAD) * v
    return r, lw, k, v, kk, kk * a, g, bonus


_N_RWKV_W = 13


def _rwkv_pre_prompt_kernel(h_ref, nm_ref, *rest):
    w = rest[:_N_RWKV_W]
    outs = rest[_N_RWKV_W:_N_RWKV_W + 8]
    sh_ref, carry_ref = rest[_N_RWKV_W + 8:]
    tm = h_ref.shape[0]

    @pl.when(pl.program_id(1) == 0)
    def _():
        carry_ref[...] = jnp.zeros_like(carry_ref)

    u = _rms(h_ref[...], nm_ref[...], NORM_EPS)
    prev = jnp.where(_iota(u.shape, 0) == 0, carry_ref[0:1, :], pltpu.roll(u, 1, axis=0))
    last = u[tm - 1:tm, :]
    carry_ref[0:1, :] = last
    sh_ref[...] = last
    for o_ref, val in zip(outs, _rwkv_pre_math(u, prev, *w)):
        o_ref[...] = val


def _rwkv_pre_decode_kernel(h_ref, prev_ref, nm_ref, *rest):
    w = rest[:_N_RWKV_W]
    outs = rest[_N_RWKV_W:_N_RWKV_W + 8]
    (sh_ref,) = rest[_N_RWKV_W + 8:]
    u = _rms(h_ref[...], nm_ref[...], NORM_EPS)
    sh_ref[...] = u
    vals = _rwkv_pre_math(u, prev_ref[...], *w)
    for o_ref, val in zip(outs[:6], vals[:6]):
        o_ref[...] = val.T
    for o_ref, val in zip(outs[6:], vals[6:]):
        o_ref[...] = val


def _rwkv_weights(p):
    row = lambda x: x.reshape(1, D_MODEL)
    return [p["mu"], p["w_rkv"].astype(BF16), row(p["w0"]), p["w1"].astype(BF16), p["w2"].astype(BF16),
            row(p["a0"]), p["a1"].astype(BF16), p["a2"].astype(BF16), p["g1"].astype(BF16),
            p["g2"].astype(BF16), row(p["k_k"]), row(p["k_a"]), row(p["r_k"])]


def _rwkv_pre_prompt(h, nm, p, tm=256):
    b, l, _ = h.shape
    tm = _row_tile(l, tm)
    ws = _rwkv_weights(p)
    tile = pl.BlockSpec((None, tm, D_MODEL), lambda i, j: (i, j, 0))
    outs = pl.pallas_call(
        _rwkv_pre_prompt_kernel,
        grid=(b, l // tm),
        in_specs=[tile, _const_spec((1, D_MODEL))] + [_const_spec(w.shape) for w in ws],
        out_specs=[tile] * 8 + [pl.BlockSpec((None, 1, D_MODEL), lambda i, j: (i, 0, 0))],
        out_shape=[jax.ShapeDtypeStruct((b, l, D_MODEL), F32)] * 8
        + [jax.ShapeDtypeStruct((b, 1, D_MODEL), F32)],
        scratch_shapes=[pltpu.VMEM((SUBLANES, D_MODEL), F32)],
        compiler_params=_params(("parallel", "arbitrary")),
        name="rwkv_pre_prompt",
    )(h, nm.reshape(1, D_MODEL), *ws)
    return outs[:8], outs[8].reshape(b, D_MODEL)


def _rwkv_pre_decode(h, prev, nm, p):
    m = h.shape[0]
    ws = _rwkv_weights(p)
    full = pl.BlockSpec((m, D_MODEL), lambda i: (0, 0))
    full_t = pl.BlockSpec((D_MODEL, m), lambda i: (0, 0))
    outs = pl.pallas_call(
        _rwkv_pre_decode_kernel,
        grid=(1,),
        in_specs=[full, full, _const_spec((1, D_MODEL))] + [_const_spec(w.shape) for w in ws],
        out_specs=[full_t] * 6 + [full] * 3,
        out_shape=[jax.ShapeDtypeStruct((D_MODEL, m), F32)] * 6 + [jax.ShapeDtypeStruct((m, D_MODEL), F32)] * 3,
        compiler_params=_params(("arbitrary",)),
        name="rwkv_pre_decode",
    )(h, prev, nm.reshape(1, D_MODEL), *ws)
    return outs[:8], outs[8]


def _rwkv_rec_prompt_kernel(r_ref, lw_ref, k_ref, v_ref, kk_ref, bb_ref, y_ref, so_ref, s_scr):
    tc = r_ref.shape[0]
    qw = RWKV_QUAD * RWKV_HEAD
    n_quads = D_MODEL // qw
    assert RWKV_QUAD * CHUNK == qw

    @pl.when(pl.program_id(1) == 0)
    def _():
        s_scr[...] = jnp.zeros_like(s_scr)

    ltri = (_iota((CHUNK, CHUNK), 0) >= _iota((CHUNK, CHUNK), 1)).astype(F32)
    t_idx, s_idx = _iota((CHUNK, qw), 0), _iota((CHUNK, qw), 1) & (CHUNK - 1)
    strict = t_idx > s_idx
    incl = t_idx >= s_idx
    eye = (t_idx == s_idx).astype(F32)
    blocks = jnp.right_shift(_iota((qw, qw), 0), 6) == jnp.right_shift(_iota((qw, qw), 1), 6)

    def bdiag(z):
        zb = z.astype(BF16)
        return jnp.where(blocks, jnp.concatenate([zb] * RWKV_QUAD, axis=0), jnp.zeros((), BF16))

    quads = range(n_quads)
    sls = [slice(q * qw, (q + 1) * qw) for q in quads]
    cc = CHUNK

    n_chunks = tc // CHUNK
    items = [(c, q) for c in range(n_chunks) for q in quads]
    every = range(len(items))

    scaled = []
    for c in range(n_chunks):
        rows = slice(c * CHUNK, (c + 1) * CHUNK)
        lw = lw_ref[rows, :]
        cum = _mm(ltri, lw, HI)
        g_in = jnp.exp(cum)
        g_inv = jnp.exp(-cum)
        g_last = g_in[CHUNK - 1:CHUNK, :]
        scaled.append(dict(
            rt=r_ref[rows, :] * g_in, at=-kk_ref[rows, :] * jnp.exp(cum - lw),
            bt=bb_ref[rows, :] * g_inv, kt=k_ref[rows, :] * g_inv, v=v_ref[rows, :], g_last=g_last))

    def part(name, i):
        c, q = items[i]
        return scaled[c][name][:, sls[q]]

    ar = [jnp.concatenate([part("at", i), part("rt", i)], axis=0).astype(BF16) for i in every]
    bk4 = [jnp.concatenate([bdiag(part("bt", i)), bdiag(part("kt", i))], axis=0) for i in every]
    gram = [_nt(ar[i], bk4[i]) for i in every]
    a_ab = [jnp.where(strict, gram[i][:cc, :qw], 0.0) for i in every]
    a_k = [jnp.concatenate([jnp.where(strict, gram[i][:cc, qw:], 0.0),
                            jnp.where(incl, gram[i][cc:, qw:], 0.0)], axis=0).astype(BF16) for i in every]
    a_rb = [jnp.where(incl, gram[i][cc:, :qw], 0.0).astype(BF16) for i in every]
    akv = [_mm(a_k[i], bdiag(part("v", i))) for i in every]
    inv = [eye + a_ab[i] for i in every]
    pw = [_mm(a_ab[i].astype(BF16), bdiag(a_ab[i])) for i in every]
    for _ in range(1, CHUNK.bit_length() - 2):
        both = [_mm(jnp.concatenate([pw[i], inv[i]], axis=0).astype(BF16), bdiag(pw[i])) for i in every]
        inv = [inv[i] + both[i][cc:] for i in every]
        pw = [both[i][:cc] for i in every]
    inv = [(inv[i] + _mm(inv[i].astype(BF16), bdiag(pw[i]))).astype(BF16) for i in every]
    tt = [_mm(inv[i], jnp.concatenate([bdiag(part("at", i)), bdiag(akv[i][:cc])], axis=1)) for i in every]
    ta = [tt[i][:, :qw].astype(BF16) for i in every]
    tav = [tt[i][:, qw:] for i in every]
    bkg = [(jnp.concatenate([part("bt", i), part("kt", i)], axis=0) * part("g_last", i)).astype(BF16)
           for i in every]
    w = [jnp.where(blocks, _tn(ta[i], bkg[i][:cc]), 0.0).astype(BF16) for i in every]
    n = [jnp.where(blocks, _tn(jnp.concatenate([tav[i], part("v", i)], axis=0).astype(BF16), bkg[i]), 0.0)
         for i in every]

    s_in = []
    state = [s_scr[q] for q in quads]
    for c in range(n_chunks):
        s_b = [state[q].astype(BF16) for q in quads]
        s_in.extend(s_b)
        state = [state[q] * scaled[c]["g_last"][:, sls[q]] + _mm(s_b[q], w[c * n_quads + q]) + n[c * n_quads + q]
                 for q in quads]
    for q in quads:
        s_scr[q] = state[q]

    xs = [_nt(jnp.concatenate([ta[i], ar[i][cc:]], axis=0), s_in[i]) for i in every]
    u = [xs[i][:cc] + tav[i] for i in every]
    for i in every:
        c, q = items[i]
        y_ref[c * CHUNK:(c + 1) * CHUNK, sls[q]] = xs[i][cc:] + akv[i][cc:] + _mm(a_rb[i], bdiag(u[i]))

    @pl.when(pl.program_id(1) == pl.num_programs(1) - 1)
    def _():
        for hd in range(RWKV_H):
            q, o = divmod(hd, RWKV_QUAD)
            so_ref[hd] = s_scr[q][o * RWKV_HEAD:(o + 1) * RWKV_HEAD, o * RWKV_HEAD:(o + 1) * RWKV_HEAD]


def _rwkv_rec_prompt(r, lw, k, v, kk, bb, tc=256):
    b, l, _ = r.shape
    tc = _row_tile(l, tc)
    tile = pl.BlockSpec((None, tc, D_MODEL), lambda i, j: (i, j, 0))
    st = (RWKV_H, RWKV_HEAD, RWKV_HEAD)
    return pl.pallas_call(
        _rwkv_rec_prompt_kernel,
        grid=(b, l // tc),
        in_specs=[tile] * 6,
        out_specs=[tile, pl.BlockSpec((None,) + st, lambda i, j: (i, 0, 0, 0))],
        out_shape=[jax.ShapeDtypeStruct((b, l, D_MODEL), F32), jax.ShapeDtypeStruct((b,) + st, F32)],
        scratch_shapes=[pltpu.VMEM((RWKV_H // RWKV_QUAD, RWKV_QUAD * RWKV_HEAD, RWKV_QUAD * RWKV_HEAD), F32)],
        compiler_params=_params(("parallel", "arbitrary")),
        name="rwkv_rec_prompt",
    )(r, lw, k, v, kk, bb)


def _rwkv_rec_decode_kernel(r_ref, lw_ref, k_ref, v_ref, kk_ref, bb_ref, s_ref, y_ref, so_ref):
    w = jnp.exp(lw_ref[...])
    a = -kk_ref[...]
    b, k, r = bb_ref[...], k_ref[...], r_ref[...]

    def row(vi, carry):
        s0 = s_ref[vi]
        sa = jnp.sum(s0 * a, axis=0, keepdims=True)
        s1 = s0 * w + sa * b + v_ref[pl.ds(vi, 1), :] * k
        so_ref[vi] = s1
        y_ref[pl.ds(vi, 1), :] = jnp.sum(s1 * r, axis=0, keepdims=True)
        return carry

    lax.fori_loop(0, RWKV_HEAD, row, 0, unroll=8)


def _rwkv_rec_decode(r, lw, k, v, kk, bb, s):
    m = r.shape[1]
    vec = pl.BlockSpec((RWKV_HEAD, m), lambda i: (i, 0))
    st = pl.BlockSpec((None, RWKV_HEAD, RWKV_HEAD, m), lambda i: (i, 0, 0, 0))
    return pl.pallas_call(
        _rwkv_rec_decode_kernel,
        grid=(RWKV_H,),
        in_specs=[vec] * 6 + [st],
        out_specs=[vec, st],
        out_shape=[jax.ShapeDtypeStruct((D_MODEL, m), F32), jax.ShapeDtypeStruct(s.shape, F32)],
        compiler_params=_params(("parallel",)),
        name="rwkv_rec_decode",
    )(r, lw, k, v, kk, bb, s)


def _rwkv_post_kernel(h_ref, y_ref, bo_ref, g_ref, lnw_ref, lnb_ref, wo_ref, nf_ref, wu_ref, wd_ref, o_ref):
    y = y_ref[...]
    d = y - _group_sum(y, RWKV_HEAD) * (1.0 / RWKV_HEAD)
    var = _group_sum(d * d, RWKV_HEAD) * (1.0 / RWKV_HEAD)
    yn = d * lax.rsqrt(var + RWKV_GN_EPS) * lnw_ref[...] + lnb_ref[...]
    h = h_ref[...] + _bdot((yn + bo_ref[...]) * g_ref[...], wo_ref[...])
    o_ref[...] = _mlp(h, nf_ref, wu_ref, wd_ref)


def _rwkv_post_ffn(h, y, bonus, g, p, nf, wu, wd, layer, tm=512):
    m = h.shape[0]
    tm = _row_tile(m, tm)
    row = pl.BlockSpec((tm, D_MODEL), lambda i: (i, 0))
    vec = _const_spec((1, D_MODEL))
    mlp_ins, mlp_specs = _mlp_operands(nf, wu, wd, layer)
    return pl.pallas_call(
        _rwkv_post_kernel,
        grid=(m // tm,),
        in_specs=[row] * 4 + [vec, vec, _const_spec((D_MODEL, D_MODEL))] + mlp_specs,
        out_specs=row,
        out_shape=jax.ShapeDtypeStruct((m, D_MODEL), F32),
        compiler_params=_params(("parallel",)),
        name="rwkv_post_ffn",
    )(h, y, bonus, g, p["ln_w"].reshape(1, D_MODEL), p["ln_b"].reshape(1, D_MODEL), p["wo"].astype(BF16),
      *mlp_ins)


def _gla_pre_kernel(h_ref, nm_ref, win_ref, wgl_ref, wgk_ref, bgk_ref, q_ref, k_ref, v_ref, g_ref, gk_ref):
    u = _rms(h_ref[...], nm_ref[...], NORM_EPS).astype(BF16)
    z = _nt(u, win_ref[...])
    q_ref[...] = z[:, :GLA_DK_TOT] * (GLA_DK ** -0.5)
    k_ref[...] = z[:, GLA_DK_TOT:2 * GLA_DK_TOT]
    v_ref[...] = z[:, 2 * GLA_DK_TOT:2 * GLA_DK_TOT + GLA_DV_TOT]
    g_ref[...] = z[:, 2 * GLA_DK_TOT + GLA_DV_TOT:]
    gl = jnp.dot(u, wgl_ref[...], preferred_element_type=F32)
    pre = _bdot(gl, wgk_ref[...]) + bgk_ref[...]
    gk_ref[...] = -_softplus(-pre) * (1.0 / GLA_NORMALIZER)


def _gla_pre(h, nm, p, tm=512):
    m = h.shape[0]
    tm = _row_tile(m, tm)
    n_main = 2 * GLA_DK_TOT + 2 * GLA_DV_TOT
    w_main =p["w_in"].T[:n_main].astype(BF16)
    w_gl = jnp.pad(p["w_in"][:, n_main:].astype(BF16), ((0, 0), (0, LANES - GLA_LR)))
    w_gk = jnp.pad(p["w_gk2"].astype(BF16), ((0, LANES - GLA_LR), (0, 0)))
    row = lambda n: pl.BlockSpec((tm, n), lambda i: (i, 0))
    widths = (GLA_DK_TOT, GLA_DK_TOT, GLA_DV_TOT, GLA_DV_TOT, GLA_DK_TOT)
    return pl.pallas_call(
        _gla_pre_kernel,
        grid=(m // tm,),
        in_specs=[row(D_MODEL), _const_spec((1, D_MODEL)), _const_spec(w_main.shape), _const_spec(w_gl.shape),
                  _const_spec(w_gk.shape), _const_spec((1, GLA_DK_TOT))],
        out_specs=[row(n) for n in widths],
        out_shape=[jax.ShapeDtypeStruct((m, n), F32) for n in widths],
        compiler_params=_params(("parallel",)),
        name="gla_pre",
    )(h, nm.reshape(1, D_MODEL), w_main, w_gl, w_gk, p["b_gk"].reshape(1, GLA_DK_TOT))


def _gla_rec_prompt_kernel(q_ref, k_ref, v_ref, gk_ref, o_ref, so_ref, s_scr):
    tc = q_ref.shape[0]

    @pl.when(pl.program_id(1) == 0)
    def _():
        s_scr[...] = jnp.zeros_like(s_scr)

    ri, ci = _iota((CHUNK, CHUNK), 0), _iota((CHUNK, CHUNK), 1)
    incl = ri >= ci
    ltri = incl.astype(F32)
    n_sub = CHUNK // GLA_SUB
    sub_of_row = jnp.right_shift(_iota((CHUNK, 1), 0), GLA_SUB.bit_length() - 1)

    heads = range(GLA_H)
    ks = [slice(hd * GLA_DK, (hd + 1) * GLA_DK) for hd in heads]
    vs = [slice(hd * GLA_DV, (hd + 1) * GLA_DV) for hd in heads]
    n_chunks = tc // CHUNK
    items = [(c, hd) for c in range(n_chunks) for hd in heads]

    prep = []
    for c in range(n_chunks):
        rows = slice(c * CHUNK, (c + 1) * CHUNK)
        cum = _mm(ltri, gk_ref[rows, :], HI)
        q, k = q_ref[rows, :], k_ref[rows, :]
        last = cum[CHUNK - 1:CHUNK, :]
        refs = [jnp.zeros_like(last)] + [cum[i * GLA_SUB - 1:i * GLA_SUB, :] for i in range(1, n_sub)]
        ref_of_row = refs[0]
        for i in range(1, n_sub):
            ref_of_row = jnp.where(sub_of_row >= i, refs[i], ref_of_row)
        q_in = (q * jnp.exp(cum - ref_of_row)).astype(BF16)
        k_in = [(k * jnp.exp(jnp.where(sub_of_row <= i, refs[i] - cum, 0.0))).astype(BF16) for i in range(n_sub)]
        prep.append(dict(q_in=q_in, k_in=k_in, qe=(q * jnp.exp(cum)).astype(BF16),
                         kh=(k * jnp.exp(last - cum)).astype(BF16), g_last=jnp.exp(last),
                         v=v_ref[rows, :].astype(BF16)))

    def scores(c, hd):
        p = prep[c]
        blocks = [_nt(p["q_in"][i * GLA_SUB:(i + 1) * GLA_SUB, ks[hd]], p["k_in"][i][:, ks[hd]])
                  for i in range(n_sub)]
        return jnp.where(incl, jnp.concatenate(blocks, axis=0), 0.0).astype(BF16)

    att = [scores(c, hd) for c, hd in items]
    intra = [_mm(att[i], prep[c]["v"][:, vs[hd]]) for i, (c, hd) in enumerate(items)]
    kv = [_tn(prep[c]["v"][:, vs[hd]], prep[c]["kh"][:, ks[hd]]) for c, hd in items]

    s_in = []
    state = [s_scr[hd] for hd in heads]
    for c in range(n_chunks):
        s_in.extend(state[hd].astype(BF16) for hd in heads)
        state = [state[hd] * prep[c]["g_last"][:, ks[hd]] + kv[c * GLA_H + hd] for hd in heads]
    for hd in heads:
        s_scr[hd] = state[hd]

    for i, (c, hd) in enumerate(items):
        o_ref[c * CHUNK:(c + 1) * CHUNK, vs[hd]] = _nt(prep[c]["qe"][:, ks[hd]], s_in[i]) + intra[i]

    @pl.when(pl.program_id(1) == pl.num_programs(1) - 1)
    def _():
        for hd in range(GLA_H):
            so_ref[hd] = s_scr[hd].T


def _gla_rec_prompt(q, k, v, gk, tc=256):
    b, l, _ = q.shape
    tc = _row_tile(l, tc)
    tile = lambda n: pl.BlockSpec((None, tc, n), lambda i, j: (i, j, 0))
    st = (GLA_H, GLA_DK, GLA_DV)
    return pl.pallas_call(
        _gla_rec_prompt_kernel,
        grid=(b, l // tc),
        in_specs=[tile(GLA_DK_TOT), tile(GLA_DK_TOT), tile(GLA_DV_TOT), tile(GLA_DK_TOT)],
        out_specs=[tile(GLA_DV_TOT), pl.BlockSpec((None,) + st, lambda i, j: (i, 0, 0, 0))],
        out_shape=[jax.ShapeDtypeStruct((b, l, GLA_DV_TOT), F32), jax.ShapeDtypeStruct((b,) + st, F32)],
        scratch_shapes=[pltpu.VMEM((GLA_H, GLA_DV, GLA_DK), F32)],
        compiler_params=_params(("parallel", "arbitrary")),
        name="gla_rec_prompt",
    )(q, k, v, gk)


def _gla_rec_decode_kernel(q_ref, k_ref, v_ref, gk_ref, s_ref, o_ref, so_ref):
    eye = (_iota((GLA_DK, GLA_DK), 0) == _iota((GLA_DK, GLA_DK), 1)).astype(F32)[None]

    def col(x):
        return jnp.sum(eye * x, axis=-1, keepdims=True)

    for hd in range(GLA_H):
        hs = slice(hd, hd + 1)
        s1 = s_ref[:, hd] * col(jnp.exp(gk_ref[:, hs, :])) + col(k_ref[:, hs, :]) * v_ref[:, hs, :]
        so_ref[:, hd] = s1
        o_ref[:, hs, :] = jnp.sum(col(q_ref[:, hs, :]) * s1, axis=1, keepdims=True)


def _gla_rec_decode(q, k, v, gk, s, bblk=8):
    m = q.shape[0]
    kvec = pl.BlockSpec((bblk, GLA_H, GLA_DK), lambda i: (i, 0, 0))
    vvec = pl.BlockSpec((bblk, GLA_H, GLA_DV), lambda i: (i, 0, 0))
    st = pl.BlockSpec((bblk, GLA_H, GLA_DK, GLA_DV), lambda i: (i, 0, 0, 0))
    hk = lambda x: x.reshape(m, GLA_H, GLA_DK)
    o, s_new = pl.pallas_call(
        _gla_rec_decode_kernel,
        grid=(m // bblk,),
        in_specs=[kvec, kvec, vvec, kvec, st],
        out_specs=[vvec, st],
        out_shape=[jax.ShapeDtypeStruct((m, GLA_H, GLA_DV), F32), jax.ShapeDtypeStruct(s.shape, F32)],
        compiler_params=_params(("parallel",)),
        name="gla_rec_decode",
    )(hk(q), hk(k), v.reshape(m, GLA_H, GLA_DV), hk(gk), s)
    return o.reshape(m, GLA_DV_TOT), s_new


def _gla_post_kernel(h_ref, o_ref_in, g_ref, nw_ref, wo_ref, nf_ref, wu_ref, wd_ref, out_ref):
    o = o_ref_in[...]
    parts = []
    for hd in range(GLA_H):
        oh = o[:, hd * GLA_DV:(hd + 1) * GLA_DV]
        parts.append(oh * lax.rsqrt(jnp.mean(oh * oh, axis=-1, keepdims=True) + GLA_NORM_EPS))
    on = jnp.concatenate(parts, axis=1) * nw_ref[...]
    g = g_ref[...]
    h = h_ref[...] + _bdot(on * (g * jax.nn.sigmoid(g)), wo_ref[...])
    out_ref[...] = _mlp(h, nf_ref, wu_ref, wd_ref)


def _gla_post_ffn(h, o, g, p, nf, wu, wd, layer, tm=512):
    m = h.shape[0]
    tm = _row_tile(m, tm)
    row = pl.BlockSpec((tm, D_MODEL), lambda i: (i, 0))
    mlp_ins, mlp_specs = _mlp_operands(nf, wu, wd, layer)
    return pl.pallas_call(
        _gla_post_kernel,
        grid=(m // tm,),
        in_specs=[row] * 3 + [_const_spec((1, D_MODEL)), _const_spec((D_MODEL, D_MODEL))] + mlp_specs,
        out_specs=row,
        out_shape=jax.ShapeDtypeStruct((m, D_MODEL), F32),
        compiler_params=_params(("parallel",)),
        name="gla_post_ffn",
    )(h, o, g, jnp.tile(p["norm"], GLA_H).reshape(1, D_MODEL), p["wo"].astype(BF16), *mlp_ins)


def _conv_gates(h_ref, nm_ref, win_ref):
    u = _rms(h_ref[...], nm_ref[...], NORM_EPS).astype(BF16)
    z = jnp.dot(u, win_ref[...], preferred_element_type=F32)
    return z[:, :D_MODEL], z[:, D_MODEL:2 * D_MODEL] * z[:, 2 * D_MODEL:]


def _conv_prompt_kernel(h_ref, nm_ref, win_ref, cw_ref, wo_ref, o_ref, st_ref, carry_ref):
    tm = h_ref.shape[0]

    @pl.when(pl.program_id(1) == 0)
    def _():
        carry_ref[...] = jnp.zeros_like(carry_ref)

    ts = tm // SUBTILES
    subs = [slice(i * ts, (i + 1) * ts) for i in range(SUBTILES)]
    gates = [_conv_gates(h_ref.at[sl], nm_ref, win_ref) for sl in subs]
    row = _iota((ts, D_MODEL), 0)
    prev2, prev1 = carry_ref[0:1, :], carry_ref[1:2, :]
    gated = []
    for g_b, zc in gates:
        z1 = jnp.where(row == 0, prev1, pltpu.roll(zc, 1, axis=0))
        z2 = jnp.where(row == 0, prev2, jnp.where(row == 1, prev1, pltpu.roll(zc, 2, axis=0)))
        gated.append((g_b * (cw_ref[0:1, :] * z2 + cw_ref[1:2, :] * z1 + cw_ref[2:3, :] * zc)).astype(BF16))
        prev2, prev1 = zc[ts - 2:ts - 1, :], zc[ts - 1:ts, :]
    for sl, x in zip(subs, gated):
        o_ref[sl, :] = h_ref[sl, :] + jnp.dot(x, wo_ref[...], preferred_element_type=F32)
    tail = gates[-1][1][ts - (CONV_W - 1):, :]
    carry_ref[0:CONV_W - 1, :] = tail
    st_ref[...] = tail


def _conv_prompt(h, nm, p, tm=1024):
    b, l, _ = h.shape
    tm = _row_tile(l, tm)
    tile = pl.BlockSpec((None, tm, D_MODEL), lambda i, j: (i, j, 0))
    return pl.pallas_call(
        _conv_prompt_kernel,
        grid=(b, l // tm),
        in_specs=[tile, _const_spec((1, D_MODEL)), _const_spec((D_MODEL, 3 * D_MODEL)),
                  _const_spec((CONV_W, D_MODEL)), _const_spec((D_MODEL, D_MODEL))],
        out_specs=[tile, pl.BlockSpec((None, CONV_W - 1, D_MODEL), lambda i, j: (i, 0, 0))],
        out_shape=[jax.ShapeDtypeStruct((b, l, D_MODEL), F32),
                   jax.ShapeDtypeStruct((b, CONV_W - 1, D_MODEL), F32)],
        scratch_shapes=[pltpu.VMEM((SUBLANES, D_MODEL), F32)],
        compiler_params=_params(("parallel", "arbitrary")),
        name="conv_prompt",
    )(h, nm.reshape(1, D_MODEL), p["w_in"].astype(BF16), p["w"], p["wo"].astype(BF16))


def _conv_decode_kernel(h_ref, buf_ref, nm_ref, win_ref, cw_ref, wo_ref, o_ref, st_ref):
    g_b, zc = _conv_gates(h_ref, nm_ref, win_ref)
    z2 = buf_ref[:, 0, :]
    z1 = buf_ref[:, 1, :]
    conv = cw_ref[0:1, :] * z2 + cw_ref[1:2, :] * z1 + cw_ref[2:3, :] * zc
    o_ref[...] = h_ref[...] + _bdot(g_b * conv, wo_ref[...])
    st_ref[:, 0, :] = z1
    st_ref[:, 1, :] = zc


def _conv_decode(h, buf, nm, p):
    m = h.shape[0]
    full = pl.BlockSpec((m, D_MODEL), lambda i: (0, 0))
    bufs = pl.BlockSpec(buf.shape, lambda i: (0, 0, 0))
    return pl.pallas_call(
        _conv_decode_kernel,
        grid=(1,),
        in_specs=[full, bufs, _const_spec((1, D_MODEL)), _const_spec((D_MODEL, 3 * D_MODEL)),
                  _const_spec((CONV_W, D_MODEL)), _const_spec((D_MODEL, D_MODEL))],
        out_specs=[full, bufs],
        out_shape=[jax.ShapeDtypeStruct((m, D_MODEL), F32), jax.ShapeDtypeStruct(buf.shape, F32)],
        compiler_params=_params(("arbitrary",)),
        name="conv_decode",
    )(h, buf, nm.reshape(1, D_MODEL), p["w_in"].astype(BF16), p["w"], p["wo"].astype(BF16))


def _pool_project(d_groups, pw_ref, sc_ref):
    ys = [_bdot(d, pw_ref[gi]) for gi, d in enumerate(d_groups)]
    return jnp.concatenate(ys, axis=1) * sc_ref[...]


def _pool_prompt_kernel(h_ref, nm_ref, pw_ref, sc_ref, nf_ref, wu_ref, wd_ref, gf_ref, o_ref, st_ref, carry_ref):
    tm = h_ref.shape[0]
    hist = carry_ref.shape[0]
    j = pl.program_id(1)

    @pl.when(j == 0)
    def _():
        carry_ref[...] = jnp.zeros_like(carry_ref)

    u = _rms(h_ref[...], nm_ref[...], NORM_EPS)
    ext = jnp.concatenate([carry_ref[...], u], axis=0)
    pos = j * tm + _iota((tm, 1), 0)
    d_groups = []
    for gi, w in enumerate(POOL_WINDOWS):
        sl = slice(gi * POOL_G, (gi + 1) * POOL_G)
        s = ext[:, sl]
        span = 1
        while span < w:
            s = s + pltpu.roll(s, span, axis=0)
            span *= 2
        cnt = jnp.minimum(w, pos + 1).astype(F32)
        d_groups.append(s[hist:, :] / cnt - u[:, sl])
    o_ref[...] = _mlp(h_ref[...] + _pool_project(d_groups, pw_ref, sc_ref), nf_ref, wu_ref, wd_ref, gf_ref)
    tail = u[tm - hist:, :]
    carry_ref[...] = tail
    st_ref[...] = tail


def _pool_prompt_ffn(h, nm, p, nf, wu, wd, layer, final_g, tm=512):
    b, l, _ = h.shape
    tm = _row_tile(l, tm)
    hist = POOL_BUF + 1
    tile = pl.BlockSpec((None, tm, D_MODEL), lambda i, j: (i, j, 0))
    mlp_ins, mlp_specs = _mlp_operands(nf, wu, wd, layer, final_g)
    out, st = pl.pallas_call(
        _pool_prompt_kernel,
        grid=(b, l // tm),
        in_specs=[tile, _const_spec((1, D_MODEL)), _const_spec(p["w"].shape), _const_spec((1, D_MODEL))]
        + mlp_specs,
        out_specs=[tile, pl.BlockSpec((None, hist, D_MODEL), lambda i, j: (i, 0, 0))],
        out_shape=[jax.ShapeDtypeStruct((b, l, D_MODEL), F32), jax.ShapeDtypeStruct((b, hist, D_MODEL), F32)],
        scratch_shapes=[pltpu.VMEM((hist, D_MODEL), F32)],
        compiler_params=_params(("parallel", "arbitrary")),
        name="pool_prompt_ffn",
    )(h, nm.reshape(1, D_MODEL), p["w"].astype(BF16), p["scale"].reshape(1, D_MODEL), *mlp_ins)
    return out, st[:, 1:]


def _pool_decode_kernel(h_ref, buf_ref, nm_ref, pw_ref, sc_ref, o_ref, st_ref):
    u = _rms(h_ref[...], nm_ref[...], NORM_EPS)
    d_groups = []
    for gi, w in enumerate(POOL_WINDOWS):
        sl = slice(gi * POOL_G, (gi + 1) * POOL_G)
        s = u[:, sl]
        for i in range(1, w):
            s = s + buf_ref[:, POOL_BUF - i, sl]
        cnt = float(min(w, PAST_LEN + 1))
        d_groups.append(s / cnt - u[:, sl])
    o_ref[...] = h_ref[...] + _pool_project(d_groups, pw_ref, sc_ref)
    st_ref[:, 0:POOL_BUF - 1, :] = buf_ref[:, 1:POOL_BUF, :]
    st_ref[:, POOL_BUF - 1, :] = u


def _pool_decode(h, buf, nm, p):
    m = h.shape[0]
    full = pl.BlockSpec((m, D_MODEL), lambda i: (0, 0))
    bufs = pl.BlockSpec(buf.shape, lambda i: (0, 0, 0))
    return pl.pallas_call(
        _pool_decode_kernel,
        grid=(1,),
        in_specs=[full, bufs, _const_spec((1, D_MODEL)), _const_spec(p["w"].shape), _const_spec((1, D_MODEL))],
        out_specs=[full, bufs],
        out_shape=[jax.ShapeDtypeStruct((m, D_MODEL), F32), jax.ShapeDtypeStruct(buf.shape, F32)],
        compiler_params=_params(("arbitrary",)),
        name="pool_decode",
    )(h, buf, nm.reshape(1, D_MODEL), p["w"].astype(BF16), p["scale"].reshape(1, D_MODEL))


def _trunk_prompt(x, nm, nf, nfin, wu, wd, rw, gl, cv, po):
    b, l, _ = x.shape
    flat = lambda t: t.reshape(b * l, t.shape[-1])
    seq = lambda t: t.reshape(b, l, t.shape[-1])

    (r, lw, k, v, kk, bb, g, bonus), shift = _rwkv_pre_prompt(x, nm[0], rw)
    y, wkv = _rwkv_rec_prompt(r, lw, k, v, kk, bb)
    h = _rwkv_post_ffn(flat(x), flat(y), flat(bonus), flat(g), rw, nf[0], wu, wd, 0)

    q, k, v, g, gk = _gla_pre(h, nm[1], gl)
    o, gla_s = _gla_rec_prompt(seq(q), seq(k), seq(v), seq(gk))
    h = _gla_post_ffn(h, flat(o), g, gl, nf[1], wu, wd, 1)

    h, conv_s = _conv_prompt(seq(h), nm[2], cv)
    h = _ffn(flat(h), nf[2], wu, wd, 2)

    y, pool_s = _pool_prompt_ffn(seq(h), nm[3], po, nf[3], wu, wd, 3, nfin)
    return y, wkv[None], shift[None], gla_s[None], conv_s[None], pool_s[None]


def _trunk_decode(x, st_wkv, st_shift, st_gla, st_conv, st_pool, nm, nf, nfin, wu, wd, rw, gl, cv, po):
    m = x.shape[0]
    h = x.reshape(m, D_MODEL)

    (r, lw, k, v, kk, bb, g, bonus), shift = _rwkv_pre_decode(h, st_shift[0], nm[0], rw)
    y_t, wkv_t = _rwkv_rec_decode(r, lw, k, v, kk, bb, jnp.transpose(st_wkv[0], (1, 2, 3, 0)))
    wkv = jnp.transpose(wkv_t, (3, 0, 1, 2))
    h = _rwkv_post_ffn(h, y_t.T, bonus, g, rw, nf[0], wu, wd, 0)

    q, k, v, g, gk = _gla_pre(h, nm[1], gl)
    o, gla_s = _gla_rec_decode(q, k, v, gk, st_gla[0])
    h = _gla_post_ffn(h, o, g, gl, nf[1], wu, wd, 1)

    h, conv_s = _conv_decode(h, st_conv[0], nm[2], cv)
    h = _ffn(h, nf[2], wu, wd, 2)

    h, pool_s = _pool_decode(h, st_pool[0], nm[3], po)
    y = _ffn(h, nf[3], wu, wd, 3, final_g=nfin)
    return y.reshape(m, 1, D_MODEL), wkv[None], shift[None], gla_s[None], conv_s[None], pool_s[None]


def kernel(x_prompt, x_sample, state_rwkv_wkv, state_rwkv_shift, state_gla, state_conv, state_pool, norm_mix, norm_ffn, norm_final, ffn_up, ffn_down, rwkv_mu, rwkv_w_rkv, rwkv_w0, rwkv_w1, rwkv_w2, rwkv_a0, rwkv_a1, rwkv_a2, rwkv_g1, rwkv_g2, rwkv_k_k, rwkv_k_a, rwkv_r_k, rwkv_ln_w, rwkv_ln_b, rwkv_wo, gla_w_in, gla_w_gk2, gla_b_gk, gla_norm, gla_wo, conv_w_in, conv_w, conv_wo, pool_w, pool_scale):
    assert x_prompt.shape[1] % CHUNK == 0 and x_sample.shape[1] == 1
    wu = ffn_up.astype(BF16)
    wd = ffn_down.astype(BF16)
    rw = dict(mu=rwkv_mu[0], w_rkv=rwkv_w_rkv[0], w0=rwkv_w0[0], w1=rwkv_w1[0], w2=rwkv_w2[0], a0=rwkv_a0[0],
              a1=rwkv_a1[0], a2=rwkv_a2[0], g1=rwkv_g1[0], g2=rwkv_g2[0], k_k=rwkv_k_k[0], k_a=rwkv_k_a[0],
              r_k=rwkv_r_k[0], ln_w=rwkv_ln_w[0], ln_b=rwkv_ln_b[0], wo=rwkv_wo[0])
    gl = dict(w_in=gla_w_in[0], w_gk2=gla_w_gk2[0], b_gk=gla_b_gk[0], norm=gla_norm[0], wo=gla_wo[0])
    cv = dict(w_in=conv_w_in[0], w=conv_w[0], wo=conv_wo[0])
    po = dict(w=pool_w[0], scale=pool_scale[0])
    shared = (norm_mix, norm_ffn, norm_final, wu, wd, rw, gl, cv, po)
    y_p, wkv_p, sh_p, gla_p, conv_p, pool_p = _trunk_prompt(x_prompt, *shared)
    y_s, wkv_s, sh_s, gla_s, conv_s, pool_s = _trunk_decode(
        x_sample, state_rwkv_wkv, state_rwkv_shift, state_gla, state_conv, state_pool, *shared)
    return (y_p, y_s, wkv_p, wkv_s, sh_p, sh_s, gla_p, gla_s, conv_p, conv_s, pool_p, pool_s)
```

```python
import functools
import math

import jax
import jax.numpy as jnp
from jax import lax
from jax.experimental import pallas as pl
from jax.experimental.pallas import tpu as pltpu

F32 = jnp.float32
BF16 = jnp.bfloat16

D_MODEL = 1024
D_FF = 4 * D_MODEL
NORM_EPS = 1e-6
PAST_LEN = 16384

RWKV_HEAD = 64
RWKV_H = D_MODEL // RWKV_HEAD
RWKV_GN_EPS = 64e-5
RWKV_QUAD = 4

GLA_H = 4
GLA_DK_TOT = D_MODEL // 2
GLA_DV_TOT = D_MODEL
GLA_DK = GLA_DK_TOT // GLA_H
GLA_DV = GLA_DV_TOT // GLA_H
GLA_LR = 16
GLA_NORMALIZER = 16.0
GLA_NORM_EPS = 1e-5
GLA_SUB = 16

CONV_W = 3
POOL_WINDOWS = (2, 4, 8, 16)
POOL_G = D_MODEL // len(POOL_WINDOWS)
POOL_BUF = max(POOL_WINDOWS) - 1

LANES = 128
SUBLANES = 8
MXU_DIM = 256
VMEM_LIMIT_BYTES = 56 * 1024 * 1024

CHUNK = 64
FFN_COLS = 1024
SUBTILES = 2


def _const_spec(shape):
    nd = len(shape)
    return pl.BlockSpec(shape, lambda *_: (0,) * nd, pipeline_mode=pl.Buffered(1))


def _params(sem):
    return pltpu.CompilerParams(dimension_semantics=sem, vmem_limit_bytes=VMEM_LIMIT_BYTES)


def _rms(x, g, eps):
    return x * lax.rsqrt(jnp.mean(x * x, axis=-1, keepdims=True) + eps) * g


def _bdot(a, w):
    return jnp.dot(a.astype(BF16), w, preferred_element_type=F32)


def _mm(a, b, prec=None):
    return jnp.dot(a, b, preferred_element_type=F32, precision=prec)


def _nt(a, b, prec=None):
    return lax.dot_general(a, b, (((1,), (1,)), ((), ())), preferred_element_type=F32, precision=prec)


def _tn(a, b, prec=None):
    return lax.dot_general(a, b, (((0,), (0,)), ((), ())), preferred_element_type=F32, precision=prec)


def _iota(shape, dim):
    return lax.broadcasted_iota(jnp.int32, shape, dim)


def _group_sum(x, group, split=False):
    shift = group.bit_length() - 1
    bd = (jnp.right_shift(_iota((MXU_DIM, MXU_DIM), 0), shift)
          == jnp.right_shift(_iota((MXU_DIM, MXU_DIM), 1), shift)).astype(BF16)
    terms = [x.astype(BF16)]
    if split:
        terms.append((x - terms[0].astype(F32)).astype(BF16))
    cols = [sum(_mm(t[:, c * MXU_DIM:(c + 1) * MXU_DIM], bd) for t in terms)
            for c in range(x.shape[1] // MXU_DIM)]
    return jnp.concatenate(cols, axis=1)


def _chunk_cumsum(x):
    c = x.shape[0]
    ltri = (_iota((c, c), 0) >= _iota((c, c), 1)).astype(BF16)
    t0 = x.astype(BF16)
    r1 = x - t0.astype(F32)
    t1 = r1.astype(BF16)
    t2 = (r1 - t1.astype(F32)).astype(BF16)
    return _mm(ltri, t0) + _mm(ltri, t1) + _mm(ltri, t2)


def _softplus(x):
    return jnp.maximum(x, 0.0) + jnp.log(1.0 + jnp.exp(-jnp.abs(x)))


def _row_tile(m, want):
    t = min(want, m)
    assert m % t == 0, (m, t)
    return t


def _mlp(h, g_ref, wu_ref, wd_ref, gf_ref=None):
    x = _rms(h, g_ref[...], NORM_EPS).astype(BF16)
    acc = h
    for c in range(D_FF // FFN_COLS):
        sl = slice(c * FFN_COLS, (c + 1) * FFN_COLS)
        a = jnp.dot(x, wu_ref[:, sl], preferred_element_type=F32)
        a = jnp.square(jnp.maximum(a, 0.0)).astype(BF16)
        acc = acc + jnp.dot(a, wd_ref[sl, :], preferred_element_type=F32)
    if gf_ref is not None:
        acc = _rms(acc, gf_ref[...], NORM_EPS)
    return acc


def _mlp_operands(nf, wu, wd, layer, final_g=None):
    layer_spec = lambda shape: pl.BlockSpec((None,) + shape[1:], lambda *_: (layer, 0, 0),
                                            pipeline_mode=pl.Buffered(1))
    ins = [nf.reshape(1, D_MODEL), wu, wd]
    specs = [_const_spec((1, D_MODEL)), layer_spec(wu.shape), layer_spec(wd.shape)]
    if final_g is not None:
        ins.append(final_g.reshape(1, D_MODEL))
        specs.append(_const_spec((1, D_MODEL)))
    return ins, specs


def _ffn_kernel(h_ref, g_ref, wu_ref, wd_ref, *rest):
    *gf_ref, o_ref = rest
    o_ref[...] = _mlp(h_ref[...], g_ref, wu_ref, wd_ref, *gf_ref)


def _ffn(h, g, wu, wd, layer, final_g=None, tm=512):
    m = h.shape[0]
    tm = _row_tile(m, tm)
    row = pl.BlockSpec((tm, D_MODEL), lambda i: (i, 0))
    mlp_ins, mlp_specs = _mlp_operands(g, wu, wd, layer, final_g)
    return pl.pallas_call(
        _ffn_kernel,
        grid=(m // tm,),
        in_specs=[row] + mlp_specs,
        out_specs=row,
        out_shape=jax.ShapeDtypeStruct((m, D_MODEL), F32),
        compiler_params=_params(("parallel",)),
        name="ffn",
    )(h, *mlp_ins)


def _rwkv_pre_math(u, prev, mu_ref, wrkv_ref, w0_ref, w1_ref, w2_ref, a0_ref, a1_ref, a2_ref,
                   g1_ref, g2_ref, kk_ref, ka_ref, rk_ref):
    dx = prev - u

    def mix(i):
        return (u + dx * mu_ref[i:i + 1, :]).astype(BF16)

    r = _bdot(mix(0), wrkv_ref[0])
    k = _bdot(mix(2), wrkv_ref[1])
    v = _bdot(mix(3), wrkv_ref[2])
    wl = w0_ref[...] + _bdot(jnp.tanh(_bdot(mix(1), w1_ref[...])), w2_ref[...])
    lw = jax.nn.sigmoid(wl) * (-math.exp(-0.5))
    a = jax.nn.sigmoid(a0_ref[...] + _bdot(_bdot(mix(4), a1_ref[...]), a2_ref[...]))
    g = _bdot(jax.nn.sigmoid(_bdot(mix(5), g1_ref[...])), g2_ref[...])
    kk = k * kk_ref[...]
    kk = kk * lax.rsqrt(jnp.maximum(_group_sum(kk * kk, RWKV_HEAD), 1e-24))
    k = k * (1.0 + (a - 1.0) * ka_ref[...])
    bonus = _group_sum(r * k * rk_ref[...], RWKV_HEAD) * v
    return r, lw, k, v, kk, kk * a, g, bonus


_N_RWKV_W = 13


def _rwkv_pre_prompt_kernel(h_ref, nm_ref, *rest):
    w = rest[:_N_RWKV_W]
    outs = rest[_N_RWKV_W:_N_RWKV_W + 8]
    sh_ref, carry_ref = rest[_N_RWKV_W + 8:]
    tm = h_ref.shape[0]

    @pl.when(pl.program_id(1) == 0)
    def _():
        carry_ref[...] = jnp.zeros_like(carry_ref)

    u = _rms(h_ref[...], nm_ref[...], NORM_EPS)
    prev = jnp.where(_iota(u.shape, 0) == 0, carry_ref[0:1, :], pltpu.roll(u, 1, axis=0))
    last = u[tm - 1:tm, :]
    carry_ref[0:1, :] = last
    sh_ref[...] = last
    for o_ref, val in zip(outs, _rwkv_pre_math(u, prev, *w)):
        o_ref[...] = val


def _rwkv_pre_decode_kernel(h_ref, prev_ref, nm_ref, *rest):
    w = rest[:_N_RWKV_W]
    outs = rest[_N_RWKV_W:_N_RWKV_W + 8]
    (sh_ref,) = rest[_N_RWKV_W + 8:]
    u = _rms(h_ref[...], nm_ref[...], NORM_EPS)
    sh_ref[...] = u
    vals = _rwkv_pre_math(u, prev_ref[...], *w)
    for o_ref, val in zip(outs[:6], vals[:6]):
        o_ref[...] = val.T
    for o_ref, val in zip(outs[6:], vals[6:]):
        o_ref[...] = val


def _rwkv_weights(p):
    row = lambda x: x.reshape(1, D_MODEL)
    return [p["mu"], p["w_rkv"].astype(BF16), row(p["w0"]), p["w1"].astype(BF16), p["w2"].astype(BF16),
            row(p["a0"]), p["a1"].astype(BF16), p["a2"].astype(BF16), p["g1"].astype(BF16),
            p["g2"].astype(BF16), row(p["k_k"]), row(p["k_a"]), row(p["r_k"])]


def _rwkv_pre_prompt(h, nm, p, tm=256):
    b, l, _ = h.shape
    tm = _row_tile(l, tm)
    ws = _rwkv_weights(p)
    tile = pl.BlockSpec((None, tm, D_MODEL), lambda i, j: (i, j, 0))
    outs = pl.pallas_call(
        _rwkv_pre_prompt_kernel,
        grid=(b, l // tm),
        in_specs=[tile, _const_spec((1, D_MODEL))] + [_const_spec(w.shape) for w in ws],
        out_specs=[tile] * 8 + [pl.BlockSpec((None, 1, D_MODEL), lambda i, j: (i, 0, 0))],
        out_shape=[jax.ShapeDtypeStruct((b, l, D_MODEL), F32)] * 8
        + [jax.ShapeDtypeStruct((b, 1, D_MODEL), F32)],
        scratch_shapes=[pltpu.VMEM((SUBLANES, D_MODEL), F32)],
        compiler_params=_params(("parallel", "arbitrary")),
        name="rwkv_pre_prompt",
    )(h, nm.reshape(1, D_MODEL), *ws)
    return outs[:8], outs[8].reshape(b, D_MODEL)


def _rwkv_pre_decode(h, prev, nm, p):
    m = h.shape[0]
    ws = _rwkv_weights(p)
    full = pl.BlockSpec((m, D_MODEL), lambda i: (0, 0))
    full_t = pl.BlockSpec((D_MODEL, m), lambda i: (0, 0))
    outs = pl.pallas_call(
        _rwkv_pre_decode_kernel,
        grid=(1,),
        in_specs=[full, full, _const_spec((1, D_MODEL))] + [_const_spec(w.shape) for w in ws],
        out_specs=[full_t] * 6 + [full] * 3,
        out_shape=[jax.ShapeDtypeStruct((D_MODEL, m), F32)] * 6 + [jax.ShapeDtypeStruct((m, D_MODEL), F32)] * 3,
        compiler_params=_params(("arbitrary",)),
        name="rwkv_pre_decode",
    )(h, prev, nm.reshape(1, D_MODEL), *ws)
    return outs[:8], outs[8]


def _rwkv_rec_prompt_kernel(r_ref, lw_ref, k_ref, v_ref, kk_ref, bb_ref, y_ref, so_ref, s_scr):
    tc = r_ref.shape[0]
    qw = RWKV_QUAD * RWKV_HEAD
    n_quads = D_MODEL // qw
    assert RWKV_QUAD * CHUNK == qw

    @pl.when(pl.program_id(1) == 0)
    def _():
        s_scr[...] = jnp.zeros_like(s_scr)

    t_idx, s_idx = _iota((CHUNK, qw), 0), _iota((CHUNK, qw), 1) & (CHUNK - 1)
    strict = t_idx > s_idx
    incl = t_idx >= s_idx
    eye = (t_idx == s_idx).astype(F32)
    blocks = jnp.right_shift(_iota((qw, qw), 0), 6) == jnp.right_shift(_iota((qw, qw), 1), 6)

    def bdiag(z):
        zb = z.astype(BF16)
        return jnp.where(blocks, jnp.concatenate([zb] * RWKV_QUAD, axis=0), jnp.zeros((), BF16))

    quads = range(n_quads)
    sls = [slice(q * qw, (q + 1) * qw) for q in quads]
    cc = CHUNK

    n_chunks = tc // CHUNK
    items = [(c, q) for c in range(n_chunks) for q in quads]
    every = range(len(items))

    scaled = []
    for c in range(n_chunks):
        rows = slice(c * CHUNK, (c + 1) * CHUNK)
        lw = lw_ref[rows, :]
        cum = _chunk_cumsum(lw)
        g_in = jnp.exp(cum)
        g_inv = jnp.exp(-cum)
        g_last = g_in[CHUNK - 1:CHUNK, :]
        scaled.append(dict(
            rt=r_ref[rows, :] * g_in, at=-kk_ref[rows, :] * jnp.exp(cum - lw),
            bt=bb_ref[rows, :] * g_inv, kt=k_ref[rows, :] * g_inv, v=v_ref[rows, :], g_last=g_last))

    def part(name, i):
        c, q = items[i]
        return scaled[c][name][:, sls[q]]

    ar = [jnp.concatenate([part("at", i), part("rt", i)], axis=0).astype(BF16) for i in every]
    bk4 = [jnp.concatenate([bdiag(part("bt", i)), bdiag(part("kt", i))], axis=0) for i in every]
    gram = [_nt(ar[i], bk4[i]) for i in every]
    a_ab = [jnp.where(strict, gram[i][:cc, :qw], 0.0) for i in every]
    a_k = [jnp.concatenate([jnp.where(strict, gram[i][:cc, qw:], 0.0),
                            jnp.where(incl, gram[i][cc:, qw:], 0.0)], axis=0).astype(BF16) for i in every]
    a_rb = [jnp.where(incl, gram[i][cc:, :qw], 0.0).astype(BF16) for i in every]
    akv = [_mm(a_k[i], bdiag(part("v", i))) for i in every]
    inv = [eye + a_ab[i] for i in every]
    pw = [_mm(a_ab[i].astype(BF16), bdiag(a_ab[i])) for i in every]
    for _ in range(1, CHUNK.bit_length() - 2):
        both = [_mm(jnp.concatenate([pw[i], inv[i]], axis=0).astype(BF16), bdiag(pw[i])) for i in every]
        inv = [inv[i] + both[i][cc:] for i in every]
        pw = [both[i][:cc] for i in every]
    inv = [(inv[i] + _mm(inv[i].astype(BF16), bdiag(pw[i]))).astype(BF16) for i in every]
    tt = [_mm(inv[i], jnp.concatenate([bdiag(part("at", i)), bdiag(akv[i][:cc])], axis=1)) for i in every]
    ta = [tt[i][:, :qw].astype(BF16) for i in every]
    tav = [tt[i][:, qw:] for i in every]
    bkg = [(jnp.concatenate([part("bt", i), part("kt", i)], axis=0) * part("g_last", i)).astype(BF16)
           for i in every]
    w = [jnp.where(blocks, _tn(ta[i], bkg[i][:cc]), 0.0).astype(BF16) for i in every]
    n = [jnp.where(blocks, _tn(jnp.concatenate([tav[i], part("v", i)], axis=0).astype(BF16), bkg[i]), 0.0)
         for i in every]

    s_in = []
    state = [s_scr[q] for q in quads]
    for c in range(n_chunks):
        s_b = [state[q].astype(BF16) for q in quads]
        s_in.extend(s_b)
        state = [state[q] * scaled[c]["g_last"][:, sls[q]] + _mm(s_b[q], w[c * n_quads + q]) + n[c * n_quads + q]
                 for q in quads]
    for q in quads:
        s_scr[q] = state[q]

    xs = [_nt(jnp.concatenate([ta[i], ar[i][cc:]], axis=0), s_in[i]) for i in every]
    u = [xs[i][:cc] + tav[i] for i in every]
    for i in every:
        c, q = items[i]
        y_ref[c * CHUNK:(c + 1) * CHUNK, sls[q]] = xs[i][cc:] + akv[i][cc:] + _mm(a_rb[i], bdiag(u[i]))

    @pl.when(pl.program_id(1) == pl.num_programs(1) - 1)
    def _():
        for hd in range(RWKV_H):
            q, o = divmod(hd, RWKV_QUAD)
            so_ref[hd] = s_scr[q][o * RWKV_HEAD:(o + 1) * RWKV_HEAD, o * RWKV_HEAD:(o + 1) * RWKV_HEAD]


def _rwkv_rec_prompt(r, lw, k, v, kk, bb, tc=256):
    b, l, _ = r.shape
    tc = _row_tile(l, tc)
    tile = pl.BlockSpec((None, tc, D_MODEL), lambda i, j: (i, j, 0))
    st = (RWKV_H, RWKV_HEAD, RWKV_HEAD)
    return pl.pallas_call(
        _rwkv_rec_prompt_kernel,
        grid=(b, l // tc),
        in_specs=[tile] * 6,
        out_specs=[tile, pl.BlockSpec((None,) + st, lambda i, j: (i, 0, 0, 0))],
        out_shape=[jax.ShapeDtypeStruct((b, l, D_MODEL), F32), jax.ShapeDtypeStruct((b,) + st, F32)],
        scratch_shapes=[pltpu.VMEM((RWKV_H // RWKV_QUAD, RWKV_QUAD * RWKV_HEAD, RWKV_QUAD * RWKV_HEAD), F32)],
        compiler_params=_params(("parallel", "arbitrary")),
        name="rwkv_rec_prompt",
    )(r, lw, k, v, kk, bb)


def _rwkv_rec_decode_kernel(r_ref, lw_ref, k_ref, v_ref, kk_ref, bb_ref, s_ref, y_ref, so_ref):
    w = jnp.exp(lw_ref[...])
    a = -kk_ref[...]
    b, k, r = bb_ref[...], k_ref[...], r_ref[...]

    def row(vi, carry):
        s0 = s_ref[vi]
        sa = jnp.sum(s0 * a, axis=0, keepdims=True)
        s1 = s0 * w + sa * b + v_ref[pl.ds(vi, 1), :] * k
        so_ref[vi] = s1
        y_ref[pl.ds(vi, 1), :] = jnp.sum(s1 * r, axis=0, keepdims=True)
        return carry

    lax.fori_loop(0, RWKV_HEAD, row, 0, unroll=8)


def _rwkv_rec_decode(r, lw, k, v, kk, bb, s):
    m = r.shape[1]
    vec = pl.BlockSpec((RWKV_HEAD, m), lambda i: (i, 0))
    st = pl.BlockSpec((None, RWKV_HEAD, RWKV_HEAD, m), lambda i: (i, 0, 0, 0))
    return pl.pallas_call(
        _rwkv_rec_decode_kernel,
        grid=(RWKV_H,),
        in_specs=[vec] * 6 + [st],
        out_specs=[vec, st],
        out_shape=[jax.ShapeDtypeStruct((D_MODEL, m), F32), jax.ShapeDtypeStruct(s.shape, F32)],
        compiler_params=_params(("parallel",)),
        name="rwkv_rec_decode",
    )(r, lw, k, v, kk, bb, s)


def _rwkv_post_kernel(h_ref, y_ref, bo_ref, g_ref, lnw_ref, lnb_ref, wo_ref, nf_ref, wu_ref, wd_ref, o_ref):
    y = y_ref[...]
    d = y - _group_sum(y, RWKV_HEAD, split=True) * (1.0 / RWKV_HEAD)
    var = _group_sum(d * d, RWKV_HEAD) * (1.0 / RWKV_HEAD)
    yn = d * lax.rsqrt(var + RWKV_GN_EPS) * lnw_ref[...] + lnb_ref[...]
    h = h_ref[...] + _bdot((yn + bo_ref[...]) * g_ref[...], wo_ref[...])
    o_ref[...] = _mlp(h, nf_ref, wu_ref, wd_ref)


def _rwkv_post_ffn(h, y, bonus, g, p, nf, wu, wd, layer, tm=512):
    m = h.shape[0]
    tm = _row_tile(m, tm)
    row = pl.BlockSpec((tm, D_MODEL), lambda i: (i, 0))
    vec = _const_spec((1, D_MODEL))
    mlp_ins, mlp_specs = _mlp_operands(nf, wu, wd, layer)
    return pl.pallas_call(
        _rwkv_post_kernel,
        grid=(m // tm,),
        in_specs=[row] * 4 + [vec, vec, _const_spec((D_MODEL, D_MODEL))] + mlp_specs,
        out_specs=row,
        out_shape=jax.ShapeDtypeStruct((m, D_MODEL), F32),
        compiler_params=_params(("parallel",)),
        name="rwkv_post_ffn",
    )(h, y, bonus, g, p["ln_w"].reshape(1, D_MODEL), p["ln_b"].reshape(1, D_MODEL), p["wo"].astype(BF16),
      *mlp_ins)


def _gla_pre_kernel(h_ref, nm_ref, win_ref, wgl_ref, wgk_ref, bgk_ref, q_ref, k_ref, v_ref, g_ref, gk_ref):
    u = _rms(h_ref[...], nm_ref[...], NORM_EPS).astype(BF16)
    z = _nt(u, win_ref[...])
    q_ref[...] = z[:, :GLA_DK_TOT] * (GLA_DK ** -0.5)
    k_ref[...] = z[:, GLA_DK_TOT:2 * GLA_DK_TOT]
    v_ref[...] = z[:, 2 * GLA_DK_TOT:2 * GLA_DK_TOT + GLA_DV_TOT]
    g_ref[...] = z[:, 2 * GLA_DK_TOT + GLA_DV_TOT:]
    gl = _nt(u, wgl_ref[...])
    pre = _bdot(gl, wgk_ref[...]) + bgk_ref[...]
    gk_ref[...] = -_softplus(-pre) * (1.0 / GLA_NORMALIZER)


def _gla_pre(h, nm, p, tm=512):
    m = h.shape[0]
    tm = _row_tile(m, tm)
    n_main = 2 * GLA_DK_TOT + 2 * GLA_DV_TOT
    w_t = p["w_in"].T
    w_main = w_t[:n_main].astype(BF16)
    w_gl = jnp.pad(w_t[n_main:].astype(BF16), ((0, LANES - GLA_LR), (0, 0)))
    w_gk = jnp.pad(p["w_gk2"].astype(BF16), ((0, LANES - GLA_LR), (0, 0)))
    row = lambda n: pl.BlockSpec((tm, n), lambda i: (i, 0))
    widths = (GLA_DK_TOT, GLA_DK_TOT, GLA_DV_TOT, GLA_DV_TOT, GLA_DK_TOT)
    return pl.pallas_call(
        _gla_pre_kernel,
        grid=(m // tm,),
        in_specs=[row(D_MODEL), _const_spec((1, D_MODEL)), _const_spec(w_main.shape), _const_spec(w_gl.shape),
                  _const_spec(w_gk.shape), _const_spec((1, GLA_DK_TOT))],
        out_specs=[row(n) for n in widths],
        out_shape=[jax.ShapeDtypeStruct((m, n), F32) for n in widths],
        compiler_params=_params(("parallel",)),
        name="gla_pre",
    )(h, nm.reshape(1, D_MODEL), w_main, w_gl, w_gk, p["b_gk"].reshape(1, GLA_DK_TOT))


def _gla_rec_prompt_kernel(q_ref, k_ref, v_ref, gk_ref, o_ref, so_ref, s_scr):
    tc = q_ref.shape[0]

    @pl.when(pl.program_id(1) == 0)
    def _():
        s_scr[...] = jnp.zeros_like(s_scr)

    ri, ci = _iota((CHUNK, CHUNK), 0), _iota((CHUNK, CHUNK), 1)
    incl = ri >= ci
    n_sub = CHUNK // GLA_SUB
    sub_of_row = jnp.right_shift(_iota((CHUNK, 1), 0), GLA_SUB.bit_length() - 1)

    heads = range(GLA_H)
    ks = [slice(hd * GLA_DK, (hd + 1) * GLA_DK) for hd in heads]
    vs = [slice(hd * GLA_DV, (hd + 1) * GLA_DV) for hd in heads]
    n_chunks = tc // CHUNK
    items = [(c, hd) for c in range(n_chunks) for hd in heads]

    prep = []
    for c in range(n_chunks):
        rows = slice(c * CHUNK, (c + 1) * CHUNK)
        cum = _chunk_cumsum(gk_ref[rows, :])
        q, k = q_ref[rows, :], k_ref[rows, :]
        last = cum[CHUNK - 1:CHUNK, :]
        refs = [jnp.zeros_like(last)] + [cum[i * GLA_SUB - 1:i * GLA_SUB, :] for i in range(1, n_sub)]
        ref_of_row = refs[0]
        for i in range(1, n_sub):
            ref_of_row = jnp.where(sub_of_row >= i, refs[i], ref_of_row)
        q_in = (q * jnp.exp(cum - ref_of_row)).astype(BF16)
        k_in = []
        for i in range(n_sub):
            n = (i + 1) * GLA_SUB
            part = (k[:n] * jnp.exp(refs[i] - cum[:n])).astype(BF16)
            k_in.append(part if n == CHUNK else
                        jnp.concatenate([part, jnp.zeros((CHUNK - n, part.shape[1]), BF16)], axis=0))
        prep.append(dict(q_in=q_in, k_in=k_in, qe=(q * jnp.exp(cum)).astype(BF16),
                         kh=(k * jnp.exp(last - cum)).astype(BF16), g_last=jnp.exp(last),
                         v=v_ref[rows, :].astype(BF16)))

    def scores(c, hd):
        p = prep[c]
        blocks = [_nt(p["q_in"][i * GLA_SUB:(i + 1) * GLA_SUB, ks[hd]], p["k_in"][i][:, ks[hd]])
                  for i in range(n_sub)]
        return jnp.where(incl, jnp.concatenate(blocks, axis=0), 0.0).astype(BF16)

    att = [scores(c, hd) for c, hd in items]
    intra = [_mm(att[i], prep[c]["v"][:, vs[hd]]) for i, (c, hd) in enumerate(items)]
    kv = [_tn(prep[c]["v"][:, vs[hd]], prep[c]["kh"][:, ks[hd]]) for c, hd in items]

    s_in = []
    state = [s_scr[hd] for hd in heads]
    for c in range(n_chunks):
        s_in.extend(state[hd].astype(BF16) for hd in heads)
        state = [state[hd] * prep[c]["g_last"][:, ks[hd]] + kv[c * GLA_H + hd] for hd in heads]
    for hd in heads:
        s_scr[hd] = state[hd]

    for i, (c, hd) in enumerate(items):
        o_ref[c * CHUNK:(c + 1) * CHUNK, vs[hd]] = _nt(prep[c]["qe"][:, ks[hd]], s_in[i]) + intra[i]

    @pl.when(pl.program_id(1) == pl.num_programs(1) - 1)
    def _():
        for hd in range(GLA_H):
            so_ref[hd] = s_scr[hd].T


def _gla_rec_prompt(q, k, v, gk, tc=256):
    b, l, _ = q.shape
    tc = _row_tile(l, tc)
    tile = lambda n: pl.BlockSpec((None, tc, n), lambda i, j: (i, j, 0))
    st = (GLA_H, GLA_DK, GLA_DV)
    return pl.pallas_call(
        _gla_rec_prompt_kernel,
        grid=(b, l // tc),
        in_specs=[tile(GLA_DK_TOT), tile(GLA_DK_TOT), tile(GLA_DV_TOT), tile(GLA_DK_TOT)],
        out_specs=[tile(GLA_DV_TOT), pl.BlockSpec((None,) + st, lambda i, j: (i, 0, 0, 0))],
        out_shape=[jax.ShapeDtypeStruct((b, l, GLA_DV_TOT), F32), jax.ShapeDtypeStruct((b,) + st, F32)],
        scratch_shapes=[pltpu.VMEM((GLA_H, GLA_DV, GLA_DK), F32)],
        compiler_params=_params(("parallel", "arbitrary")),
        name="gla_rec_prompt",
    )(q, k, v, gk)


def _gla_rec_decode_kernel(q_ref, k_ref, v_ref, gk_ref, s_ref, o_ref, so_ref):
    eye = (_iota((GLA_DK, GLA_DK), 0) == _iota((GLA_DK, GLA_DK), 1)).astype(F32)[None]

    def col(x):
        return jnp.sum(eye * x, axis=-1, keepdims=True)

    for hd in range(GLA_H):
        hs = slice(hd, hd + 1)
        s1 = s_ref[:, hd] * col(jnp.exp(gk_ref[:, hs, :])) + col(k_ref[:, hs, :]) * v_ref[:, hs, :]
        so_ref[:, hd] = s1
        o_ref[:, hs, :] = jnp.sum(col(q_ref[:, hs, :]) * s1, axis=1, keepdims=True)


def _gla_rec_decode(q, k, v, gk, s, bblk=8):
    m = q.shape[0]
    kvec = pl.BlockSpec((bblk, GLA_H, GLA_DK), lambda i: (i, 0, 0))
    vvec = pl.BlockSpec((bblk, GLA_H, GLA_DV), lambda i: (i, 0, 0))
    st = pl.BlockSpec((bblk, GLA_H, GLA_DK, GLA_DV), lambda i: (i, 0, 0, 0))
    hk = lambda x: x.reshape(m, GLA_H, GLA_DK)
    o, s_new = pl.pallas_call(
        _gla_rec_decode_kernel,
        grid=(m // bblk,),
        in_specs=[kvec, kvec, vvec, kvec, st],
        out_specs=[vvec, st],
        out_shape=[jax.ShapeDtypeStruct((m, GLA_H, GLA_DV), F32), jax.ShapeDtypeStruct(s.shape, F32)],
        compiler_params=_params(("parallel",)),
        name="gla_rec_decode",
    )(hk(q), hk(k), v.reshape(m, GLA_H, GLA_DV), hk(gk), s)
    return o.reshape(m, GLA_DV_TOT), s_new


def _gla_post_kernel(h_ref, o_ref_in, g_ref, nw_ref, wo_ref, nf_ref, wu_ref, wd_ref, out_ref):
    o = o_ref_in[...]
    parts = []
    for hd in range(GLA_H):
        oh = o[:, hd * GLA_DV:(hd + 1) * GLA_DV]
        parts.append(oh * lax.rsqrt(jnp.mean(oh * oh, axis=-1, keepdims=True) + GLA_NORM_EPS))
    on = jnp.concatenate(parts, axis=1) * nw_ref[...]
    g = g_ref[...]
    h = h_ref[...] + _bdot(on * (g * jax.nn.sigmoid(g)), wo_ref[...])
    out_ref[...] = _mlp(h, nf_ref, wu_ref, wd_ref)


def _gla_post_ffn(h, o, g, p, nf, wu, wd, layer, tm=512):
    m = h.shape[0]
    tm = _row_tile(m, tm)
    row = pl.BlockSpec((tm, D_MODEL), lambda i: (i, 0))
    mlp_ins, mlp_specs = _mlp_operands(nf, wu, wd, layer)
    return pl.pallas_call(
        _gla_post_kernel,
        grid=(m // tm,),
        in_specs=[row] * 3 + [_const_spec((1, D_MODEL)), _const_spec((D_MODEL, D_MODEL))] + mlp_specs,
        out_specs=row,
        out_shape=jax.ShapeDtypeStruct((m, D_MODEL), F32),
        compiler_params=_params(("parallel",)),
        name="gla_post_ffn",
    )(h, o, g, jnp.tile(p["norm"], GLA_H).reshape(1, D_MODEL), p["wo"].astype(BF16), *mlp_ins)


def _conv_gates(h_ref, nm_ref, win_ref):
    u = _rms(h_ref[...], nm_ref[...], NORM_EPS).astype(BF16)
    z = jnp.dot(u, win_ref[...], preferred_element_type=F32)
    return z[:, :D_MODEL], z[:, D_MODEL:2 * D_MODEL] * z[:, 2 * D_MODEL:]


def _conv_prompt_kernel(h_ref, nm_ref, win_ref, cw_ref, wo_ref, o_ref, st_ref, carry_ref):
    tm = h_ref.shape[0]

    @pl.when(pl.program_id(1) == 0)
    def _():
        carry_ref[...] = jnp.zeros_like(carry_ref)

    ts = tm // SUBTILES
    subs = [slice(i * ts, (i + 1) * ts) for i in range(SUBTILES)]
    gates = [_conv_gates(h_ref.at[sl], nm_ref, win_ref) for sl in subs]
    row = _iota((ts, D_MODEL), 0)
    prev2, prev1 = carry_ref[0:1, :], carry_ref[1:2, :]
    gated = []
    for g_b, zc in gates:
        z1 = jnp.where(row == 0, prev1, pltpu.roll(zc, 1, axis=0))
        z2 = jnp.where(row == 0, prev2, jnp.where(row == 1, prev1, pltpu.roll(zc, 2, axis=0)))
        gated.append((g_b * (cw_ref[0:1, :] * z2 + cw_ref[1:2, :] * z1 + cw_ref[2:3, :] * zc)).astype(BF16))
        prev2, prev1 = zc[ts - 2:ts - 1, :], zc[ts - 1:ts, :]
    for sl, x in zip(subs, gated):
        o_ref[sl, :] = h_ref[sl, :] + jnp.dot(x, wo_ref[...], preferred_element_type=F32)
    tail = gates[-1][1][ts - (CONV_W - 1):, :]
    carry_ref[0:CONV_W - 1, :] = tail
    st_ref[...] = tail


def _conv_prompt(h, nm, p, tm=1024):
    b, l, _ = h.shape
    tm = _row_tile(l, tm)
    tile = pl.BlockSpec((None, tm, D_MODEL), lambda i, j: (i, j, 0))
    return pl.pallas_call(
        _conv_prompt_kernel,
        grid=(b, l // tm),
        in_specs=[tile, _const_spec((1, D_MODEL)), _const_spec((D_MODEL, 3 * D_MODEL)),
                  _const_spec((CONV_W, D_MODEL)), _const_spec((D_MODEL, D_MODEL))],
        out_specs=[tile, pl.BlockSpec((None, CONV_W - 1, D_MODEL), lambda i, j: (i, 0, 0))],
        out_shape=[jax.ShapeDtypeStruct((b, l, D_MODEL), F32),
                   jax.ShapeDtypeStruct((b, CONV_W - 1, D_MODEL), F32)],
        scratch_shapes=[pltpu.VMEM((SUBLANES, D_MODEL), F32)],
        compiler_params=_params(("parallel", "arbitrary")),
        name="conv_prompt",
    )(h, nm.reshape(1, D_MODEL), p["w_in"].astype(BF16), p["w"], p["wo"].astype(BF16))


def _conv_decode_kernel(h_ref, buf_ref, nm_ref, win_ref, cw_ref, wo_ref, o_ref, st_ref):
    g_b, zc = _conv_gates(h_ref, nm_ref, win_ref)
    z2 = buf_ref[:, 0, :]
    z1 = buf_ref[:, 1, :]
    conv = cw_ref[0:1, :] * z2 + cw_ref[1:2, :] * z1 + cw_ref[2:3, :] * zc
    o_ref[...] = h_ref[...] + _bdot(g_b * conv, wo_ref[...])
    st_ref[:, 0, :] = z1
    st_ref[:, 1, :] = zc


def _conv_decode(h, buf, nm, p):
    m = h.shape[0]
    full = pl.BlockSpec((m, D_MODEL), lambda i: (0, 0))
    bufs = pl.BlockSpec(buf.shape, lambda i: (0, 0, 0))
    return pl.pallas_call(
        _conv_decode_kernel,
        grid=(1,),
        in_specs=[full, bufs, _const_spec((1, D_MODEL)), _const_spec((D_MODEL, 3 * D_MODEL)),
                  _const_spec((CONV_W, D_MODEL)), _const_spec((D_MODEL, D_MODEL))],
        out_specs=[full, bufs],
        out_shape=[jax.ShapeDtypeStruct((m, D_MODEL), F32), jax.ShapeDtypeStruct(buf.shape, F32)],
        compiler_params=_params(("arbitrary",)),
        name="conv_decode",
    )(h, buf, nm.reshape(1, D_MODEL), p["w_in"].astype(BF16), p["w"], p["wo"].astype(BF16))


def _pool_project(d_groups, pw_ref, sc_ref):
    ys = [_bdot(d, pw_ref[gi]) for gi, d in enumerate(d_groups)]
    return jnp.concatenate(ys, axis=1) * sc_ref[...]


def _pool_prompt_kernel(h_ref, nm_ref, pw_ref, sc_ref, nf_ref, wu_ref, wd_ref, gf_ref, o_ref, st_ref, carry_ref):
    tm = h_ref.shape[0]
    hist = carry_ref.shape[0]
    j = pl.program_id(1)

    @pl.when(j == 0)
    def _():
        carry_ref[...] = jnp.zeros_like(carry_ref)

    u = _rms(h_ref[...], nm_ref[...], NORM_EPS)
    ext = jnp.concatenate([carry_ref[...], u], axis=0)
    pos = j * tm + _iota((tm, 1), 0)
    d_groups = []
    for gi, w in enumerate(POOL_WINDOWS):
        sl = slice(gi * POOL_G, (gi + 1) * POOL_G)
        s = ext[:, sl]
        span = 1
        while span < w:
            s = s + pltpu.roll(s, span, axis=0)
            span *= 2
        cnt = jnp.minimum(w, pos + 1).astype(F32)
        d_groups.append(s[hist:, :] / cnt - u[:, sl])
    o_ref[...] = _mlp(h_ref[...] + _pool_project(d_groups, pw_ref, sc_ref), nf_ref, wu_ref, wd_ref, gf_ref)
    tail = u[tm - hist:, :]
    carry_ref[...] = tail
    st_ref[...] = tail


def _pool_prompt_ffn(h, nm, p, nf, wu, wd, layer, final_g, tm=512):
    b, l, _ = h.shape
    tm = _row_tile(l, tm)
    hist = POOL_BUF + 1
    tile = pl.BlockSpec((None, tm, D_MODEL), lambda i, j: (i, j, 0))
    mlp_ins, mlp_specs = _mlp_operands(nf, wu, wd, layer, final_g)
    out, st = pl.pallas_call(
        _pool_prompt_kernel,
        grid=(b, l // tm),
        in_specs=[tile, _const_spec((1, D_MODEL)), _const_spec(p["w"].shape), _const_spec((1, D_MODEL))]
        + mlp_specs,
        out_specs=[tile, pl.BlockSpec((None, hist, D_MODEL), lambda i, j: (i, 0, 0))],
        out_shape=[jax.ShapeDtypeStruct((b, l, D_MODEL), F32), jax.ShapeDtypeStruct((b, hist, D_MODEL), F32)],
        scratch_shapes=[pltpu.VMEM((hist, D_MODEL), F32)],
        compiler_params=_params(("parallel", "arbitrary")),
        name="pool_prompt_ffn",
    )(h, nm.reshape(1, D_MODEL), p["w"].astype(BF16), p["scale"].reshape(1, D_MODEL), *mlp_ins)
    return out, st[:, 1:]


def _pool_decode_kernel(h_ref, buf_ref, nm_ref, pw_ref, sc_ref, o_ref, st_ref):
    u = _rms(h_ref[...], nm_ref[...], NORM_EPS)
    d_groups = []
    for gi, w in enumerate(POOL_WINDOWS):
        sl = slice(gi * POOL_G, (gi + 1) * POOL_G)
        s = u[:, sl]
        for i in range(1, w):
            s = s + buf_ref[POOL_BUF - i, :, sl]
        cnt = float(min(w, PAST_LEN + 1))
        d_groups.append(s / cnt - u[:, sl])
    o_ref[...] = h_ref[...] + _pool_project(d_groups, pw_ref, sc_ref)
    st_ref[0:POOL_BUF - 1] = buf_ref[1:POOL_BUF]
    st_ref[POOL_BUF - 1] = u


def _pool_decode(h, buf, nm, p):
    m = h.shape[0]
    full = pl.BlockSpec((m, D_MODEL), lambda i: (0, 0))
    bufs = pl.BlockSpec(buf.shape, lambda i: (0, 0, 0))
    return pl.pallas_call(
        _pool_decode_kernel,
        grid=(1,),
        in_specs=[full, bufs, _const_spec((1, D_MODEL)), _const_spec(p["w"].shape), _const_spec((1, D_MODEL))],
        out_specs=[full, bufs],
        out_shape=[jax.ShapeDtypeStruct((m, D_MODEL), F32), jax.ShapeDtypeStruct(buf.shape, F32)],
        compiler_params=_params(("arbitrary",)),
        name="pool_decode",
    )(h, buf, nm.reshape(1, D_MODEL), p["w"].astype(BF16), p["scale"].reshape(1, D_MODEL))


def _trunk_prompt(x, nm, nf, nfin, wu, wd, rw, gl, cv, po):
    b, l, _ = x.shape
    flat = lambda t: t.reshape(b * l, t.shape[-1])
    seq = lambda t: t.reshape(b, l, t.shape[-1])

    (r, lw, k, v, kk, bb, g, bonus), shift = _rwkv_pre_prompt(x, nm[0], rw)
    y, wkv = _rwkv_rec_prompt(r, lw, k, v, kk, bb)
    h = _rwkv_post_ffn(flat(x), flat(y), flat(bonus), flat(g), rw, nf[0], wu, wd, 0)

    q, k, v, g, gk = _gla_pre(h, nm[1], gl)
    o, gla_s = _gla_rec_prompt(seq(q), seq(k), seq(v), seq(gk))
    h = _gla_post_ffn(h, flat(o), g, gl, nf[1], wu, wd, 1)

    h, conv_s = _conv_prompt(seq(h), nm[2], cv)
    h = _ffn(flat(h), nf[2], wu, wd, 2)

    y, pool_s = _pool_prompt_ffn(seq(h), nm[3], po, nf[3], wu, wd, 3, nfin)
    return y, wkv[None], shift[None], gla_s[None], conv_s[None], pool_s[None]


def _trunk_decode(x, st_wkv, st_shift, st_gla, st_conv, st_pool, nm, nf, nfin, wu, wd, rw, gl, cv, po):
    m = x.shape[0]
    h = x.reshape(m, D_MODEL)

    (r, lw, k, v, kk, bb, g, bonus), shift = _rwkv_pre_decode(h, st_shift[0], nm[0], rw)
    y_t, wkv_t = _rwkv_rec_decode(r, lw, k, v, kk, bb, jnp.transpose(st_wkv[0], (1, 2, 3, 0)))
    wkv = jnp.transpose(wkv_t, (3, 0, 1, 2))
    h = _rwkv_post_ffn(h, y_t.T, bonus, g, rw, nf[0], wu, wd, 0)

    q, k, v, g, gk = _gla_pre(h, nm[1], gl)
    o, gla_s = _gla_rec_decode(q, k, v, gk, st_gla[0])
    h = _gla_post_ffn(h, o, g, gl, nf[1], wu, wd, 1)

    h, conv_s = _conv_decode(h, st_conv[0], nm[2], cv)
    h = _ffn(h, nf[2], wu, wd, 2)

    h, pool_t = _pool_decode(h, jnp.transpose(st_pool[0], (1, 0, 2)), nm[3], po)
    pool_s = jnp.transpose(pool_t, (1, 0, 2))
    y = _ffn(h, nf[3], wu, wd, 3, final_g=nfin)
    return y.reshape(m, 1, D_MODEL), wkv[None], shift[None], gla_s[None], conv_s[None], pool_s[None]


def kernel(x_prompt, x_sample, state_rwkv_wkv, state_rwkv_shift, state_gla, state_conv, state_pool, norm_mix, norm_ffn, norm_final, ffn_up, ffn_down, rwkv_mu, rwkv_w_rkv, rwkv_w0, rwkv_w1, rwkv_w2, rwkv_a0, rwkv_a1, rwkv_a2, rwkv_g1, rwkv_g2, rwkv_k_k, rwkv_k_a, rwkv_r_k, rwkv_ln_w, rwkv_ln_b, rwkv_wo, gla_w_in, gla_w_gk2, gla_b_gk, gla_norm, gla_wo, conv_w_in, conv_w, conv_wo, pool_w, pool_scale):
    assert x_prompt.shape[1] % CHUNK == 0 and x_sample.shape[1] == 1
    wu = ffn_up.astype(BF16)
    wd = ffn_down.astype(BF16)
    rw = dict(mu=rwkv_mu[0], w_rkv=rwkv_w_rkv[0], w0=rwkv_w0[0], w1=rwkv_w1[0], w2=rwkv_w2[0], a0=rwkv_a0[0],
              a1=rwkv_a1[0], a2=rwkv_a2[0], g1=rwkv_g1[0], g2=rwkv_g2[0], k_k=rwkv_k_k[0], k_a=rwkv_k_a[0],
              r_k=rwkv_r_k[0], ln_w=rwkv_ln_w[0], ln_b=rwkv_ln_b[0], wo=rwkv_wo[0])
    gl = dict(w_in=gla_w_in[0], w_gk2=gla_w_gk2[0], b_gk=gla_b_gk[0], norm=gla_norm[0], wo=gla_wo[0])
    cv = dict(w_in=conv_w_in[0], w=conv_w[0], wo=conv_wo[0])
    po = dict(w=pool_w[0], scale=pool_scale[0])
    shared = (norm_mix, norm_ffn, norm_final, wu, wd, rw, gl, cv, po)
    y_p, wkv_p, sh_p, gla_p, conv_p, pool_p = _trunk_prompt(x_prompt, *shared)
    y_s, wkv_s, sh_s, gla_s, conv_s, pool_s = _trunk_decode(
        x_sample, state_rwkv_wkv, state_rwkv_shift, state_gla, state_conv, state_pool, *shared)
    return (y_p, y_s, wkv_p, wkv_s, sh_p, sh_s, gla_p, gla_s, conv_p, conv_s, pool_p, pool_s)
```

```python
import functools
import math

import jax
import jax.numpy as jnp
from jax import lax
from jax.experimental import pallas as pl
from jax.experimental.pallas import tpu as pltpu

F32 = jnp.float32
BF16 = jnp.bfloat16

D_MODEL = 1024
D_FF = 4 * D_MODEL
NORM_EPS = 1e-6
PAST_LEN = 16384

RWKV_HEAD = 64
RWKV_H = D_MODEL // RWKV_HEAD
RWKV_GN_EPS = 64e-5
RWKV_QUAD = 4

GLA_H = 4
GLA_DK_TOT = D_MODEL // 2
GLA_DV_TOT = D_MODEL
GLA_DK = GLA_DK_TOT // GLA_H
GLA_DV = GLA_DV_TOT // GLA_H
GLA_LR = 16
GLA_NORMALIZER = 16.0
GLA_NORM_EPS = 1e-5
GLA_SUB = 16

CONV_W = 3
POOL_WINDOWS = (2, 4, 8, 16)
POOL_G = D_MODEL // len(POOL_WINDOWS)
POOL_BUF = max(POOL_WINDOWS) - 1

LANES = 128
SUBLANES = 8
MXU_DIM = 256
VMEM_LIMIT_BYTES = 56 * 1024 * 1024

CHUNK = 64
FFN_COLS = 1024
SUBTILES = 2


def _const_spec(shape):
    nd = len(shape)
    return pl.BlockSpec(shape, lambda *_: (0,) * nd, pipeline_mode=pl.Buffered(1))


def _params(sem):
    return pltpu.CompilerParams(dimension_semantics=sem, vmem_limit_bytes=VMEM_LIMIT_BYTES)


def _rms(x, g, eps):
    return x * lax.rsqrt(jnp.mean(x * x, axis=-1, keepdims=True) + eps) * g


def _bdot(a, w):
    return jnp.dot(a.astype(BF16), w, preferred_element_type=F32)


def _mm(a, b, prec=None):
    return jnp.dot(a, b, preferred_element_type=F32, precision=prec)


def _nt(a, b, prec=None):
    return lax.dot_general(a, b, (((1,), (1,)), ((), ())), preferred_element_type=F32, precision=prec)


def _tn(a, b, prec=None):
    return lax.dot_general(a, b, (((0,), (0,)), ((), ())), preferred_element_type=F32, precision=prec)


def _iota(shape, dim):
    return lax.broadcasted_iota(jnp.int32, shape, dim)


def _group_sum(x, group, split=False):
    shift = group.bit_length() - 1
    bd = (jnp.right_shift(_iota((MXU_DIM, MXU_DIM), 0), shift)
          == jnp.right_shift(_iota((MXU_DIM, MXU_DIM), 1), shift)).astype(BF16)
    terms = [x.astype(BF16)]
    if split:
        terms.append((x - terms[0].astype(F32)).astype(BF16))
    cols = [sum(_mm(t[:, c * MXU_DIM:(c + 1) * MXU_DIM], bd) for t in terms)
            for c in range(x.shape[1] // MXU_DIM)]
    return jnp.concatenate(cols, axis=1)


def _chunk_cumsum(x):
    c = x.shape[0]
    ltri = (_iota((c, c), 0) >= _iota((c, c), 1)).astype(BF16)
    t0 = x.astype(BF16)
    r1 = x - t0.astype(F32)
    t1 = r1.astype(BF16)
    t2 = (r1 - t1.astype(F32)).astype(BF16)
    return _mm(ltri, t0) + _mm(ltri, t1) + _mm(ltri, t2)


def _softplus(x):
    return jnp.maximum(x, 0.0) + jnp.log(1.0 + jnp.exp(-jnp.abs(x)))


def _row_tile(m, want):
    t = min(want, m)
    assert m % t == 0, (m, t)
    return t


def _mlp(h, g_ref, wu_ref, wd_ref, gf_ref=None):
    x = _rms(h, g_ref[...], NORM_EPS).astype(BF16)
    acc = h
    for c in range(D_FF // FFN_COLS):
        sl = slice(c * FFN_COLS, (c + 1) * FFN_COLS)
        a = jnp.dot(x, wu_ref[:, sl], preferred_element_type=F32)
        a = jnp.square(jnp.maximum(a, 0.0)).astype(BF16)
        acc = acc + jnp.dot(a, wd_ref[sl, :], preferred_element_type=F32)
    if gf_ref is not None:
        acc = _rms(acc, gf_ref[...], NORM_EPS)
    return acc


def _mlp_operands(nf, wu, wd, layer, final_g=None):
    layer_spec = lambda shape: pl.BlockSpec((None,) + shape[1:], lambda *_: (layer, 0, 0),
                                            pipeline_mode=pl.Buffered(1))
    ins = [nf.reshape(1, D_MODEL), wu, wd]
    specs = [_const_spec((1, D_MODEL)), layer_spec(wu.shape), layer_spec(wd.shape)]
    if final_g is not None:
        ins.append(final_g.reshape(1, D_MODEL))
        specs.append(_const_spec((1, D_MODEL)))
    return ins, specs


def _ffn_kernel(h_ref, g_ref, wu_ref, wd_ref, *rest):
    *gf_ref, o_ref = rest
    o_ref[...] = _mlp(h_ref[...], g_ref, wu_ref, wd_ref, *gf_ref)


def _ffn(h, g, wu, wd, layer, final_g=None, tm=512):
    m = h.shape[0]
    tm = _row_tile(m, tm)
    row = pl.BlockSpec((tm, D_MODEL), lambda i: (i, 0))
    mlp_ins, mlp_specs = _mlp_operands(g, wu, wd, layer, final_g)
    return pl.pallas_call(
        _ffn_kernel,
        grid=(m // tm,),
        in_specs=[row] + mlp_specs,
        out_specs=row,
        out_shape=jax.ShapeDtypeStruct((m, D_MODEL), F32),
        compiler_params=_params(("parallel",)),
        name="ffn",
    )(h, *mlp_ins)


def _rwkv_pre_math(u, prev, mu_ref, wrkv_ref, w0_ref, w1_ref, w2_ref, a0_ref, a1_ref, a2_ref,
                   g1_ref, g2_ref, kk_ref, ka_ref, rk_ref):
    dx = prev - u

    def mix(i):
        return (u + dx * mu_ref[i:i + 1, :]).astype(BF16)

    r = _bdot(mix(0), wrkv_ref[0])
    k = _bdot(mix(2), wrkv_ref[1])
    v = _bdot(mix(3), wrkv_ref[2])
    wl = w0_ref[...] + _bdot(jnp.tanh(_bdot(mix(1), w1_ref[...])), w2_ref[...])
    lw = jax.nn.sigmoid(wl) * (-math.exp(-0.5))
    a = jax.nn.sigmoid(a0_ref[...] + _bdot(_bdot(mix(4), a1_ref[...]), a2_ref[...]))
    g = _bdot(jax.nn.sigmoid(_bdot(mix(5), g1_ref[...])), g2_ref[...])
    kk = k * kk_ref[...]
    kk = kk * lax.rsqrt(jnp.maximum(_group_sum(kk * kk, RWKV_HEAD), 1e-24))
    k = k * (1.0 + (a - 1.0) * ka_ref[...])
    bonus = _group_sum(r * k * rk_ref[...], RWKV_HEAD) * v
    return r, lw, k, v, kk, kk * a, g, bonus


_N_RWKV_W = 13


def _rwkv_pre_prompt_kernel(h_ref, nm_ref, *rest):
    w = rest[:_N_RWKV_W]
    outs = rest[_N_RWKV_W:_N_RWKV_W + 8]
    sh_ref, carry_ref = rest[_N_RWKV_W + 8:]
    tm = h_ref.shape[0]

    @pl.when(pl.program_id(1) == 0)
    def _():
        carry_ref[...] = jnp.zeros_like(carry_ref)

    u = _rms(h_ref[...], nm_ref[...], NORM_EPS)
    prev = jnp.where(_iota(u.shape, 0) == 0, carry_ref[0:1, :], pltpu.roll(u, 1, axis=0))
    last = u[tm - 1:tm, :]
    carry_ref[0:1, :] = last
    sh_ref[...] = last
    for o_ref, val in zip(outs, _rwkv_pre_math(u, prev, *w)):
        o_ref[...] = val


def _rwkv_pre_decode_kernel(h_ref, prev_ref, nm_ref, *rest):
    w = rest[:_N_RWKV_W]
    outs = rest[_N_RWKV_W:_N_RWKV_W + 8]
    (sh_ref,) = rest[_N_RWKV_W + 8:]
    u = _rms(h_ref[...], nm_ref[...], NORM_EPS)
    sh_ref[...] = u
    vals = _rwkv_pre_math(u, prev_ref[...], *w)
    for o_ref, val in zip(outs[:6], vals[:6]):
        o_ref[...] = val.T
    for o_ref, val in zip(outs[6:], vals[6:]):
        o_ref[...] = val


def _rwkv_weights(p):
    row = lambda x: x.reshape(1, D_MODEL)
    return [p["mu"], p["w_rkv"].astype(BF16), row(p["w0"]), p["w1"].astype(BF16), p["w2"].astype(BF16),
            row(p["a0"]), p["a1"].astype(BF16), p["a2"].astype(BF16), p["g1"].astype(BF16),
            p["g2"].astype(BF16), row(p["k_k"]), row(p["k_a"]), row(p["r_k"])]


def _rwkv_pre_prompt(h, nm, p, tm=512):
    b, l, _ = h.shape
    tm = _row_tile(l, tm)
    ws = _rwkv_weights(p)
    tile = pl.BlockSpec((None, tm, D_MODEL), lambda i, j: (i, j, 0))
    outs = pl.pallas_call(
        _rwkv_pre_prompt_kernel,
        grid=(b, l // tm),
        in_specs=[tile, _const_spec((1, D_MODEL))] + [_const_spec(w.shape) for w in ws],
        out_specs=[tile] * 8 + [pl.BlockSpec((None, 1, D_MODEL), lambda i, j: (i, 0, 0))],
        out_shape=[jax.ShapeDtypeStruct((b, l, D_MODEL), F32)] * 8
        + [jax.ShapeDtypeStruct((b, 1, D_MODEL), F32)],
        scratch_shapes=[pltpu.VMEM((SUBLANES, D_MODEL), F32)],
        compiler_params=_params(("parallel", "arbitrary")),
        name="rwkv_pre_prompt",
    )(h, nm.reshape(1, D_MODEL), *ws)
    return outs[:8], outs[8].reshape(b, D_MODEL)


def _rwkv_pre_decode(h, prev, nm, p):
    m = h.shape[0]
    ws = _rwkv_weights(p)
    full = pl.BlockSpec((m, D_MODEL), lambda i: (0, 0))
    full_t = pl.BlockSpec((D_MODEL, m), lambda i: (0, 0))
    outs = pl.pallas_call(
        _rwkv_pre_decode_kernel,
        grid=(1,),
        in_specs=[full, full, _const_spec((1, D_MODEL))] + [_const_spec(w.shape) for w in ws],
        out_specs=[full_t] * 6 + [full] * 3,
        out_shape=[jax.ShapeDtypeStruct((D_MODEL, m), F32)] * 6 + [jax.ShapeDtypeStruct((m, D_MODEL), F32)] * 3,
        compiler_params=_params(("arbitrary",)),
        name="rwkv_pre_decode",
    )(h, prev, nm.reshape(1, D_MODEL), *ws)
    return outs[:8], outs[8]


def _rwkv_rec_prompt_kernel(r_ref, lw_ref, k_ref, v_ref, kk_ref, bb_ref, y_ref, so_ref, s_scr):
    tc = r_ref.shape[0]
    qw = RWKV_QUAD * RWKV_HEAD
    n_quads = D_MODEL // qw
    assert RWKV_QUAD * CHUNK == qw

    @pl.when(pl.program_id(1) == 0)
    def _():
        s_scr[...] = jnp.zeros_like(s_scr)

    t_idx, s_idx = _iota((CHUNK, qw), 0), _iota((CHUNK, qw), 1) & (CHUNK - 1)
    strict = t_idx > s_idx
    incl = t_idx >= s_idx
    eye = (t_idx == s_idx).astype(F32)
    blocks = jnp.right_shift(_iota((qw, qw), 0), 6) == jnp.right_shift(_iota((qw, qw), 1), 6)

    def bdiag(z):
        zb = z.astype(BF16)
        return jnp.where(blocks, jnp.concatenate([zb] * RWKV_QUAD, axis=0), jnp.zeros((), BF16))

    quads = range(n_quads)
    sls = [slice(q * qw, (q + 1) * qw) for q in quads]
    cc = CHUNK

    n_chunks = tc // CHUNK
    items = [(c, q) for c in range(n_chunks) for q in quads]
    every = range(len(items))

    scaled = []
    for c in range(n_chunks):
        rows = slice(c * CHUNK, (c + 1) * CHUNK)
        lw = lw_ref[rows, :]
        cum = _chunk_cumsum(lw)
        g_in = jnp.exp(cum)
        g_inv = jnp.exp(-cum)
        g_last = g_in[CHUNK - 1:CHUNK, :]
        scaled.append(dict(
            rt=r_ref[rows, :] * g_in, at=-kk_ref[rows, :] * jnp.exp(cum - lw),
            bt=bb_ref[rows, :] * g_inv, kt=k_ref[rows, :] * g_inv, v=v_ref[rows, :], g_last=g_last))

    def part(name, i):
        c, q = items[i]
        return scaled[c][name][:, sls[q]]

    ar = [jnp.concatenate([part("at", i), part("rt", i)], axis=0).astype(BF16) for i in every]
    bk4 = [jnp.concatenate([bdiag(part("bt", i)), bdiag(part("kt", i))], axis=0) for i in every]
    gram = [_nt(ar[i], bk4[i]) for i in every]
    a_ab = [jnp.where(strict, gram[i][:cc, :qw], 0.0) for i in every]
    a_k = [jnp.concatenate([jnp.where(strict, gram[i][:cc, qw:], 0.0),
                            jnp.where(incl, gram[i][cc:, qw:], 0.0)], axis=0).astype(BF16) for i in every]
    a_rb = [jnp.where(incl, gram[i][cc:, :qw], 0.0).astype(BF16) for i in every]
    akv = [_mm(a_k[i], bdiag(part("v", i))) for i in every]
    inv = [eye + a_ab[i] for i in every]
    pw = [_mm(a_ab[i].astype(BF16), bdiag(a_ab[i])) for i in every]
    for _ in range(1, CHUNK.bit_length() - 2):
        both = [_mm(jnp.concatenate([pw[i], inv[i]], axis=0).astype(BF16), bdiag(pw[i])) for i in every]
        inv = [inv[i] + both[i][cc:] for i in every]
        pw = [both[i][:cc] for i in every]
    inv = [(inv[i] + _mm(inv[i].astype(BF16), bdiag(pw[i]))).astype(BF16) for i in every]
    tt = [_mm(inv[i], jnp.concatenate([bdiag(part("at", i)), bdiag(akv[i][:cc])], axis=1)) for i in every]
    ta = [tt[i][:, :qw].astype(BF16) for i in every]
    tav = [tt[i][:, qw:] for i in every]
    bkg = [(jnp.concatenate([part("bt", i), part("kt", i)], axis=0) * part("g_last", i)).astype(BF16)
           for i in every]
    w = [jnp.where(blocks, _tn(ta[i], bkg[i][:cc]), 0.0).astype(BF16) for i in every]
    n = [jnp.where(blocks, _tn(jnp.concatenate([tav[i], part("v", i)], axis=0).astype(BF16), bkg[i]), 0.0)
         for i in every]

    s_in = []
    state = [s_scr[q] for q in quads]
    for c in range(n_chunks):
        s_b = [state[q].astype(BF16) for q in quads]
        s_in.extend(s_b)
        state = [state[q] * scaled[c]["g_last"][:, sls[q]] + _mm(s_b[q], w[c * n_quads + q]) + n[c * n_quads + q]
                 for q in quads]
    for q in quads:
        s_scr[q] = state[q]

    xs = [_nt(jnp.concatenate([ta[i], ar[i][cc:]], axis=0), s_in[i]) for i in every]
    u = [xs[i][:cc] + tav[i] for i in every]
    for i in every:
        c, q = items[i]
        y_ref[c * CHUNK:(c + 1) * CHUNK, sls[q]] = xs[i][cc:] + akv[i][cc:] + _mm(a_rb[i], bdiag(u[i]))

    @pl.when(pl.program_id(1) == pl.num_programs(1) - 1)
    def _():
        for hd in range(RWKV_H):
            q, o = divmod(hd, RWKV_QUAD)
            so_ref[hd] = s_scr[q][o * RWKV_HEAD:(o + 1) * RWKV_HEAD, o * RWKV_HEAD:(o + 1) * RWKV_HEAD]


def _rwkv_rec_prompt(r, lw, k, v, kk, bb, tc=256):
    b, l, _ = r.shape
    tc = _row_tile(l, tc)
    tile = pl.BlockSpec((None, tc, D_MODEL), lambda i, j: (i, j, 0))
    st = (RWKV_H, RWKV_HEAD, RWKV_HEAD)
    return pl.pallas_call(
        _rwkv_rec_prompt_kernel,
        grid=(b, l // tc),
        in_specs=[tile] * 6,
        out_specs=[tile, pl.BlockSpec((None,) + st, lambda i, j: (i, 0, 0, 0))],
        out_shape=[jax.ShapeDtypeStruct((b, l, D_MODEL), F32), jax.ShapeDtypeStruct((b,) + st, F32)],
        scratch_shapes=[pltpu.VMEM((RWKV_H // RWKV_QUAD, RWKV_QUAD * RWKV_HEAD, RWKV_QUAD * RWKV_HEAD), F32)],
        compiler_params=_params(("parallel", "arbitrary")),
        name="rwkv_rec_prompt",
    )(r, lw, k, v, kk, bb)


def _rwkv_rec_decode_kernel(r_ref, lw_ref, k_ref, v_ref, kk_ref, bb_ref, s_ref, y_ref, so_ref):
    w = jnp.exp(lw_ref[...])
    a = -kk_ref[...]
    b, k, r = bb_ref[...], k_ref[...], r_ref[...]

    def row(vi, carry):
        s0 = s_ref[vi]
        sa = jnp.sum(s0 * a, axis=0, keepdims=True)
        s1 = s0 * w + sa * b + v_ref[pl.ds(vi, 1), :] * k
        so_ref[vi] = s1
        y_ref[pl.ds(vi, 1), :] = jnp.sum(s1 * r, axis=0, keepdims=True)
        return carry

    lax.fori_loop(0, RWKV_HEAD, row, 0, unroll=8)


def _rwkv_rec_decode(r, lw, k, v, kk, bb, s):
    m = r.shape[1]
    vec = pl.BlockSpec((RWKV_HEAD, m), lambda i: (i, 0))
    st = pl.BlockSpec((None, RWKV_HEAD, RWKV_HEAD, m), lambda i: (i, 0, 0, 0))
    return pl.pallas_call(
        _rwkv_rec_decode_kernel,
        grid=(RWKV_H,),
        in_specs=[vec] * 6 + [st],
        out_specs=[vec, st],
        out_shape=[jax.ShapeDtypeStruct((D_MODEL, m), F32), jax.ShapeDtypeStruct(s.shape, F32)],
        compiler_params=_params(("parallel",)),
        name="rwkv_rec_decode",
    )(r, lw, k, v, kk, bb, s)


def _rwkv_post_kernel(h_ref, y_ref, bo_ref, g_ref, lnw_ref, lnb_ref, wo_ref, nf_ref, wu_ref, wd_ref, o_ref):
    y = y_ref[...]
    d = y - _group_sum(y, RWKV_HEAD, split=True) * (1.0 / RWKV_HEAD)
    var = _group_sum(d * d, RWKV_HEAD) * (1.0 / RWKV_HEAD)
    yn = d * lax.rsqrt(var + RWKV_GN_EPS) * lnw_ref[...] + lnb_ref[...]
    h = h_ref[...] + _bdot((yn + bo_ref[...]) * g_ref[...], wo_ref[...])
    o_ref[...] = _mlp(h, nf_ref, wu_ref, wd_ref)


def _rwkv_post_ffn(h, y, bonus, g, p, nf, wu, wd, layer, tm=512):
    m = h.shape[0]
    tm = _row_tile(m, tm)
    row = pl.BlockSpec((tm, D_MODEL), lambda i: (i, 0))
    vec = _const_spec((1, D_MODEL))
    mlp_ins, mlp_specs = _mlp_operands(nf, wu, wd, layer)
    return pl.pallas_call(
        _rwkv_post_kernel,
        grid=(m // tm,),
        in_specs=[row] * 4 + [vec, vec, _const_spec((D_MODEL, D_MODEL))] + mlp_specs,
        out_specs=row,
        out_shape=jax.ShapeDtypeStruct((m, D_MODEL), F32),
        compiler_params=_params(("parallel",)),
        name="rwkv_post_ffn",
    )(h, y, bonus, g, p["ln_w"].reshape(1, D_MODEL), p["ln_b"].reshape(1, D_MODEL), p["wo"].astype(BF16),
      *mlp_ins)


def _gla_pre_kernel(h_ref, nm_ref, win_ref, wgl_ref, wgk_ref, bgk_ref, q_ref, k_ref, v_ref, g_ref, gk_ref):
    u = _rms(h_ref[...], nm_ref[...], NORM_EPS).astype(BF16)
    z = _nt(u, win_ref[...])
    q_ref[...] = z[:, :GLA_DK_TOT] * (GLA_DK ** -0.5)
    k_ref[...] = z[:, GLA_DK_TOT:2 * GLA_DK_TOT]
    v_ref[...] = z[:, 2 * GLA_DK_TOT:2 * GLA_DK_TOT + GLA_DV_TOT]
    g_ref[...] = z[:, 2 * GLA_DK_TOT + GLA_DV_TOT:]
    gl = _nt(u, wgl_ref[...])
    pre = _bdot(gl, wgk_ref[...]) + bgk_ref[...]
    gk_ref[...] = -_softplus(-pre) * (1.0 / GLA_NORMALIZER)


def _gla_pre(h, nm, p, tm=512):
    m = h.shape[0]
    tm = _row_tile(m, tm)
    n_main = 2 * GLA_DK_TOT + 2 * GLA_DV_TOT
    w_t = p["w_in"].T
    w_main = w_t[:n_main].astype(BF16)
    w_gl = jnp.pad(w_t[n_main:].astype(BF16), ((0, LANES - GLA_LR), (0, 0)))
    w_gk = jnp.pad(p["w_gk2"].astype(BF16), ((0, LANES - GLA_LR), (0, 0)))
    row = lambda n: pl.BlockSpec((tm, n), lambda i: (i, 0))
    widths = (GLA_DK_TOT, GLA_DK_TOT, GLA_DV_TOT, GLA_DV_TOT, GLA_DK_TOT)
    return pl.pallas_call(
        _gla_pre_kernel,
        grid=(m // tm,),
        in_specs=[row(D_MODEL), _const_spec((1, D_MODEL)), _const_spec(w_main.shape), _const_spec(w_gl.shape),
                  _const_spec(w_gk.shape), _const_spec((1, GLA_DK_TOT))],
        out_specs=[row(n) for n in widths],
        out_shape=[jax.ShapeDtypeStruct((m, n), F32) for n in widths],
        compiler_params=_params(("parallel",)),
        name="gla_pre",
    )(h, nm.reshape(1, D_MODEL), w_main, w_gl, w_gk, p["b_gk"].reshape(1, GLA_DK_TOT))


def _gla_rec_prompt_kernel(q_ref, k_ref, v_ref, gk_ref, o_ref, so_ref, s_scr):
    tc = q_ref.shape[0]

    @pl.when(pl.program_id(1) == 0)
    def _():
        s_scr[...] = jnp.zeros_like(s_scr)

    ri, ci = _iota((CHUNK, CHUNK), 0), _iota((CHUNK, CHUNK), 1)
    incl = ri >= ci
    n_sub = CHUNK // GLA_SUB
    sub_of_row = jnp.right_shift(_iota((CHUNK, 1), 0), GLA_SUB.bit_length() - 1)

    heads = range(GLA_H)
    ks = [slice(hd * GLA_DK, (hd + 1) * GLA_DK) for hd in heads]
    vs = [slice(hd * GLA_DV, (hd + 1) * GLA_DV) for hd in heads]
    n_chunks = tc // CHUNK
    items = [(c, hd) for c in range(n_chunks) for hd in heads]

    prep = []
    for c in range(n_chunks):
        rows = slice(c * CHUNK, (c + 1) * CHUNK)
        cum = _chunk_cumsum(gk_ref[rows, :])
        q, k = q_ref[rows, :], k_ref[rows, :]
        last = cum[CHUNK - 1:CHUNK, :]
        refs = [jnp.zeros_like(last)] + [cum[i * GLA_SUB - 1:i * GLA_SUB, :] for i in range(1, n_sub)]
        ref_of_row = refs[0]
        for i in range(1, n_sub):
            ref_of_row = jnp.where(sub_of_row >= i, refs[i], ref_of_row)
        q_in = (q * jnp.exp(cum - ref_of_row)).astype(BF16)
        k_in = []
        for i in range(n_sub):
            n = (i + 1) * GLA_SUB
            part = (k[:n] * jnp.exp(refs[i] - cum[:n])).astype(BF16)
            k_in.append(part if n == CHUNK else
                        jnp.concatenate([part, jnp.zeros((CHUNK - n, part.shape[1]), BF16)], axis=0))
        prep.append(dict(q_in=q_in, k_in=k_in, qe=(q * jnp.exp(cum)).astype(BF16),
                         kh=(k * jnp.exp(last - cum)).astype(BF16), g_last=jnp.exp(last),
                         v=v_ref[rows, :].astype(BF16)))

    def scores(c, hd):
        p = prep[c]
        blocks = [_nt(p["q_in"][i * GLA_SUB:(i + 1) * GLA_SUB, ks[hd]], p["k_in"][i][:, ks[hd]])
                  for i in range(n_sub)]
        return jnp.where(incl, jnp.concatenate(blocks, axis=0), 0.0).astype(BF16)

    att = [scores(c, hd) for c, hd in items]
    intra = [_mm(att[i], prep[c]["v"][:, vs[hd]]) for i, (c, hd) in enumerate(items)]
    kv = [_tn(prep[c]["v"][:, vs[hd]], prep[c]["kh"][:, ks[hd]]) for c, hd in items]

    s_in = []
    state = [s_scr[hd] for hd in heads]
    for c in range(n_chunks):
        s_in.extend(state[hd].astype(BF16) for hd in heads)
        state = [state[hd] * prep[c]["g_last"][:, ks[hd]] + kv[c * GLA_H + hd] for hd in heads]
    for hd in heads:
        s_scr[hd] = state[hd]

    for i, (c, hd) in enumerate(items):
        o_ref[c * CHUNK:(c + 1) * CHUNK, vs[hd]] = _nt(prep[c]["qe"][:, ks[hd]], s_in[i]) + intra[i]

    @pl.when(pl.program_id(1) == pl.num_programs(1) - 1)
    def _():
        for hd in range(GLA_H):
            so_ref[hd] = s_scr[hd].T


def _gla_rec_prompt(q, k, v, gk, tc=256):
    b, l, _ = q.shape
    tc = _row_tile(l, tc)
    tile = lambda n: pl.BlockSpec((None, tc, n), lambda i, j: (i, j, 0))
    st = (GLA_H, GLA_DK, GLA_DV)
    return pl.pallas_call(
        _gla_rec_prompt_kernel,
        grid=(b, l // tc),
        in_specs=[tile(GLA_DK_TOT), tile(GLA_DK_TOT), tile(GLA_DV_TOT), tile(GLA_DK_TOT)],
        out_specs=[tile(GLA_DV_TOT), pl.BlockSpec((None,) + st, lambda i, j: (i, 0, 0, 0))],
        out_shape=[jax.ShapeDtypeStruct((b, l, GLA_DV_TOT), F32), jax.ShapeDtypeStruct((b,) + st, F32)],
        scratch_shapes=[pltpu.VMEM((GLA_H, GLA_DV, GLA_DK), F32)],
        compiler_params=_params(("parallel", "arbitrary")),
        name="gla_rec_prompt",
    )(q, k, v, gk)


def _gla_rec_decode_kernel(q_ref, k_ref, v_ref, gk_ref, s_ref, o_ref, so_ref):
    eye = (_iota((GLA_DK, GLA_DK), 0) == _iota((GLA_DK, GLA_DK), 1)).astype(F32)[None]

    def col(x):
        return jnp.sum(eye * x, axis=-1, keepdims=True)

    for hd in range(GLA_H):
        hs = slice(hd, hd + 1)
        s1 = s_ref[:, hd] * col(jnp.exp(gk_ref[:, hs, :])) + col(k_ref[:, hs, :]) * v_ref[:, hs, :]
        so_ref[:, hd] = s1
        o_ref[:, hs, :] = jnp.sum(col(q_ref[:, hs, :]) * s1, axis=1, keepdims=True)


def _gla_rec_decode(q, k, v, gk, s, bblk=16):
    m = q.shape[0]
    kvec = pl.BlockSpec((bblk, GLA_H, GLA_DK), lambda i: (i, 0, 0))
    vvec = pl.BlockSpec((bblk, GLA_H, GLA_DV), lambda i: (i, 0, 0))
    st = pl.BlockSpec((bblk, GLA_H, GLA_DK, GLA_DV), lambda i: (i, 0, 0, 0))
    hk = lambda x: x.reshape(m, GLA_H, GLA_DK)
    o, s_new = pl.pallas_call(
        _gla_rec_decode_kernel,
        grid=(m // bblk,),
        in_specs=[kvec, kvec, vvec, kvec, st],
        out_specs=[vvec, st],
        out_shape=[jax.ShapeDtypeStruct((m, GLA_H, GLA_DV), F32), jax.ShapeDtypeStruct(s.shape, F32)],
        compiler_params=_params(("parallel",)),
        name="gla_rec_decode",
    )(hk(q), hk(k), v.reshape(m, GLA_H, GLA_DV), hk(gk), s)
    return o.reshape(m, GLA_DV_TOT), s_new


def _gla_post_kernel(h_ref, o_ref_in, g_ref, nw_ref, wo_ref, nf_ref, wu_ref, wd_ref, out_ref):
    o = o_ref_in[...]
    parts = []
    for hd in range(GLA_H):
        oh = o[:, hd * GLA_DV:(hd + 1) * GLA_DV]
        parts.append(oh * lax.rsqrt(jnp.mean(oh * oh, axis=-1, keepdims=True) + GLA_NORM_EPS))
    on = jnp.concatenate(parts, axis=1) * nw_ref[...]
    g = g_ref[...]
    h = h_ref[...] + _bdot(on * (g * jax.nn.sigmoid(g)), wo_ref[...])
    out_ref[...] = _mlp(h, nf_ref, wu_ref, wd_ref)


def _gla_post_ffn(h, o, g, p, nf, wu, wd, layer, tm=512):
    m = h.shape[0]
    tm = _row_tile(m, tm)
    row = pl.BlockSpec((tm, D_MODEL), lambda i: (i, 0))
    mlp_ins, mlp_specs = _mlp_operands(nf, wu, wd, layer)
    return pl.pallas_call(
        _gla_post_kernel,
        grid=(m // tm,),
        in_specs=[row] * 3 + [_const_spec((1, D_MODEL)), _const_spec((D_MODEL, D_MODEL))] + mlp_specs,
        out_specs=row,
        out_shape=jax.ShapeDtypeStruct((m, D_MODEL), F32),
        compiler_params=_params(("parallel",)),
        name="gla_post_ffn",
    )(h, o, g, jnp.tile(p["norm"], GLA_H).reshape(1, D_MODEL), p["wo"].astype(BF16), *mlp_ins)


def _conv_gates(h_ref, nm_ref, win_ref):
    u = _rms(h_ref[...], nm_ref[...], NORM_EPS).astype(BF16)
    z = jnp.dot(u, win_ref[...], preferred_element_type=F32)
    return z[:, :D_MODEL], z[:, D_MODEL:2 * D_MODEL] * z[:, 2 * D_MODEL:]


def _conv_prompt_kernel(h_ref, nm_ref, win_ref, cw_ref, wo_ref, o_ref, st_ref, carry_ref):
    tm = h_ref.shape[0]

    @pl.when(pl.program_id(1) == 0)
    def _():
        carry_ref[...] = jnp.zeros_like(carry_ref)

    ts = tm // SUBTILES
    subs = [slice(i * ts, (i + 1) * ts) for i in range(SUBTILES)]
    gates = [_conv_gates(h_ref.at[sl], nm_ref, win_ref) for sl in subs]
    row = _iota((ts, D_MODEL), 0)
    prev2, prev1 = carry_ref[0:1, :], carry_ref[1:2, :]
    gated = []
    for g_b, zc in gates:
        z1 = jnp.where(row == 0, prev1, pltpu.roll(zc, 1, axis=0))
        z2 = jnp.where(row == 0, prev2, jnp.where(row == 1, prev1, pltpu.roll(zc, 2, axis=0)))
        gated.append((g_b * (cw_ref[0:1, :] * z2 + cw_ref[1:2, :] * z1 + cw_ref[2:3, :] * zc)).astype(BF16))
        prev2, prev1 = zc[ts - 2:ts - 1, :], zc[ts - 1:ts, :]
    for sl, x in zip(subs, gated):
        o_ref[sl, :] = h_ref[sl, :] + jnp.dot(x, wo_ref[...], preferred_element_type=F32)
    tail = gates[-1][1][ts - (CONV_W - 1):, :]
    carry_ref[0:CONV_W - 1, :] = tail
    st_ref[...] = tail


def _conv_prompt(h, nm, p, tm=1024):
    b, l, _ = h.shape
    tm = _row_tile(l, tm)
    tile = pl.BlockSpec((None, tm, D_MODEL), lambda i, j: (i, j, 0))
    return pl.pallas_call(
        _conv_prompt_kernel,
        grid=(b, l // tm),
        in_specs=[tile, _const_spec((1, D_MODEL)), _const_spec((D_MODEL, 3 * D_MODEL)),
                  _const_spec((CONV_W, D_MODEL)), _const_spec((D_MODEL, D_MODEL))],
        out_specs=[tile, pl.BlockSpec((None, CONV_W - 1, D_MODEL), lambda i, j: (i, 0, 0))],
        out_shape=[jax.ShapeDtypeStruct((b, l, D_MODEL), F32),
                   jax.ShapeDtypeStruct((b, CONV_W - 1, D_MODEL), F32)],
        scratch_shapes=[pltpu.VMEM((SUBLANES, D_MODEL), F32)],
        compiler_params=_params(("parallel", "arbitrary")),
        name="conv_prompt",
    )(h, nm.reshape(1, D_MODEL), p["w_in"].astype(BF16), p["w"], p["wo"].astype(BF16))


def _conv_decode_kernel(h_ref, buf_ref, nm_ref, win_ref, cw_ref, wo_ref, o_ref, st_ref):
    g_b, zc = _conv_gates(h_ref, nm_ref, win_ref)
    z2 = buf_ref[:, 0, :]
    z1 = buf_ref[:, 1, :]
    conv = cw_ref[0:1, :] * z2 + cw_ref[1:2, :] * z1 + cw_ref[2:3, :] * zc
    o_ref[...] = h_ref[...] + _bdot(g_b * conv, wo_ref[...])
    st_ref[:, 0, :] = z1
    st_ref[:, 1, :] = zc


def _conv_decode(h, buf, nm, p):
    m = h.shape[0]
    full = pl.BlockSpec((m, D_MODEL), lambda i: (0, 0))
    bufs = pl.BlockSpec(buf.shape, lambda i: (0, 0, 0))
    return pl.pallas_call(
        _conv_decode_kernel,
        grid=(1,),
        in_specs=[full, bufs, _const_spec((1, D_MODEL)), _const_spec((D_MODEL, 3 * D_MODEL)),
                  _const_spec((CONV_W, D_MODEL)), _const_spec((D_MODEL, D_MODEL))],
        out_specs=[full, bufs],
        out_shape=[jax.ShapeDtypeStruct((m, D_MODEL), F32), jax.ShapeDtypeStruct(buf.shape, F32)],
        compiler_params=_params(("arbitrary",)),
        name="conv_decode",
    )(h, buf, nm.reshape(1, D_MODEL), p["w_in"].astype(BF16), p["w"], p["wo"].astype(BF16))


def _pool_project(d_groups, pw_ref, sc_ref):
    ys = [_bdot(d, pw_ref[gi]) for gi, d in enumerate(d_groups)]
    return jnp.concatenate(ys, axis=1) * sc_ref[...]


def _pool_prompt_kernel(h_ref, nm_ref, pw_ref, sc_ref, nf_ref, wu_ref, wd_ref, gf_ref, o_ref, st_ref, carry_ref):
    tm = h_ref.shape[0]
    hist = carry_ref.shape[0]
    j = pl.program_id(1)

    @pl.when(j == 0)
    def _():
        carry_ref[...] = jnp.zeros_like(carry_ref)

    u = _rms(h_ref[...], nm_ref[...], NORM_EPS)
    ext = jnp.concatenate([carry_ref[...], u], axis=0)
    pos = j * tm + _iota((tm, 1), 0)
    d_groups = []
    for gi, w in enumerate(POOL_WINDOWS):
        sl = slice(gi * POOL_G, (gi + 1) * POOL_G)
        s = ext[:, sl]
        span = 1
        while span < w:
            s = s + pltpu.roll(s, span, axis=0)
            span *= 2
        cnt = jnp.minimum(w, pos + 1).astype(F32)
        d_groups.append(s[hist:, :] / cnt - u[:, sl])
    o_ref[...] = _mlp(h_ref[...] + _pool_project(d_groups, pw_ref, sc_ref), nf_ref, wu_ref, wd_ref, gf_ref)
    tail = u[tm - hist:, :]
    carry_ref[...] = tail
    st_ref[...] = tail


def _pool_prompt_ffn(h, nm, p, nf, wu, wd, layer, final_g, tm=512):
    b, l, _ = h.shape
    tm = _row_tile(l, tm)
    hist = POOL_BUF + 1
    tile = pl.BlockSpec((None, tm, D_MODEL), lambda i, j: (i, j, 0))
    mlp_ins, mlp_specs = _mlp_operands(nf, wu, wd, layer, final_g)
    out, st = pl.pallas_call(
        _pool_prompt_kernel,
        grid=(b, l // tm),
        in_specs=[tile, _const_spec((1, D_MODEL)), _const_spec(p["w"].shape), _const_spec((1, D_MODEL))]
        + mlp_specs,
        out_specs=[tile, pl.BlockSpec((None, hist, D_MODEL), lambda i, j: (i, 0, 0))],
        out_shape=[jax.ShapeDtypeStruct((b, l, D_MODEL), F32), jax.ShapeDtypeStruct((b, hist, D_MODEL), F32)],
        scratch_shapes=[pltpu.VMEM((hist, D_MODEL), F32)],
        compiler_params=_params(("parallel", "arbitrary")),
        name="pool_prompt_ffn",
    )(h, nm.reshape(1, D_MODEL), p["w"].astype(BF16), p["scale"].reshape(1, D_MODEL), *mlp_ins)
    return out, st[:, 1:]


def _pool_decode_kernel(h_ref, buf_ref, nm_ref, pw_ref, sc_ref, o_ref, st_ref):
    u = _rms(h_ref[...], nm_ref[...], NORM_EPS)
    d_groups = []
    for gi, w in enumerate(POOL_WINDOWS):
        sl = slice(gi * POOL_G, (gi + 1) * POOL_G)
        s = u[:, sl]
        for i in range(1, w):
            s = s + buf_ref[POOL_BUF - i, :, sl]
        cnt = float(min(w, PAST_LEN + 1))
        d_groups.append(s / cnt - u[:, sl])
    o_ref[...] = h_ref[...] + _pool_project(d_groups, pw_ref, sc_ref)
    st_ref[0:POOL_BUF - 1] = buf_ref[1:POOL_BUF]
    st_ref[POOL_BUF - 1] = u


def _pool_decode(h, buf, nm, p):
    m = h.shape[0]
    full = pl.BlockSpec((m, D_MODEL), lambda i: (0, 0))
    bufs = pl.BlockSpec(buf.shape, lambda i: (0, 0, 0))
    return pl.pallas_call(
        _pool_decode_kernel,
        grid=(1,),
        in_specs=[full, bufs, _const_spec((1, D_MODEL)), _const_spec(p["w"].shape), _const_spec((1, D_MODEL))],
        out_specs=[full, bufs],
        out_shape=[jax.ShapeDtypeStruct((m, D_MODEL), F32), jax.ShapeDtypeStruct(buf.shape, F32)],
        compiler_params=_params(("arbitrary",)),
        name="pool_decode",
    )(h, buf, nm.reshape(1, D_MODEL), p["w"].astype(BF16), p["scale"].reshape(1, D_MODEL))


def _trunk_prompt(x, nm, nf, nfin, wu, wd, rw, gl, cv, po):
    b, l, _ = x.shape
    flat = lambda t: t.reshape(b * l, t.shape[-1])
    seq = lambda t: t.reshape(b, l, t.shape[-1])

    (r, lw, k, v, kk, bb, g, bonus), shift = _rwkv_pre_prompt(x, nm[0], rw)
    y, wkv = _rwkv_rec_prompt(r, lw, k, v, kk, bb)
    h = _rwkv_post_ffn(flat(x), flat(y), flat(bonus), flat(g), rw, nf[0], wu, wd, 0)

    q, k, v, g, gk = _gla_pre(h, nm[1], gl)
    o, gla_s = _gla_rec_prompt(seq(q), seq(k), seq(v), seq(gk))
    h = _gla_post_ffn(h, flat(o), g, gl, nf[1], wu, wd, 1)

    h, conv_s = _conv_prompt(seq(h), nm[2], cv)
    h = _ffn(flat(h), nf[2], wu, wd, 2)

    y, pool_s = _pool_prompt_ffn(seq(h), nm[3], po, nf[3], wu, wd, 3, nfin)
    return y, wkv[None], shift[None], gla_s[None], conv_s[None], pool_s[None]


def _trunk_decode(x, st_wkv, st_shift, st_gla, st_conv, st_pool, nm, nf, nfin, wu, wd, rw, gl, cv, po):
    m = x.shape[0]
    h = x.reshape(m, D_MODEL)

    (r, lw, k, v, kk, bb, g, bonus), shift = _rwkv_pre_decode(h, st_shift[0], nm[0], rw)
    y_t, wkv_t = _rwkv_rec_decode(r, lw, k, v, kk, bb, jnp.transpose(st_wkv[0], (1, 2, 3, 0)))
    wkv = jnp.transpose(wkv_t, (3, 0, 1, 2))
    h = _rwkv_post_ffn(h, y_t.T, bonus, g, rw, nf[0], wu, wd, 0)

    q, k, v, g, gk = _gla_pre(h, nm[1], gl)
    o, gla_s = _gla_rec_decode(q, k, v, gk, st_gla[0])
    h = _gla_post_ffn(h, o, g, gl, nf[1], wu, wd, 1)

    h, conv_s = _conv_decode(h, st_conv[0], nm[2], cv)
    h = _ffn(h, nf[2], wu, wd, 2)

    h, pool_t = _pool_decode(h, jnp.transpose(st_pool[0], (1, 0, 2)), nm[3], po)
    pool_s = jnp.transpose(pool_t, (1, 0, 2))
    y = _ffn(h, nf[3], wu, wd, 3, final_g=nfin)
    return y.reshape(m, 1, D_MODEL), wkv[None], shift[None], gla_s[None], conv_s[None], pool_s[None]


def kernel(x_prompt, x_sample, state_rwkv_wkv, state_rwkv_shift, state_gla, state_conv, state_pool, norm_mix, norm_ffn, norm_final, ffn_up, ffn_down, rwkv_mu, rwkv_w_rkv, rwkv_w0, rwkv_w1, rwkv_w2, rwkv_a0, rwkv_a1, rwkv_a2, rwkv_g1, rwkv_g2, rwkv_k_k, rwkv_k_a, rwkv_r_k, rwkv_ln_w, rwkv_ln_b, rwkv_wo, gla_w_in, gla_w_gk2, gla_b_gk, gla_norm, gla_wo, conv_w_in, conv_w, conv_wo, pool_w, pool_scale):
    assert x_prompt.shape[1] % CHUNK == 0 and x_sample.shape[1] == 1
    wu = ffn_up.astype(BF16)
    wd = ffn_down.astype(BF16)
    rw = dict(mu=rwkv_mu[0], w_rkv=rwkv_w_rkv[0], w0=rwkv_w0[0], w1=rwkv_w1[0], w2=rwkv_w2[0], a0=rwkv_a0[0],
              a1=rwkv_a1[0], a2=rwkv_a2[0], g1=rwkv_g1[0], g2=rwkv_g2[0], k_k=rwkv_k_k[0], k_a=rwkv_k_a[0],
              r_k=rwkv_r_k[0], ln_w=rwkv_ln_w[0], ln_b=rwkv_ln_b[0], wo=rwkv_wo[0])
    gl = dict(w_in=gla_w_in[0], w_gk2=gla_w_gk2[0], b_gk=gla_b_gk[0], norm=gla_norm[0], wo=gla_wo[0])
    cv = dict(w_in=conv_w_in[0], w=conv_w[0], wo=conv_wo[0])
    po = dict(w=pool_w[0], scale=pool_scale[0])
    shared = (norm_mix, norm_ffn, norm_final, wu, wd, rw, gl, cv, po)
    y_p, wkv_p, sh_p, gla_p, conv_p, pool_p = _trunk_prompt(x_prompt, *shared)
    y_s, wkv_s, sh_s, gla_s, conv_s, pool_s = _trunk_decode(
        x_sample, state_rwkv_wkv, state_rwkv_shift, state_gla, state_conv, state_pool, *shared)
    return (y_p, y_s, wkv_p, wkv_s, sh_p, sh_s, gla_p, gla_s, conv_p, conv_s, pool_p, pool_s)
```

```python
import functools
import math

import jax
import jax.numpy as jnp
from jax import lax
from jax.experimental import pallas as pl
from jax.experimental.pallas import tpu as pltpu

F32 = jnp.float32
BF16 = jnp.bfloat16

D_MODEL = 1024
D_FF = 4 * D_MODEL
NORM_EPS = 1e-6
PAST_LEN = 16384

RWKV_HEAD = 64
RWKV_H = D_MODEL // RWKV_HEAD
RWKV_GN_EPS = 64e-5
RWKV_QUAD = 4

GLA_H = 4
GLA_DK_TOT = D_MODEL // 2
GLA_DV_TOT = D_MODEL
GLA_DK = GLA_DK_TOT // GLA_H
GLA_DV = GLA_DV_TOT // GLA_H
GLA_LR = 16
GLA_NORMALIZER = 16.0
GLA_NORM_EPS = 1e-5
GLA_SUB = 16

CONV_W = 3
POOL_WINDOWS = (2, 4, 8, 16)
POOL_G = D_MODEL // len(POOL_WINDOWS)
POOL_BUF = max(POOL_WINDOWS) - 1

LANES = 128
SUBLANES = 8
MXU_DIM = 256
VMEM_LIMIT_BYTES = 56 * 1024 * 1024

CHUNK = 64
FFN_COLS = 1024
SUBTILES = 2


def _const_spec(shape):
    nd = len(shape)
    return pl.BlockSpec(shape, lambda *_: (0,) * nd, pipeline_mode=pl.Buffered(1))


def _params(sem):
    return pltpu.CompilerParams(dimension_semantics=sem, vmem_limit_bytes=VMEM_LIMIT_BYTES)


def _rms(x, g, eps):
    return x * lax.rsqrt(jnp.mean(x * x, axis=-1, keepdims=True) + eps) * g


def _bdot(a, w):
    return jnp.dot(a.astype(BF16), w, preferred_element_type=F32)


def _mm(a, b, prec=None):
    return jnp.dot(a, b, preferred_element_type=F32, precision=prec)


def _nt(a, b, prec=None):
    return lax.dot_general(a, b, (((1,), (1,)), ((), ())), preferred_element_type=F32, precision=prec)


def _tn(a, b, prec=None):
    return lax.dot_general(a, b, (((0,), (0,)), ((), ())), preferred_element_type=F32, precision=prec)


def _iota(shape, dim):
    return lax.broadcasted_iota(jnp.int32, shape, dim)


def _group_sum(x, group, split=False):
    shift = group.bit_length() - 1
    bd = (jnp.right_shift(_iota((MXU_DIM, MXU_DIM), 0), shift)
          == jnp.right_shift(_iota((MXU_DIM, MXU_DIM), 1), shift)).astype(BF16)
    terms = [x.astype(BF16)]
    if split:
        terms.append((x - terms[0].astype(F32)).astype(BF16))
    cols = [sum(_mm(t[:, c * MXU_DIM:(c + 1) * MXU_DIM], bd) for t in terms)
            for c in range(x.shape[1] // MXU_DIM)]
    return jnp.concatenate(cols, axis=1)


def _chunk_cumsum(x):
    c = x.shape[0]
    ltri = (_iota((c, c), 0) >= _iota((c, c), 1)).astype(BF16)
    t0 = x.astype(BF16)
    r1 = x - t0.astype(F32)
    t1 = r1.astype(BF16)
    t2 = (r1 - t1.astype(F32)).astype(BF16)
    return _mm(ltri, t0) + _mm(ltri, t1) + _mm(ltri, t2)


def _softplus(x):
    return jnp.maximum(x, 0.0) + jnp.log(1.0 + jnp.exp(-jnp.abs(x)))


def _row_tile(m, want):
    t = min(want, m)
    assert m % t == 0, (m, t)
    return t


def _mlp(h, g_ref, wu_ref, wd_ref, gf_ref=None):
    x = _rms(h, g_ref[...], NORM_EPS).astype(BF16)
    acc = h
    for c in range(D_FF // FFN_COLS):
        sl = slice(c * FFN_COLS, (c + 1) * FFN_COLS)
        a = jnp.dot(x, wu_ref[:, sl], preferred_element_type=F32)
        a = jnp.square(jnp.maximum(a, 0.0)).astype(BF16)
        acc = acc + jnp.dot(a, wd_ref[sl, :], preferred_element_type=F32)
    if gf_ref is not None:
        acc = _rms(acc, gf_ref[...], NORM_EPS)
    return acc


def _mlp_operands(mlp, final_g=None):
    nf, wu, wd = mlp
    ins = [nf.reshape(1, D_MODEL), wu, wd]
    specs = [_const_spec((1, D_MODEL)), _const_spec(wu.shape), _const_spec(wd.shape)]
    if final_g is not None:
        ins.append(final_g.reshape(1, D_MODEL))
        specs.append(_const_spec((1, D_MODEL)))
    return ins, specs


def _ffn_kernel(h_ref, g_ref, wu_ref, wd_ref, *rest):
    *gf_ref, o_ref = rest
    o_ref[...] = _mlp(h_ref[...], g_ref, wu_ref, wd_ref, *gf_ref)


def _ffn(h, mlp, final_g=None, tm=1024):
    m = h.shape[0]
    tm = _row_tile(m, tm)
    row = pl.BlockSpec((tm, D_MODEL), lambda i: (i, 0))
    mlp_ins, mlp_specs = _mlp_operands(mlp, final_g)
    return pl.pallas_call(
        _ffn_kernel,
        grid=(m // tm,),
        in_specs=[row] + mlp_specs,
        out_specs=row,
        out_shape=jax.ShapeDtypeStruct((m, D_MODEL), F32),
        compiler_params=_params(("parallel",)),
        name="ffn",
    )(h, *mlp_ins)


def _ffn_cast_kernel(h_ref, g_ref, wu_ref, wd_ref, *rest):
    *gf_ref, o_ref, wub_ref, wdb_ref, x_scr, acc_scr = rest
    c = pl.program_id(0)

    @pl.when(c == 0)
    def _():
        h = h_ref[...]
        x_scr[...] = _rms(h, g_ref[...], NORM_EPS).astype(BF16)
        acc_scr[...] = h

    wu = wu_ref[...].astype(BF16)
    wd = wd_ref[...].astype(BF16)
    wub_ref[...] = wu
    wdb_ref[...] = wd
    a = jnp.dot(x_scr[...], wu, preferred_element_type=F32)
    a = jnp.square(jnp.maximum(a, 0.0)).astype(BF16)
    acc_scr[...] += jnp.dot(a, wd, preferred_element_type=F32)

    @pl.when(c == pl.num_programs(0) - 1)
    def _():
        acc = acc_scr[...]
        o_ref[...] = _rms(acc, gf_ref[0][...], NORM_EPS) if gf_ref else acc


def _ffn_cast(h, nf, w_up, w_down, layer, final_g=None, cols=FFN_COLS):
    m = h.shape[0]
    full = pl.BlockSpec((m, D_MODEL), lambda c: (0, 0))
    vec = pl.BlockSpec((1, D_MODEL), lambda c: (0, 0))
    ins = [h, nf.reshape(1, D_MODEL), w_up, w_down]
    specs = [full, vec, pl.BlockSpec((None, D_MODEL, cols), lambda c: (layer, 0, c)),
             pl.BlockSpec((None, cols, D_MODEL), lambda c: (layer, c, 0))]
    if final_g is not None:
        ins.append(final_g.reshape(1, D_MODEL))
        specs.append(vec)
    return pl.pallas_call(
        _ffn_cast_kernel,
        grid=(D_FF // cols,),
        in_specs=specs,
        out_specs=[full, pl.BlockSpec((D_MODEL, cols), lambda c: (0, c)),
                   pl.BlockSpec((cols, D_MODEL), lambda c: (c, 0))],
        out_shape=[jax.ShapeDtypeStruct((m, D_MODEL), F32), jax.ShapeDtypeStruct((D_MODEL, D_FF), BF16),
                   jax.ShapeDtypeStruct((D_FF, D_MODEL), BF16)],
        scratch_shapes=[pltpu.VMEM((m, D_MODEL), BF16), pltpu.VMEM((m, D_MODEL), F32)],
        compiler_params=_params(("arbitrary",)),
        name="ffn_cast",
    )(*ins)


def _rwkv_pre_math(u, prev, mu_ref, wrkv_ref, w0_ref, w1_ref, w2_ref, a0_ref, a1_ref, a2_ref,
                   g1_ref, g2_ref, kk_ref, ka_ref, rk_ref):
    dx = prev - u

    def mix(i):
        return (u + dx * mu_ref[i:i + 1, :]).astype(BF16)

    r = _bdot(mix(0), wrkv_ref[0])
    k = _bdot(mix(2), wrkv_ref[1])
    v = _bdot(mix(3), wrkv_ref[2])
    wl = w0_ref[...] + _bdot(jnp.tanh(_bdot(mix(1), w1_ref[...])), w2_ref[...])
    lw = jax.nn.sigmoid(wl) * (-math.exp(-0.5))
    a = jax.nn.sigmoid(a0_ref[...] + _bdot(_bdot(mix(4), a1_ref[...]), a2_ref[...]))
    g = _bdot(jax.nn.sigmoid(_bdot(mix(5), g1_ref[...])), g2_ref[...])
    kk = k * kk_ref[...]
    kk = kk * lax.rsqrt(jnp.maximum(_group_sum(kk * kk, RWKV_HEAD), 1e-24))
    k = k * (1.0 + (a - 1.0) * ka_ref[...])
    bonus = _group_sum(r * k * rk_ref[...], RWKV_HEAD) * v
    return r, lw, k, v, kk, kk * a, g, bonus


_N_RWKV_W = 13


def _rwkv_pre_prompt_kernel(h_ref, nm_ref, *rest):
    w = rest[:_N_RWKV_W]
    outs = rest[_N_RWKV_W:_N_RWKV_W + 8]
    sh_ref, carry_ref = rest[_N_RWKV_W + 8:]
    tm = h_ref.shape[0]

    @pl.when(pl.program_id(1) == 0)
    def _():
        carry_ref[...] = jnp.zeros_like(carry_ref)

    u = _rms(h_ref[...], nm_ref[...], NORM_EPS)
    prev = jnp.where(_iota(u.shape, 0) == 0, carry_ref[0:1, :], pltpu.roll(u, 1, axis=0))
    last = u[tm - 1:tm, :]
    carry_ref[0:1, :] = last
    sh_ref[...] = last
    for o_ref, val in zip(outs, _rwkv_pre_math(u, prev, *w)):
        o_ref[...] = val


def _rwkv_pre_decode_kernel(h_ref, prev_ref, nm_ref, *rest):
    w = rest[:_N_RWKV_W]
    outs = rest[_N_RWKV_W:_N_RWKV_W + 8]
    (sh_ref,) = rest[_N_RWKV_W + 8:]
    u = _rms(h_ref[...], nm_ref[...], NORM_EPS)
    sh_ref[...] = u
    vals = _rwkv_pre_math(u, prev_ref[...], *w)
    for o_ref, val in zip(outs[:6], vals[:6]):
        o_ref[...] = val.T
    for o_ref, val in zip(outs[6:], vals[6:]):
        o_ref[...] = val


def _rwkv_weights(p):
    row = lambda x: x.reshape(1, D_MODEL)
    return [p["mu"], p["w_rkv"].astype(BF16), row(p["w0"]), p["w1"].astype(BF16), p["w2"].astype(BF16),
            row(p["a0"]), p["a1"].astype(BF16), p["a2"].astype(BF16), p["g1"].astype(BF16),
            p["g2"].astype(BF16), row(p["k_k"]), row(p["k_a"]), row(p["r_k"])]


def _rwkv_pre_prompt(h, nm, p, tm=512):
    b, l, _ = h.shape
    tm = _row_tile(l, tm)
    ws = _rwkv_weights(p)
    tile = pl.BlockSpec((None, tm, D_MODEL), lambda i, j: (i, j, 0))
    outs = pl.pallas_call(
        _rwkv_pre_prompt_kernel,
        grid=(b, l // tm),
        in_specs=[tile, _const_spec((1, D_MODEL))] + [_const_spec(w.shape) for w in ws],
        out_specs=[tile] * 8 + [pl.BlockSpec((None, 1, D_MODEL), lambda i, j: (i, 0, 0))],
        out_shape=[jax.ShapeDtypeStruct((b, l, D_MODEL), F32)] * 8
        + [jax.ShapeDtypeStruct((b, 1, D_MODEL), F32)],
        scratch_shapes=[pltpu.VMEM((SUBLANES, D_MODEL), F32)],
        compiler_params=_params(("parallel", "arbitrary")),
        name="rwkv_pre_prompt",
    )(h, nm.reshape(1, D_MODEL), *ws)
    return outs[:8], outs[8].reshape(b, D_MODEL)


def _rwkv_pre_decode(h, prev, nm, p):
    m = h.shape[0]
    ws = _rwkv_weights(p)
    full = pl.BlockSpec((m, D_MODEL), lambda i: (0, 0))
    full_t = pl.BlockSpec((D_MODEL, m), lambda i: (0, 0))
    outs = pl.pallas_call(
        _rwkv_pre_decode_kernel,
        grid=(1,),
        in_specs=[full, full, _const_spec((1, D_MODEL))] + [_const_spec(w.shape) for w in ws],
        out_specs=[full_t] * 6 + [full] * 3,
        out_shape=[jax.ShapeDtypeStruct((D_MODEL, m), F32)] * 6 + [jax.ShapeDtypeStruct((m, D_MODEL), F32)] * 3,
        compiler_params=_params(("arbitrary",)),
        name="rwkv_pre_decode",
    )(h, prev, nm.reshape(1, D_MODEL), *ws)
    return outs[:8], outs[8]


def _rwkv_rec_prompt_kernel(r_ref, lw_ref, k_ref, v_ref, kk_ref, bb_ref, y_ref, so_ref, s_scr):
    tc = r_ref.shape[0]
    qw = RWKV_QUAD * RWKV_HEAD
    n_quads = D_MODEL // qw
    assert RWKV_QUAD * CHUNK == qw

    @pl.when(pl.program_id(1) == 0)
    def _():
        s_scr[...] = jnp.zeros_like(s_scr)

    t_idx, s_idx = _iota((CHUNK, qw), 0), _iota((CHUNK, qw), 1) & (CHUNK - 1)
    strict = t_idx > s_idx
    incl = t_idx >= s_idx
    eye = (t_idx == s_idx).astype(F32)
    blocks = jnp.right_shift(_iota((qw, qw), 0), 6) == jnp.right_shift(_iota((qw, qw), 1), 6)

    def bdiag(z):
        zb = z.astype(BF16)
        return jnp.where(blocks, jnp.concatenate([zb] * RWKV_QUAD, axis=0), jnp.zeros((), BF16))

    quads = range(n_quads)
    sls = [slice(q * qw, (q + 1) * qw) for q in quads]
    cc = CHUNK

    n_chunks = tc // CHUNK
    items = [(c, q) for c in range(n_chunks) for q in quads]
    every = range(len(items))

    scaled = []
    for c in range(n_chunks):
        rows = slice(c * CHUNK, (c + 1) * CHUNK)
        lw = lw_ref[rows, :]
        cum = _chunk_cumsum(lw)
        g_in = jnp.exp(cum)
        g_inv = jnp.exp(-cum)
        g_last = g_in[CHUNK - 1:CHUNK, :]
        scaled.append(dict(
            rt=r_ref[rows, :] * g_in, at=-kk_ref[rows, :] * jnp.exp(cum - lw),
            bt=bb_ref[rows, :] * g_inv, kt=k_ref[rows, :] * g_inv, v=v_ref[rows, :], g_last=g_last))

    def part(name, i):
        c, q = items[i]
        return scaled[c][name][:, sls[q]]

    ar = [jnp.concatenate([part("at", i), part("rt", i)], axis=0).astype(BF16) for i in every]
    bk4 = [jnp.concatenate([bdiag(part("bt", i)), bdiag(part("kt", i))], axis=0) for i in every]
    gram = [_nt(ar[i], bk4[i]) for i in every]
    a_ab = [jnp.where(strict, gram[i][:cc, :qw], 0.0) for i in every]
    a_k = [jnp.concatenate([jnp.where(strict, gram[i][:cc, qw:], 0.0),
                            jnp.where(incl, gram[i][cc:, qw:], 0.0)], axis=0).astype(BF16) for i in every]
    a_rb = [jnp.where(incl, gram[i][cc:, :qw], 0.0).astype(BF16) for i in every]
    akv = [_mm(a_k[i], bdiag(part("v", i))) for i in every]
    inv = [eye + a_ab[i] for i in every]
    pw = [_mm(a_ab[i].astype(BF16), bdiag(a_ab[i])) for i in every]
    for _ in range(1, CHUNK.bit_length() - 2):
        both = [_mm(jnp.concatenate([pw[i], inv[i]], axis=0).astype(BF16), bdiag(pw[i])) for i in every]
        inv = [inv[i] + both[i][cc:] for i in every]
        pw = [both[i][:cc] for i in every]
    inv = [(inv[i] + _mm(inv[i].astype(BF16), bdiag(pw[i]))).astype(BF16) for i in every]
    tt = [_mm(inv[i], jnp.concatenate([bdiag(part("at", i)), bdiag(akv[i][:cc])], axis=1)) for i in every]
    ta = [tt[i][:, :qw].astype(BF16) for i in every]
    tav = [tt[i][:, qw:] for i in every]
    bkg = [(jnp.concatenate([part("bt", i), part("kt", i)], axis=0) * part("g_last", i)).astype(BF16)
           for i in every]
    w = [jnp.where(blocks, _tn(ta[i], bkg[i][:cc]), 0.0).astype(BF16) for i in every]
    n = [jnp.where(blocks, _tn(jnp.concatenate([tav[i], part("v", i)], axis=0).astype(BF16), bkg[i]), 0.0)
         for i in every]

    s_in = []
    state = [s_scr[q] for q in quads]
    for c in range(n_chunks):
        s_b = [state[q].astype(BF16) for q in quads]
        s_in.extend(s_b)
        state = [state[q] * scaled[c]["g_last"][:, sls[q]] + _mm(s_b[q], w[c * n_quads + q]) + n[c * n_quads + q]
                 for q in quads]
    for q in quads:
        s_scr[q] = state[q]

    xs = [_nt(jnp.concatenate([ta[i], ar[i][cc:]], axis=0), s_in[i]) for i in every]
    u = [xs[i][:cc] + tav[i] for i in every]
    for i in every:
        c, q = items[i]
        y_ref[c * CHUNK:(c + 1) * CHUNK, sls[q]] = xs[i][cc:] + akv[i][cc:] + _mm(a_rb[i], bdiag(u[i]))

    @pl.when(pl.program_id(1) == pl.num_programs(1) - 1)
    def _():
        for hd in range(RWKV_H):
            q, o = divmod(hd, RWKV_QUAD)
            so_ref[hd] = s_scr[q][o * RWKV_HEAD:(o + 1) * RWKV_HEAD, o * RWKV_HEAD:(o + 1) * RWKV_HEAD]


def _rwkv_rec_prompt(r, lw, k, v, kk, bb, tc=256):
    b, l, _ = r.shape
    tc = _row_tile(l, tc)
    tile = pl.BlockSpec((None, tc, D_MODEL), lambda i, j: (i, j, 0))
    st = (RWKV_H, RWKV_HEAD, RWKV_HEAD)
    return pl.pallas_call(
        _rwkv_rec_prompt_kernel,
        grid=(b, l // tc),
        in_specs=[tile] * 6,
        out_specs=[tile, pl.BlockSpec((None,) + st, lambda i, j: (i, 0, 0, 0))],
        out_shape=[jax.ShapeDtypeStruct((b, l, D_MODEL), F32), jax.ShapeDtypeStruct((b,) + st, F32)],
        scratch_shapes=[pltpu.VMEM((RWKV_H // RWKV_QUAD, RWKV_QUAD * RWKV_HEAD, RWKV_QUAD * RWKV_HEAD), F32)],
        compiler_params=_params(("parallel", "arbitrary")),
        name="rwkv_rec_prompt",
    )(r, lw, k, v, kk, bb)


def _rwkv_rec_decode_kernel(r_ref, lw_ref, k_ref, v_ref, kk_ref, bb_ref, s_ref, y_ref, so_ref):
    w = jnp.exp(lw_ref[...])
    a = -kk_ref[...]
    b, k, r = bb_ref[...], k_ref[...], r_ref[...]

    def row(vi, carry):
        s0 = s_ref[vi]
        sa = jnp.sum(s0 * a, axis=0, keepdims=True)
        s1 = s0 * w + sa * b + v_ref[pl.ds(vi, 1), :] * k
        so_ref[vi] = s1
        y_ref[pl.ds(vi, 1), :] = jnp.sum(s1 * r, axis=0, keepdims=True)
        return carry

    lax.fori_loop(0, RWKV_HEAD, row, 0, unroll=8)


def _rwkv_rec_decode(r, lw, k, v, kk, bb, s):
    m = r.shape[1]
    vec = pl.BlockSpec((RWKV_HEAD, m), lambda i: (i, 0))
    st = pl.BlockSpec((None, RWKV_HEAD, RWKV_HEAD, m), lambda i: (i, 0, 0, 0))
    return pl.pallas_call(
        _rwkv_rec_decode_kernel,
        grid=(RWKV_H,),
        in_specs=[vec] * 6 + [st],
        out_specs=[vec, st],
        out_shape=[jax.ShapeDtypeStruct((D_MODEL, m), F32), jax.ShapeDtypeStruct(s.shape, F32)],
        compiler_params=_params(("parallel",)),
        name="rwkv_rec_decode",
    )(r, lw, k, v, kk, bb, s)


def _rwkv_post_kernel(h_ref, y_ref, bo_ref, g_ref, lnw_ref, lnb_ref, wo_ref, *rest):
    *mlp_refs, o_ref = rest
    y = y_ref[...]
    d = y - _group_sum(y, RWKV_HEAD, split=True) * (1.0 / RWKV_HEAD)
    var = _group_sum(d * d, RWKV_HEAD) * (1.0 / RWKV_HEAD)
    yn = d * lax.rsqrt(var + RWKV_GN_EPS) * lnw_ref[...] + lnb_ref[...]
    h = h_ref[...] + _bdot((yn + bo_ref[...]) * g_ref[...], wo_ref[...])
    o_ref[...] = _mlp(h, *mlp_refs) if mlp_refs else h


def _rwkv_post_ffn(h, y, bonus, g, p, mlp=None, tm=512):
    m = h.shape[0]
    tm = _row_tile(m, tm)
    row = pl.BlockSpec((tm, D_MODEL), lambda i: (i, 0))
    vec = _const_spec((1, D_MODEL))
    mlp_ins, mlp_specs = _mlp_operands(mlp) if mlp else ([], [])
    return pl.pallas_call(
        _rwkv_post_kernel,
        grid=(m // tm,),
        in_specs=[row] * 4 + [vec, vec, _const_spec((D_MODEL, D_MODEL))] + mlp_specs,
        out_specs=row,
        out_shape=jax.ShapeDtypeStruct((m, D_MODEL), F32),
        compiler_params=_params(("parallel",)),
        name="rwkv_post_ffn",
    )(h, y, bonus, g, p["ln_w"].reshape(1, D_MODEL), p["ln_b"].reshape(1, D_MODEL), p["wo"].astype(BF16),
      *mlp_ins)


def _gla_pre_kernel(h_ref, nm_ref, win_ref, wgl_ref, wgk_ref, bgk_ref, q_ref, k_ref, v_ref, g_ref, gk_ref):
    u = _rms(h_ref[...], nm_ref[...], NORM_EPS).astype(BF16)
    z = _nt(u, win_ref[...])
    q_ref[...] = z[:, :GLA_DK_TOT] * (GLA_DK ** -0.5)
    k_ref[...] = z[:, GLA_DK_TOT:2 * GLA_DK_TOT]
    v_ref[...] = z[:, 2 * GLA_DK_TOT:2 * GLA_DK_TOT + GLA_DV_TOT]
    g_ref[...] = z[:, 2 * GLA_DK_TOT + GLA_DV_TOT:]
    gl = _nt(u, wgl_ref[...])
    pre = _bdot(gl, wgk_ref[...]) + bgk_ref[...]
    gk_ref[...] = -_softplus(-pre) * (1.0 / GLA_NORMALIZER)


def _gla_pre(h, nm, p, tm=1024):
    m = h.shape[0]
    tm = _row_tile(m, tm)
    n_main = 2 * GLA_DK_TOT + 2 * GLA_DV_TOT
    w_t = p["w_in"].T
    w_main = w_t[:n_main].astype(BF16)
    w_gl = jnp.pad(w_t[n_main:].astype(BF16), ((0, LANES - GLA_LR), (0, 0)))
    w_gk = jnp.pad(p["w_gk2"].astype(BF16), ((0, LANES - GLA_LR), (0, 0)))
    row = lambda n: pl.BlockSpec((tm, n), lambda i: (i, 0))
    widths = (GLA_DK_TOT, GLA_DK_TOT, GLA_DV_TOT, GLA_DV_TOT, GLA_DK_TOT)
    return pl.pallas_call(
        _gla_pre_kernel,
        grid=(m // tm,),
        in_specs=[row(D_MODEL), _const_spec((1, D_MODEL)), _const_spec(w_main.shape), _const_spec(w_gl.shape),
                  _const_spec(w_gk.shape), _const_spec((1, GLA_DK_TOT))],
        out_specs=[row(n) for n in widths],
        out_shape=[jax.ShapeDtypeStruct((m, n), F32) for n in widths],
        compiler_params=_params(("parallel",)),
        name="gla_pre",
    )(h, nm.reshape(1, D_MODEL), w_main, w_gl, w_gk, p["b_gk"].reshape(1, GLA_DK_TOT))


def _gla_rec_prompt_kernel(q_ref, k_ref, v_ref, gk_ref, o_ref, so_ref, s_scr):
    tc = q_ref.shape[0]

    @pl.when(pl.program_id(1) == 0)
    def _():
        s_scr[...] = jnp.zeros_like(s_scr)

    ri, ci = _iota((CHUNK, CHUNK), 0), _iota((CHUNK, CHUNK), 1)
    incl = ri >= ci
    n_sub = CHUNK // GLA_SUB
    sub_of_row = jnp.right_shift(_iota((CHUNK, 1), 0), GLA_SUB.bit_length() - 1)

    heads = range(GLA_H)
    ks = [slice(hd * GLA_DK, (hd + 1) * GLA_DK) for hd in heads]
    vs = [slice(hd * GLA_DV, (hd + 1) * GLA_DV) for hd in heads]
    n_chunks = tc // CHUNK
    items = [(c, hd) for c in range(n_chunks) for hd in heads]

    prep = []
    for c in range(n_chunks):
        rows = slice(c * CHUNK, (c + 1) * CHUNK)
        cum = _chunk_cumsum(gk_ref[rows, :])
        q, k = q_ref[rows, :], k_ref[rows, :]
        last = cum[CHUNK - 1:CHUNK, :]
        refs = [jnp.zeros_like(last)] + [cum[i * GLA_SUB - 1:i * GLA_SUB, :] for i in range(1, n_sub)]
        ref_of_row = refs[0]
        for i in range(1, n_sub):
            ref_of_row = jnp.where(sub_of_row >= i, refs[i], ref_of_row)
        q_in = (q * jnp.exp(cum - ref_of_row)).astype(BF16)
        k_in = []
        for i in range(n_sub):
            n = (i + 1) * GLA_SUB
            part = (k[:n] * jnp.exp(refs[i] - cum[:n])).astype(BF16)
            k_in.append(part if n == CHUNK else
                        jnp.concatenate([part, jnp.zeros((CHUNK - n, part.shape[1]), BF16)], axis=0))
        prep.append(dict(q_in=q_in, k_in=k_in, qe=(q * jnp.exp(cum)).astype(BF16),
                         kh=(k * jnp.exp(last - cum)).astype(BF16), g_last=jnp.exp(last),
                         v=v_ref[rows, :].astype(BF16)))

    def scores(c, hd):
        p = prep[c]
        blocks = [_nt(p["q_in"][i * GLA_SUB:(i + 1) * GLA_SUB, ks[hd]], p["k_in"][i][:, ks[hd]])
                  for i in range(n_sub)]
        return jnp.where(incl, jnp.concatenate(blocks, axis=0), 0.0).astype(BF16)

    att = [scores(c, hd) for c, hd in items]
    intra = [_mm(att[i], prep[c]["v"][:, vs[hd]]) for i, (c, hd) in enumerate(items)]
    kv = [_tn(prep[c]["v"][:, vs[hd]], prep[c]["kh"][:, ks[hd]]) for c, hd in items]

    s_in = []
    state = [s_scr[hd] for hd in heads]
    for c in range(n_chunks):
        s_in.extend(state[hd].astype(BF16) for hd in heads)
        state = [state[hd] * prep[c]["g_last"][:, ks[hd]] + kv[c * GLA_H + hd] for hd in heads]
    for hd in heads:
        s_scr[hd] = state[hd]

    for i, (c, hd) in enumerate(items):
        o_ref[c * CHUNK:(c + 1) * CHUNK, vs[hd]] = _nt(prep[c]["qe"][:, ks[hd]], s_in[i]) + intra[i]

    @pl.when(pl.program_id(1) == pl.num_programs(1) - 1)
    def _():
        for hd in range(GLA_H):
            so_ref[hd] = s_scr[hd].T


def _gla_rec_prompt(q, k, v, gk, tc=256):
    b, l, _ = q.shape
    tc = _row_tile(l, tc)
    tile = lambda n: pl.BlockSpec((None, tc, n), lambda i, j: (i, j, 0))
    st = (GLA_H, GLA_DK, GLA_DV)
    return pl.pallas_call(
        _gla_rec_prompt_kernel,
        grid=(b, l // tc),
        in_specs=[tile(GLA_DK_TOT), tile(GLA_DK_TOT), tile(GLA_DV_TOT), tile(GLA_DK_TOT)],
        out_specs=[tile(GLA_DV_TOT), pl.BlockSpec((None,) + st, lambda i, j: (i, 0, 0, 0))],
        out_shape=[jax.ShapeDtypeStruct((b, l, GLA_DV_TOT), F32), jax.ShapeDtypeStruct((b,) + st, F32)],
        scratch_shapes=[pltpu.VMEM((GLA_H, GLA_DV, GLA_DK), F32)],
        compiler_params=_params(("parallel", "arbitrary")),
        name="gla_rec_prompt",
    )(q, k, v, gk)


def _gla_rec_decode_kernel(q_ref, k_ref, v_ref, gk_ref, s_ref, o_ref, so_ref):
    eye = (_iota((GLA_DK, GLA_DK), 0) == _iota((GLA_DK, GLA_DK), 1)).astype(F32)[None]

    def col(x):
        return jnp.sum(eye * x, axis=-1, keepdims=True)

    for hd in range(GLA_H):
        hs = slice(hd, hd + 1)
        s1 = s_ref[:, hd] * col(jnp.exp(gk_ref[:, hs, :])) + col(k_ref[:, hs, :]) * v_ref[:, hs, :]
        so_ref[:, hd] = s1
        o_ref[:, hs, :] = jnp.sum(col(q_ref[:, hs, :]) * s1, axis=1, keepdims=True)


def _gla_rec_decode(q, k, v, gk, s, bblk=16):
    m = q.shape[0]
    kvec = pl.BlockSpec((bblk, GLA_H, GLA_DK), lambda i: (i, 0, 0))
    vvec = pl.BlockSpec((bblk, GLA_H, GLA_DV), lambda i: (i, 0, 0))
    st = pl.BlockSpec((bblk, GLA_H, GLA_DK, GLA_DV), lambda i: (i, 0, 0, 0))
    hk = lambda x: x.reshape(m, GLA_H, GLA_DK)
    o, s_new = pl.pallas_call(
        _gla_rec_decode_kernel,
        grid=(m // bblk,),
        in_specs=[kvec, kvec, vvec, kvec, st],
        out_specs=[vvec, st],
        out_shape=[jax.ShapeDtypeStruct((m, GLA_H, GLA_DV), F32), jax.ShapeDtypeStruct(s.shape, F32)],
        compiler_params=_params(("parallel",)),
        name="gla_rec_decode",
    )(hk(q), hk(k), v.reshape(m, GLA_H, GLA_DV), hk(gk), s)
    return o.reshape(m, GLA_DV_TOT), s_new


def _gla_post_kernel(h_ref, o_ref_in, g_ref, nw_ref, wo_ref, *rest):
    *mlp_refs, out_ref = rest
    o = o_ref_in[...]
    parts = []
    for hd in range(GLA_H):
        oh = o[:, hd * GLA_DV:(hd + 1) * GLA_DV]
        parts.append(oh * lax.rsqrt(jnp.mean(oh * oh, axis=-1, keepdims=True) + GLA_NORM_EPS))
    on = jnp.concatenate(parts, axis=1) * nw_ref[...]
    g = g_ref[...]
    h = h_ref[...] + _bdot(on * (g * jax.nn.sigmoid(g)), wo_ref[...])
    out_ref[...] = _mlp(h, *mlp_refs) if mlp_refs else h


def _gla_post_ffn(h, o, g, p, mlp=None, tm=512):
    m = h.shape[0]
    tm = _row_tile(m, tm)
    row = pl.BlockSpec((tm, D_MODEL), lambda i: (i, 0))
    mlp_ins, mlp_specs = _mlp_operands(mlp) if mlp else ([], [])
    return pl.pallas_call(
        _gla_post_kernel,
        grid=(m // tm,),
        in_specs=[row] * 3 + [_const_spec((1, D_MODEL)), _const_spec((D_MODEL, D_MODEL))] + mlp_specs,
        out_specs=row,
        out_shape=jax.ShapeDtypeStruct((m, D_MODEL), F32),
        compiler_params=_params(("parallel",)),
        name="gla_post_ffn",
    )(h, o, g, jnp.tile(p["norm"], GLA_H).reshape(1, D_MODEL), p["wo"].astype(BF16), *mlp_ins)


def _conv_gates(h_ref, nm_ref, win_ref):
    u = _rms(h_ref[...], nm_ref[...], NORM_EPS).astype(BF16)
    z = jnp.dot(u, win_ref[...], preferred_element_type=F32)
    return z[:, :D_MODEL], z[:, D_MODEL:2 * D_MODEL] * z[:, 2 * D_MODEL:]


def _conv_prompt_kernel(h_ref, nm_ref, win_ref, cw_ref, wo_ref, o_ref, st_ref, carry_ref):
    tm = h_ref.shape[0]

    @pl.when(pl.program_id(1) == 0)
    def _():
        carry_ref[...] = jnp.zeros_like(carry_ref)

    ts = tm // SUBTILES
    subs = [slice(i * ts, (i + 1) * ts) for i in range(SUBTILES)]
    gates = [_conv_gates(h_ref.at[sl], nm_ref, win_ref) for sl in subs]
    row = _iota((ts, D_MODEL), 0)
    prev2, prev1 = carry_ref[0:1, :], carry_ref[1:2, :]
    gated = []
    for g_b, zc in gates:
        z1 = jnp.where(row == 0, prev1, pltpu.roll(zc, 1, axis=0))
        z2 = jnp.where(row == 0, prev2, jnp.where(row == 1, prev1, pltpu.roll(zc, 2, axis=0)))
        gated.append((g_b * (cw_ref[0:1, :] * z2 + cw_ref[1:2, :] * z1 + cw_ref[2:3, :] * zc)).astype(BF16))
        prev2, prev1 = zc[ts - 2:ts - 1, :], zc[ts - 1:ts, :]
    for sl, x in zip(subs, gated):
        o_ref[sl, :] = h_ref[sl, :] + jnp.dot(x, wo_ref[...], preferred_element_type=F32)
    tail = gates[-1][1][ts - (CONV_W - 1):, :]
    carry_ref[0:CONV_W - 1, :] = tail
    st_ref[...] = tail


def _conv_prompt(h, nm, p, tm=1024):
    b, l, _ = h.shape
    tm = _row_tile(l, tm)
    tile = pl.BlockSpec((None, tm, D_MODEL), lambda i, j: (i, j, 0))
    return pl.pallas_call(
        _conv_prompt_kernel,
        grid=(b, l // tm),
        in_specs=[tile, _const_spec((1, D_MODEL)), _const_spec((D_MODEL, 3 * D_MODEL)),
                  _const_spec((CONV_W, D_MODEL)), _const_spec((D_MODEL, D_MODEL))],
        out_specs=[tile, pl.BlockSpec((None, CONV_W - 1, D_MODEL), lambda i, j: (i, 0, 0))],
        out_shape=[jax.ShapeDtypeStruct((b, l, D_MODEL), F32),
                   jax.ShapeDtypeStruct((b, CONV_W - 1, D_MODEL), F32)],
        scratch_shapes=[pltpu.VMEM((SUBLANES, D_MODEL), F32)],
        compiler_params=_params(("parallel", "arbitrary")),
        name="conv_prompt",
    )(h, nm.reshape(1, D_MODEL), p["w_in"].astype(BF16), p["w"], p["wo"].astype(BF16))


def _conv_decode_kernel(h_ref, buf_ref, nm_ref, win_ref, cw_ref, wo_ref, o_ref, st_ref):
    g_b, zc = _conv_gates(h_ref, nm_ref, win_ref)
    z2 = buf_ref[:, 0, :]
    z1 = buf_ref[:, 1, :]
    conv = cw_ref[0:1, :] * z2 + cw_ref[1:2, :] * z1 + cw_ref[2:3, :] * zc
    o_ref[...] = h_ref[...] + _bdot(g_b * conv, wo_ref[...])
    st_ref[:, 0, :] = z1
    st_ref[:, 1, :] = zc


def _conv_decode(h, buf, nm, p):
    m = h.shape[0]
    full = pl.BlockSpec((m, D_MODEL), lambda i: (0, 0))
    bufs = pl.BlockSpec(buf.shape, lambda i: (0, 0, 0))
    return pl.pallas_call(
        _conv_decode_kernel,
        grid=(1,),
        in_specs=[full, bufs, _const_spec((1, D_MODEL)), _const_spec((D_MODEL, 3 * D_MODEL)),
                  _const_spec((CONV_W, D_MODEL)), _const_spec((D_MODEL, D_MODEL))],
        out_specs=[full, bufs],
        out_shape=[jax.ShapeDtypeStruct((m, D_MODEL), F32), jax.ShapeDtypeStruct(buf.shape, F32)],
        compiler_params=_params(("arbitrary",)),
        name="conv_decode",
    )(h, buf, nm.reshape(1, D_MODEL), p["w_in"].astype(BF16), p["w"], p["wo"].astype(BF16))


def _pool_project(d_groups, pw_ref, sc_ref):
    ys = [_bdot(d, pw_ref[gi]) for gi, d in enumerate(d_groups)]
    return jnp.concatenate(ys, axis=1) * sc_ref[...]


def _pool_prompt_kernel(h_ref, nm_ref, pw_ref, sc_ref, nf_ref, wu_ref, wd_ref, gf_ref, o_ref, st_ref, carry_ref):
    tm = h_ref.shape[0]
    hist = carry_ref.shape[0]
    j = pl.program_id(1)

    @pl.when(j == 0)
    def _():
        carry_ref[...] = jnp.zeros_like(carry_ref)

    u = _rms(h_ref[...], nm_ref[...], NORM_EPS)
    ext = jnp.concatenate([carry_ref[...], u], axis=0)
    pos = j * tm + _iota((tm, 1), 0)
    d_groups = []
    for gi, w in enumerate(POOL_WINDOWS):
        sl = slice(gi * POOL_G, (gi + 1) * POOL_G)
        s = ext[:, sl]
        span = 1
        while span < w:
            s = s + pltpu.roll(s, span, axis=0)
            span *= 2
        cnt = jnp.minimum(w, pos + 1).astype(F32)
        d_groups.append(s[hist:, :] / cnt - u[:, sl])
    o_ref[...] = _mlp(h_ref[...] + _pool_project(d_groups, pw_ref, sc_ref), nf_ref, wu_ref, wd_ref, gf_ref)
    tail = u[tm - hist:, :]
    carry_ref[...] = tail
    st_ref[...] = tail


def _pool_prompt_ffn(h, nm, p, mlp, final_g, tm=512):
    b, l, _ = h.shape
    tm = _row_tile(l, tm)
    hist = POOL_BUF + 1
    tile = pl.BlockSpec((None, tm, D_MODEL), lambda i, j: (i, j, 0))
    mlp_ins, mlp_specs = _mlp_operands(mlp, final_g)
    out, st = pl.pallas_call(
        _pool_prompt_kernel,
        grid=(b, l // tm),
        in_specs=[tile, _const_spec((1, D_MODEL)), _const_spec(p["w"].shape), _const_spec((1, D_MODEL))]
        + mlp_specs,
        out_specs=[tile, pl.BlockSpec((None, hist, D_MODEL), lambda i, j: (i, 0, 0))],
        out_shape=[jax.ShapeDtypeStruct((b, l, D_MODEL), F32), jax.ShapeDtypeStruct((b, hist, D_MODEL), F32)],
        scratch_shapes=[pltpu.VMEM((hist, D_MODEL), F32)],
        compiler_params=_params(("parallel", "arbitrary")),
        name="pool_prompt_ffn",
    )(h, nm.reshape(1, D_MODEL), p["w"].astype(BF16), p["scale"].reshape(1, D_MODEL), *mlp_ins)
    return out, st[:, 1:]


def _pool_decode_kernel(h_ref, buf_ref, nm_ref, pw_ref, sc_ref, o_ref, st_ref):
    u = _rms(h_ref[...], nm_ref[...], NORM_EPS)
    d_groups = []
    for gi, w in enumerate(POOL_WINDOWS):
        sl = slice(gi * POOL_G, (gi + 1) * POOL_G)
        s = u[:, sl]
        for i in range(1, w):
            s = s + buf_ref[POOL_BUF - i, :, sl]
        cnt = float(min(w, PAST_LEN + 1))
        d_groups.append(s / cnt - u[:, sl])
    o_ref[...] = h_ref[...] + _pool_project(d_groups, pw_ref, sc_ref)
    st_ref[0:POOL_BUF - 1] = buf_ref[1:POOL_BUF]
    st_ref[POOL_BUF - 1] = u


def _pool_decode(h, buf, nm, p):
    m = h.shape[0]
    full = pl.BlockSpec((m, D_MODEL), lambda i: (0, 0))
    bufs = pl.BlockSpec(buf.shape, lambda i: (0, 0, 0))
    return pl.pallas_call(
        _pool_decode_kernel,
        grid=(1,),
        in_specs=[full, bufs, _const_spec((1, D_MODEL)), _const_spec(p["w"].shape), _const_spec((1, D_MODEL))],
        out_specs=[full, bufs],
        out_shape=[jax.ShapeDtypeStruct((m, D_MODEL), F32), jax.ShapeDtypeStruct(buf.shape, F32)],
        compiler_params=_params(("arbitrary",)),
        name="pool_decode",
    )(h, buf, nm.reshape(1, D_MODEL), p["w"].astype(BF16), p["scale"].reshape(1, D_MODEL))


def _trunk_prompt(x, nm, nfin, mlps, rw, gl, cv, po):
    b, l, _ = x.shape
    flat = lambda t: t.reshape(b * l, t.shape[-1])
    seq = lambda t: t.reshape(b, l, t.shape[-1])

    (r, lw, k, v, kk, bb, g, bonus), shift = _rwkv_pre_prompt(x, nm[0], rw)
    y, wkv = _rwkv_rec_prompt(r, lw, k, v, kk, bb)
    h = _rwkv_post_ffn(flat(x), flat(y), flat(bonus), flat(g), rw, mlps[0])

    q, k, v, g, gk = _gla_pre(h, nm[1], gl)
    o, gla_s = _gla_rec_prompt(seq(q), seq(k), seq(v), seq(gk))
    h = _gla_post_ffn(h, flat(o), g, gl, mlps[1])

    h, conv_s = _conv_prompt(seq(h), nm[2], cv)
    h = _ffn(flat(h), mlps[2])

    y, pool_s = _pool_prompt_ffn(seq(h), nm[3], po, mlps[3], nfin)
    return y, wkv[None], shift[None], gla_s[None], conv_s[None], pool_s[None]


def _trunk_decode(x, st_wkv, st_shift, st_gla, st_conv, st_pool, nm, nf, nfin, w_up, w_down, rw, gl, cv, po):
    m = x.shape[0]
    h = x.reshape(m, D_MODEL)
    mlps = []

    def mlp(h, i, final_g=None):
        h, wub, wdb = _ffn_cast(h, nf[i], w_up, w_down, i, final_g)
        mlps.append((nf[i], wub, wdb))
        return h

    (r, lw, k, v, kk, bb, g, bonus), shift = _rwkv_pre_decode(h, st_shift[0], nm[0], rw)
    y_t, wkv_t = _rwkv_rec_decode(r, lw, k, v, kk, bb, jnp.transpose(st_wkv[0], (1, 2, 3, 0)))
    wkv = jnp.transpose(wkv_t, (3, 0, 1, 2))
    h = mlp(_rwkv_post_ffn(h, y_t.T, bonus, g, rw), 0)

    q, k, v, g, gk = _gla_pre(h, nm[1], gl)
    o, gla_s = _gla_rec_decode(q, k, v, gk, st_gla[0])
    h = mlp(_gla_post_ffn(h, o, g, gl), 1)

    h, conv_s = _conv_decode(h, st_conv[0], nm[2], cv)
    h = mlp(h, 2)

    h, pool_t = _pool_decode(h, jnp.transpose(st_pool[0], (1, 0, 2)), nm[3], po)
    pool_s = jnp.transpose(pool_t, (1, 0, 2))
    y = mlp(h, 3, final_g=nfin)
    return (y.reshape(m, 1, D_MODEL), wkv[None], shift[None], gla_s[None], conv_s[None], pool_s[None]), mlps


def kernel(x_prompt, x_sample, state_rwkv_wkv, state_rwkv_shift, state_gla, state_conv, state_pool, norm_mix, norm_ffn, norm_final, ffn_up, ffn_down, rwkv_mu, rwkv_w_rkv, rwkv_w0, rwkv_w1, rwkv_w2, rwkv_a0, rwkv_a1, rwkv_a2, rwkv_g1, rwkv_g2, rwkv_k_k, rwkv_k_a, rwkv_r_k, rwkv_ln_w, rwkv_ln_b, rwkv_wo, gla_w_in, gla_w_gk2, gla_b_gk, gla_norm, gla_wo, conv_w_in, conv_w, conv_wo, pool_w, pool_scale):
    assert x_prompt.shape[1] % CHUNK == 0 and x_sample.shape[1] == 1
    rw = dict(mu=rwkv_mu[0], w_rkv=rwkv_w_rkv[0], w0=rwkv_w0[0], w1=rwkv_w1[0], w2=rwkv_w2[0], a0=rwkv_a0[0],
              a1=rwkv_a1[0], a2=rwkv_a2[0], g1=rwkv_g1[0], g2=rwkv_g2[0], k_k=rwkv_k_k[0], k_a=rwkv_k_a[0],
              r_k=rwkv_r_k[0], ln_w=rwkv_ln_w[0], ln_b=rwkv_ln_b[0], wo=rwkv_wo[0])
    gl = dict(w_in=gla_w_in[0], w_gk2=gla_w_gk2[0], b_gk=gla_b_gk[0], norm=gla_norm[0], wo=gla_wo[0])
    cv = dict(w_in=conv_w_in[0], w=conv_w[0], wo=conv_wo[0])
    po = dict(w=pool_w[0], scale=pool_scale[0])
    (y_s, wkv_s, sh_s, gla_s, conv_s, pool_s), mlps = _trunk_decode(
        x_sample, state_rwkv_wkv, state_rwkv_shift, state_gla, state_conv, state_pool,
        norm_mix, norm_ffn, norm_final, ffn_up, ffn_down, rw, gl, cv, po)
    y_p, wkv_p, sh_p, gla_p, conv_p, pool_p = _trunk_prompt(x_prompt, norm_mix, norm_final, mlps, rw, gl, cv, po)
    return (y_p, y_s, wkv_p, wkv_s, sh_p, sh_s, gla_p, gla_s, conv_p, conv_s, pool_p, pool_s)
```

```python
import functools
import math

import jax
import jax.numpy as jnp
from jax import lax
from jax.experimental import pallas as pl
from jax.experimental.pallas import tpu as pltpu

F32 = jnp.float32
BF16 = jnp.bfloat16

D_MODEL = 1024
D_FF = 4 * D_MODEL
NORM_EPS = 1e-6
PAST_LEN = 16384

RWKV_HEAD = 64
RWKV_H = D_MODEL // RWKV_HEAD
RWKV_GN_EPS = 64e-5
RWKV_QUAD = 4

GLA_H = 4
GLA_DK_TOT = D_MODEL // 2
GLA_DV_TOT = D_MODEL
GLA_DK = GLA_DK_TOT // GLA_H
GLA_DV = GLA_DV_TOT // GLA_H
GLA_LR = 16
GLA_NORMALIZER = 16.0
GLA_NORM_EPS = 1e-5
GLA_SUB = 16

CONV_W = 3
POOL_WINDOWS = (2, 4, 8, 16)
POOL_G = D_MODEL // len(POOL_WINDOWS)
POOL_BUF = max(POOL_WINDOWS) - 1

LANES = 128
SUBLANES = 8
MXU_DIM = 256
VMEM_LIMIT_BYTES = 56 * 1024 * 1024

CHUNK = 64
FFN_COLS = 1024
SUBTILES = 2


def _const_spec(shape):
    nd = len(shape)
    return pl.BlockSpec(shape, lambda *_: (0,) * nd, pipeline_mode=pl.Buffered(1))


def _params(sem):
    return pltpu.CompilerParams(dimension_semantics=sem, vmem_limit_bytes=VMEM_LIMIT_BYTES)


def _rms(x, g, eps):
    return x * lax.rsqrt(jnp.mean(x * x, axis=-1, keepdims=True) + eps) * g


def _bdot(a, w):
    return jnp.dot(a.astype(BF16), w, preferred_element_type=F32)


def _mm(a, b, prec=None):
    return jnp.dot(a, b, preferred_element_type=F32, precision=prec)


def _nt(a, b, prec=None):
    return lax.dot_general(a, b, (((1,), (1,)), ((), ())), preferred_element_type=F32, precision=prec)


def _tn(a, b, prec=None):
    return lax.dot_general(a, b, (((0,), (0,)), ((), ())), preferred_element_type=F32, precision=prec)


def _iota(shape, dim):
    return lax.broadcasted_iota(jnp.int32, shape, dim)


def _group_sum(x, group, split=False):
    shift = group.bit_length() - 1
    bd = (jnp.right_shift(_iota((MXU_DIM, MXU_DIM), 0), shift)
          == jnp.right_shift(_iota((MXU_DIM, MXU_DIM), 1), shift)).astype(BF16)
    terms = [x.astype(BF16)]
    if split:
        terms.append((x - terms[0].astype(F32)).astype(BF16))
    cols = [sum(_mm(t[:, c * MXU_DIM:(c + 1) * MXU_DIM], bd) for t in terms)
            for c in range(x.shape[1] // MXU_DIM)]
    return jnp.concatenate(cols, axis=1)


def _chunk_cumsum(x):
    c = x.shape[0]
    ltri = (_iota((c, c), 0) >= _iota((c, c), 1)).astype(BF16)
    t0 = x.astype(BF16)
    r1 = x - t0.astype(F32)
    t1 = r1.astype(BF16)
    t2 = (r1 - t1.astype(F32)).astype(BF16)
    return _mm(ltri, t0) + _mm(ltri, t1) + _mm(ltri, t2)


def _softplus(x):
    return jnp.maximum(x, 0.0) + jnp.log(1.0 + jnp.exp(-jnp.abs(x)))


def _row_tile(m, want):
    t = min(want, m)
    assert m % t == 0, (m, t)
    return t


def _mlp(h, g_ref, wu_ref, wd_ref, gf_ref=None):
    x = _rms(h, g_ref[...], NORM_EPS).astype(BF16)
    acc = h
    for c in range(D_FF // FFN_COLS):
        sl = slice(c * FFN_COLS, (c + 1) * FFN_COLS)
        a = jnp.dot(x, wu_ref[:, sl], preferred_element_type=F32)
        a = jnp.square(jnp.maximum(a, 0.0)).astype(BF16)
        acc = acc + jnp.dot(a, wd_ref[sl, :], preferred_element_type=F32)
    if gf_ref is not None:
        acc = _rms(acc, gf_ref[...], NORM_EPS)
    return acc


def _mlp_operands(mlp, final_g=None):
    nf, wu, wd = mlp
    ins = [nf.reshape(1, D_MODEL), wu, wd]
    specs = [_const_spec((1, D_MODEL)), _const_spec(wu.shape), _const_spec(wd.shape)]
    if final_g is not None:
        ins.append(final_g.reshape(1, D_MODEL))
        specs.append(_const_spec((1, D_MODEL)))
    return ins, specs


def _ffn_kernel(h_ref, g_ref, wu_ref, wd_ref, *rest):
    *gf_ref, o_ref = rest
    o_ref[...] = _mlp(h_ref[...], g_ref, wu_ref, wd_ref, *gf_ref)


def _ffn(h, mlp, final_g=None, tm=1024):
    m = h.shape[0]
    tm = _row_tile(m, tm)
    row = pl.BlockSpec((tm, D_MODEL), lambda i: (i, 0))
    mlp_ins, mlp_specs = _mlp_operands(mlp, final_g)
    return pl.pallas_call(
        _ffn_kernel,
        grid=(m // tm,),
        in_specs=[row] + mlp_specs,
        out_specs=row,
        out_shape=jax.ShapeDtypeStruct((m, D_MODEL), F32),
        compiler_params=_params(("parallel",)),
        name="ffn",
    )(h, *mlp_ins)


def _ffn_cast_kernel(h_ref, g_ref, wu_ref, wd_ref, *rest):
    *gf_ref, o_ref, wub_ref, wdb_ref, x_scr, acc_scr = rest
    c = pl.program_id(0)

    @pl.when(c == 0)
    def _():
        h = h_ref[...]
        x_scr[...] = _rms(h, g_ref[...], NORM_EPS).astype(BF16)
        acc_scr[...] = h

    wu = wu_ref[...].astype(BF16)
    wd = wd_ref[...].astype(BF16)
    wub_ref[...] = wu
    wdb_ref[...] = wd
    a = jnp.dot(x_scr[...], wu, preferred_element_type=F32)
    a = jnp.square(jnp.maximum(a, 0.0)).astype(BF16)
    acc_scr[...] += jnp.dot(a, wd, preferred_element_type=F32)

    @pl.when(c == pl.num_programs(0) - 1)
    def _():
        acc = acc_scr[...]
        o_ref[...] = _rms(acc, gf_ref[0][...], NORM_EPS) if gf_ref else acc


def _ffn_cast(h, nf, w_up, w_down, layer, final_g=None, cols=FFN_COLS):
    m = h.shape[0]
    full = pl.BlockSpec((m, D_MODEL), lambda c: (0, 0))
    vec = pl.BlockSpec((1, D_MODEL), lambda c: (0, 0))
    ins = [h, nf.reshape(1, D_MODEL), w_up, w_down]
    specs = [full, vec, pl.BlockSpec((None, D_MODEL, cols), lambda c: (layer, 0, c)),
             pl.BlockSpec((None, cols, D_MODEL), lambda c: (layer, c, 0))]
    if final_g is not None:
        ins.append(final_g.reshape(1, D_MODEL))
        specs.append(vec)
    return pl.pallas_call(
        _ffn_cast_kernel,
        grid=(D_FF // cols,),
        in_specs=specs,
        out_specs=[full, pl.BlockSpec((D_MODEL, cols), lambda c: (0, c)),
                   pl.BlockSpec((cols, D_MODEL), lambda c: (c, 0))],
        out_shape=[jax.ShapeDtypeStruct((m, D_MODEL), F32), jax.ShapeDtypeStruct((D_MODEL, D_FF), BF16),
                   jax.ShapeDtypeStruct((D_FF, D_MODEL), BF16)],
        scratch_shapes=[pltpu.VMEM((m, D_MODEL), BF16), pltpu.VMEM((m, D_MODEL), F32)],
        compiler_params=_params(("arbitrary",)),
        name="ffn_cast",
    )(*ins)


def _rwkv_pre_math(u, prev, mu_ref, wrkv_ref, w0_ref, w1_ref, w2_ref, a0_ref, a1_ref, a2_ref,
                   g1_ref, g2_ref, kk_ref, ka_ref, rk_ref):
    dx = prev - u

    def mix(i):
        return (u + dx * mu_ref[i:i + 1, :]).astype(BF16)

    r = _bdot(mix(0), wrkv_ref[0])
    w_mid = _bdot(mix(1), w1_ref[...])
    a_mid = _bdot(mix(4), a1_ref[...])
    g_mid = _bdot(mix(5), g1_ref[...])
    k = _bdot(mix(2), wrkv_ref[1])
    wl = w0_ref[...] + _bdot(jnp.tanh(w_mid), w2_ref[...])
    a = jax.nn.sigmoid(a0_ref[...] + _bdot(a_mid, a2_ref[...]))
    g = _bdot(jax.nn.sigmoid(g_mid), g2_ref[...])
    lw = jax.nn.sigmoid(wl) * (-math.exp(-0.5))
    kk = k * kk_ref[...]
    kk_ss = _group_sum(kk * kk, RWKV_HEAD)
    v = _bdot(mix(3), wrkv_ref[2])
    k = k * (1.0 + (a - 1.0) * ka_ref[...])
    rk_sum = _group_sum(r * k * rk_ref[...], RWKV_HEAD)
    kk = kk * lax.rsqrt(jnp.maximum(kk_ss, 1e-24))
    return r, lw, k, v, kk, kk * a, g, rk_sum * v


_N_RWKV_W = 13


def _rwkv_pre_prompt_kernel(h_ref, nm_ref, *rest):
    w = rest[:_N_RWKV_W]
    outs = rest[_N_RWKV_W:_N_RWKV_W + 8]
    sh_ref, carry_ref = rest[_N_RWKV_W + 8:]
    tm = h_ref.shape[0]

    @pl.when(pl.program_id(1) == 0)
    def _():
        carry_ref[...] = jnp.zeros_like(carry_ref)

    u = _rms(h_ref[...], nm_ref[...], NORM_EPS)
    prev = jnp.where(_iota(u.shape, 0) == 0, carry_ref[0:1, :], pltpu.roll(u, 1, axis=0))
    last = u[tm - 1:tm, :]
    carry_ref[0:1, :] = last
    sh_ref[...] = last
    for o_ref, val in zip(outs, _rwkv_pre_math(u, prev, *w)):
        o_ref[...] = val


def _rwkv_pre_decode_kernel(h_ref, prev_ref, nm_ref, *rest):
    w = rest[:_N_RWKV_W]
    outs = rest[_N_RWKV_W:_N_RWKV_W + 8]
    (sh_ref,) = rest[_N_RWKV_W + 8:]
    u = _rms(h_ref[...], nm_ref[...], NORM_EPS)
    sh_ref[...] = u
    vals = _rwkv_pre_math(u, prev_ref[...], *w)
    for o_ref, val in zip(outs[:6], vals[:6]):
        o_ref[...] = val.T
    for o_ref, val in zip(outs[6:], vals[6:]):
        o_ref[...] = val


def _rwkv_weights(p):
    row = lambda x: x.reshape(1, D_MODEL)
    return [p["mu"], p["w_rkv"].astype(BF16), row(p["w0"]), p["w1"].astype(BF16), p["w2"].astype(BF16),
            row(p["a0"]), p["a1"].astype(BF16), p["a2"].astype(BF16), p["g1"].astype(BF16),
            p["g2"].astype(BF16), row(p["k_k"]), row(p["k_a"]), row(p["r_k"])]


def _rwkv_pre_prompt(h, nm, p, tm=512):
    b, l, _ = h.shape
    tm = _row_tile(l, tm)
    ws = _rwkv_weights(p)
    tile = pl.BlockSpec((None, tm, D_MODEL), lambda i, j: (i, j, 0))
    outs = pl.pallas_call(
        _rwkv_pre_prompt_kernel,
        grid=(b, l // tm),
        in_specs=[tile, _const_spec((1, D_MODEL))] + [_const_spec(w.shape) for w in ws],
        out_specs=[tile] * 8 + [pl.BlockSpec((None, 1, D_MODEL), lambda i, j: (i, 0, 0))],
        out_shape=[jax.ShapeDtypeStruct((b, l, D_MODEL), F32)] * 8
        + [jax.ShapeDtypeStruct((b, 1, D_MODEL), F32)],
        scratch_shapes=[pltpu.VMEM((SUBLANES, D_MODEL), F32)],
        compiler_params=_params(("parallel", "arbitrary")),
        name="rwkv_pre_prompt",
    )(h, nm.reshape(1, D_MODEL), *ws)
    return outs[:8], outs[8].reshape(b, D_MODEL)


def _rwkv_pre_decode(h, prev, nm, p):
    m = h.shape[0]
    ws = _rwkv_weights(p)
    full = pl.BlockSpec((m, D_MODEL), lambda i: (0, 0))
    full_t = pl.BlockSpec((D_MODEL, m), lambda i: (0, 0))
    outs = pl.pallas_call(
        _rwkv_pre_decode_kernel,
        grid=(1,),
        in_specs=[full, full, _const_spec((1, D_MODEL))] + [_const_spec(w.shape) for w in ws],
        out_specs=[full_t] * 6 + [full] * 3,
        out_shape=[jax.ShapeDtypeStruct((D_MODEL, m), F32)] * 6 + [jax.ShapeDtypeStruct((m, D_MODEL), F32)] * 3,
        compiler_params=_params(("arbitrary",)),
        name="rwkv_pre_decode",
    )(h, prev, nm.reshape(1, D_MODEL), *ws)
    return outs[:8], outs[8]


def _rwkv_rec_prompt_kernel(r_ref, lw_ref, k_ref, v_ref, kk_ref, bb_ref, y_ref, so_ref, s_scr):
    tc = r_ref.shape[0]
    qw = RWKV_QUAD * RWKV_HEAD
    n_quads = D_MODEL // qw
    assert RWKV_QUAD * CHUNK == qw

    @pl.when(pl.program_id(1) == 0)
    def _():
        s_scr[...] = jnp.zeros_like(s_scr)

    t_idx, s_idx = _iota((CHUNK, qw), 0), _iota((CHUNK, qw), 1) & (CHUNK - 1)
    strict = t_idx > s_idx
    incl = t_idx >= s_idx
    eye = (t_idx == s_idx).astype(F32)
    blocks = jnp.right_shift(_iota((qw, qw), 0), 6) == jnp.right_shift(_iota((qw, qw), 1), 6)

    def bdiag(z):
        zb = z.astype(BF16)
        return jnp.where(blocks, jnp.concatenate([zb] * RWKV_QUAD, axis=0), jnp.zeros((), BF16))

    quads = range(n_quads)
    sls = [slice(q * qw, (q + 1) * qw) for q in quads]
    cc = CHUNK

    n_chunks = tc // CHUNK
    items = [(c, q) for c in range(n_chunks) for q in quads]
    every = range(len(items))

    scaled = []
    for c in range(n_chunks):
        rows = slice(c * CHUNK, (c + 1) * CHUNK)
        lw = lw_ref[rows, :]
        cum = _chunk_cumsum(lw)
        g_in = jnp.exp(cum)
        g_inv = jnp.exp(-cum)
        g_last = g_in[CHUNK - 1:CHUNK, :]
        scaled.append(dict(
            rt=r_ref[rows, :] * g_in, at=-kk_ref[rows, :] * jnp.exp(cum - lw),
            bt=bb_ref[rows, :] * g_inv, kt=k_ref[rows, :] * g_inv, v=v_ref[rows, :], g_last=g_last))

    def part(name, i):
        c, q = items[i]
        return scaled[c][name][:, sls[q]]

    ar = [jnp.concatenate([part("at", i), part("rt", i)], axis=0).astype(BF16) for i in every]
    bk4 = [jnp.concatenate([bdiag(part("bt", i)), bdiag(part("kt", i))], axis=0) for i in every]
    gram = [_nt(ar[i], bk4[i]) for i in every]
    a_ab = [jnp.where(strict, gram[i][:cc, :qw], 0.0) for i in every]
    a_k = [jnp.concatenate([jnp.where(strict, gram[i][:cc, qw:], 0.0),
                            jnp.where(incl, gram[i][cc:, qw:], 0.0)], axis=0).astype(BF16) for i in every]
    a_rb = [jnp.where(incl, gram[i][cc:, :qw], 0.0).astype(BF16) for i in every]
    akv = [_mm(a_k[i], bdiag(part("v", i))) for i in every]
    inv = [eye + a_ab[i] for i in every]
    pw = [_mm(a_ab[i].astype(BF16), bdiag(a_ab[i])) for i in every]
    for _ in range(1, CHUNK.bit_length() - 2):
        both = [_mm(jnp.concatenate([pw[i], inv[i]], axis=0).astype(BF16), bdiag(pw[i])) for i in every]
        inv = [inv[i] + both[i][cc:] for i in every]
        pw = [both[i][:cc] for i in every]
    inv = [(inv[i] + _mm(inv[i].astype(BF16), bdiag(pw[i]))).astype(BF16) for i in every]
    tt = [_mm(inv[i], jnp.concatenate([bdiag(part("at", i)), bdiag(akv[i][:cc])], axis=1)) for i in every]
    ta = [tt[i][:, :qw].astype(BF16) for i in every]
    tav = [tt[i][:, qw:] for i in every]
    bkg = [(jnp.concatenate([part("bt", i), part("kt", i)], axis=0) * part("g_last", i)).astype(BF16)
           for i in every]
    w = [jnp.where(blocks, _tn(ta[i], bkg[i][:cc]), 0.0).astype(BF16) for i in every]
    n = [jnp.where(blocks, _tn(jnp.concatenate([tav[i], part("v", i)], axis=0).astype(BF16), bkg[i]), 0.0)
         for i in every]

    s_in = []
    state = [s_scr[q] for q in quads]
    for c in range(n_chunks):
        s_b = [state[q].astype(BF16) for q in quads]
        s_in.extend(s_b)
        state = [state[q] * scaled[c]["g_last"][:, sls[q]] + _mm(s_b[q], w[c * n_quads + q]) + n[c * n_quads + q]
                 for q in quads]
    for q in quads:
        s_scr[q] = state[q]

    xs = [_nt(jnp.concatenate([ta[i], ar[i][cc:]], axis=0), s_in[i]) for i in every]
    u = [xs[i][:cc] + tav[i] for i in every]
    for i in every:
        c, q = items[i]
        y_ref[c * CHUNK:(c + 1) * CHUNK, sls[q]] = xs[i][cc:] + akv[i][cc:] + _mm(a_rb[i], bdiag(u[i]))

    @pl.when(pl.program_id(1) == pl.num_programs(1) - 1)
    def _():
        for hd in range(RWKV_H):
            q, o = divmod(hd, RWKV_QUAD)
            so_ref[hd] = s_scr[q][o * RWKV_HEAD:(o + 1) * RWKV_HEAD, o * RWKV_HEAD:(o + 1) * RWKV_HEAD]


def _rwkv_rec_prompt(r, lw, k, v, kk, bb, tc=256):
    b, l, _ = r.shape
    tc = _row_tile(l, tc)
    tile = pl.BlockSpec((None, tc, D_MODEL), lambda i, j: (i, j, 0))
    st = (RWKV_H, RWKV_HEAD, RWKV_HEAD)
    return pl.pallas_call(
        _rwkv_rec_prompt_kernel,
        grid=(b, l // tc),
        in_specs=[tile] * 6,
        out_specs=[tile, pl.BlockSpec((None,) + st, lambda i, j: (i, 0, 0, 0))],
        out_shape=[jax.ShapeDtypeStruct((b, l, D_MODEL), F32), jax.ShapeDtypeStruct((b,) + st, F32)],
        scratch_shapes=[pltpu.VMEM((RWKV_H // RWKV_QUAD, RWKV_QUAD * RWKV_HEAD, RWKV_QUAD * RWKV_HEAD), F32)],
        compiler_params=_params(("parallel", "arbitrary")),
        name="rwkv_rec_prompt",
    )(r, lw, k, v, kk, bb)


def _rwkv_rec_decode_kernel(r_ref, lw_ref, k_ref, v_ref, kk_ref, bb_ref, s_ref, y_ref, so_ref):
    w = jnp.exp(lw_ref[...])
    a = -kk_ref[...]
    b, k, r = bb_ref[...], k_ref[...], r_ref[...]

    def row(vi, carry):
        s0 = s_ref[vi]
        sa = jnp.sum(s0 * a, axis=0, keepdims=True)
        s1 = s0 * w + sa * b + v_ref[pl.ds(vi, 1), :] * k
        so_ref[vi] = s1
        y_ref[pl.ds(vi, 1), :] = jnp.sum(s1 * r, axis=0, keepdims=True)
        return carry

    lax.fori_loop(0, RWKV_HEAD, row, 0, unroll=8)


def _rwkv_rec_decode(r, lw, k, v, kk, bb, s):
    m = r.shape[1]
    vec = pl.BlockSpec((RWKV_HEAD, m), lambda i: (i, 0))
    st = pl.BlockSpec((None, RWKV_HEAD, RWKV_HEAD, m), lambda i: (i, 0, 0, 0))
    return pl.pallas_call(
        _rwkv_rec_decode_kernel,
        grid=(RWKV_H,),
        in_specs=[vec] * 6 + [st],
        out_specs=[vec, st],
        out_shape=[jax.ShapeDtypeStruct((D_MODEL, m), F32), jax.ShapeDtypeStruct(s.shape, F32)],
        compiler_params=_params(("parallel",)),
        name="rwkv_rec_decode",
    )(r, lw, k, v, kk, bb, s)


def _rwkv_post_kernel(h_ref, y_ref, bo_ref, g_ref, lnw_ref, lnb_ref, wo_ref, *rest):
    *mlp_refs, o_ref = rest
    y = y_ref[...]
    d = y - _group_sum(y, RWKV_HEAD, split=True) * (1.0 / RWKV_HEAD)
    var = _group_sum(d * d, RWKV_HEAD) * (1.0 / RWKV_HEAD)
    yn = d * lax.rsqrt(var + RWKV_GN_EPS) * lnw_ref[...] + lnb_ref[...]
    h = h_ref[...] + _bdot((yn + bo_ref[...]) * g_ref[...], wo_ref[...])
    o_ref[...] = _mlp(h, *mlp_refs) if mlp_refs else h


def _rwkv_post_ffn(h, y, bonus, g, p, mlp=None, tm=512):
    m = h.shape[0]
    tm = _row_tile(m, tm)
    row = pl.BlockSpec((tm, D_MODEL), lambda i: (i, 0))
    vec = _const_spec((1, D_MODEL))
    mlp_ins, mlp_specs = _mlp_operands(mlp) if mlp else ([], [])
    return pl.pallas_call(
        _rwkv_post_kernel,
        grid=(m // tm,),
        in_specs=[row] * 4 + [vec, vec, _const_spec((D_MODEL, D_MODEL))] + mlp_specs,
        out_specs=row,
        out_shape=jax.ShapeDtypeStruct((m, D_MODEL), F32),
        compiler_params=_params(("parallel",)),
        name="rwkv_post_ffn",
    )(h, y, bonus, g, p["ln_w"].reshape(1, D_MODEL), p["ln_b"].reshape(1, D_MODEL), p["wo"].astype(BF16),
      *mlp_ins)


def _gla_pre_kernel(h_ref, nm_ref, win_ref, wgl_ref, wgk_ref, bgk_ref, q_ref, k_ref, v_ref, g_ref, gk_ref):
    u = _rms(h_ref[...], nm_ref[...], NORM_EPS).astype(BF16)
    z = _nt(u, win_ref[...])
    q_ref[...] = z[:, :GLA_DK_TOT] * (GLA_DK ** -0.5)
    k_ref[...] = z[:, GLA_DK_TOT:2 * GLA_DK_TOT]
    v_ref[...] = z[:, 2 * GLA_DK_TOT:2 * GLA_DK_TOT + GLA_DV_TOT]
    g_ref[...] = z[:, 2 * GLA_DK_TOT + GLA_DV_TOT:]
    gl = _nt(u, wgl_ref[...])
    pre = _bdot(gl, wgk_ref[...]) + bgk_ref[...]
    gk_ref[...] = -_softplus(-pre) * (1.0 / GLA_NORMALIZER)


def _gla_pre(h, nm, p, tm=1024):
    m = h.shape[0]
    tm = _row_tile(m, tm)
    n_main = 2 * GLA_DK_TOT + 2 * GLA_DV_TOT
    w_t = p["w_in"].T
    w_main = w_t[:n_main].astype(BF16)
    w_gl = jnp.pad(w_t[n_main:].astype(BF16), ((0, LANES - GLA_LR), (0, 0)))
    w_gk = jnp.pad(p["w_gk2"].astype(BF16), ((0, LANES - GLA_LR), (0, 0)))
    row = lambda n: pl.BlockSpec((tm, n), lambda i: (i, 0))
    widths = (GLA_DK_TOT, GLA_DK_TOT, GLA_DV_TOT, GLA_DV_TOT, GLA_DK_TOT)
    return pl.pallas_call(
        _gla_pre_kernel,
        grid=(m // tm,),
        in_specs=[row(D_MODEL), _const_spec((1, D_MODEL)), _const_spec(w_main.shape), _const_spec(w_gl.shape),
                  _const_spec(w_gk.shape), _const_spec((1, GLA_DK_TOT))],
        out_specs=[row(n) for n in widths],
        out_shape=[jax.ShapeDtypeStruct((m, n), F32) for n in widths],
        compiler_params=_params(("parallel",)),
        name="gla_pre",
    )(h, nm.reshape(1, D_MODEL), w_main, w_gl, w_gk, p["b_gk"].reshape(1, GLA_DK_TOT))


def _gla_rec_prompt_kernel(q_ref, k_ref, v_ref, gk_ref, o_ref, so_ref, s_scr):
    tc = q_ref.shape[0]

    @pl.when(pl.program_id(1) == 0)
    def _():
        s_scr[...] = jnp.zeros_like(s_scr)

    ri, ci = _iota((CHUNK, CHUNK), 0), _iota((CHUNK, CHUNK), 1)
    incl = ri >= ci
    n_sub = CHUNK // GLA_SUB
    sub_of_row = jnp.right_shift(_iota((CHUNK, 1), 0), GLA_SUB.bit_length() - 1)

    heads = range(GLA_H)
    ks = [slice(hd * GLA_DK, (hd + 1) * GLA_DK) for hd in heads]
    vs = [slice(hd * GLA_DV, (hd + 1) * GLA_DV) for hd in heads]
    n_chunks = tc // CHUNK
    items = [(c, hd) for c in range(n_chunks) for hd in heads]

    prep = []
    for c in range(n_chunks):
        rows = slice(c * CHUNK, (c + 1) * CHUNK)
        cum = _chunk_cumsum(gk_ref[rows, :])
        q, k = q_ref[rows, :], k_ref[rows, :]
        last = cum[CHUNK - 1:CHUNK, :]
        refs = [jnp.zeros_like(last)] + [cum[i * GLA_SUB - 1:i * GLA_SUB, :] for i in range(1, n_sub)]
        ref_of_row = refs[0]
        for i in range(1, n_sub):
            ref_of_row = jnp.where(sub_of_row >= i, refs[i], ref_of_row)
        q_in = (q * jnp.exp(cum - ref_of_row)).astype(BF16)
        k_in = []
        for i in range(n_sub):
            n = (i + 1) * GLA_SUB
            part = (k[:n] * jnp.exp(refs[i] - cum[:n])).astype(BF16)
            k_in.append(part if n == CHUNK else
                        jnp.concatenate([part, jnp.zeros((CHUNK - n, part.shape[1]), BF16)], axis=0))
        prep.append(dict(q_in=q_in, k_in=k_in, qe=(q * jnp.exp(cum)).astype(BF16),
                         kh=(k * jnp.exp(last - cum)).astype(BF16), g_last=jnp.exp(last),
                         v=v_ref[rows, :].astype(BF16)))

    def scores(c, hd):
        p = prep[c]
        blocks = [_nt(p["q_in"][i * GLA_SUB:(i + 1) * GLA_SUB, ks[hd]], p["k_in"][i][:, ks[hd]])
                  for i in range(n_sub)]
        return jnp.where(incl, jnp.concatenate(blocks, axis=0), 0.0).astype(BF16)

    att = [scores(c, hd) for c, hd in items]
    intra = [_mm(att[i], prep[c]["v"][:, vs[hd]]) for i, (c, hd) in enumerate(items)]
    kv = [_tn(prep[c]["v"][:, vs[hd]], prep[c]["kh"][:, ks[hd]]) for c, hd in items]

    s_in = []
    state = [s_scr[hd] for hd in heads]
    for c in range(n_chunks):
        s_in.extend(state[hd].astype(BF16) for hd in heads)
        state = [state[hd] * prep[c]["g_last"][:, ks[hd]] + kv[c * GLA_H + hd] for hd in heads]
    for hd in heads:
        s_scr[hd] = state[hd]

    for i, (c, hd) in enumerate(items):
        o_ref[c * CHUNK:(c + 1) * CHUNK, vs[hd]] = _nt(prep[c]["qe"][:, ks[hd]], s_in[i]) + intra[i]

    @pl.when(pl.program_id(1) == pl.num_programs(1) - 1)
    def _():
        for hd in range(GLA_H):
            so_ref[hd] = s_scr[hd].T


def _gla_rec_prompt(q, k, v, gk, tc=256):
    b, l, _ = q.shape
    tc = _row_tile(l, tc)
    tile = lambda n: pl.BlockSpec((None, tc, n), lambda i, j: (i, j, 0))
    st = (GLA_H, GLA_DK, GLA_DV)
    return pl.pallas_call(
        _gla_rec_prompt_kernel,
        grid=(b, l // tc),
        in_specs=[tile(GLA_DK_TOT), tile(GLA_DK_TOT), tile(GLA_DV_TOT), tile(GLA_DK_TOT)],
        out_specs=[tile(GLA_DV_TOT), pl.BlockSpec((None,) + st, lambda i, j: (i, 0, 0, 0))],
        out_shape=[jax.ShapeDtypeStruct((b, l, GLA_DV_TOT), F32), jax.ShapeDtypeStruct((b,) + st, F32)],
        scratch_shapes=[pltpu.VMEM((GLA_H, GLA_DV, GLA_DK), F32)],
        compiler_params=_params(("parallel", "arbitrary")),
        name="gla_rec_prompt",
    )(q, k, v, gk)


def _gla_rec_decode_kernel(q_ref, k_ref, v_ref, gk_ref, s_ref, o_ref, so_ref):
    eye = (_iota((GLA_DK, GLA_DK), 0) == _iota((GLA_DK, GLA_DK), 1)).astype(F32)[None]

    def col(x):
        return jnp.sum(eye * x, axis=-1, keepdims=True)

    for hd in range(GLA_H):
        hs = slice(hd, hd + 1)
        s1 = s_ref[:, hd] * col(jnp.exp(gk_ref[:, hs, :])) + col(k_ref[:, hs, :]) * v_ref[:, hs, :]
        so_ref[:, hd] = s1
        o_ref[:, hs, :] = jnp.sum(col(q_ref[:, hs, :]) * s1, axis=1, keepdims=True)


def _gla_rec_decode(q, k, v, gk, s, bblk=16):
    m = q.shape[0]
    kvec = pl.BlockSpec((bblk, GLA_H, GLA_DK), lambda i: (i, 0, 0))
    vvec = pl.BlockSpec((bblk, GLA_H, GLA_DV), lambda i: (i, 0, 0))
    st = pl.BlockSpec((bblk, GLA_H, GLA_DK, GLA_DV), lambda i: (i, 0, 0, 0))
    hk = lambda x: x.reshape(m, GLA_H, GLA_DK)
    o, s_new = pl.pallas_call(
        _gla_rec_decode_kernel,
        grid=(m // bblk,),
        in_specs=[kvec, kvec, vvec, kvec, st],
        out_specs=[vvec, st],
        out_shape=[jax.ShapeDtypeStruct((m, GLA_H, GLA_DV), F32), jax.ShapeDtypeStruct(s.shape, F32)],
        compiler_params=_params(("parallel",)),
        name="gla_rec_decode",
    )(hk(q), hk(k), v.reshape(m, GLA_H, GLA_DV), hk(gk), s)
    return o.reshape(m, GLA_DV_TOT), s_new


def _gla_post_kernel(h_ref, o_ref_in, g_ref, nw_ref, wo_ref, *rest):
    *mlp_refs, out_ref = rest
    o = o_ref_in[...]
    parts = []
    for hd in range(GLA_H):
        oh = o[:, hd * GLA_DV:(hd + 1) * GLA_DV]
        parts.append(oh * lax.rsqrt(jnp.mean(oh * oh, axis=-1, keepdims=True) + GLA_NORM_EPS))
    on = jnp.concatenate(parts, axis=1) * nw_ref[...]
    g = g_ref[...]
    h = h_ref[...] + _bdot(on * (g * jax.nn.sigmoid(g)), wo_ref[...])
    out_ref[...] = _mlp(h, *mlp_refs) if mlp_refs else h


def _gla_post_ffn(h, o, g, p, mlp=None, tm=512):
    m = h.shape[0]
    tm = _row_tile(m, tm)
    row = pl.BlockSpec((tm, D_MODEL), lambda i: (i, 0))
    mlp_ins, mlp_specs = _mlp_operands(mlp) if mlp else ([], [])
    return pl.pallas_call(
        _gla_post_kernel,
        grid=(m // tm,),
        in_specs=[row] * 3 + [_const_spec((1, D_MODEL)), _const_spec((D_MODEL, D_MODEL))] + mlp_specs,
        out_specs=row,
        out_shape=jax.ShapeDtypeStruct((m, D_MODEL), F32),
        compiler_params=_params(("parallel",)),
        name="gla_post_ffn",
    )(h, o, g, jnp.tile(p["norm"], GLA_H).reshape(1, D_MODEL), p["wo"].astype(BF16), *mlp_ins)


def _conv_gates(h_ref, nm_ref, win_ref):
    u = _rms(h_ref[...], nm_ref[...], NORM_EPS).astype(BF16)
    z = jnp.dot(u, win_ref[...], preferred_element_type=F32)
    return z[:, :D_MODEL], z[:, D_MODEL:2 * D_MODEL] * z[:, 2 * D_MODEL:]


def _conv_prompt_kernel(h_ref, nm_ref, win_ref, cw_ref, wo_ref, o_ref, st_ref, carry_ref):
    tm = h_ref.shape[0]

    @pl.when(pl.program_id(1) == 0)
    def _():
        carry_ref[...] = jnp.zeros_like(carry_ref)

    ts = tm // SUBTILES
    subs = [slice(i * ts, (i + 1) * ts) for i in range(SUBTILES)]
    gates = [_conv_gates(h_ref.at[sl], nm_ref, win_ref) for sl in subs]
    row = _iota((ts, D_MODEL), 0)
    prev2, prev1 = carry_ref[0:1, :], carry_ref[1:2, :]
    gated = []
    for g_b, zc in gates:
        z1 = jnp.where(row == 0, prev1, pltpu.roll(zc, 1, axis=0))
        z2 = jnp.where(row == 0, prev2, jnp.where(row == 1, prev1, pltpu.roll(zc, 2, axis=0)))
        gated.append((g_b * (cw_ref[0:1, :] * z2 + cw_ref[1:2, :] * z1 + cw_ref[2:3, :] * zc)).astype(BF16))
        prev2, prev1 = zc[ts - 2:ts - 1, :], zc[ts - 1:ts, :]
    for sl, x in zip(subs, gated):
        o_ref[sl, :] = h_ref[sl, :] + jnp.dot(x, wo_ref[...], preferred_element_type=F32)
    tail = gates[-1][1][ts - (CONV_W - 1):, :]
    carry_ref[0:CONV_W - 1, :] = tail
    st_ref[...] = tail


def _conv_prompt(h, nm, p, tm=1024):
    b, l, _ = h.shape
    tm = _row_tile(l, tm)
    tile = pl.BlockSpec((None, tm, D_MODEL), lambda i, j: (i, j, 0))
    return pl.pallas_call(
        _conv_prompt_kernel,
        grid=(b, l // tm),
        in_specs=[tile, _const_spec((1, D_MODEL)), _const_spec((D_MODEL, 3 * D_MODEL)),
                  _const_spec((CONV_W, D_MODEL)), _const_spec((D_MODEL, D_MODEL))],
        out_specs=[tile, pl.BlockSpec((None, CONV_W - 1, D_MODEL), lambda i, j: (i, 0, 0))],
        out_shape=[jax.ShapeDtypeStruct((b, l, D_MODEL), F32),
                   jax.ShapeDtypeStruct((b, CONV_W - 1, D_MODEL), F32)],
        scratch_shapes=[pltpu.VMEM((SUBLANES, D_MODEL), F32)],
        compiler_params=_params(("parallel", "arbitrary")),
        name="conv_prompt",
    )(h, nm.reshape(1, D_MODEL), p["w_in"].astype(BF16), p["w"], p["wo"].astype(BF16))


def _conv_decode_kernel(h_ref, buf_ref, nm_ref, win_ref, cw_ref, wo_ref, o_ref, st_ref):
    g_b, zc = _conv_gates(h_ref, nm_ref, win_ref)
    z2 = buf_ref[:, 0, :]
    z1 = buf_ref[:, 1, :]
    conv = cw_ref[0:1, :] * z2 + cw_ref[1:2, :] * z1 + cw_ref[2:3, :] * zc
    o_ref[...] = h_ref[...] + _bdot(g_b * conv, wo_ref[...])
    st_ref[:, 0, :] = z1
    st_ref[:, 1, :] = zc


def _conv_decode(h, buf, nm, p):
    m = h.shape[0]
    full = pl.BlockSpec((m, D_MODEL), lambda i: (0, 0))
    bufs = pl.BlockSpec(buf.shape, lambda i: (0, 0, 0))
    return pl.pallas_call(
        _conv_decode_kernel,
        grid=(1,),
        in_specs=[full, bufs, _const_spec((1, D_MODEL)), _const_spec((D_MODEL, 3 * D_MODEL)),
                  _const_spec((CONV_W, D_MODEL)), _const_spec((D_MODEL, D_MODEL))],
        out_specs=[full, bufs],
        out_shape=[jax.ShapeDtypeStruct((m, D_MODEL), F32), jax.ShapeDtypeStruct(buf.shape, F32)],
        compiler_params=_params(("arbitrary",)),
        name="conv_decode",
    )(h, buf, nm.reshape(1, D_MODEL), p["w_in"].astype(BF16), p["w"], p["wo"].astype(BF16))


def _pool_project(d_groups, pw_ref, sc_ref):
    ys = [_bdot(d, pw_ref[gi]) for gi, d in enumerate(d_groups)]
    return jnp.concatenate(ys, axis=1) * sc_ref[...]


def _pool_prompt_kernel(h_ref, nm_ref, pw_ref, sc_ref, nf_ref, wu_ref, wd_ref, gf_ref, o_ref, st_ref, carry_ref):
    tm = h_ref.shape[0]
    hist = carry_ref.shape[0]
    j = pl.program_id(1)

    @pl.when(j == 0)
    def _():
        carry_ref[...] = jnp.zeros_like(carry_ref)

    u = _rms(h_ref[...], nm_ref[...], NORM_EPS)
    ext = jnp.concatenate([carry_ref[...], u], axis=0)
    pos = j * tm + _iota((tm, 1), 0)
    d_groups = []
    for gi, w in enumerate(POOL_WINDOWS):
        sl = slice(gi * POOL_G, (gi + 1) * POOL_G)
        s = ext[:, sl]
        span = 1
        while span < w:
            s = s + pltpu.roll(s, span, axis=0)
            span *= 2
        cnt = jnp.minimum(w, pos + 1).astype(F32)
        d_groups.append(s[hist:, :] / cnt - u[:, sl])
    o_ref[...] = _mlp(h_ref[...] + _pool_project(d_groups, pw_ref, sc_ref), nf_ref, wu_ref, wd_ref, gf_ref)
    tail = u[tm - hist:, :]
    carry_ref[...] = tail
    st_ref[...] = tail


def _pool_prompt_ffn(h, nm, p, mlp, final_g, tm=512):
    b, l, _ = h.shape
    tm = _row_tile(l, tm)
    hist = POOL_BUF + 1
    tile = pl.BlockSpec((None, tm, D_MODEL), lambda i, j: (i, j, 0))
    mlp_ins, mlp_specs = _mlp_operands(mlp, final_g)
    out, st = pl.pallas_call(
        _pool_prompt_kernel,
        grid=(b, l // tm),
        in_specs=[tile, _const_spec((1, D_MODEL)), _const_spec(p["w"].shape), _const_spec((1, D_MODEL))]
        + mlp_specs,
        out_specs=[tile, pl.BlockSpec((None, hist, D_MODEL), lambda i, j: (i, 0, 0))],
        out_shape=[jax.ShapeDtypeStruct((b, l, D_MODEL), F32), jax.ShapeDtypeStruct((b, hist, D_MODEL), F32)],
        scratch_shapes=[pltpu.VMEM((hist, D_MODEL), F32)],
        compiler_params=_params(("parallel", "arbitrary")),
        name="pool_prompt_ffn",
    )(h, nm.reshape(1, D_MODEL), p["w"].astype(BF16), p["scale"].reshape(1, D_MODEL), *mlp_ins)
    return out, st[:, 1:]


def _pool_decode_kernel(h_ref, buf_ref, nm_ref, pw_ref, sc_ref, o_ref, st_ref):
    u = _rms(h_ref[...], nm_ref[...], NORM_EPS)
    d_groups = []
    for gi, w in enumerate(POOL_WINDOWS):
        sl = slice(gi * POOL_G, (gi + 1) * POOL_G)
        s = u[:, sl]
        for i in range(1, w):
            s = s + buf_ref[POOL_BUF - i, :, sl]
        cnt = float(min(w, PAST_LEN + 1))
        d_groups.append(s / cnt - u[:, sl])
    o_ref[...] = h_ref[...] + _pool_project(d_groups, pw_ref, sc_ref)
    st_ref[0:POOL_BUF - 1] = buf_ref[1:POOL_BUF]
    st_ref[POOL_BUF - 1] = u


def _pool_decode(h, buf, nm, p):
    m = h.shape[0]
    full = pl.BlockSpec((m, D_MODEL), lambda i: (0, 0))
    bufs = pl.BlockSpec(buf.shape, lambda i: (0, 0, 0))
    return pl.pallas_call(
        _pool_decode_kernel,
        grid=(1,),
        in_specs=[full, bufs, _const_spec((1, D_MODEL)), _const_spec(p["w"].shape), _const_spec((1, D_MODEL))],
        out_specs=[full, bufs],
        out_shape=[jax.ShapeDtypeStruct((m, D_MODEL), F32), jax.ShapeDtypeStruct(buf.shape, F32)],
        compiler_params=_params(("arbitrary",)),
        name="pool_decode",
    )(h, buf, nm.reshape(1, D_MODEL), p["w"].astype(BF16), p["scale"].reshape(1, D_MODEL))


def _trunk_prompt(x, nm, nfin, mlps, rw, gl, cv, po):
    b, l, _ = x.shape
    flat = lambda t: t.reshape(b * l, t.shape[-1])
    seq = lambda t: t.reshape(b, l, t.shape[-1])

    (r, lw, k, v, kk, bb, g, bonus), shift = _rwkv_pre_prompt(x, nm[0], rw)
    y, wkv = _rwkv_rec_prompt(r, lw, k, v, kk, bb)
    h = _rwkv_post_ffn(flat(x), flat(y), flat(bonus), flat(g), rw, mlps[0])

    q, k, v, g, gk = _gla_pre(h, nm[1], gl)
    o, gla_s = _gla_rec_prompt(seq(q), seq(k), seq(v), seq(gk))
    h = _gla_post_ffn(h, flat(o), g, gl, mlps[1])

    h, conv_s = _conv_prompt(seq(h), nm[2], cv)
    h = _ffn(flat(h), mlps[2])

    y, pool_s = _pool_prompt_ffn(seq(h), nm[3], po, mlps[3], nfin)
    return y, wkv[None], shift[None], gla_s[None], conv_s[None], pool_s[None]


def _trunk_decode(x, st_wkv, st_shift, st_gla, st_conv, st_pool, nm, nf, nfin, w_up, w_down, rw, gl, cv, po):
    m = x.shape[0]
    h = x.reshape(m, D_MODEL)
    mlps = []

    def mlp(h, i, final_g=None):
        h, wub, wdb = _ffn_cast(h, nf[i], w_up, w_down, i, final_g)
        mlps.append((nf[i], wub, wdb))
        return h

    (r, lw, k, v, kk, bb, g, bonus), shift = _rwkv_pre_decode(h, st_shift[0], nm[0], rw)
    y_t, wkv_t = _rwkv_rec_decode(r, lw, k, v, kk, bb, jnp.transpose(st_wkv[0], (1, 2, 3, 0)))
    wkv = jnp.transpose(wkv_t, (3, 0, 1, 2))
    h = mlp(_rwkv_post_ffn(h, y_t.T, bonus, g, rw), 0)

    q, k, v, g, gk = _gla_pre(h, nm[1], gl)
    o, gla_s = _gla_rec_decode(q, k, v, gk, st_gla[0])
    h = mlp(_gla_post_ffn(h, o, g, gl), 1)

    h, conv_s = _conv_decode(h, st_conv[0], nm[2], cv)
    h = mlp(h, 2)

    h, pool_t = _pool_decode(h, jnp.transpose(st_pool[0], (1, 0, 2)), nm[3], po)
    pool_s = jnp.transpose(pool_t, (1, 0, 2))
    y = mlp(h, 3, final_g=nfin)
    return (y.reshape(m, 1, D_MODEL), wkv[None], shift[None], gla_s[None], conv_s[None], pool_s[None]), mlps


def kernel(x_prompt, x_sample, state_rwkv_wkv, state_rwkv_shift, state_gla, state_conv, state_pool, norm_mix, norm_ffn, norm_final, ffn_up, ffn_down, rwkv_mu, rwkv_w_rkv, rwkv_w0, rwkv_w1, rwkv_w2, rwkv_a0, rwkv_a1, rwkv_a2, rwkv_g1, rwkv_g2, rwkv_k_k, rwkv_k_a, rwkv_r_k, rwkv_ln_w, rwkv_ln_b, rwkv_wo, gla_w_in, gla_w_gk2, gla_b_gk, gla_norm, gla_wo, conv_w_in, conv_w, conv_wo, pool_w, pool_scale):
    assert x_prompt.shape[1] % CHUNK == 0 and x_sample.shape[1] == 1
    rw = dict(mu=rwkv_mu[0], w_rkv=rwkv_w_rkv[0], w0=rwkv_w0[0], w1=rwkv_w1[0], w2=rwkv_w2[0], a0=rwkv_a0[0],
              a1=rwkv_a1[0], a2=rwkv_a2[0], g1=rwkv_g1[0], g2=rwkv_g2[0], k_k=rwkv_k_k[0], k_a=rwkv_k_a[0],
              r_k=rwkv_r_k[0], ln_w=rwkv_ln_w[0], ln_b=rwkv_ln_b[0], wo=rwkv_wo[0])
    gl = dict(w_in=gla_w_in[0], w_gk2=gla_w_gk2[0], b_gk=gla_b_gk[0], norm=gla_norm[0], wo=gla_wo[0])
    cv = dict(w_in=conv_w_in[0], w=conv_w[0], wo=conv_wo[0])
    po = dict(w=pool_w[0], scale=pool_scale[0])
    (y_s, wkv_s, sh_s, gla_s, conv_s, pool_s), mlps = _trunk_decode(
        x_sample, state_rwkv_wkv, state_rwkv_shift, state_gla, state_conv, state_pool,
        norm_mix, norm_ffn, norm_final, ffn_up, ffn_down, rw, gl, cv, po)
    y_p, wkv_p, sh_p, gla_p, conv_p, pool_p = _trunk_prompt(x_prompt, norm_mix, norm_final, mlps, rw, gl, cv, po)
    return (y_p, y_s, wkv_p, wkv_s, sh_p, sh_s, gla_p, gla_s, conv_p, conv_s, pool_p, pool_s)
```

```python
import functools
import math

import jax
import jax.numpy as jnp
from jax import lax
from jax.experimental import pallas as pl
from jax.experimental.pallas import tpu as pltpu

F32 = jnp.float32
BF16 = jnp.bfloat16

D_MODEL = 1024
D_FF = 4 * D_MODEL
NORM_EPS = 1e-6
PAST_LEN = 16384

RWKV_HEAD = 64
RWKV_H = D_MODEL // RWKV_HEAD
RWKV_GN_EPS = 64e-5
RWKV_QUAD = 4

GLA_H = 4
GLA_DK_TOT = D_MODEL // 2
GLA_DV_TOT = D_MODEL
GLA_DK = GLA_DK_TOT // GLA_H
GLA_DV = GLA_DV_TOT // GLA_H
GLA_LR = 16
GLA_NORMALIZER = 16.0
GLA_NORM_EPS = 1e-5
GLA_SUB = 16

CONV_W = 3
POOL_WINDOWS = (2, 4, 8, 16)
POOL_G = D_MODEL // len(POOL_WINDOWS)
POOL_BUF = max(POOL_WINDOWS) - 1

LANES = 128
SUBLANES = 8
MXU_DIM = 256
VMEM_LIMIT_BYTES = 56 * 1024 * 1024

CHUNK = 64
FFN_COLS = 1024
SUBTILES = 2


def _const_spec(shape):
    nd = len(shape)
    return pl.BlockSpec(shape, lambda *_: (0,) * nd, pipeline_mode=pl.Buffered(1))


def _params(sem):
    return pltpu.CompilerParams(dimension_semantics=sem, vmem_limit_bytes=VMEM_LIMIT_BYTES)


def _rms(x, g, eps):
    return x * lax.rsqrt(jnp.mean(x * x, axis=-1, keepdims=True) + eps) * g


def _bdot(a, w):
    return jnp.dot(a.astype(BF16), w, preferred_element_type=F32)


def _mm(a, b, prec=None):
    return jnp.dot(a, b, preferred_element_type=F32, precision=prec)


def _nt(a, b, prec=None):
    return lax.dot_general(a, b, (((1,), (1,)), ((), ())), preferred_element_type=F32, precision=prec)


def _tn(a, b, prec=None):
    return lax.dot_general(a, b, (((0,), (0,)), ((), ())), preferred_element_type=F32, precision=prec)


def _iota(shape, dim):
    return lax.broadcasted_iota(jnp.int32, shape, dim)


def _group_sum(x, group, split=False):
    shift = group.bit_length() - 1
    bd = (jnp.right_shift(_iota((MXU_DIM, MXU_DIM), 0), shift)
          == jnp.right_shift(_iota((MXU_DIM, MXU_DIM), 1), shift)).astype(BF16)
    terms = [x.astype(BF16)]
    if split:
        terms.append((x - terms[0].astype(F32)).astype(BF16))
    cols = [sum(_mm(t[:, c * MXU_DIM:(c + 1) * MXU_DIM], bd) for t in terms)
            for c in range(x.shape[1] // MXU_DIM)]
    return jnp.concatenate(cols, axis=1)


def _chunk_cumsum(x):
    c = x.shape[0]
    ltri = (_iota((c, c), 0) >= _iota((c, c), 1)).astype(BF16)
    t0 = x.astype(BF16)
    r1 = x - t0.astype(F32)
    t1 = r1.astype(BF16)
    t2 = (r1 - t1.astype(F32)).astype(BF16)
    return _mm(ltri, t0) + _mm(ltri, t1) + _mm(ltri, t2)


def _softplus(x):
    return jnp.maximum(x, 0.0) + jnp.log(1.0 + jnp.exp(-jnp.abs(x)))


def _row_tile(m, want):
    t = min(want, m)
    assert m % t == 0, (m, t)
    return t


def _mlp(h, g_ref, wu_ref, wd_ref, gf_ref=None):
    x = _rms(h, g_ref[...], NORM_EPS).astype(BF16)
    acc = h
    for c in range(D_FF // FFN_COLS):
        sl = slice(c * FFN_COLS, (c + 1) * FFN_COLS)
        a = jnp.dot(x, wu_ref[:, sl], preferred_element_type=F32)
        a = jnp.square(jnp.maximum(a, 0.0)).astype(BF16)
        acc = acc + jnp.dot(a, wd_ref[sl, :], preferred_element_type=F32)
    if gf_ref is not None:
        acc = _rms(acc, gf_ref[...], NORM_EPS)
    return acc


def _mlp_operands(mlp, final_g=None):
    nf, wu, wd = mlp
    ins = [nf.reshape(1, D_MODEL), wu, wd]
    specs = [_const_spec((1, D_MODEL)), _const_spec(wu.shape), _const_spec(wd.shape)]
    if final_g is not None:
        ins.append(final_g.reshape(1, D_MODEL))
        specs.append(_const_spec((1, D_MODEL)))
    return ins, specs


def _ffn_kernel(h_ref, g_ref, wu_ref, wd_ref, *rest):
    *gf_ref, o_ref = rest
    o_ref[...] = _mlp(h_ref[...], g_ref, wu_ref, wd_ref, *gf_ref)


def _ffn(h, mlp, final_g=None, tm=1024):
    m = h.shape[0]
    tm = _row_tile(m, tm)
    row = pl.BlockSpec((tm, D_MODEL), lambda i: (i, 0))
    mlp_ins, mlp_specs = _mlp_operands(mlp, final_g)
    return pl.pallas_call(
        _ffn_kernel,
        grid=(m // tm,),
        in_specs=[row] + mlp_specs,
        out_specs=row,
        out_shape=jax.ShapeDtypeStruct((m, D_MODEL), F32),
        compiler_params=_params(("parallel",)),
        name="ffn",
    )(h, *mlp_ins)


def _ffn_cast_kernel(h_ref, g_ref, wu_ref, wd_ref, *rest):
    *gf_ref, o_ref, wub_ref, wdb_ref, x_scr, acc_scr = rest
    c = pl.program_id(0)

    @pl.when(c == 0)
    def _():
        h = h_ref[...]
        x_scr[...] = _rms(h, g_ref[...], NORM_EPS).astype(BF16)
        acc_scr[...] = h

    wu = wu_ref[...].astype(BF16)
    wd = wd_ref[...].astype(BF16)
    wub_ref[...] = wu
    wdb_ref[...] = wd
    a = jnp.dot(x_scr[...], wu, preferred_element_type=F32)
    a = jnp.square(jnp.maximum(a, 0.0)).astype(BF16)
    acc_scr[...] += jnp.dot(a, wd, preferred_element_type=F32)

    @pl.when(c == pl.num_programs(0) - 1)
    def _():
        acc = acc_scr[...]
        o_ref[...] = _rms(acc, gf_ref[0][...], NORM_EPS) if gf_ref else acc


def _ffn_cast(h, nf, w_up, w_down, layer, final_g=None, cols=FFN_COLS):
    m = h.shape[0]
    full = pl.BlockSpec((m, D_MODEL), lambda c: (0, 0))
    vec = pl.BlockSpec((1, D_MODEL), lambda c: (0, 0))
    ins = [h, nf.reshape(1, D_MODEL), w_up, w_down]
    specs = [full, vec, pl.BlockSpec((None, D_MODEL, cols), lambda c: (layer, 0, c)),
             pl.BlockSpec((None, cols, D_MODEL), lambda c: (layer, c, 0))]
    if final_g is not None:
        ins.append(final_g.reshape(1, D_MODEL))
        specs.append(vec)
    return pl.pallas_call(
        _ffn_cast_kernel,
        grid=(D_FF // cols,),
        in_specs=specs,
        out_specs=[full, pl.BlockSpec((D_MODEL, cols), lambda c: (0, c)),
                   pl.BlockSpec((cols, D_MODEL), lambda c: (c, 0))],
        out_shape=[jax.ShapeDtypeStruct((m, D_MODEL), F32), jax.ShapeDtypeStruct((D_MODEL, D_FF), BF16),
                   jax.ShapeDtypeStruct((D_FF, D_MODEL), BF16)],
        scratch_shapes=[pltpu.VMEM((m, D_MODEL), BF16), pltpu.VMEM((m, D_MODEL), F32)],
        compiler_params=_params(("arbitrary",)),
        name="ffn_cast",
    )(*ins)


def _rwkv_pre_math(u, prev, mu_ref, wrkv_ref, w0_ref, w1_ref, w2_ref, a0_ref, a1_ref, a2_ref,
                   g1_ref, g2_ref, kk_ref, ka_ref, rk_ref):
    dx = prev - u

    def mix(i):
        return (u + dx * mu_ref[i:i + 1, :]).astype(BF16)

    r = _bdot(mix(0), wrkv_ref[0])
    w_mid = _bdot(mix(1), w1_ref[...])
    a_mid = _bdot(mix(4), a1_ref[...])
    g_mid = _bdot(mix(5), g1_ref[...])
    k = _bdot(mix(2), wrkv_ref[1])
    wl = w0_ref[...] + _bdot(jnp.tanh(w_mid), w2_ref[...])
    a = jax.nn.sigmoid(a0_ref[...] + _bdot(a_mid, a2_ref[...]))
    g = _bdot(jax.nn.sigmoid(g_mid), g2_ref[...])
    lw = jax.nn.sigmoid(wl) * (-math.exp(-0.5))
    kk = k * kk_ref[...]
    kk_ss = _group_sum(kk * kk, RWKV_HEAD)
    v = _bdot(mix(3), wrkv_ref[2])
    k = k * (1.0 + (a - 1.0) * ka_ref[...])
    rk_sum = _group_sum(r * k * rk_ref[...], RWKV_HEAD)
    kk = kk * lax.rsqrt(jnp.maximum(kk_ss, 1e-24))
    return r, lw, k, v, kk, kk * a, g, rk_sum * v


_N_RWKV_W = 13


def _rwkv_pre_prompt_kernel(h_ref, nm_ref, *rest):
    w = rest[:_N_RWKV_W]
    outs = rest[_N_RWKV_W:_N_RWKV_W + 8]
    sh_ref, carry_ref = rest[_N_RWKV_W + 8:]
    tm = h_ref.shape[0]

    @pl.when(pl.program_id(1) == 0)
    def _():
        carry_ref[...] = jnp.zeros_like(carry_ref)

    u = _rms(h_ref[...], nm_ref[...], NORM_EPS)
    prev = jnp.where(_iota(u.shape, 0) == 0, carry_ref[0:1, :], pltpu.roll(u, 1, axis=0))
    last = u[tm - 1:tm, :]
    carry_ref[0:1, :] = last
    sh_ref[...] = last
    for o_ref, val in zip(outs, _rwkv_pre_math(u, prev, *w)):
        o_ref[...] = val


def _rwkv_pre_decode_kernel(h_ref, prev_ref, nm_ref, *rest):
    w = rest[:_N_RWKV_W]
    outs = rest[_N_RWKV_W:_N_RWKV_W + 8]
    (sh_ref,) = rest[_N_RWKV_W + 8:]
    u = _rms(h_ref[...], nm_ref[...], NORM_EPS)
    sh_ref[...] = u
    vals = _rwkv_pre_math(u, prev_ref[...], *w)
    for o_ref, val in zip(outs[:6], vals[:6]):
        o_ref[...] = val.T
    for o_ref, val in zip(outs[6:], vals[6:]):
        o_ref[...] = val


def _rwkv_weights(p):
    row = lambda x: x.reshape(1, D_MODEL)
    return [p["mu"], p["w_rkv"].astype(BF16), row(p["w0"]), p["w1"].astype(BF16), p["w2"].astype(BF16),
            row(p["a0"]), p["a1"].astype(BF16), p["a2"].astype(BF16), p["g1"].astype(BF16),
            p["g2"].astype(BF16), row(p["k_k"]), row(p["k_a"]), row(p["r_k"])]


def _rwkv_pre_prompt(h, nm, p, tm=512):
    b, l, _ = h.shape
    tm = _row_tile(l, tm)
    ws = _rwkv_weights(p)
    tile = pl.BlockSpec((None, tm, D_MODEL), lambda i, j: (i, j, 0))
    outs = pl.pallas_call(
        _rwkv_pre_prompt_kernel,
        grid=(b, l // tm),
        in_specs=[tile, _const_spec((1, D_MODEL))] + [_const_spec(w.shape) for w in ws],
        out_specs=[tile] * 8 + [pl.BlockSpec((None, 1, D_MODEL), lambda i, j: (i, 0, 0))],
        out_shape=[jax.ShapeDtypeStruct((b, l, D_MODEL), F32)] * 8
        + [jax.ShapeDtypeStruct((b, 1, D_MODEL), F32)],
        scratch_shapes=[pltpu.VMEM((SUBLANES, D_MODEL), F32)],
        compiler_params=_params(("parallel", "arbitrary")),
        name="rwkv_pre_prompt",
    )(h, nm.reshape(1, D_MODEL), *ws)
    return outs[:8], outs[8].reshape(b, D_MODEL)


def _rwkv_pre_decode(h, prev, nm, p):
    m = h.shape[0]
    ws = _rwkv_weights(p)
    full = pl.BlockSpec((m, D_MODEL), lambda i: (0, 0))
    full_t = pl.BlockSpec((D_MODEL, m), lambda i: (0, 0))
    outs = pl.pallas_call(
        _rwkv_pre_decode_kernel,
        grid=(1,),
        in_specs=[full, full, _const_spec((1, D_MODEL))] + [_const_spec(w.shape) for w in ws],
        out_specs=[full_t] * 6 + [full] * 3,
        out_shape=[jax.ShapeDtypeStruct((D_MODEL, m), F32)] * 6 + [jax.ShapeDtypeStruct((m, D_MODEL), F32)] * 3,
        compiler_params=_params(("arbitrary",)),
        name="rwkv_pre_decode",
    )(h, prev, nm.reshape(1, D_MODEL), *ws)
    return outs[:8], outs[8]


def _rwkv_rec_prompt_kernel(r_ref, lw_ref, k_ref, v_ref, kk_ref, bb_ref, y_ref, so_ref, s_scr):
    tc = r_ref.shape[0]
    qw = RWKV_QUAD * RWKV_HEAD
    n_quads = D_MODEL // qw
    assert RWKV_QUAD * CHUNK == qw

    @pl.when(pl.program_id(1) == 0)
    def _():
        s_scr[...] = jnp.zeros_like(s_scr)

    t_idx, s_idx = _iota((CHUNK, qw), 0), _iota((CHUNK, qw), 1) & (CHUNK - 1)
    strict = t_idx > s_idx
    incl = t_idx >= s_idx
    eye = (t_idx == s_idx).astype(F32)
    blocks = jnp.right_shift(_iota((qw, qw), 0), 6) == jnp.right_shift(_iota((qw, qw), 1), 6)

    def bdiag(z):
        zb = z.astype(BF16)
        return jnp.where(blocks, jnp.concatenate([zb] * RWKV_QUAD, axis=0), jnp.zeros((), BF16))

    quads = range(n_quads)
    sls = [slice(q * qw, (q + 1) * qw) for q in quads]
    cc = CHUNK

    n_chunks = tc // CHUNK
    items = [(c, q) for c in range(n_chunks) for q in quads]
    every = range(len(items))

    scaled = []
    for c in range(n_chunks):
        rows = slice(c * CHUNK, (c + 1) * CHUNK)
        lw = lw_ref[rows, :]
        cum = _chunk_cumsum(lw)
        g_in = jnp.exp(cum)
        g_inv = jnp.exp(-cum)
        g_last = g_in[CHUNK - 1:CHUNK, :]
        scaled.append(dict(
            rt=r_ref[rows, :] * g_in, at=-kk_ref[rows, :] * jnp.exp(cum - lw),
            bt=bb_ref[rows, :] * g_inv, kt=k_ref[rows, :] * g_inv, v=v_ref[rows, :], g_last=g_last))

    def part(name, i):
        c, q = items[i]
        return scaled[c][name][:, sls[q]]

    ar = [jnp.concatenate([part("at", i), part("rt", i)], axis=0).astype(BF16) for i in every]
    bk4 = [jnp.concatenate([bdiag(part("bt", i)), bdiag(part("kt", i))], axis=0) for i in every]
    gram = [_nt(ar[i], bk4[i]) for i in every]
    a_ab = [jnp.where(strict, gram[i][:cc, :qw], 0.0) for i in every]
    a_k = [jnp.concatenate([jnp.where(strict, gram[i][:cc, qw:], 0.0),
                            jnp.where(incl, gram[i][cc:, qw:], 0.0)], axis=0).astype(BF16) for i in every]
    a_rb = [jnp.where(incl, gram[i][cc:, :qw], 0.0).astype(BF16) for i in every]
    akv = [_mm(a_k[i], bdiag(part("v", i))) for i in every]
    inv = [eye + a_ab[i] for i in every]
    pw = [_mm(a_ab[i].astype(BF16), bdiag(a_ab[i])) for i in every]
    for _ in range(1, CHUNK.bit_length() - 2):
        both = [_mm(jnp.concatenate([pw[i], inv[i]], axis=0).astype(BF16), bdiag(pw[i])) for i in every]
        inv = [inv[i] + both[i][cc:] for i in every]
        pw = [both[i][:cc] for i in every]
    inv = [(inv[i] + _mm(inv[i].astype(BF16), bdiag(pw[i]))).astype(BF16) for i in every]
    tt = [_mm(inv[i], jnp.concatenate([bdiag(part("at", i)), bdiag(akv[i][:cc])], axis=1)) for i in every]
    ta = [tt[i][:, :qw].astype(BF16) for i in every]
    tav = [tt[i][:, qw:] for i in every]
    bkg = [(jnp.concatenate([part("bt", i), part("kt", i)], axis=0) * part("g_last", i)).astype(BF16)
           for i in every]
    v_b = [part("v", i).astype(BF16) for i in every]

    xs, u = [], []

    def emit_outputs(c):
        for i in range(c * n_quads, (c + 1) * n_quads):
            y_ref[c * CHUNK:(c + 1) * CHUNK, sls[items[i][1]]] = xs[i][cc:] + akv[i][cc:] + _mm(a_rb[i], bdiag(u[i]))

    state = [s_scr[q] for q in quads]
    for c in range(n_chunks):
        mine = [c * n_quads + q for q in quads]
        xs += [_nt(jnp.concatenate([ta[i], ar[i][cc:]], axis=0), state[q].astype(BF16))
               for q, i in enumerate(mine)]
        u += [xs[i][:cc] + tav[i] for i in mine]
        state = [state[q] * scaled[c]["g_last"][:, sls[q]]
                 + jnp.where(blocks, _tn(jnp.concatenate([u[i].astype(BF16), v_b[i]], axis=0), bkg[i]), 0.0)
                 for q, i in enumerate(mine)]
        if c > 0:
            emit_outputs(c - 1)
    emit_outputs(n_chunks - 1)
    for q in quads:
        s_scr[q] = state[q]

    @pl.when(pl.program_id(1) == pl.num_programs(1) - 1)
    def _():
        for hd in range(RWKV_H):
            q, o = divmod(hd, RWKV_QUAD)
            so_ref[hd] = s_scr[q][o * RWKV_HEAD:(o + 1) * RWKV_HEAD, o * RWKV_HEAD:(o + 1) * RWKV_HEAD]


def _rwkv_rec_prompt(r, lw, k, v, kk, bb, tc=256):
    b, l, _ = r.shape
    tc = _row_tile(l, tc)
    tile = pl.BlockSpec((None, tc, D_MODEL), lambda i, j: (i, j, 0))
    st = (RWKV_H, RWKV_HEAD, RWKV_HEAD)
    return pl.pallas_call(
        _rwkv_rec_prompt_kernel,
        grid=(b, l // tc),
        in_specs=[tile] * 6,
        out_specs=[tile, pl.BlockSpec((None,) + st, lambda i, j: (i, 0, 0, 0))],
        out_shape=[jax.ShapeDtypeStruct((b, l, D_MODEL), F32), jax.ShapeDtypeStruct((b,) + st, F32)],
        scratch_shapes=[pltpu.VMEM((RWKV_H // RWKV_QUAD, RWKV_QUAD * RWKV_HEAD, RWKV_QUAD * RWKV_HEAD), F32)],
        compiler_params=_params(("parallel", "arbitrary")),
        name="rwkv_rec_prompt",
    )(r, lw, k, v, kk, bb)


def _rwkv_rec_decode_kernel(r_ref, lw_ref, k_ref, v_ref, kk_ref, bb_ref, s_ref, y_ref, so_ref):
    w = jnp.exp(lw_ref[...])
    a = -kk_ref[...]
    b, k, r = bb_ref[...], k_ref[...], r_ref[...]

    def row(vi, carry):
        s0 = s_ref[vi]
        sa = jnp.sum(s0 * a, axis=0, keepdims=True)
        s1 = s0 * w + sa * b + v_ref[pl.ds(vi, 1), :] * k
        so_ref[vi] = s1
        y_ref[pl.ds(vi, 1), :] = jnp.sum(s1 * r, axis=0, keepdims=True)
        return carry

    lax.fori_loop(0, RWKV_HEAD, row, 0, unroll=8)


def _rwkv_rec_decode(r, lw, k, v, kk, bb, s):
    m = r.shape[1]
    vec = pl.BlockSpec((RWKV_HEAD, m), lambda i: (i, 0))
    st = pl.BlockSpec((None, RWKV_HEAD, RWKV_HEAD, m), lambda i: (i, 0, 0, 0))
    return pl.pallas_call(
        _rwkv_rec_decode_kernel,
        grid=(RWKV_H,),
        in_specs=[vec] * 6 + [st],
        out_specs=[vec, st],
        out_shape=[jax.ShapeDtypeStruct((D_MODEL, m), F32), jax.ShapeDtypeStruct(s.shape, F32)],
        compiler_params=_params(("parallel",)),
        name="rwkv_rec_decode",
    )(r, lw, k, v, kk, bb, s)


def _rwkv_post_kernel(h_ref, y_ref, bo_ref, g_ref, lnw_ref, lnb_ref, wo_ref, *rest):
    *mlp_refs, o_ref = rest
    y = y_ref[...]
    d = y - _group_sum(y, RWKV_HEAD, split=True) * (1.0 / RWKV_HEAD)
    var = _group_sum(d * d, RWKV_HEAD) * (1.0 / RWKV_HEAD)
    yn = d * lax.rsqrt(var + RWKV_GN_EPS) * lnw_ref[...] + lnb_ref[...]
    h = h_ref[...] + _bdot((yn + bo_ref[...]) * g_ref[...], wo_ref[...])
    o_ref[...] = _mlp(h, *mlp_refs) if mlp_refs else h


def _rwkv_post_ffn(h, y, bonus, g, p, mlp=None, tm=512):
    m = h.shape[0]
    tm = _row_tile(m, tm)
    row = pl.BlockSpec((tm, D_MODEL), lambda i: (i, 0))
    vec = _const_spec((1, D_MODEL))
    mlp_ins, mlp_specs = _mlp_operands(mlp) if mlp else ([], [])
    return pl.pallas_call(
        _rwkv_post_kernel,
        grid=(m // tm,),
        in_specs=[row] * 4 + [vec, vec, _const_spec((D_MODEL, D_MODEL))] + mlp_specs,
        out_specs=row,
        out_shape=jax.ShapeDtypeStruct((m, D_MODEL), F32),
        compiler_params=_params(("parallel",)),
        name="rwkv_post_ffn",
    )(h, y, bonus, g, p["ln_w"].reshape(1, D_MODEL), p["ln_b"].reshape(1, D_MODEL), p["wo"].astype(BF16),
      *mlp_ins)


def _gla_pre_kernel(h_ref, nm_ref, win_ref, wgl_ref, wgk_ref, bgk_ref, q_ref, k_ref, v_ref, g_ref, gk_ref):
    u = _rms(h_ref[...], nm_ref[...], NORM_EPS).astype(BF16)
    z = _nt(u, win_ref[...])
    q_ref[...] = z[:, :GLA_DK_TOT] * (GLA_DK ** -0.5)
    k_ref[...] = z[:, GLA_DK_TOT:2 * GLA_DK_TOT]
    v_ref[...] = z[:, 2 * GLA_DK_TOT:2 * GLA_DK_TOT + GLA_DV_TOT]
    g_ref[...] = z[:, 2 * GLA_DK_TOT + GLA_DV_TOT:]
    gl = _nt(u, wgl_ref[...])
    pre = _bdot(gl, wgk_ref[...]) + bgk_ref[...]
    gk_ref[...] = -_softplus(-pre) * (1.0 / GLA_NORMALIZER)


def _gla_pre(h, nm, p, tm=1024):
    m = h.shape[0]
    tm = _row_tile(m, tm)
    n_main = 2 * GLA_DK_TOT + 2 * GLA_DV_TOT
    w_t = p["w_in"].T
    w_main = w_t[:n_main].astype(BF16)
    w_gl = jnp.pad(w_t[n_main:].astype(BF16), ((0, LANES - GLA_LR), (0, 0)))
    w_gk = jnp.pad(p["w_gk2"].astype(BF16), ((0, LANES - GLA_LR), (0, 0)))
    row = lambda n: pl.BlockSpec((tm, n), lambda i: (i, 0))
    widths = (GLA_DK_TOT, GLA_DK_TOT, GLA_DV_TOT, GLA_DV_TOT, GLA_DK_TOT)
    return pl.pallas_call(
        _gla_pre_kernel,
        grid=(m // tm,),
        in_specs=[row(D_MODEL), _const_spec((1, D_MODEL)), _const_spec(w_main.shape), _const_spec(w_gl.shape),
                  _const_spec(w_gk.shape), _const_spec((1, GLA_DK_TOT))],
        out_specs=[row(n) for n in widths],
        out_shape=[jax.ShapeDtypeStruct((m, n), F32) for n in widths],
        compiler_params=_params(("parallel",)),
        name="gla_pre",
    )(h, nm.reshape(1, D_MODEL), w_main, w_gl, w_gk, p["b_gk"].reshape(1, GLA_DK_TOT))


def _gla_rec_prompt_kernel(q_ref, k_ref, v_ref, gk_ref, o_ref, so_ref, s_scr):
    tc = q_ref.shape[0]

    @pl.when(pl.program_id(1) == 0)
    def _():
        s_scr[...] = jnp.zeros_like(s_scr)

    ri, ci = _iota((CHUNK, CHUNK), 0), _iota((CHUNK, CHUNK), 1)
    incl = ri >= ci
    n_sub = CHUNK // GLA_SUB
    sub_of_row = jnp.right_shift(_iota((CHUNK, 1), 0), GLA_SUB.bit_length() - 1)

    heads = range(GLA_H)
    ks = [slice(hd * GLA_DK, (hd + 1) * GLA_DK) for hd in heads]
    vs = [slice(hd * GLA_DV, (hd + 1) * GLA_DV) for hd in heads]
    n_chunks = tc // CHUNK
    items = [(c, hd) for c in range(n_chunks) for hd in heads]

    prep = []
    for c in range(n_chunks):
        rows = slice(c * CHUNK, (c + 1) * CHUNK)
        cum = _chunk_cumsum(gk_ref[rows, :])
        q, k = q_ref[rows, :], k_ref[rows, :]
        last = cum[CHUNK - 1:CHUNK, :]
        refs = [jnp.zeros_like(last)] + [cum[i * GLA_SUB - 1:i * GLA_SUB, :] for i in range(1, n_sub)]
        ref_of_row = refs[0]
        for i in range(1, n_sub):
            ref_of_row = jnp.where(sub_of_row >= i, refs[i], ref_of_row)
        q_in = (q * jnp.exp(cum - ref_of_row)).astype(BF16)
        k_in = []
        for i in range(n_sub):
            n = (i + 1) * GLA_SUB
            part = (k[:n] * jnp.exp(refs[i] - cum[:n])).astype(BF16)
            k_in.append(part if n == CHUNK else
                        jnp.concatenate([part, jnp.zeros((CHUNK - n, part.shape[1]), BF16)], axis=0))
        prep.append(dict(q_in=q_in, k_in=k_in, qe=(q * jnp.exp(cum)).astype(BF16),
                         kh=(k * jnp.exp(last - cum)).astype(BF16), g_last=jnp.exp(last),
                         v=v_ref[rows, :].astype(BF16)))

    def scores(c, hd):
        p = prep[c]
        blocks = [_nt(p["q_in"][i * GLA_SUB:(i + 1) * GLA_SUB, ks[hd]], p["k_in"][i][:, ks[hd]])
                  for i in range(n_sub)]
        return jnp.where(incl, jnp.concatenate(blocks, axis=0), 0.0).astype(BF16)

    att = [scores(c, hd) for c, hd in items]
    intra = [_mm(att[i], prep[c]["v"][:, vs[hd]]) for i, (c, hd) in enumerate(items)]
    kv = [_tn(prep[c]["v"][:, vs[hd]], prep[c]["kh"][:, ks[hd]]) for c, hd in items]

    s_in = []
    state = [s_scr[hd] for hd in heads]
    for c in range(n_chunks):
        s_in.extend(state[hd].astype(BF16) for hd in heads)
        state = [state[hd] * prep[c]["g_last"][:, ks[hd]] + kv[c * GLA_H + hd] for hd in heads]
    for hd in heads:
        s_scr[hd] = state[hd]

    for i, (c, hd) in enumerate(items):
        o_ref[c * CHUNK:(c + 1) * CHUNK, vs[hd]] = _nt(prep[c]["qe"][:, ks[hd]], s_in[i]) + intra[i]

    @pl.when(pl.program_id(1) == pl.num_programs(1) - 1)
    def _():
        for hd in range(GLA_H):
            so_ref[hd] = s_scr[hd].T


def _gla_rec_prompt(q, k, v, gk, tc=256):
    b, l, _ = q.shape
    tc = _row_tile(l, tc)
    tile = lambda n: pl.BlockSpec((None, tc, n), lambda i, j: (i, j, 0))
    st = (GLA_H, GLA_DK, GLA_DV)
    return pl.pallas_call(
        _gla_rec_prompt_kernel,
        grid=(b, l // tc),
        in_specs=[tile(GLA_DK_TOT), tile(GLA_DK_TOT), tile(GLA_DV_TOT), tile(GLA_DK_TOT)],
        out_specs=[tile(GLA_DV_TOT), pl.BlockSpec((None,) + st, lambda i, j: (i, 0, 0, 0))],
        out_shape=[jax.ShapeDtypeStruct((b, l, GLA_DV_TOT), F32), jax.ShapeDtypeStruct((b,) + st, F32)],
        scratch_shapes=[pltpu.VMEM((GLA_H, GLA_DV, GLA_DK), F32)],
        compiler_params=_params(("parallel", "arbitrary")),
        name="gla_rec_prompt",
    )(q, k, v, gk)


def _gla_rec_decode_kernel(q_ref, k_ref, v_ref, gk_ref, s_ref, o_ref, so_ref):
    eye = (_iota((GLA_DK, GLA_DK), 0) == _iota((GLA_DK, GLA_DK), 1)).astype(F32)[None]

    def col(x):
        return jnp.sum(eye * x, axis=-1, keepdims=True)

    for hd in range(GLA_H):
        hs = slice(hd, hd + 1)
        s1 = s_ref[:, hd] * col(jnp.exp(gk_ref[:, hs, :])) + col(k_ref[:, hs, :]) * v_ref[:, hs, :]
        so_ref[:, hd] = s1
        o_ref[:, hs, :] = jnp.sum(col(q_ref[:, hs, :]) * s1, axis=1, keepdims=True)


def _gla_rec_decode(q, k, v, gk, s, bblk=16):
    m = q.shape[0]
    kvec = pl.BlockSpec((bblk, GLA_H, GLA_DK), lambda i: (i, 0, 0))
    vvec = pl.BlockSpec((bblk, GLA_H, GLA_DV), lambda i: (i, 0, 0))
    st = pl.BlockSpec((bblk, GLA_H, GLA_DK, GLA_DV), lambda i: (i, 0, 0, 0))
    hk = lambda x: x.reshape(m, GLA_H, GLA_DK)
    o, s_new = pl.pallas_call(
        _gla_rec_decode_kernel,
        grid=(m // bblk,),
        in_specs=[kvec, kvec, vvec, kvec, st],
        out_specs=[vvec, st],
        out_shape=[jax.ShapeDtypeStruct((m, GLA_H, GLA_DV), F32), jax.ShapeDtypeStruct(s.shape, F32)],
        compiler_params=_params(("parallel",)),
        name="gla_rec_decode",
    )(hk(q), hk(k), v.reshape(m, GLA_H, GLA_DV), hk(gk), s)
    return o.reshape(m, GLA_DV_TOT), s_new


def _gla_post_kernel(h_ref, o_ref_in, g_ref, nw_ref, wo_ref, *rest):
    *mlp_refs, out_ref = rest
    o = o_ref_in[...]
    parts = []
    for hd in range(GLA_H):
        oh = o[:, hd * GLA_DV:(hd + 1) * GLA_DV]
        parts.append(oh * lax.rsqrt(jnp.mean(oh * oh, axis=-1, keepdims=True) + GLA_NORM_EPS))
    on = jnp.concatenate(parts, axis=1) * nw_ref[...]
    g = g_ref[...]
    h = h_ref[...] + _bdot(on * (g * jax.nn.sigmoid(g)), wo_ref[...])
    out_ref[...] = _mlp(h, *mlp_refs) if mlp_refs else h


def _gla_post_ffn(h, o, g, p, mlp=None, tm=512):
    m = h.shape[0]
    tm = _row_tile(m, tm)
    row = pl.BlockSpec((tm, D_MODEL), lambda i: (i, 0))
    mlp_ins, mlp_specs = _mlp_operands(mlp) if mlp else ([], [])
    return pl.pallas_call(
        _gla_post_kernel,
        grid=(m // tm,),
        in_specs=[row] * 3 + [_const_spec((1, D_MODEL)), _const_spec((D_MODEL, D_MODEL))] + mlp_specs,
        out_specs=row,
        out_shape=jax.ShapeDtypeStruct((m, D_MODEL), F32),
        compiler_params=_params(("parallel",)),
        name="gla_post_ffn",
    )(h, o, g, jnp.tile(p["norm"], GLA_H).reshape(1, D_MODEL), p["wo"].astype(BF16), *mlp_ins)


def _conv_gates(h_ref, nm_ref, win_ref):
    u = _rms(h_ref[...], nm_ref[...], NORM_EPS).astype(BF16)
    z = jnp.dot(u, win_ref[...], preferred_element_type=F32)
    return z[:, :D_MODEL], z[:, D_MODEL:2 * D_MODEL] * z[:, 2 * D_MODEL:]


def _conv_prompt_kernel(h_ref, nm_ref, win_ref, cw_ref, wo_ref, o_ref, st_ref, carry_ref):
    tm = h_ref.shape[0]

    @pl.when(pl.program_id(1) == 0)
    def _():
        carry_ref[...] = jnp.zeros_like(carry_ref)

    ts = tm // SUBTILES
    subs = [slice(i * ts, (i + 1) * ts) for i in range(SUBTILES)]
    gates = [_conv_gates(h_ref.at[sl], nm_ref, win_ref) for sl in subs]
    row = _iota((ts, D_MODEL), 0)
    prev2, prev1 = carry_ref[0:1, :], carry_ref[1:2, :]
    gated = []
    for g_b, zc in gates:
        z1 = jnp.where(row == 0, prev1, pltpu.roll(zc, 1, axis=0))
        z2 = jnp.where(row == 0, prev2, jnp.where(row == 1, prev1, pltpu.roll(zc, 2, axis=0)))
        gated.append((g_b * (cw_ref[0:1, :] * z2 + cw_ref[1:2, :] * z1 + cw_ref[2:3, :] * zc)).astype(BF16))
        prev2, prev1 = zc[ts - 2:ts - 1, :], zc[ts - 1:ts, :]
    for sl, x in zip(subs, gated):
        o_ref[sl, :] = h_ref[sl, :] + jnp.dot(x, wo_ref[...], preferred_element_type=F32)
    tail = gates[-1][1][ts - (CONV_W - 1):, :]
    carry_ref[0:CONV_W - 1, :] = tail
    st_ref[...] = tail


def _conv_prompt(h, nm, p, tm=1024):
    b, l, _ = h.shape
    tm = _row_tile(l, tm)
    tile = pl.BlockSpec((None, tm, D_MODEL), lambda i, j: (i, j, 0))
    return pl.pallas_call(
        _conv_prompt_kernel,
        grid=(b, l // tm),
        in_specs=[tile, _const_spec((1, D_MODEL)), _const_spec((D_MODEL, 3 * D_MODEL)),
                  _const_spec((CONV_W, D_MODEL)), _const_spec((D_MODEL, D_MODEL))],
        out_specs=[tile, pl.BlockSpec((None, CONV_W - 1, D_MODEL), lambda i, j: (i, 0, 0))],
        out_shape=[jax.ShapeDtypeStruct((b, l, D_MODEL), F32),
                   jax.ShapeDtypeStruct((b, CONV_W - 1, D_MODEL), F32)],
        scratch_shapes=[pltpu.VMEM((SUBLANES, D_MODEL), F32)],
        compiler_params=_params(("parallel", "arbitrary")),
        name="conv_prompt",
    )(h, nm.reshape(1, D_MODEL), p["w_in"].astype(BF16), p["w"], p["wo"].astype(BF16))


def _conv_decode_kernel(h_ref, buf_ref, nm_ref, win_ref, cw_ref, wo_ref, o_ref, st_ref):
    g_b, zc = _conv_gates(h_ref, nm_ref, win_ref)
    z2 = buf_ref[:, 0, :]
    z1 = buf_ref[:, 1, :]
    conv = cw_ref[0:1, :] * z2 + cw_ref[1:2, :] * z1 + cw_ref[2:3, :] * zc
    o_ref[...] = h_ref[...] + _bdot(g_b * conv, wo_ref[...])
    st_ref[:, 0, :] = z1
    st_ref[:, 1, :] = zc


def _conv_decode(h, buf, nm, p):
    m = h.shape[0]
    full = pl.BlockSpec((m, D_MODEL), lambda i: (0, 0))
    bufs = pl.BlockSpec(buf.shape, lambda i: (0, 0, 0))
    return pl.pallas_call(
        _conv_decode_kernel,
        grid=(1,),
        in_specs=[full, bufs, _const_spec((1, D_MODEL)), _const_spec((D_MODEL, 3 * D_MODEL)),
                  _const_spec((CONV_W, D_MODEL)), _const_spec((D_MODEL, D_MODEL))],
        out_specs=[full, bufs],
        out_shape=[jax.ShapeDtypeStruct((m, D_MODEL), F32), jax.ShapeDtypeStruct(buf.shape, F32)],
        compiler_params=_params(("arbitrary",)),
        name="conv_decode",
    )(h, buf, nm.reshape(1, D_MODEL), p["w_in"].astype(BF16), p["w"], p["wo"].astype(BF16))


def _pool_project(d_groups, pw_ref, sc_ref):
    ys = [_bdot(d, pw_ref[gi]) for gi, d in enumerate(d_groups)]
    return jnp.concatenate(ys, axis=1) * sc_ref[...]


def _pool_prompt_kernel(h_ref, nm_ref, pw_ref, sc_ref, nf_ref, wu_ref, wd_ref, gf_ref, o_ref, st_ref, carry_ref):
    tm = h_ref.shape[0]
    hist = carry_ref.shape[0]
    j = pl.program_id(1)

    @pl.when(j == 0)
    def _():
        carry_ref[...] = jnp.zeros_like(carry_ref)

    u = _rms(h_ref[...], nm_ref[...], NORM_EPS)
    ext = jnp.concatenate([carry_ref[...], u], axis=0)
    pos = j * tm + _iota((tm, 1), 0)
    d_groups = []
    for gi, w in enumerate(POOL_WINDOWS):
        sl = slice(gi * POOL_G, (gi + 1) * POOL_G)
        s = ext[:, sl]
        span = 1
        while span < w:
            s = s + pltpu.roll(s, span, axis=0)
            span *= 2
        cnt = jnp.minimum(w, pos + 1).astype(F32)
        d_groups.append(s[hist:, :] / cnt - u[:, sl])
    o_ref[...] = _mlp(h_ref[...] + _pool_project(d_groups, pw_ref, sc_ref), nf_ref, wu_ref, wd_ref, gf_ref)
    tail = u[tm - hist:, :]
    carry_ref[...] = tail
    st_ref[...] = tail


def _pool_prompt_ffn(h, nm, p, mlp, final_g, tm=512):
    b, l, _ = h.shape
    tm = _row_tile(l, tm)
    hist = POOL_BUF + 1
    tile = pl.BlockSpec((None, tm, D_MODEL), lambda i, j: (i, j, 0))
    mlp_ins, mlp_specs = _mlp_operands(mlp, final_g)
    out, st = pl.pallas_call(
        _pool_prompt_kernel,
        grid=(b, l // tm),
        in_specs=[tile, _const_spec((1, D_MODEL)), _const_spec(p["w"].shape), _const_spec((1, D_MODEL))]
        + mlp_specs,
        out_specs=[tile, pl.BlockSpec((None, hist, D_MODEL), lambda i, j: (i, 0, 0))],
        out_shape=[jax.ShapeDtypeStruct((b, l, D_MODEL), F32), jax.ShapeDtypeStruct((b, hist, D_MODEL), F32)],
        scratch_shapes=[pltpu.VMEM((hist, D_MODEL), F32)],
        compiler_params=_params(("parallel", "arbitrary")),
        name="pool_prompt_ffn",
    )(h, nm.reshape(1, D_MODEL), p["w"].astype(BF16), p["scale"].reshape(1, D_MODEL), *mlp_ins)
    return out, st[:, 1:]


def _pool_decode_kernel(h_ref, buf_ref, nm_ref, pw_ref, sc_ref, o_ref, st_ref):
    u = _rms(h_ref[...], nm_ref[...], NORM_EPS)
    d_groups = []
    for gi, w in enumerate(POOL_WINDOWS):
        sl = slice(gi * POOL_G, (gi + 1) * POOL_G)
        s = u[:, sl]
        for i in range(1, w):
            s = s + buf_ref[POOL_BUF - i, :, sl]
        cnt = float(min(w, PAST_LEN + 1))
        d_groups.append(s / cnt - u[:, sl])
    o_ref[...] = h_ref[...] + _pool_project(d_groups, pw_ref, sc_ref)
    st_ref[0:POOL_BUF - 1] = buf_ref[1:POOL_BUF]
    st_ref[POOL_BUF - 1] = u


def _pool_decode(h, buf, nm, p):
    m = h.shape[0]
    full = pl.BlockSpec((m, D_MODEL), lambda i: (0, 0))
    bufs = pl.BlockSpec(buf.shape, lambda i: (0, 0, 0))
    return pl.pallas_call(
        _pool_decode_kernel,
        grid=(1,),
        in_specs=[full, bufs, _const_spec((1, D_MODEL)), _const_spec(p["w"].shape), _const_spec((1, D_MODEL))],
        out_specs=[full, bufs],
        out_shape=[jax.ShapeDtypeStruct((m, D_MODEL), F32), jax.ShapeDtypeStruct(buf.shape, F32)],
        compiler_params=_params(("arbitrary",)),
        name="pool_decode",
    )(h, buf, nm.reshape(1, D_MODEL), p["w"].astype(BF16), p["scale"].reshape(1, D_MODEL))


def _trunk_prompt(x, nm, nfin, mlps, rw, gl, cv, po):
    b, l, _ = x.shape
    flat = lambda t: t.reshape(b * l, t.shape[-1])
    seq = lambda t: t.reshape(b, l, t.shape[-1])

    (r, lw, k, v, kk, bb, g, bonus), shift = _rwkv_pre_prompt(x, nm[0], rw)
    y, wkv = _rwkv_rec_prompt(r, lw, k, v, kk, bb)
    h = _rwkv_post_ffn(flat(x), flat(y), flat(bonus), flat(g), rw, mlps[0])

    q, k, v, g, gk = _gla_pre(h, nm[1], gl)
    o, gla_s = _gla_rec_prompt(seq(q), seq(k), seq(v), seq(gk))
    h = _gla_post_ffn(h, flat(o), g, gl, mlps[1])

    h, conv_s = _conv_prompt(seq(h), nm[2], cv)
    h = _ffn(flat(h), mlps[2])

    y, pool_s = _pool_prompt_ffn(seq(h), nm[3], po, mlps[3], nfin)
    return y, wkv[None], shift[None], gla_s[None], conv_s[None], pool_s[None]


def _trunk_decode(x, st_wkv, st_shift, st_gla, st_conv, st_pool, nm, nf, nfin, w_up, w_down, rw, gl, cv, po):
    m = x.shape[0]
    h = x.reshape(m, D_MODEL)
    mlps = []

    def mlp(h, i, final_g=None):
        h, wub, wdb = _ffn_cast(h, nf[i], w_up, w_down, i, final_g)
        mlps.append((nf[i], wub, wdb))
        return h

    (r, lw, k, v, kk, bb, g, bonus), shift = _rwkv_pre_decode(h, st_shift[0], nm[0], rw)
    y_t, wkv_t = _rwkv_rec_decode(r, lw, k, v, kk, bb, jnp.transpose(st_wkv[0], (1, 2, 3, 0)))
    wkv = jnp.transpose(wkv_t, (3, 0, 1, 2))
    h = mlp(_rwkv_post_ffn(h, y_t.T, bonus, g, rw), 0)

    q, k, v, g, gk = _gla_pre(h, nm[1], gl)
    o, gla_s = _gla_rec_decode(q, k, v, gk, st_gla[0])
    h = mlp(_gla_post_ffn(h, o, g, gl), 1)

    h, conv_s = _conv_decode(h, st_conv[0], nm[2], cv)
    h = mlp(h, 2)

    h, pool_t = _pool_decode(h, jnp.transpose(st_pool[0], (1, 0, 2)), nm[3], po)
    pool_s = jnp.transpose(pool_t, (1, 0, 2))
    y = mlp(h, 3, final_g=nfin)
    return (y.reshape(m, 1, D_MODEL), wkv[None], shift[None], gla_s[None], conv_s[None], pool_s[None]), mlps


def kernel(x_prompt, x_sample, state_rwkv_wkv, state_rwkv_shift, state_gla, state_conv, state_pool, norm_mix, norm_ffn, norm_final, ffn_up, ffn_down, rwkv_mu, rwkv_w_rkv, rwkv_w0, rwkv_w1, rwkv_w2, rwkv_a0, rwkv_a1, rwkv_a2, rwkv_g1, rwkv_g2, rwkv_k_k, rwkv_k_a, rwkv_r_k, rwkv_ln_w, rwkv_ln_b, rwkv_wo, gla_w_in, gla_w_gk2, gla_b_gk, gla_norm, gla_wo, conv_w_in, conv_w, conv_wo, pool_w, pool_scale):
    assert x_prompt.shape[1] % CHUNK == 0 and x_sample.shape[1] == 1
    rw = dict(mu=rwkv_mu[0], w_rkv=rwkv_w_rkv[0], w0=rwkv_w0[0], w1=rwkv_w1[0], w2=rwkv_w2[0], a0=rwkv_a0[0],
              a1=rwkv_a1[0], a2=rwkv_a2[0], g1=rwkv_g1[0], g2=rwkv_g2[0], k_k=rwkv_k_k[0], k_a=rwkv_k_a[0],
              r_k=rwkv_r_k[0], ln_w=rwkv_ln_w[0], ln_b=rwkv_ln_b[0], wo=rwkv_wo[0])
    gl = dict(w_in=gla_w_in[0], w_gk2=gla_w_gk2[0], b_gk=gla_b_gk[0], norm=gla_norm[0], wo=gla_wo[0])
    cv = dict(w_in=conv_w_in[0], w=conv_w[0], wo=conv_wo[0])
    po = dict(w=pool_w[0], scale=pool_scale[0])
    (y_s, wkv_s, sh_s, gla_s, conv_s, pool_s), mlps = _trunk_decode(
        x_sample, state_rwkv_wkv, state_rwkv_shift, state_gla, state_conv, state_pool,
        norm_mix, norm_ffn, norm_final, ffn_up, ffn_down, rw, gl, cv, po)
    y_p, wkv_p, sh_p, gla_p, conv_p, pool_p = _trunk_prompt(x_prompt, norm_mix, norm_final, mlps, rw, gl, cv, po)
    return (y_p, y_s, wkv_p, wkv_s, sh_p, sh_s, gla_p, gla_s, conv_p, conv_s, pool_p, pool_s)
```

```python
import functools
import math

import jax
import jax.numpy as jnp
from jax import lax
from jax.experimental import pallas as pl
from jax.experimental.pallas import tpu as pltpu

F32 = jnp.float32
BF16 = jnp.bfloat16

D_MODEL = 1024
D_FF = 4 * D_MODEL
NORM_EPS = 1e-6
PAST_LEN = 16384

RWKV_HEAD = 64
RWKV_H = D_MODEL // RWKV_HEAD
RWKV_GN_EPS = 64e-5
RWKV_QUAD = 4

GLA_H = 4
GLA_DK_TOT = D_MODEL // 2
GLA_DV_TOT = D_MODEL
GLA_DK = GLA_DK_TOT // GLA_H
GLA_DV = GLA_DV_TOT // GLA_H
GLA_LR = 16
GLA_NORMALIZER = 16.0
GLA_NORM_EPS = 1e-5
GLA_SUB = 16

CONV_W = 3
POOL_WINDOWS = (2, 4, 8, 16)
POOL_G = D_MODEL // len(POOL_WINDOWS)
POOL_BUF = max(POOL_WINDOWS) - 1

LANES = 128
SUBLANES = 8
MXU_DIM = 256
VMEM_LIMIT_BYTES = 56 * 1024 * 1024

CHUNK = 64
FFN_COLS = 1024
SUBTILES = 2


def _const_spec(shape):
    nd = len(shape)
    return pl.BlockSpec(shape, lambda *_: (0,) * nd, pipeline_mode=pl.Buffered(1))


def _params(sem):
    return pltpu.CompilerParams(dimension_semantics=sem, vmem_limit_bytes=VMEM_LIMIT_BYTES)


def _rms(x, g, eps):
    return x * lax.rsqrt(jnp.mean(x * x, axis=-1, keepdims=True) + eps) * g


def _bdot(a, w):
    return jnp.dot(a.astype(BF16), w, preferred_element_type=F32)


def _mm(a, b, prec=None):
    return jnp.dot(a, b, preferred_element_type=F32, precision=prec)


def _nt(a, b, prec=None):
    return lax.dot_general(a, b, (((1,), (1,)), ((), ())), preferred_element_type=F32, precision=prec)


def _tn(a, b, prec=None):
    return lax.dot_general(a, b, (((0,), (0,)), ((), ())), preferred_element_type=F32, precision=prec)


def _iota(shape, dim):
    return lax.broadcasted_iota(jnp.int32, shape, dim)


def _group_sum(x, group, split=False):
    shift = group.bit_length() - 1
    bd = (jnp.right_shift(_iota((MXU_DIM, MXU_DIM), 0), shift)
          == jnp.right_shift(_iota((MXU_DIM, MXU_DIM), 1), shift)).astype(BF16)
    terms = [x.astype(BF16)]
    if split:
        terms.append((x - terms[0].astype(F32)).astype(BF16))
    cols = [sum(_mm(t[:, c * MXU_DIM:(c + 1) * MXU_DIM], bd) for t in terms)
            for c in range(x.shape[1] // MXU_DIM)]
    return jnp.concatenate(cols, axis=1)


def _chunk_cumsum(x):
    c = x.shape[0]
    ltri = (_iota((c, c), 0) >= _iota((c, c), 1)).astype(BF16)
    t0 = x.astype(BF16)
    r1 = x - t0.astype(F32)
    t1 = r1.astype(BF16)
    t2 = (r1 - t1.astype(F32)).astype(BF16)
    return _mm(ltri, t0) + _mm(ltri, t1) + _mm(ltri, t2)


def _softplus(x):
    return jnp.maximum(x, 0.0) + jnp.log(1.0 + jnp.exp(-jnp.abs(x)))


def _row_tile(m, want):
    t = min(want, m)
    assert m % t == 0, (m, t)
    return t


def _mlp(h, g_ref, wu_ref, wd_ref, gf_ref=None):
    x = _rms(h, g_ref[...], NORM_EPS).astype(BF16)
    acc = h
    for c in range(D_FF // FFN_COLS):
        sl = slice(c * FFN_COLS, (c + 1) * FFN_COLS)
        a = jnp.dot(x, wu_ref[:, sl], preferred_element_type=F32)
        a = jnp.square(jnp.maximum(a, 0.0)).astype(BF16)
        acc = acc + jnp.dot(a, wd_ref[sl, :], preferred_element_type=F32)
    if gf_ref is not None:
        acc = _rms(acc, gf_ref[...], NORM_EPS)
    return acc


def _mlp_operands(mlp, final_g=None):
    nf, wu, wd = mlp
    ins = [nf.reshape(1, D_MODEL), wu, wd]
    specs = [_const_spec((1, D_MODEL)), _const_spec(wu.shape), _const_spec(wd.shape)]
    if final_g is not None:
        ins.append(final_g.reshape(1, D_MODEL))
        specs.append(_const_spec((1, D_MODEL)))
    return ins, specs


def _ffn_kernel(h_ref, g_ref, wu_ref, wd_ref, *rest):
    *gf_ref, o_ref = rest
    o_ref[...] = _mlp(h_ref[...], g_ref, wu_ref, wd_ref, *gf_ref)


def _ffn(h, mlp, final_g=None, tm=1024):
    m = h.shape[0]
    tm = _row_tile(m, tm)
    row = pl.BlockSpec((tm, D_MODEL), lambda i: (i, 0))
    mlp_ins, mlp_specs = _mlp_operands(mlp, final_g)
    return pl.pallas_call(
        _ffn_kernel,
        grid=(m // tm,),
        in_specs=[row] + mlp_specs,
        out_specs=row,
        out_shape=jax.ShapeDtypeStruct((m, D_MODEL), F32),
        compiler_params=_params(("parallel",)),
        name="ffn",
    )(h, *mlp_ins)


def _ffn_cast_kernel(h_ref, g_ref, wu_ref, wd_ref, *rest):
    *gf_ref, o_ref, wub_ref, wdb_ref, x_scr, acc_scr = rest
    c = pl.program_id(0)

    @pl.when(c == 0)
    def _():
        h = h_ref[...]
        x_scr[...] = _rms(h, g_ref[...], NORM_EPS).astype(BF16)
        acc_scr[...] = h

    wu = wu_ref[...].astype(BF16)
    wd = wd_ref[...].astype(BF16)
    wub_ref[...] = wu
    wdb_ref[...] = wd
    a = jnp.dot(x_scr[...], wu, preferred_element_type=F32)
    a = jnp.square(jnp.maximum(a, 0.0)).astype(BF16)
    acc_scr[...] += jnp.dot(a, wd, preferred_element_type=F32)

    @pl.when(c == pl.num_programs(0) - 1)
    def _():
        acc = acc_scr[...]
        o_ref[...] = _rms(acc, gf_ref[0][...], NORM_EPS) if gf_ref else acc


def _ffn_cast(h, nf, w_up, w_down, layer, final_g=None, cols=FFN_COLS):
    m = h.shape[0]
    full = pl.BlockSpec((m, D_MODEL), lambda c: (0, 0))
    vec = pl.BlockSpec((1, D_MODEL), lambda c: (0, 0))
    ins = [h, nf.reshape(1, D_MODEL), w_up, w_down]
    specs = [full, vec, pl.BlockSpec((None, D_MODEL, cols), lambda c: (layer, 0, c)),
             pl.BlockSpec((None, cols, D_MODEL), lambda c: (layer, c, 0))]
    if final_g is not None:
        ins.append(final_g.reshape(1, D_MODEL))
        specs.append(vec)
    return pl.pallas_call(
        _ffn_cast_kernel,
        grid=(D_FF // cols,),
        in_specs=specs,
        out_specs=[full, pl.BlockSpec((D_MODEL, cols), lambda c: (0, c)),
                   pl.BlockSpec((cols, D_MODEL), lambda c: (c, 0))],
        out_shape=[jax.ShapeDtypeStruct((m, D_MODEL), F32), jax.ShapeDtypeStruct((D_MODEL, D_FF), BF16),
                   jax.ShapeDtypeStruct((D_FF, D_MODEL), BF16)],
        scratch_shapes=[pltpu.VMEM((m, D_MODEL), BF16), pltpu.VMEM((m, D_MODEL), F32)],
        compiler_params=_params(("arbitrary",)),
        name="ffn_cast",
    )(*ins)


def _rwkv_pre_math(u, prev, mu_ref, wrkv_ref, w0_ref, w1_ref, w2_ref, a0_ref, a1_ref, a2_ref,
                   g1_ref, g2_ref, kk_ref, ka_ref, rk_ref):
    dx = prev - u

    def mix(i):
        return (u + dx * mu_ref[i:i + 1, :]).astype(BF16)

    r = _bdot(mix(0), wrkv_ref[0])
    w_mid = _bdot(mix(1), w1_ref[...])
    a_mid = _bdot(mix(4), a1_ref[...])
    g_mid = _bdot(mix(5), g1_ref[...])
    k = _bdot(mix(2), wrkv_ref[1])
    wl = w0_ref[...] + _bdot(jnp.tanh(w_mid), w2_ref[...])
    a = jax.nn.sigmoid(a0_ref[...] + _bdot(a_mid, a2_ref[...]))
    g = _bdot(jax.nn.sigmoid(g_mid), g2_ref[...])
    lw = jax.nn.sigmoid(wl) * (-math.exp(-0.5))
    kk = k * kk_ref[...]
    kk_ss = _group_sum(kk * kk, RWKV_HEAD)
    v = _bdot(mix(3), wrkv_ref[2])
    k = k * (1.0 + (a - 1.0) * ka_ref[...])
    rk_sum = _group_sum(r * k * rk_ref[...], RWKV_HEAD)
    kk = kk * lax.rsqrt(jnp.maximum(kk_ss, 1e-24))
    return r, lw, k, v, kk, kk * a, g, rk_sum * v


_N_RWKV_W = 13


def _rwkv_pre_prompt_kernel(h_ref, nm_ref, *rest):
    w = rest[:_N_RWKV_W]
    outs = rest[_N_RWKV_W:_N_RWKV_W + 8]
    sh_ref, carry_ref = rest[_N_RWKV_W + 8:]
    tm = h_ref.shape[0]

    @pl.when(pl.program_id(1) == 0)
    def _():
        carry_ref[...] = jnp.zeros_like(carry_ref)

    u = _rms(h_ref[...], nm_ref[...], NORM_EPS)
    prev = jnp.where(_iota(u.shape, 0) == 0, carry_ref[0:1, :], pltpu.roll(u, 1, axis=0))
    last = u[tm - 1:tm, :]
    carry_ref[0:1, :] = last
    sh_ref[...] = last
    for o_ref, val in zip(outs, _rwkv_pre_math(u, prev, *w)):
        o_ref[...] = val


def _rwkv_pre_decode_kernel(h_ref, prev_ref, nm_ref, *rest):
    w = rest[:_N_RWKV_W]
    outs = rest[_N_RWKV_W:_N_RWKV_W + 8]
    (sh_ref,) = rest[_N_RWKV_W + 8:]
    u = _rms(h_ref[...], nm_ref[...], NORM_EPS)
    sh_ref[...] = u
    vals = _rwkv_pre_math(u, prev_ref[...], *w)
    for o_ref, val in zip(outs[:6], vals[:6]):
        o_ref[...] = val.T
    for o_ref, val in zip(outs[6:], vals[6:]):
        o_ref[...] = val


def _rwkv_weights(p):
    row = lambda x: x.reshape(1, D_MODEL)
    return [p["mu"], p["w_rkv"].astype(BF16), row(p["w0"]), p["w1"].astype(BF16), p["w2"].astype(BF16),
            row(p["a0"]), p["a1"].astype(BF16), p["a2"].astype(BF16), p["g1"].astype(BF16),
            p["g2"].astype(BF16), row(p["k_k"]), row(p["k_a"]), row(p["r_k"])]


def _rwkv_pre_prompt(h, nm, p, tm=512):
    b, l, _ = h.shape
    tm = _row_tile(l, tm)
    ws = _rwkv_weights(p)
    tile = pl.BlockSpec((None, tm, D_MODEL), lambda i, j: (i, j, 0))
    outs = pl.pallas_call(
        _rwkv_pre_prompt_kernel,
        grid=(b, l // tm),
        in_specs=[tile, _const_spec((1, D_MODEL))] + [_const_spec(w.shape) for w in ws],
        out_specs=[tile] * 8 + [pl.BlockSpec((None, 1, D_MODEL), lambda i, j: (i, 0, 0))],
        out_shape=[jax.ShapeDtypeStruct((b, l, D_MODEL), F32)] * 8
        + [jax.ShapeDtypeStruct((b, 1, D_MODEL), F32)],
        scratch_shapes=[pltpu.VMEM((SUBLANES, D_MODEL), F32)],
        compiler_params=_params(("parallel", "arbitrary")),
        name="rwkv_pre_prompt",
    )(h, nm.reshape(1, D_MODEL), *ws)
    return outs[:8], outs[8].reshape(b, D_MODEL)


def _rwkv_pre_decode(h, prev, nm, p):
    m = h.shape[0]
    ws = _rwkv_weights(p)
    full = pl.BlockSpec((m, D_MODEL), lambda i: (0, 0))
    full_t = pl.BlockSpec((D_MODEL, m), lambda i: (0, 0))
    outs = pl.pallas_call(
        _rwkv_pre_decode_kernel,
        grid=(1,),
        in_specs=[full, full, _const_spec((1, D_MODEL))] + [_const_spec(w.shape) for w in ws],
        out_specs=[full_t] * 6 + [full] * 3,
        out_shape=[jax.ShapeDtypeStruct((D_MODEL, m), F32)] * 6 + [jax.ShapeDtypeStruct((m, D_MODEL), F32)] * 3,
        compiler_params=_params(("arbitrary",)),
        name="rwkv_pre_decode",
    )(h, prev, nm.reshape(1, D_MODEL), *ws)
    return outs[:8], outs[8]


def _rwkv_rec_prompt_kernel(r_ref, lw_ref, k_ref, v_ref, kk_ref, bb_ref, y_ref, so_ref, s_scr):
    tc = r_ref.shape[0]
    qw = RWKV_QUAD * RWKV_HEAD
    n_quads = D_MODEL // qw
    assert RWKV_QUAD * CHUNK == qw

    @pl.when(pl.program_id(1) == 0)
    def _():
        s_scr[...] = jnp.zeros_like(s_scr)

    t_idx, s_idx = _iota((CHUNK, qw), 0), _iota((CHUNK, qw), 1) & (CHUNK - 1)
    strict = t_idx > s_idx
    incl = t_idx >= s_idx
    eye = (t_idx == s_idx).astype(F32)
    blocks = jnp.right_shift(_iota((qw, qw), 0), 6) == jnp.right_shift(_iota((qw, qw), 1), 6)

    def bdiag(z):
        zb = z.astype(BF16)
        return jnp.where(blocks, jnp.concatenate([zb] * RWKV_QUAD, axis=0), jnp.zeros((), BF16))

    quads = range(n_quads)
    sls = [slice(q * qw, (q + 1) * qw) for q in quads]
    cc = CHUNK

    n_chunks = tc // CHUNK
    items = [(c, q) for c in range(n_chunks) for q in quads]
    every = range(len(items))

    scaled = []
    for c in range(n_chunks):
        rows = slice(c * CHUNK, (c + 1) * CHUNK)
        lw = lw_ref[rows, :]
        cum = _chunk_cumsum(lw)
        g_in = jnp.exp(cum)
        g_inv = jnp.exp(-cum)
        g_last = g_in[CHUNK - 1:CHUNK, :]
        scaled.append(dict(
            rt=r_ref[rows, :] * g_in, at=-kk_ref[rows, :] * jnp.exp(cum - lw),
            bt=bb_ref[rows, :] * g_inv, kt=k_ref[rows, :] * g_inv, v=v_ref[rows, :], g_last=g_last))

    def part(name, i):
        c, q = items[i]
        return scaled[c][name][:, sls[q]]

    ar = [jnp.concatenate([part("at", i), part("rt", i)], axis=0).astype(BF16) for i in every]
    bk4 = [jnp.concatenate([bdiag(part("bt", i)), bdiag(part("kt", i))], axis=0) for i in every]
    gram = [_nt(ar[i], bk4[i]) for i in every]
    a_ab = [jnp.where(strict, gram[i][:cc, :qw], 0.0) for i in every]
    a_k = [jnp.concatenate([jnp.where(strict, gram[i][:cc, qw:], 0.0),
                            jnp.where(incl, gram[i][cc:, qw:], 0.0)], axis=0).astype(BF16) for i in every]
    a_rb = [jnp.where(incl, gram[i][cc:, :qw], 0.0).astype(BF16) for i in every]
    akv = [_mm(a_k[i], bdiag(part("v", i))) for i in every]
    inv = [eye + a_ab[i] for i in every]
    pw = [_mm(a_ab[i].astype(BF16), bdiag(a_ab[i])) for i in every]
    for _ in range(1, CHUNK.bit_length() - 2):
        both = [_mm(jnp.concatenate([pw[i], inv[i]], axis=0).astype(BF16), bdiag(pw[i])) for i in every]
        inv = [inv[i] + both[i][cc:] for i in every]
        pw = [both[i][:cc] for i in every]
    inv = [(inv[i] + _mm(inv[i].astype(BF16), bdiag(pw[i]))).astype(BF16) for i in every]
    tt = [_mm(inv[i], jnp.concatenate([bdiag(part("at", i)), bdiag(akv[i][:cc])], axis=1)) for i in every]
    ta = [tt[i][:, :qw].astype(BF16) for i in every]
    tav = [tt[i][:, qw:] for i in every]
    bkg = [(jnp.concatenate([part("bt", i), part("kt", i)], axis=0) * part("g_last", i)).astype(BF16)
           for i in every]
    v_b = [part("v", i).astype(BF16) for i in every]

    xs, u = [], []

    def emit_outputs(c):
        for i in range(c * n_quads, (c + 1) * n_quads):
            y_ref[c * CHUNK:(c + 1) * CHUNK, sls[items[i][1]]] = xs[i][cc:] + akv[i][cc:] + _mm(a_rb[i], bdiag(u[i]))

    state = [s_scr[q] for q in quads]
    for c in range(n_chunks):
        mine = [c * n_quads + q for q in quads]
        xs += [_nt(jnp.concatenate([ta[i], ar[i][cc:]], axis=0), state[q].astype(BF16))
               for q, i in enumerate(mine)]
        u += [xs[i][:cc] + tav[i] for i in mine]
        state = [state[q] * scaled[c]["g_last"][:, sls[q]]
                 + jnp.where(blocks, _tn(jnp.concatenate([u[i].astype(BF16), v_b[i]], axis=0), bkg[i]), 0.0)
                 for q, i in enumerate(mine)]
        if c > 0:
            emit_outputs(c - 1)
    emit_outputs(n_chunks - 1)
    for q in quads:
        s_scr[q] = state[q]

    @pl.when(pl.program_id(1) == pl.num_programs(1) - 1)
    def _():
        for hd in range(RWKV_H):
            q, o = divmod(hd, RWKV_QUAD)
            so_ref[hd] = s_scr[q][o * RWKV_HEAD:(o + 1) * RWKV_HEAD, o * RWKV_HEAD:(o + 1) * RWKV_HEAD]


def _rwkv_rec_prompt(r, lw, k, v, kk, bb, tc=512):
    b, l, _ = r.shape
    tc = _row_tile(l, tc)
    tile = pl.BlockSpec((None, tc, D_MODEL), lambda i, j: (i, j, 0))
    st = (RWKV_H, RWKV_HEAD, RWKV_HEAD)
    return pl.pallas_call(
        _rwkv_rec_prompt_kernel,
        grid=(b, l // tc),
        in_specs=[tile] * 6,
        out_specs=[tile, pl.BlockSpec((None,) + st, lambda i, j: (i, 0, 0, 0))],
        out_shape=[jax.ShapeDtypeStruct((b, l, D_MODEL), F32), jax.ShapeDtypeStruct((b,) + st, F32)],
        scratch_shapes=[pltpu.VMEM((RWKV_H // RWKV_QUAD, RWKV_QUAD * RWKV_HEAD, RWKV_QUAD * RWKV_HEAD), F32)],
        compiler_params=_params(("parallel", "arbitrary")),
        name="rwkv_rec_prompt",
    )(r, lw, k, v, kk, bb)


def _rwkv_rec_decode_kernel(r_ref, lw_ref, k_ref, v_ref, kk_ref, bb_ref, s_ref, y_ref, so_ref):
    w = jnp.exp(lw_ref[...])
    a = -kk_ref[...]
    b, k, r = bb_ref[...], k_ref[...], r_ref[...]

    def row(vi, carry):
        s0 = s_ref[vi]
        sa = jnp.sum(s0 * a, axis=0, keepdims=True)
        s1 = s0 * w + sa * b + v_ref[pl.ds(vi, 1), :] * k
        so_ref[vi] = s1
        y_ref[pl.ds(vi, 1), :] = jnp.sum(s1 * r, axis=0, keepdims=True)
        return carry

    lax.fori_loop(0, RWKV_HEAD, row, 0, unroll=8)


def _rwkv_rec_decode(r, lw, k, v, kk, bb, s):
    m = r.shape[1]
    vec = pl.BlockSpec((RWKV_HEAD, m), lambda i: (i, 0))
    st = pl.BlockSpec((None, RWKV_HEAD, RWKV_HEAD, m), lambda i: (i, 0, 0, 0))
    return pl.pallas_call(
        _rwkv_rec_decode_kernel,
        grid=(RWKV_H,),
        in_specs=[vec] * 6 + [st],
        out_specs=[vec, st],
        out_shape=[jax.ShapeDtypeStruct((D_MODEL, m), F32), jax.ShapeDtypeStruct(s.shape, F32)],
        compiler_params=_params(("parallel",)),
        name="rwkv_rec_decode",
    )(r, lw, k, v, kk, bb, s)


def _rwkv_post_kernel(h_ref, y_ref, bo_ref, g_ref, lnw_ref, lnb_ref, wo_ref, *rest):
    *mlp_refs, o_ref = rest
    y = y_ref[...]
    d = y - _group_sum(y, RWKV_HEAD, split=True) * (1.0 / RWKV_HEAD)
    var = _group_sum(d * d, RWKV_HEAD) * (1.0 / RWKV_HEAD)
    yn = d * lax.rsqrt(var + RWKV_GN_EPS) * lnw_ref[...] + lnb_ref[...]
    h = h_ref[...] + _bdot((yn + bo_ref[...]) * g_ref[...], wo_ref[...])
    o_ref[...] = _mlp(h, *mlp_refs) if mlp_refs else h


def _rwkv_post_ffn(h, y, bonus, g, p, mlp=None, tm=512):
    m = h.shape[0]
    tm = _row_tile(m, tm)
    row = pl.BlockSpec((tm, D_MODEL), lambda i: (i, 0))
    vec = _const_spec((1, D_MODEL))
    mlp_ins, mlp_specs = _mlp_operands(mlp) if mlp else ([], [])
    return pl.pallas_call(
        _rwkv_post_kernel,
        grid=(m // tm,),
        in_specs=[row] * 4 + [vec, vec, _const_spec((D_MODEL, D_MODEL))] + mlp_specs,
        out_specs=row,
        out_shape=jax.ShapeDtypeStruct((m, D_MODEL), F32),
        compiler_params=_params(("parallel",)),
        name="rwkv_post_ffn",
    )(h, y, bonus, g, p["ln_w"].reshape(1, D_MODEL), p["ln_b"].reshape(1, D_MODEL), p["wo"].astype(BF16),
      *mlp_ins)


def _gla_pre_kernel(h_ref, nm_ref, win_ref, wgl_ref, wgk_ref, bgk_ref, q_ref, k_ref, v_ref, g_ref, gk_ref):
    u = _rms(h_ref[...], nm_ref[...], NORM_EPS).astype(BF16)
    z = _nt(u, win_ref[...])
    q_ref[...] = z[:, :GLA_DK_TOT] * (GLA_DK ** -0.5)
    k_ref[...] = z[:, GLA_DK_TOT:2 * GLA_DK_TOT]
    v_ref[...] = z[:, 2 * GLA_DK_TOT:2 * GLA_DK_TOT + GLA_DV_TOT]
    g_ref[...] = z[:, 2 * GLA_DK_TOT + GLA_DV_TOT:]
    gl = _nt(u, wgl_ref[...])
    pre = _bdot(gl, wgk_ref[...]) + bgk_ref[...]
    gk_ref[...] = -_softplus(-pre) * (1.0 / GLA_NORMALIZER)


def _gla_pre(h, nm, p, tm=1024):
    m = h.shape[0]
    tm = _row_tile(m, tm)
    n_main = 2 * GLA_DK_TOT + 2 * GLA_DV_TOT
    w_t = p["w_in"].T
    w_main = w_t[:n_main].astype(BF16)
    w_gl = jnp.pad(w_t[n_main:].astype(BF16), ((0, LANES - GLA_LR), (0, 0)))
    w_gk = jnp.pad(p["w_gk2"].astype(BF16), ((0, LANES - GLA_LR), (0, 0)))
    row = lambda n: pl.BlockSpec((tm, n), lambda i: (i, 0))
    widths = (GLA_DK_TOT, GLA_DK_TOT, GLA_DV_TOT, GLA_DV_TOT, GLA_DK_TOT)
    return pl.pallas_call(
        _gla_pre_kernel,
        grid=(m // tm,),
        in_specs=[row(D_MODEL), _const_spec((1, D_MODEL)), _const_spec(w_main.shape), _const_spec(w_gl.shape),
                  _const_spec(w_gk.shape), _const_spec((1, GLA_DK_TOT))],
        out_specs=[row(n) for n in widths],
        out_shape=[jax.ShapeDtypeStruct((m, n), F32) for n in widths],
        compiler_params=_params(("parallel",)),
        name="gla_pre",
    )(h, nm.reshape(1, D_MODEL), w_main, w_gl, w_gk, p["b_gk"].reshape(1, GLA_DK_TOT))


def _gla_rec_prompt_kernel(q_ref, k_ref, v_ref, gk_ref, o_ref, so_ref, s_scr):
    tc = q_ref.shape[0]

    @pl.when(pl.program_id(1) == 0)
    def _():
        s_scr[...] = jnp.zeros_like(s_scr)

    ri, ci = _iota((CHUNK, CHUNK), 0), _iota((CHUNK, CHUNK), 1)
    incl = ri >= ci
    n_sub = CHUNK // GLA_SUB
    sub_of_row = jnp.right_shift(_iota((CHUNK, 1), 0), GLA_SUB.bit_length() - 1)

    heads = range(GLA_H)
    ks = [slice(hd * GLA_DK, (hd + 1) * GLA_DK) for hd in heads]
    vs = [slice(hd * GLA_DV, (hd + 1) * GLA_DV) for hd in heads]
    n_chunks = tc // CHUNK
    items = [(c, hd) for c in range(n_chunks) for hd in heads]

    prep = []
    for c in range(n_chunks):
        rows = slice(c * CHUNK, (c + 1) * CHUNK)
        cum = _chunk_cumsum(gk_ref[rows, :])
        q, k = q_ref[rows, :], k_ref[rows, :]
        last = cum[CHUNK - 1:CHUNK, :]
        refs = [jnp.zeros_like(last)] + [cum[i * GLA_SUB - 1:i * GLA_SUB, :] for i in range(1, n_sub)]
        ref_of_row = refs[0]
        for i in range(1, n_sub):
            ref_of_row = jnp.where(sub_of_row >= i, refs[i], ref_of_row)
        q_in = (q * jnp.exp(cum - ref_of_row)).astype(BF16)
        k_in = []
        for i in range(n_sub):
            n = (i + 1) * GLA_SUB
            part = (k[:n] * jnp.exp(refs[i] - cum[:n])).astype(BF16)
            k_in.append(part if n == CHUNK else
                        jnp.concatenate([part, jnp.zeros((CHUNK - n, part.shape[1]), BF16)], axis=0))
        prep.append(dict(q_in=q_in, k_in=k_in, qe=(q * jnp.exp(cum)).astype(BF16),
                         kh=(k * jnp.exp(last - cum)).astype(BF16), g_last=jnp.exp(last),
                         v=v_ref[rows, :].astype(BF16)))

    def scores(c, hd):
        p = prep[c]
        blocks = [_nt(p["q_in"][i * GLA_SUB:(i + 1) * GLA_SUB, ks[hd]], p["k_in"][i][:, ks[hd]])
                  for i in range(n_sub)]
        return jnp.where(incl, jnp.concatenate(blocks, axis=0), 0.0).astype(BF16)

    att = [scores(c, hd) for c, hd in items]
    intra = [_mm(att[i], prep[c]["v"][:, vs[hd]]) for i, (c, hd) in enumerate(items)]
    kv = [_tn(prep[c]["v"][:, vs[hd]], prep[c]["kh"][:, ks[hd]]) for c, hd in items]

    s_in = []
    state = [s_scr[hd] for hd in heads]
    for c in range(n_chunks):
        s_in.extend(state[hd].astype(BF16) for hd in heads)
        state = [state[hd] * prep[c]["g_last"][:, ks[hd]] + kv[c * GLA_H + hd] for hd in heads]
    for hd in heads:
        s_scr[hd] = state[hd]

    for i, (c, hd) in enumerate(items):
        o_ref[c * CHUNK:(c + 1) * CHUNK, vs[hd]] = _nt(prep[c]["qe"][:, ks[hd]], s_in[i]) + intra[i]

    @pl.when(pl.program_id(1) == pl.num_programs(1) - 1)
    def _():
        for hd in range(GLA_H):
            so_ref[hd] = s_scr[hd].T


def _gla_rec_prompt(q, k, v, gk, tc=512):
    b, l, _ = q.shape
    tc = _row_tile(l, tc)
    tile = lambda n: pl.BlockSpec((None, tc, n), lambda i, j: (i, j, 0))
    st = (GLA_H, GLA_DK, GLA_DV)
    return pl.pallas_call(
        _gla_rec_prompt_kernel,
        grid=(b, l // tc),
        in_specs=[tile(GLA_DK_TOT), tile(GLA_DK_TOT), tile(GLA_DV_TOT), tile(GLA_DK_TOT)],
        out_specs=[tile(GLA_DV_TOT), pl.BlockSpec((None,) + st, lambda i, j: (i, 0, 0, 0))],
        out_shape=[jax.ShapeDtypeStruct((b, l, GLA_DV_TOT), F32), jax.ShapeDtypeStruct((b,) + st, F32)],
        scratch_shapes=[pltpu.VMEM((GLA_H, GLA_DV, GLA_DK), F32)],
        compiler_params=_params(("parallel", "arbitrary")),
        name="gla_rec_prompt",
    )(q, k, v, gk)


def _gla_rec_decode_kernel(q_ref, k_ref, v_ref, gk_ref, s_ref, o_ref, so_ref):
    eye = (_iota((GLA_DK, GLA_DK), 0) == _iota((GLA_DK, GLA_DK), 1)).astype(F32)[None]

    def col(x):
        return jnp.sum(eye * x, axis=-1, keepdims=True)

    for hd in range(GLA_H):
        hs = slice(hd, hd + 1)
        s1 = s_ref[:, hd] * col(jnp.exp(gk_ref[:, hs, :])) + col(k_ref[:, hs, :]) * v_ref[:, hs, :]
        so_ref[:, hd] = s1
        o_ref[:, hs, :] = jnp.sum(col(q_ref[:, hs, :]) * s1, axis=1, keepdims=True)


def _gla_rec_decode(q, k, v, gk, s, bblk=16):
    m = q.shape[0]
    kvec = pl.BlockSpec((bblk, GLA_H, GLA_DK), lambda i: (i, 0, 0))
    vvec = pl.BlockSpec((bblk, GLA_H, GLA_DV), lambda i: (i, 0, 0))
    st = pl.BlockSpec((bblk, GLA_H, GLA_DK, GLA_DV), lambda i: (i, 0, 0, 0))
    hk = lambda x: x.reshape(m, GLA_H, GLA_DK)
    o, s_new = pl.pallas_call(
        _gla_rec_decode_kernel,
        grid=(m // bblk,),
        in_specs=[kvec, kvec, vvec, kvec, st],
        out_specs=[vvec, st],
        out_shape=[jax.ShapeDtypeStruct((m, GLA_H, GLA_DV), F32), jax.ShapeDtypeStruct(s.shape, F32)],
        compiler_params=_params(("parallel",)),
        name="gla_rec_decode",
    )(hk(q), hk(k), v.reshape(m, GLA_H, GLA_DV), hk(gk), s)
    return o.reshape(m, GLA_DV_TOT), s_new


def _gla_post_kernel(h_ref, o_ref_in, g_ref, nw_ref, wo_ref, *rest):
    *mlp_refs, out_ref = rest
    o = o_ref_in[...]
    parts = []
    for hd in range(GLA_H):
        oh = o[:, hd * GLA_DV:(hd + 1) * GLA_DV]
        parts.append(oh * lax.rsqrt(jnp.mean(oh * oh, axis=-1, keepdims=True) + GLA_NORM_EPS))
    on = jnp.concatenate(parts, axis=1) * nw_ref[...]
    g = g_ref[...]
    h = h_ref[...] + _bdot(on * (g * jax.nn.sigmoid(g)), wo_ref[...])
    out_ref[...] = _mlp(h, *mlp_refs) if mlp_refs else h


def _gla_post_ffn(h, o, g, p, mlp=None, tm=512):
    m = h.shape[0]
    tm = _row_tile(m, tm)
    row = pl.BlockSpec((tm, D_MODEL), lambda i: (i, 0))
    mlp_ins, mlp_specs = _mlp_operands(mlp) if mlp else ([], [])
    return pl.pallas_call(
        _gla_post_kernel,
        grid=(m // tm,),
        in_specs=[row] * 3 + [_const_spec((1, D_MODEL)), _const_spec((D_MODEL, D_MODEL))] + mlp_specs,
        out_specs=row,
        out_shape=jax.ShapeDtypeStruct((m, D_MODEL), F32),
        compiler_params=_params(("parallel",)),
        name="gla_post_ffn",
    )(h, o, g, jnp.tile(p["norm"], GLA_H).reshape(1, D_MODEL), p["wo"].astype(BF16), *mlp_ins)


def _conv_gates(h_ref, nm_ref, win_ref):
    u = _rms(h_ref[...], nm_ref[...], NORM_EPS).astype(BF16)
    z = jnp.dot(u, win_ref[...], preferred_element_type=F32)
    return z[:, :D_MODEL], z[:, D_MODEL:2 * D_MODEL] * z[:, 2 * D_MODEL:]


def _conv_prompt_kernel(h_ref, nm_ref, win_ref, cw_ref, wo_ref, o_ref, st_ref, carry_ref):
    tm = h_ref.shape[0]

    @pl.when(pl.program_id(1) == 0)
    def _():
        carry_ref[...] = jnp.zeros_like(carry_ref)

    ts = tm // SUBTILES
    subs = [slice(i * ts, (i + 1) * ts) for i in range(SUBTILES)]
    gates = [_conv_gates(h_ref.at[sl], nm_ref, win_ref) for sl in subs]
    row = _iota((ts, D_MODEL), 0)
    prev2, prev1 = carry_ref[0:1, :], carry_ref[1:2, :]
    gated = []
    for g_b, zc in gates:
        z1 = jnp.where(row == 0, prev1, pltpu.roll(zc, 1, axis=0))
        z2 = jnp.where(row == 0, prev2, jnp.where(row == 1, prev1, pltpu.roll(zc, 2, axis=0)))
        gated.append((g_b * (cw_ref[0:1, :] * z2 + cw_ref[1:2, :] * z1 + cw_ref[2:3, :] * zc)).astype(BF16))
        prev2, prev1 = zc[ts - 2:ts - 1, :], zc[ts - 1:ts, :]
    for sl, x in zip(subs, gated):
        o_ref[sl, :] = h_ref[sl, :] + jnp.dot(x, wo_ref[...], preferred_element_type=F32)
    tail = gates[-1][1][ts - (CONV_W - 1):, :]
    carry_ref[0:CONV_W - 1, :] = tail
    st_ref[...] = tail


def _conv_prompt(h, nm, p, tm=1024):
    b, l, _ = h.shape
    tm = _row_tile(l, tm)
    tile = pl.BlockSpec((None, tm, D_MODEL), lambda i, j: (i, j, 0))
    return pl.pallas_call(
        _conv_prompt_kernel,
        grid=(b, l // tm),
        in_specs=[tile, _const_spec((1, D_MODEL)), _const_spec((D_MODEL, 3 * D_MODEL)),
                  _const_spec((CONV_W, D_MODEL)), _const_spec((D_MODEL, D_MODEL))],
        out_specs=[tile, pl.BlockSpec((None, CONV_W - 1, D_MODEL), lambda i, j: (i, 0, 0))],
        out_shape=[jax.ShapeDtypeStruct((b, l, D_MODEL), F32),
                   jax.ShapeDtypeStruct((b, CONV_W - 1, D_MODEL), F32)],
        scratch_shapes=[pltpu.VMEM((SUBLANES, D_MODEL), F32)],
        compiler_params=_params(("parallel", "arbitrary")),
        name="conv_prompt",
    )(h, nm.reshape(1, D_MODEL), p["w_in"].astype(BF16), p["w"], p["wo"].astype(BF16))


def _conv_decode_kernel(h_ref, buf_ref, nm_ref, win_ref, cw_ref, wo_ref, o_ref, st_ref):
    g_b, zc = _conv_gates(h_ref, nm_ref, win_ref)
    z2 = buf_ref[:, 0, :]
    z1 = buf_ref[:, 1, :]
    conv = cw_ref[0:1, :] * z2 + cw_ref[1:2, :] * z1 + cw_ref[2:3, :] * zc
    o_ref[...] = h_ref[...] + _bdot(g_b * conv, wo_ref[...])
    st_ref[:, 0, :] = z1
    st_ref[:, 1, :] = zc


def _conv_decode(h, buf, nm, p):
    m = h.shape[0]
    full = pl.BlockSpec((m, D_MODEL), lambda i: (0, 0))
    bufs = pl.BlockSpec(buf.shape, lambda i: (0, 0, 0))
    return pl.pallas_call(
        _conv_decode_kernel,
        grid=(1,),
        in_specs=[full, bufs, _const_spec((1, D_MODEL)), _const_spec((D_MODEL, 3 * D_MODEL)),
                  _const_spec((CONV_W, D_MODEL)), _const_spec((D_MODEL, D_MODEL))],
        out_specs=[full, bufs],
        out_shape=[jax.ShapeDtypeStruct((m, D_MODEL), F32), jax.ShapeDtypeStruct(buf.shape, F32)],
        compiler_params=_params(("arbitrary",)),
        name="conv_decode",
    )(h, buf, nm.reshape(1, D_MODEL), p["w_in"].astype(BF16), p["w"], p["wo"].astype(BF16))


def _pool_project(d_groups, pw_ref, sc_ref):
    ys = [_bdot(d, pw_ref[gi]) for gi, d in enumerate(d_groups)]
    return jnp.concatenate(ys, axis=1) * sc_ref[...]


def _pool_prompt_kernel(h_ref, nm_ref, pw_ref, sc_ref, nf_ref, wu_ref, wd_ref, gf_ref, o_ref, st_ref, carry_ref):
    tm = h_ref.shape[0]
    hist = carry_ref.shape[0]
    j = pl.program_id(1)

    @pl.when(j == 0)
    def _():
        carry_ref[...] = jnp.zeros_like(carry_ref)

    u = _rms(h_ref[...], nm_ref[...], NORM_EPS)
    ext = jnp.concatenate([carry_ref[...], u], axis=0)
    pos = j * tm + _iota((tm, 1), 0)
    d_groups = []
    for gi, w in enumerate(POOL_WINDOWS):
        sl = slice(gi * POOL_G, (gi + 1) * POOL_G)
        s = ext[:, sl]
        span = 1
        while span < w:
            s = s + pltpu.roll(s, span, axis=0)
            span *= 2
        cnt = jnp.minimum(w, pos + 1).astype(F32)
        d_groups.append(s[hist:, :] / cnt - u[:, sl])
    o_ref[...] = _mlp(h_ref[...] + _pool_project(d_groups, pw_ref, sc_ref), nf_ref, wu_ref, wd_ref, gf_ref)
    tail = u[tm - hist:, :]
    carry_ref[...] = tail
    st_ref[...] = tail


def _pool_prompt_ffn(h, nm, p, mlp, final_g, tm=512):
    b, l, _ = h.shape
    tm = _row_tile(l, tm)
    hist = POOL_BUF + 1
    tile = pl.BlockSpec((None, tm, D_MODEL), lambda i, j: (i, j, 0))
    mlp_ins, mlp_specs = _mlp_operands(mlp, final_g)
    out, st = pl.pallas_call(
        _pool_prompt_kernel,
        grid=(b, l // tm),
        in_specs=[tile, _const_spec((1, D_MODEL)), _const_spec(p["w"].shape), _const_spec((1, D_MODEL))]
        + mlp_specs,
        out_specs=[tile, pl.BlockSpec((None, hist, D_MODEL), lambda i, j: (i, 0, 0))],
        out_shape=[jax.ShapeDtypeStruct((b, l, D_MODEL), F32), jax.ShapeDtypeStruct((b, hist, D_MODEL), F32)],
        scratch_shapes=[pltpu.VMEM((hist, D_MODEL), F32)],
        compiler_params=_params(("parallel", "arbitrary")),
        name="pool_prompt_ffn",
    )(h, nm.reshape(1, D_MODEL), p["w"].astype(BF16), p["scale"].reshape(1, D_MODEL), *mlp_ins)
    return out, st[:, 1:]


def _pool_decode_kernel(h_ref, buf_ref, nm_ref, pw_ref, sc_ref, o_ref, st_ref):
    u = _rms(h_ref[...], nm_ref[...], NORM_EPS)
    d_groups = []
    for gi, w in enumerate(POOL_WINDOWS):
        sl = slice(gi * POOL_G, (gi + 1) * POOL_G)
        s = u[:, sl]
        for i in range(1, w):
            s = s + buf_ref[POOL_BUF - i, :, sl]
        cnt = float(min(w, PAST_LEN + 1))
        d_groups.append(s / cnt - u[:, sl])
    o_ref[...] = h_ref[...] + _pool_project(d_groups, pw_ref, sc_ref)
    st_ref[0:POOL_BUF - 1] = buf_ref[1:POOL_BUF]
    st_ref[POOL_BUF - 1] = u


def _pool_decode(h, buf, nm, p):
    m = h.shape[0]
    full = pl.BlockSpec((m, D_MODEL), lambda i: (0, 0))
    bufs = pl.BlockSpec(buf.shape, lambda i: (0, 0, 0))
    return pl.pallas_call(
        _pool_decode_kernel,
        grid=(1,),
        in_specs=[full, bufs, _const_spec((1, D_MODEL)), _const_spec(p["w"].shape), _const_spec((1, D_MODEL))],
        out_specs=[full, bufs],
        out_shape=[jax.ShapeDtypeStruct((m, D_MODEL), F32), jax.ShapeDtypeStruct(buf.shape, F32)],
        compiler_params=_params(("arbitrary",)),
        name="pool_decode",
    )(h, buf, nm.reshape(1, D_MODEL), p["w"].astype(BF16), p["scale"].reshape(1, D_MODEL))


def _trunk_prompt(x, nm, nfin, mlps, rw, gl, cv, po):
    b, l, _ = x.shape
    flat = lambda t: t.reshape(b * l, t.shape[-1])
    seq = lambda t: t.reshape(b, l, t.shape[-1])

    (r, lw, k, v, kk, bb, g, bonus), shift = _rwkv_pre_prompt(x, nm[0], rw)
    y, wkv = _rwkv_rec_prompt(r, lw, k, v, kk, bb)
    h = _rwkv_post_ffn(flat(x), flat(y), flat(bonus), flat(g), rw, mlps[0])

    q, k, v, g, gk = _gla_pre(h, nm[1], gl)
    o, gla_s = _gla_rec_prompt(seq(q), seq(k), seq(v), seq(gk))
    h = _gla_post_ffn(h, flat(o), g, gl, mlps[1])

    h, conv_s = _conv_prompt(seq(h), nm[2], cv)
    h = _ffn(flat(h), mlps[2])

    y, pool_s = _pool_prompt_ffn(seq(h), nm[3], po, mlps[3], nfin)
    return y, wkv[None], shift[None], gla_s[None], conv_s[None], pool_s[None]


def _trunk_decode(x, st_wkv, st_shift, st_gla, st_conv, st_pool, nm, nf, nfin, w_up, w_down, rw, gl, cv, po):
    m = x.shape[0]
    h = x.reshape(m, D_MODEL)
    mlps = []

    def mlp(h, i, final_g=None):
        h, wub, wdb = _ffn_cast(h, nf[i], w_up, w_down, i, final_g)
        mlps.append((nf[i], wub, wdb))
        return h

    (r, lw, k, v, kk, bb, g, bonus), shift = _rwkv_pre_decode(h, st_shift[0], nm[0], rw)
    y_t, wkv_t = _rwkv_rec_decode(r, lw, k, v, kk, bb, jnp.transpose(st_wkv[0], (1, 2, 3, 0)))
    wkv = jnp.transpose(wkv_t, (3, 0, 1, 2))
    h = mlp(_rwkv_post_ffn(h, y_t.T, bonus, g, rw), 0)

    q, k, v, g, gk = _gla_pre(h, nm[1], gl)
    o, gla_s = _gla_rec_decode(q, k, v, gk, st_gla[0])
    h = mlp(_gla_post_ffn(h, o, g, gl), 1)

    h, conv_s = _conv_decode(h, st_conv[0], nm[2], cv)
    h = mlp(h, 2)

    h, pool_t = _pool_decode(h, jnp.transpose(st_pool[0], (1, 0, 2)), nm[3], po)
    pool_s = jnp.transpose(pool_t, (1, 0, 2))
    y = mlp(h, 3, final_g=nfin)
    return (y.reshape(m, 1, D_MODEL), wkv[None], shift[None], gla_s[None], conv_s[None], pool_s[None]), mlps


def kernel(x_prompt, x_sample, state_rwkv_wkv, state_rwkv_shift, state_gla, state_conv, state_pool, norm_mix, norm_ffn, norm_final, ffn_up, ffn_down, rwkv_mu, rwkv_w_rkv, rwkv_w0, rwkv_w1, rwkv_w2, rwkv_a0, rwkv_a1, rwkv_a2, rwkv_g1, rwkv_g2, rwkv_k_k, rwkv_k_a, rwkv_r_k, rwkv_ln_w, rwkv_ln_b, rwkv_wo, gla_w_in, gla_w_gk2, gla_b_gk, gla_norm, gla_wo, conv_w_in, conv_w, conv_wo, pool_w, pool_scale):
    assert x_prompt.shape[1] % CHUNK == 0 and x_sample.shape[1] == 1
    rw = dict(mu=rwkv_mu[0], w_rkv=rwkv_w_rkv[0], w0=rwkv_w0[0], w1=rwkv_w1[0], w2=rwkv_w2[0], a0=rwkv_a0[0],
              a1=rwkv_a1[0], a2=rwkv_a2[0], g1=rwkv_g1[0], g2=rwkv_g2[0], k_k=rwkv_k_k[0], k_a=rwkv_k_a[0],
              r_k=rwkv_r_k[0], ln_w=rwkv_ln_w[0], ln_b=rwkv_ln_b[0], wo=rwkv_wo[0])
    gl = dict(w_in=gla_w_in[0], w_gk2=gla_w_gk2[0], b_gk=gla_b_gk[0], norm=gla_norm[0], wo=gla_wo[0])
    cv = dict(w_in=conv_w_in[0], w=conv_w[0], wo=conv_wo[0])
    po = dict(w=pool_w[0], scale=pool_scale[0])
    (y_s, wkv_s, sh_s, gla_s, conv_s, pool_s), mlps = _trunk_decode(
        x_sample, state_rwkv_wkv, state_rwkv_shift, state_gla, state_conv, state_pool,
        norm_mix, norm_ffn, norm_final, ffn_up, ffn_down, rw, gl, cv, po)
    y_p, wkv_p, sh_p, gla_p, conv_p, pool_p = _trunk_prompt(x_prompt, norm_mix, norm_final, mlps, rw, gl, cv, po)
    return (y_p, y_s, wkv_p, wkv_s, sh_p, sh_s, gla_p, gla_s, conv_p, conv_s, pool_p, pool_s)
```

```python
import functools
import math

import jax
import jax.numpy as jnp
from jax import lax
from jax.experimental import pallas as pl
from jax.experimental.pallas import tpu as pltpu

F32 = jnp.float32
BF16 = jnp.bfloat16

D_MODEL = 1024
D_FF = 4 * D_MODEL
NORM_EPS = 1e-6
PAST_LEN = 16384

RWKV_HEAD = 64
RWKV_H = D_MODEL // RWKV_HEAD
RWKV_GN_EPS = 64e-5
RWKV_QUAD = 4

GLA_H = 4
GLA_DK_TOT = D_MODEL // 2
GLA_DV_TOT = D_MODEL
GLA_DK = GLA_DK_TOT // GLA_H
GLA_DV = GLA_DV_TOT // GLA_H
GLA_LR = 16
GLA_NORMALIZER = 16.0
GLA_NORM_EPS = 1e-5
GLA_SUB = 16

CONV_W = 3
POOL_WINDOWS = (2, 4, 8, 16)
POOL_G = D_MODEL // len(POOL_WINDOWS)
POOL_BUF = max(POOL_WINDOWS) - 1

LANES = 128
SUBLANES = 8
MXU_DIM = 256
VMEM_LIMIT_BYTES = 56 * 1024 * 1024

CHUNK = 64
FFN_COLS = 1024
SUBTILES = 2


def _const_spec(shape):
    nd = len(shape)
    return pl.BlockSpec(shape, lambda *_: (0,) * nd, pipeline_mode=pl.Buffered(1))


def _params(sem):
    return pltpu.CompilerParams(dimension_semantics=sem, vmem_limit_bytes=VMEM_LIMIT_BYTES)


def _rms(x, g, eps):
    return x * lax.rsqrt(jnp.mean(x * x, axis=-1, keepdims=True) + eps) * g


def _bdot(a, w):
    return jnp.dot(a.astype(BF16), w, preferred_element_type=F32)


def _mm(a, b, prec=None):
    return jnp.dot(a, b, preferred_element_type=F32, precision=prec)


def _nt(a, b, prec=None):
    return lax.dot_general(a, b, (((1,), (1,)), ((), ())), preferred_element_type=F32, precision=prec)


def _tn(a, b, prec=None):
    return lax.dot_general(a, b, (((0,), (0,)), ((), ())), preferred_element_type=F32, precision=prec)


def _iota(shape, dim):
    return lax.broadcasted_iota(jnp.int32, shape, dim)


def _group_sum(x, group, split=False):
    shift = group.bit_length() - 1
    bd = (jnp.right_shift(_iota((MXU_DIM, MXU_DIM), 0), shift)
          == jnp.right_shift(_iota((MXU_DIM, MXU_DIM), 1), shift)).astype(BF16)
    terms = [x.astype(BF16)]
    if split:
        terms.append((x - terms[0].astype(F32)).astype(BF16))
    cols = [sum(_mm(t[:, c * MXU_DIM:(c + 1) * MXU_DIM], bd) for t in terms)
            for c in range(x.shape[1] // MXU_DIM)]
    return jnp.concatenate(cols, axis=1)


def _chunk_cumsum(x):
    c = x.shape[0]
    ltri = (_iota((c, c), 0) >= _iota((c, c), 1)).astype(BF16)
    t0 = x.astype(BF16)
    r1 = x - t0.astype(F32)
    t1 = r1.astype(BF16)
    t2 = (r1 - t1.astype(F32)).astype(BF16)
    return _mm(ltri, t0) + _mm(ltri, t1) + _mm(ltri, t2)


def _softplus(x):
    return jnp.maximum(x, 0.0) + jnp.log(1.0 + jnp.exp(-jnp.abs(x)))


def _row_tile(m, want):
    t = min(want, m)
    assert m % t == 0, (m, t)
    return t


def _mlp(h, g_ref, wu_ref, wd_ref, gf_ref=None):
    x = _rms(h, g_ref[...], NORM_EPS).astype(BF16)
    acc = h
    for c in range(D_FF // FFN_COLS):
        sl = slice(c * FFN_COLS, (c + 1) * FFN_COLS)
        a = jnp.dot(x, wu_ref[:, sl], preferred_element_type=F32)
        a = jnp.square(jnp.maximum(a, 0.0)).astype(BF16)
        acc = acc + jnp.dot(a, wd_ref[sl, :], preferred_element_type=F32)
    if gf_ref is not None:
        acc = _rms(acc, gf_ref[...], NORM_EPS)
    return acc


def _mlp_operands(mlp, final_g=None):
    nf, wu, wd = mlp
    ins = [nf.reshape(1, D_MODEL), wu, wd]
    specs = [_const_spec((1, D_MODEL)), _const_spec(wu.shape), _const_spec(wd.shape)]
    if final_g is not None:
        ins.append(final_g.reshape(1, D_MODEL))
        specs.append(_const_spec((1, D_MODEL)))
    return ins, specs


def _ffn_kernel(h_ref, g_ref, wu_ref, wd_ref, *rest):
    *gf_ref, o_ref = rest
    o_ref[...] = _mlp(h_ref[...], g_ref, wu_ref, wd_ref, *gf_ref)


def _ffn(h, mlp, final_g=None, tm=1024):
    m = h.shape[0]
    tm = _row_tile(m, tm)
    row = pl.BlockSpec((tm, D_MODEL), lambda i: (i, 0))
    mlp_ins, mlp_specs = _mlp_operands(mlp, final_g)
    return pl.pallas_call(
        _ffn_kernel,
        grid=(m // tm,),
        in_specs=[row] + mlp_specs,
        out_specs=row,
        out_shape=jax.ShapeDtypeStruct((m, D_MODEL), F32),
        compiler_params=_params(("parallel",)),
        name="ffn",
    )(h, *mlp_ins)


def _ffn_cast_kernel(h_ref, g_ref, wu_ref, wd_ref, *rest):
    *gf_ref, o_ref, wub_ref, wdb_ref, x_scr, acc_scr = rest
    c = pl.program_id(0)

    @pl.when(c == 0)
    def _():
        h = h_ref[...]
        x_scr[...] = _rms(h, g_ref[...], NORM_EPS).astype(BF16)
        acc_scr[...] = h

    wu = wu_ref[...].astype(BF16)
    wd = wd_ref[...].astype(BF16)
    wub_ref[...] = wu
    wdb_ref[...] = wd
    a = jnp.dot(x_scr[...], wu, preferred_element_type=F32)
    a = jnp.square(jnp.maximum(a, 0.0)).astype(BF16)
    acc_scr[...] += jnp.dot(a, wd, preferred_element_type=F32)

    @pl.when(c == pl.num_programs(0) - 1)
    def _():
        acc = acc_scr[...]
        o_ref[...] = _rms(acc, gf_ref[0][...], NORM_EPS) if gf_ref else acc


def _ffn_cast(h, nf, w_up, w_down, layer, final_g=None, cols=FFN_COLS // 2):
    m = h.shape[0]
    full = pl.BlockSpec((m, D_MODEL), lambda c: (0, 0))
    vec = pl.BlockSpec((1, D_MODEL), lambda c: (0, 0))
    ins = [h, nf.reshape(1, D_MODEL), w_up, w_down]
    specs = [full, vec, pl.BlockSpec((None, D_MODEL, cols), lambda c: (layer, 0, c)),
             pl.BlockSpec((None, cols, D_MODEL), lambda c: (layer, c, 0))]
    if final_g is not None:
        ins.append(final_g.reshape(1, D_MODEL))
        specs.append(vec)
    return pl.pallas_call(
        _ffn_cast_kernel,
        grid=(D_FF // cols,),
        in_specs=specs,
        out_specs=[full, pl.BlockSpec((D_MODEL, cols), lambda c: (0, c)),
                   pl.BlockSpec((cols, D_MODEL), lambda c: (c, 0))],
        out_shape=[jax.ShapeDtypeStruct((m, D_MODEL), F32), jax.ShapeDtypeStruct((D_MODEL, D_FF), BF16),
                   jax.ShapeDtypeStruct((D_FF, D_MODEL), BF16)],
        scratch_shapes=[pltpu.VMEM((m, D_MODEL), BF16), pltpu.VMEM((m, D_MODEL), F32)],
        compiler_params=_params(("arbitrary",)),
        name="ffn_cast",
    )(*ins)


def _rwkv_pre_math(u, prev, mu_ref, wrkv_ref, w0_ref, w1_ref, w2_ref, a0_ref, a1_ref, a2_ref,
                   g1_ref, g2_ref, kk_ref, ka_ref, rk_ref):
    dx = prev - u

    def mix(i):
        return (u + dx * mu_ref[i:i + 1, :]).astype(BF16)

    r = _bdot(mix(0), wrkv_ref[0])
    w_mid = _bdot(mix(1), w1_ref[...])
    a_mid = _bdot(mix(4), a1_ref[...])
    g_mid = _bdot(mix(5), g1_ref[...])
    k = _bdot(mix(2), wrkv_ref[1])
    wl = w0_ref[...] + _bdot(jnp.tanh(w_mid), w2_ref[...])
    a = jax.nn.sigmoid(a0_ref[...] + _bdot(a_mid, a2_ref[...]))
    g = _bdot(jax.nn.sigmoid(g_mid), g2_ref[...])
    lw = jax.nn.sigmoid(wl) * (-math.exp(-0.5))
    kk = k * kk_ref[...]
    kk_ss = _group_sum(kk * kk, RWKV_HEAD)
    v = _bdot(mix(3), wrkv_ref[2])
    k = k * (1.0 + (a - 1.0) * ka_ref[...])
    rk_sum = _group_sum(r * k * rk_ref[...], RWKV_HEAD)
    kk = kk * lax.rsqrt(jnp.maximum(kk_ss, 1e-24))
    return r, lw, k, v, kk, kk * a, g, rk_sum * v


_N_RWKV_W = 13


def _rwkv_pre_prompt_kernel(h_ref, nm_ref, *rest):
    w = rest[:_N_RWKV_W]
    outs = rest[_N_RWKV_W:_N_RWKV_W + 8]
    sh_ref, carry_ref = rest[_N_RWKV_W + 8:]
    tm = h_ref.shape[0]

    @pl.when(pl.program_id(1) == 0)
    def _():
        carry_ref[...] = jnp.zeros_like(carry_ref)

    u = _rms(h_ref[...], nm_ref[...], NORM_EPS)
    prev = jnp.where(_iota(u.shape, 0) == 0, carry_ref[0:1, :], pltpu.roll(u, 1, axis=0))
    last = u[tm - 1:tm, :]
    carry_ref[0:1, :] = last
    sh_ref[...] = last
    for o_ref, val in zip(outs, _rwkv_pre_math(u, prev, *w)):
        o_ref[...] = val


def _rwkv_pre_decode_kernel(h_ref, prev_ref, nm_ref, *rest):
    w = rest[:_N_RWKV_W]
    outs = rest[_N_RWKV_W:_N_RWKV_W + 8]
    (sh_ref,) = rest[_N_RWKV_W + 8:]
    u = _rms(h_ref[...], nm_ref[...], NORM_EPS)
    sh_ref[...] = u
    vals = _rwkv_pre_math(u, prev_ref[...], *w)
    for o_ref, val in zip(outs[:6], vals[:6]):
        o_ref[...] = val.T
    for o_ref, val in zip(outs[6:], vals[6:]):
        o_ref[...] = val


def _rwkv_weights(p):
    row = lambda x: x.reshape(1, D_MODEL)
    return [p["mu"], p["w_rkv"].astype(BF16), row(p["w0"]), p["w1"].astype(BF16), p["w2"].astype(BF16),
            row(p["a0"]), p["a1"].astype(BF16), p["a2"].astype(BF16), p["g1"].astype(BF16),
            p["g2"].astype(BF16), row(p["k_k"]), row(p["k_a"]), row(p["r_k"])]


def _rwkv_pre_prompt(h, nm, p, tm=512):
    b, l, _ = h.shape
    tm = _row_tile(l, tm)
    ws = _rwkv_weights(p)
    tile = pl.BlockSpec((None, tm, D_MODEL), lambda i, j: (i, j, 0))
    outs = pl.pallas_call(
        _rwkv_pre_prompt_kernel,
        grid=(b, l // tm),
        in_specs=[tile, _const_spec((1, D_MODEL))] + [_const_spec(w.shape) for w in ws],
        out_specs=[tile] * 8 + [pl.BlockSpec((None, 1, D_MODEL), lambda i, j: (i, 0, 0))],
        out_shape=[jax.ShapeDtypeStruct((b, l, D_MODEL), F32)] * 8
        + [jax.ShapeDtypeStruct((b, 1, D_MODEL), F32)],
        scratch_shapes=[pltpu.VMEM((SUBLANES, D_MODEL), F32)],
        compiler_params=_params(("parallel", "arbitrary")),
        name="rwkv_pre_prompt",
    )(h, nm.reshape(1, D_MODEL), *ws)
    return outs[:8], outs[8].reshape(b, D_MODEL)


def _rwkv_pre_decode(h, prev, nm, p):
    m = h.shape[0]
    ws = _rwkv_weights(p)
    full = pl.BlockSpec((m, D_MODEL), lambda i: (0, 0))
    full_t = pl.BlockSpec((D_MODEL, m), lambda i: (0, 0))
    outs = pl.pallas_call(
        _rwkv_pre_decode_kernel,
        grid=(1,),
        in_specs=[full, full, _const_spec((1, D_MODEL))] + [_const_spec(w.shape) for w in ws],
        out_specs=[full_t] * 6 + [full] * 3,
        out_shape=[jax.ShapeDtypeStruct((D_MODEL, m), F32)] * 6 + [jax.ShapeDtypeStruct((m, D_MODEL), F32)] * 3,
        compiler_params=_params(("arbitrary",)),
        name="rwkv_pre_decode",
    )(h, prev, nm.reshape(1, D_MODEL), *ws)
    return outs[:8], outs[8]


def _rwkv_rec_prompt_kernel(r_ref, lw_ref, k_ref, v_ref, kk_ref, bb_ref, y_ref, so_ref, s_scr):
    tc = r_ref.shape[0]
    qw = RWKV_QUAD * RWKV_HEAD
    n_quads = D_MODEL // qw
    assert RWKV_QUAD * CHUNK == qw

    @pl.when(pl.program_id(1) == 0)
    def _():
        s_scr[...] = jnp.zeros_like(s_scr)

    t_idx, s_idx = _iota((CHUNK, qw), 0), _iota((CHUNK, qw), 1) & (CHUNK - 1)
    strict = t_idx > s_idx
    incl = t_idx >= s_idx
    eye = (t_idx == s_idx).astype(F32)
    blocks = jnp.right_shift(_iota((qw, qw), 0), 6) == jnp.right_shift(_iota((qw, qw), 1), 6)

    def bdiag(z):
        zb = z.astype(BF16)
        return jnp.where(blocks, jnp.concatenate([zb] * RWKV_QUAD, axis=0), jnp.zeros((), BF16))

    quads = range(n_quads)
    sls = [slice(q * qw, (q + 1) * qw) for q in quads]
    cc = CHUNK

    n_chunks = tc // CHUNK
    items = [(c, q) for c in range(n_chunks) for q in quads]
    every = range(len(items))

    scaled = []
    for c in range(n_chunks):
        rows = slice(c * CHUNK, (c + 1) * CHUNK)
        lw = lw_ref[rows, :]
        cum = _chunk_cumsum(lw)
        g_in = jnp.exp(cum)
        g_inv = jnp.exp(-cum)
        g_last = g_in[CHUNK - 1:CHUNK, :]
        scaled.append(dict(
            rt=r_ref[rows, :] * g_in, at=-kk_ref[rows, :] * jnp.exp(cum - lw),
            bt=bb_ref[rows, :] * g_inv, kt=k_ref[rows, :] * g_inv, v=v_ref[rows, :], g_last=g_last))

    def part(name, i):
        c, q = items[i]
        return scaled[c][name][:, sls[q]]

    ar = [jnp.concatenate([part("at", i), part("rt", i)], axis=0).astype(BF16) for i in every]
    bk4 = [jnp.concatenate([bdiag(part("bt", i)), bdiag(part("kt", i))], axis=0) for i in every]
    gram = [_nt(ar[i], bk4[i]) for i in every]
    a_ab = [jnp.where(strict, gram[i][:cc, :qw], 0.0) for i in every]
    a_k = [jnp.concatenate([jnp.where(strict, gram[i][:cc, qw:], 0.0),
                            jnp.where(incl, gram[i][cc:, qw:], 0.0)], axis=0).astype(BF16) for i in every]
    a_rb = [jnp.where(incl, gram[i][cc:, :qw], 0.0).astype(BF16) for i in every]
    akv = [_mm(a_k[i], bdiag(part("v", i))) for i in every]
    inv = [eye + a_ab[i] for i in every]
    pw = [_mm(a_ab[i].astype(BF16), bdiag(a_ab[i])) for i in every]
    for _ in range(1, CHUNK.bit_length() - 2):
        both = [_mm(jnp.concatenate([pw[i], inv[i]], axis=0).astype(BF16), bdiag(pw[i])) for i in every]
        inv = [inv[i] + both[i][cc:] for i in every]
        pw = [both[i][:cc] for i in every]
    inv = [(inv[i] + _mm(inv[i].astype(BF16), bdiag(pw[i]))).astype(BF16) for i in every]
    tt = [_mm(inv[i], jnp.concatenate([bdiag(part("at", i)), bdiag(akv[i][:cc])], axis=1)) for i in every]
    ta = [tt[i][:, :qw].astype(BF16) for i in every]
    tav = [tt[i][:, qw:] for i in every]
    bkg = [(jnp.concatenate([part("bt", i), part("kt", i)], axis=0) * part("g_last", i)).astype(BF16)
           for i in every]
    v_b = [part("v", i).astype(BF16) for i in every]

    xs, u = [], []

    def emit_outputs(c):
        for i in range(c * n_quads, (c + 1) * n_quads):
            y_ref[c * CHUNK:(c + 1) * CHUNK, sls[items[i][1]]] = xs[i][cc:] + akv[i][cc:] + _mm(a_rb[i], bdiag(u[i]))

    state = [s_scr[q] for q in quads]
    for c in range(n_chunks):
        mine = [c * n_quads + q for q in quads]
        xs += [_nt(jnp.concatenate([ta[i], ar[i][cc:]], axis=0), state[q].astype(BF16))
               for q, i in enumerate(mine)]
        u += [xs[i][:cc] + tav[i] for i in mine]
        state = [state[q] * scaled[c]["g_last"][:, sls[q]]
                 + jnp.where(blocks, _tn(jnp.concatenate([u[i].astype(BF16), v_b[i]], axis=0), bkg[i]), 0.0)
                 for q, i in enumerate(mine)]
        if c > 0:
            emit_outputs(c - 1)
    emit_outputs(n_chunks - 1)
    for q in quads:
        s_scr[q] = state[q]

    @pl.when(pl.program_id(1) == pl.num_programs(1) - 1)
    def _():
        for hd in range(RWKV_H):
            q, o = divmod(hd, RWKV_QUAD)
            so_ref[hd] = s_scr[q][o * RWKV_HEAD:(o + 1) * RWKV_HEAD, o * RWKV_HEAD:(o + 1) * RWKV_HEAD]


def _rwkv_rec_prompt(r, lw, k, v, kk, bb, tc=512):
    b, l, _ = r.shape
    tc = _row_tile(l, tc)
    tile = pl.BlockSpec((None, tc, D_MODEL), lambda i, j: (i, j, 0))
    st = (RWKV_H, RWKV_HEAD, RWKV_HEAD)
    return pl.pallas_call(
        _rwkv_rec_prompt_kernel,
        grid=(b, l // tc),
        in_specs=[tile] * 6,
        out_specs=[tile, pl.BlockSpec((None,) + st, lambda i, j: (i, 0, 0, 0))],
        out_shape=[jax.ShapeDtypeStruct((b, l, D_MODEL), F32), jax.ShapeDtypeStruct((b,) + st, F32)],
        scratch_shapes=[pltpu.VMEM((RWKV_H // RWKV_QUAD, RWKV_QUAD * RWKV_HEAD, RWKV_QUAD * RWKV_HEAD), F32)],
        compiler_params=_params(("parallel", "arbitrary")),
        name="rwkv_rec_prompt",
    )(r, lw, k, v, kk, bb)


def _rwkv_rec_decode_kernel(r_ref, lw_ref, k_ref, v_ref, kk_ref, bb_ref, s_ref, y_ref, so_ref):
    w = jnp.exp(lw_ref[...])
    a = -kk_ref[...]
    b, k, r = bb_ref[...], k_ref[...], r_ref[...]

    def row(vi, carry):
        s0 = s_ref[vi]
        sa = jnp.sum(s0 * a, axis=0, keepdims=True)
        s1 = s0 * w + sa * b + v_ref[pl.ds(vi, 1), :] * k
        so_ref[vi] = s1
        y_ref[pl.ds(vi, 1), :] = jnp.sum(s1 * r, axis=0, keepdims=True)
        return carry

    lax.fori_loop(0, RWKV_HEAD, row, 0, unroll=8)


def _rwkv_rec_decode(r, lw, k, v, kk, bb, s):
    m = r.shape[1]
    vec = pl.BlockSpec((RWKV_HEAD, m), lambda i: (i, 0))
    st = pl.BlockSpec((None, RWKV_HEAD, RWKV_HEAD, m), lambda i: (i, 0, 0, 0))
    return pl.pallas_call(
        _rwkv_rec_decode_kernel,
        grid=(RWKV_H,),
        in_specs=[vec] * 6 + [st],
        out_specs=[vec, st],
        out_shape=[jax.ShapeDtypeStruct((D_MODEL, m), F32), jax.ShapeDtypeStruct(s.shape, F32)],
        compiler_params=_params(("parallel",)),
        name="rwkv_rec_decode",
    )(r, lw, k, v, kk, bb, s)


def _rwkv_post_kernel(h_ref, y_ref, bo_ref, g_ref, lnw_ref, lnb_ref, wo_ref, *rest):
    *mlp_refs, o_ref = rest
    y = y_ref[...]
    d = y - _group_sum(y, RWKV_HEAD, split=True) * (1.0 / RWKV_HEAD)
    var = _group_sum(d * d, RWKV_HEAD) * (1.0 / RWKV_HEAD)
    yn = d * lax.rsqrt(var + RWKV_GN_EPS) * lnw_ref[...] + lnb_ref[...]
    h = h_ref[...] + _bdot((yn + bo_ref[...]) * g_ref[...], wo_ref[...])
    o_ref[...] = _mlp(h, *mlp_refs) if mlp_refs else h


def _rwkv_post_ffn(h, y, bonus, g, p, mlp=None, tm=512):
    m = h.shape[0]
    tm = _row_tile(m, tm)
    row = pl.BlockSpec((tm, D_MODEL), lambda i: (i, 0))
    vec = _const_spec((1, D_MODEL))
    mlp_ins, mlp_specs = _mlp_operands(mlp) if mlp else ([], [])
    return pl.pallas_call(
        _rwkv_post_kernel,
        grid=(m // tm,),
        in_specs=[row] * 4 + [vec, vec, _const_spec((D_MODEL, D_MODEL))] + mlp_specs,
        out_specs=row,
        out_shape=jax.ShapeDtypeStruct((m, D_MODEL), F32),
        compiler_params=_params(("parallel",)),
        name="rwkv_post_ffn",
    )(h, y, bonus, g, p["ln_w"].reshape(1, D_MODEL), p["ln_b"].reshape(1, D_MODEL), p["wo"].astype(BF16),
      *mlp_ins)


def _gla_pre_kernel(h_ref, nm_ref, win_ref, wgl_ref, wgk_ref, bgk_ref, q_ref, k_ref, v_ref, g_ref, gk_ref):
    u = _rms(h_ref[...], nm_ref[...], NORM_EPS).astype(BF16)
    z = _nt(u, win_ref[...])
    q_ref[...] = z[:, :GLA_DK_TOT] * (GLA_DK ** -0.5)
    k_ref[...] = z[:, GLA_DK_TOT:2 * GLA_DK_TOT]
    v_ref[...] = z[:, 2 * GLA_DK_TOT:2 * GLA_DK_TOT + GLA_DV_TOT]
    g_ref[...] = z[:, 2 * GLA_DK_TOT + GLA_DV_TOT:]
    gl = _nt(u, wgl_ref[...])
    pre = _bdot(gl, wgk_ref[...]) + bgk_ref[...]
    gk_ref[...] = -_softplus(-pre) * (1.0 / GLA_NORMALIZER)


def _gla_pre(h, nm, p, tm=1024):
    m = h.shape[0]
    tm = _row_tile(m, tm)
    n_main = 2 * GLA_DK_TOT + 2 * GLA_DV_TOT
    w_t = p["w_in"].T
    w_main = w_t[:n_main].astype(BF16)
    w_gl = jnp.pad(w_t[n_main:].astype(BF16), ((0, LANES - GLA_LR), (0, 0)))
    w_gk = jnp.pad(p["w_gk2"].astype(BF16), ((0, LANES - GLA_LR), (0, 0)))
    row = lambda n: pl.BlockSpec((tm, n), lambda i: (i, 0))
    widths = (GLA_DK_TOT, GLA_DK_TOT, GLA_DV_TOT, GLA_DV_TOT, GLA_DK_TOT)
    return pl.pallas_call(
        _gla_pre_kernel,
        grid=(m // tm,),
        in_specs=[row(D_MODEL), _const_spec((1, D_MODEL)), _const_spec(w_main.shape), _const_spec(w_gl.shape),
                  _const_spec(w_gk.shape), _const_spec((1, GLA_DK_TOT))],
        out_specs=[row(n) for n in widths],
        out_shape=[jax.ShapeDtypeStruct((m, n), F32) for n in widths],
        compiler_params=_params(("parallel",)),
        name="gla_pre",
    )(h, nm.reshape(1, D_MODEL), w_main, w_gl, w_gk, p["b_gk"].reshape(1, GLA_DK_TOT))


def _gla_rec_prompt_kernel(q_ref, k_ref, v_ref, gk_ref, o_ref, so_ref, s_scr):
    tc = q_ref.shape[0]

    @pl.when(pl.program_id(1) == 0)
    def _():
        s_scr[...] = jnp.zeros_like(s_scr)

    ri, ci = _iota((CHUNK, CHUNK), 0), _iota((CHUNK, CHUNK), 1)
    incl = ri >= ci
    n_sub = CHUNK // GLA_SUB
    sub_of_row = jnp.right_shift(_iota((CHUNK, 1), 0), GLA_SUB.bit_length() - 1)

    heads = range(GLA_H)
    ks = [slice(hd * GLA_DK, (hd + 1) * GLA_DK) for hd in heads]
    vs = [slice(hd * GLA_DV, (hd + 1) * GLA_DV) for hd in heads]
    n_chunks = tc // CHUNK
    items = [(c, hd) for c in range(n_chunks) for hd in heads]

    prep = []
    for c in range(n_chunks):
        rows = slice(c * CHUNK, (c + 1) * CHUNK)
        cum = _chunk_cumsum(gk_ref[rows, :])
        q, k = q_ref[rows, :], k_ref[rows, :]
        last = cum[CHUNK - 1:CHUNK, :]
        refs = [jnp.zeros_like(last)] + [cum[i * GLA_SUB - 1:i * GLA_SUB, :] for i in range(1, n_sub)]
        ref_of_row = refs[0]
        for i in range(1, n_sub):
            ref_of_row = jnp.where(sub_of_row >= i, refs[i], ref_of_row)
        q_in = (q * jnp.exp(cum - ref_of_row)).astype(BF16)
        k_in = []
        for i in range(n_sub):
            n = (i + 1) * GLA_SUB
            part = (k[:n] * jnp.exp(refs[i] - cum[:n])).astype(BF16)
            k_in.append(part if n == CHUNK else
                        jnp.concatenate([part, jnp.zeros((CHUNK - n, part.shape[1]), BF16)], axis=0))
        prep.append(dict(q_in=q_in, k_in=k_in, qe=(q * jnp.exp(cum)).astype(BF16),
                         kh=(k * jnp.exp(last - cum)).astype(BF16), g_last=jnp.exp(last),
                         v=v_ref[rows, :].astype(BF16)))

    def scores(c, hd):
        p = prep[c]
        blocks = [_nt(p["q_in"][i * GLA_SUB:(i + 1) * GLA_SUB, ks[hd]], p["k_in"][i][:, ks[hd]])
                  for i in range(n_sub)]
        return jnp.where(incl, jnp.concatenate(blocks, axis=0), 0.0).astype(BF16)

    att = [scores(c, hd) for c, hd in items]
    intra = [_mm(att[i], prep[c]["v"][:, vs[hd]]) for i, (c, hd) in enumerate(items)]
    kv = [_tn(prep[c]["v"][:, vs[hd]], prep[c]["kh"][:, ks[hd]]) for c, hd in items]

    s_in = []
    state = [s_scr[hd] for hd in heads]
    for c in range(n_chunks):
        s_in.extend(state[hd].astype(BF16) for hd in heads)
        state = [state[hd] * prep[c]["g_last"][:, ks[hd]] + kv[c * GLA_H + hd] for hd in heads]
    for hd in heads:
        s_scr[hd] = state[hd]

    for i, (c, hd) in enumerate(items):
        o_ref[c * CHUNK:(c + 1) * CHUNK, vs[hd]] = _nt(prep[c]["qe"][:, ks[hd]], s_in[i]) + intra[i]

    @pl.when(pl.program_id(1) == pl.num_programs(1) - 1)
    def _():
        for hd in range(GLA_H):
            so_ref[hd] = s_scr[hd].T


def _gla_rec_prompt(q, k, v, gk, tc=1024):
    b, l, _ = q.shape
    tc = _row_tile(l, tc)
    tile = lambda n: pl.BlockSpec((None, tc, n), lambda i, j: (i, j, 0))
    st = (GLA_H, GLA_DK, GLA_DV)
    return pl.pallas_call(
        _gla_rec_prompt_kernel,
        grid=(b, l // tc),
        in_specs=[tile(GLA_DK_TOT), tile(GLA_DK_TOT), tile(GLA_DV_TOT), tile(GLA_DK_TOT)],
        out_specs=[tile(GLA_DV_TOT), pl.BlockSpec((None,) + st, lambda i, j: (i, 0, 0, 0))],
        out_shape=[jax.ShapeDtypeStruct((b, l, GLA_DV_TOT), F32), jax.ShapeDtypeStruct((b,) + st, F32)],
        scratch_shapes=[pltpu.VMEM((GLA_H, GLA_DV, GLA_DK), F32)],
        compiler_params=_params(("parallel", "arbitrary")),
        name="gla_rec_prompt",
    )(q, k, v, gk)


def _gla_rec_decode_kernel(q_ref, k_ref, v_ref, gk_ref, s_ref, o_ref, so_ref):
    eye = (_iota((GLA_DK, GLA_DK), 0) == _iota((GLA_DK, GLA_DK), 1)).astype(F32)[None]

    def col(x):
        return jnp.sum(eye * x, axis=-1, keepdims=True)

    for hd in range(GLA_H):
        hs = slice(hd, hd + 1)
        s1 = s_ref[:, hd] * col(jnp.exp(gk_ref[:, hs, :])) + col(k_ref[:, hs, :]) * v_ref[:, hs, :]
        so_ref[:, hd] = s1
        o_ref[:, hs, :] = jnp.sum(col(q_ref[:, hs, :]) * s1, axis=1, keepdims=True)


def _gla_rec_decode(q, k, v, gk, s, bblk=16):
    m = q.shape[0]
    kvec = pl.BlockSpec((bblk, GLA_H, GLA_DK), lambda i: (i, 0, 0))
    vvec = pl.BlockSpec((bblk, GLA_H, GLA_DV), lambda i: (i, 0, 0))
    st = pl.BlockSpec((bblk, GLA_H, GLA_DK, GLA_DV), lambda i: (i, 0, 0, 0))
    hk = lambda x: x.reshape(m, GLA_H, GLA_DK)
    o, s_new = pl.pallas_call(
        _gla_rec_decode_kernel,
        grid=(m // bblk,),
        in_specs=[kvec, kvec, vvec, kvec, st],
        out_specs=[vvec, st],
        out_shape=[jax.ShapeDtypeStruct((m, GLA_H, GLA_DV), F32), jax.ShapeDtypeStruct(s.shape, F32)],
        compiler_params=_params(("parallel",)),
        name="gla_rec_decode",
    )(hk(q), hk(k), v.reshape(m, GLA_H, GLA_DV), hk(gk), s)
    return o.reshape(m, GLA_DV_TOT), s_new


def _gla_post_kernel(h_ref, o_ref_in, g_ref, nw_ref, wo_ref, *rest):
    *mlp_refs, out_ref = rest
    o = o_ref_in[...]
    parts = []
    for hd in range(GLA_H):
        oh = o[:, hd * GLA_DV:(hd + 1) * GLA_DV]
        parts.append(oh * lax.rsqrt(jnp.mean(oh * oh, axis=-1, keepdims=True) + GLA_NORM_EPS))
    on = jnp.concatenate(parts, axis=1) * nw_ref[...]
    g = g_ref[...]
    h = h_ref[...] + _bdot(on * (g * jax.nn.sigmoid(g)), wo_ref[...])
    out_ref[...] = _mlp(h, *mlp_refs) if mlp_refs else h


def _gla_post_ffn(h, o, g, p, mlp=None, tm=512):
    m = h.shape[0]
    tm = _row_tile(m, tm)
    row = pl.BlockSpec((tm, D_MODEL), lambda i: (i, 0))
    mlp_ins, mlp_specs = _mlp_operands(mlp) if mlp else ([], [])
    return pl.pallas_call(
        _gla_post_kernel,
        grid=(m // tm,),
        in_specs=[row] * 3 + [_const_spec((1, D_MODEL)), _const_spec((D_MODEL, D_MODEL))] + mlp_specs,
        out_specs=row,
        out_shape=jax.ShapeDtypeStruct((m, D_MODEL), F32),
        compiler_params=_params(("parallel",)),
        name="gla_post_ffn",
    )(h, o, g, jnp.tile(p["norm"], GLA_H).reshape(1, D_MODEL), p["wo"].astype(BF16), *mlp_ins)


def _conv_gates(h_ref, nm_ref, win_ref):
    u = _rms(h_ref[...], nm_ref[...], NORM_EPS).astype(BF16)
    z = jnp.dot(u, win_ref[...], preferred_element_type=F32)
    return z[:, :D_MODEL], z[:, D_MODEL:2 * D_MODEL] * z[:, 2 * D_MODEL:]


def _conv_prompt_kernel(h_ref, nm_ref, win_ref, cw_ref, wo_ref, o_ref, st_ref, carry_ref):
    tm = h_ref.shape[0]

    @pl.when(pl.program_id(1) == 0)
    def _():
        carry_ref[...] = jnp.zeros_like(carry_ref)

    ts = tm // SUBTILES
    subs = [slice(i * ts, (i + 1) * ts) for i in range(SUBTILES)]
    gates = [_conv_gates(h_ref.at[sl], nm_ref, win_ref) for sl in subs]
    row = _iota((ts, D_MODEL), 0)
    prev2, prev1 = carry_ref[0:1, :], carry_ref[1:2, :]
    gated = []
    for g_b, zc in gates:
        z1 = jnp.where(row == 0, prev1, pltpu.roll(zc, 1, axis=0))
        z2 = jnp.where(row == 0, prev2, jnp.where(row == 1, prev1, pltpu.roll(zc, 2, axis=0)))
        gated.append((g_b * (cw_ref[0:1, :] * z2 + cw_ref[1:2, :] * z1 + cw_ref[2:3, :] * zc)).astype(BF16))
        prev2, prev1 = zc[ts - 2:ts - 1, :], zc[ts - 1:ts, :]
    for sl, x in zip(subs, gated):
        o_ref[sl, :] = h_ref[sl, :] + jnp.dot(x, wo_ref[...], preferred_element_type=F32)
    tail = gates[-1][1][ts - (CONV_W - 1):, :]
    carry_ref[0:CONV_W - 1, :] = tail
    st_ref[...] = tail


def _conv_prompt(h, nm, p, tm=1024):
    b, l, _ = h.shape
    tm = _row_tile(l, tm)
    tile = pl.BlockSpec((None, tm, D_MODEL), lambda i, j: (i, j, 0))
    return pl.pallas_call(
        _conv_prompt_kernel,
        grid=(b, l // tm),
        in_specs=[tile, _const_spec((1, D_MODEL)), _const_spec((D_MODEL, 3 * D_MODEL)),
                  _const_spec((CONV_W, D_MODEL)), _const_spec((D_MODEL, D_MODEL))],
        out_specs=[tile, pl.BlockSpec((None, CONV_W - 1, D_MODEL), lambda i, j: (i, 0, 0))],
        out_shape=[jax.ShapeDtypeStruct((b, l, D_MODEL), F32),
                   jax.ShapeDtypeStruct((b, CONV_W - 1, D_MODEL), F32)],
        scratch_shapes=[pltpu.VMEM((SUBLANES, D_MODEL), F32)],
        compiler_params=_params(("parallel", "arbitrary")),
        name="conv_prompt",
    )(h, nm.reshape(1, D_MODEL), p["w_in"].astype(BF16), p["w"], p["wo"].astype(BF16))


def _conv_decode_kernel(h_ref, buf_ref, nm_ref, win_ref, cw_ref, wo_ref, o_ref, st_ref):
    g_b, zc = _conv_gates(h_ref, nm_ref, win_ref)
    z2 = buf_ref[:, 0, :]
    z1 = buf_ref[:, 1, :]
    conv = cw_ref[0:1, :] * z2 + cw_ref[1:2, :] * z1 + cw_ref[2:3, :] * zc
    o_ref[...] = h_ref[...] + _bdot(g_b * conv, wo_ref[...])
    st_ref[:, 0, :] = z1
    st_ref[:, 1, :] = zc


def _conv_decode(h, buf, nm, p):
    m = h.shape[0]
    full = pl.BlockSpec((m, D_MODEL), lambda i: (0, 0))
    bufs = pl.BlockSpec(buf.shape, lambda i: (0, 0, 0))
    return pl.pallas_call(
        _conv_decode_kernel,
        grid=(1,),
        in_specs=[full, bufs, _const_spec((1, D_MODEL)), _const_spec((D_MODEL, 3 * D_MODEL)),
                  _const_spec((CONV_W, D_MODEL)), _const_spec((D_MODEL, D_MODEL))],
        out_specs=[full, bufs],
        out_shape=[jax.ShapeDtypeStruct((m, D_MODEL), F32), jax.ShapeDtypeStruct(buf.shape, F32)],
        compiler_params=_params(("arbitrary",)),
        name="conv_decode",
    )(h, buf, nm.reshape(1, D_MODEL), p["w_in"].astype(BF16), p["w"], p["wo"].astype(BF16))


def _pool_project(d_groups, pw_ref, sc_ref):
    ys = [_bdot(d, pw_ref[gi]) for gi, d in enumerate(d_groups)]
    return jnp.concatenate(ys, axis=1) * sc_ref[...]


def _pool_prompt_kernel(h_ref, nm_ref, pw_ref, sc_ref, nf_ref, wu_ref, wd_ref, gf_ref, o_ref, st_ref, carry_ref):
    tm = h_ref.shape[0]
    hist = carry_ref.shape[0]
    j = pl.program_id(1)

    @pl.when(j == 0)
    def _():
        carry_ref[...] = jnp.zeros_like(carry_ref)

    u = _rms(h_ref[...], nm_ref[...], NORM_EPS)
    ext = jnp.concatenate([carry_ref[...], u], axis=0)
    pos = j * tm + _iota((tm, 1), 0)
    d_groups = []
    for gi, w in enumerate(POOL_WINDOWS):
        sl = slice(gi * POOL_G, (gi + 1) * POOL_G)
        s = ext[:, sl]
        span = 1
        while span < w:
            s = s + pltpu.roll(s, span, axis=0)
            span *= 2
        cnt = jnp.minimum(w, pos + 1).astype(F32)
        d_groups.append(s[hist:, :] / cnt - u[:, sl])
    o_ref[...] = _mlp(h_ref[...] + _pool_project(d_groups, pw_ref, sc_ref), nf_ref, wu_ref, wd_ref, gf_ref)
    tail = u[tm - hist:, :]
    carry_ref[...] = tail
    st_ref[...] = tail


def _pool_prompt_ffn(h, nm, p, mlp, final_g, tm=512):
    b, l, _ = h.shape
    tm = _row_tile(l, tm)
    hist = POOL_BUF + 1
    tile = pl.BlockSpec((None, tm, D_MODEL), lambda i, j: (i, j, 0))
    mlp_ins, mlp_specs = _mlp_operands(mlp, final_g)
    out, st = pl.pallas_call(
        _pool_prompt_kernel,
        grid=(b, l // tm),
        in_specs=[tile, _const_spec((1, D_MODEL)), _const_spec(p["w"].shape), _const_spec((1, D_MODEL))]
        + mlp_specs,
        out_specs=[tile, pl.BlockSpec((None, hist, D_MODEL), lambda i, j: (i, 0, 0))],
        out_shape=[jax.ShapeDtypeStruct((b, l, D_MODEL), F32), jax.ShapeDtypeStruct((b, hist, D_MODEL), F32)],
        scratch_shapes=[pltpu.VMEM((hist, D_MODEL), F32)],
        compiler_params=_params(("parallel", "arbitrary")),
        name="pool_prompt_ffn",
    )(h, nm.reshape(1, D_MODEL), p["w"].astype(BF16), p["scale"].reshape(1, D_MODEL), *mlp_ins)
    return out, st[:, 1:]


def _pool_decode_kernel(h_ref, buf_ref, nm_ref, pw_ref, sc_ref, o_ref, st_ref):
    u = _rms(h_ref[...], nm_ref[...], NORM_EPS)
    d_groups = []
    for gi, w in enumerate(POOL_WINDOWS):
        sl = slice(gi * POOL_G, (gi + 1) * POOL_G)
        s = u[:, sl]
        for i in range(1, w):
            s = s + buf_ref[POOL_BUF - i, :, sl]
        cnt = float(min(w, PAST_LEN + 1))
        d_groups.append(s / cnt - u[:, sl])
    o_ref[...] = h_ref[...] + _pool_project(d_groups, pw_ref, sc_ref)
    st_ref[0:POOL_BUF - 1] = buf_ref[1:POOL_BUF]
    st_ref[POOL_BUF - 1] = u


def _pool_decode(h, buf, nm, p):
    m = h.shape[0]
    full = pl.BlockSpec((m, D_MODEL), lambda i: (0, 0))
    bufs = pl.BlockSpec(buf.shape, lambda i: (0, 0, 0))
    return pl.pallas_call(
        _pool_decode_kernel,
        grid=(1,),
        in_specs=[full, bufs, _const_spec((1, D_MODEL)), _const_spec(p["w"].shape), _const_spec((1, D_MODEL))],
        out_specs=[full, bufs],
        out_shape=[jax.ShapeDtypeStruct((m, D_MODEL), F32), jax.ShapeDtypeStruct(buf.shape, F32)],
        compiler_params=_params(("arbitrary",)),
        name="pool_decode",
    )(h, buf, nm.reshape(1, D_MODEL), p["w"].astype(BF16), p["scale"].reshape(1, D_MODEL))


def _trunk_prompt(x, nm, nfin, mlps, rw, gl, cv, po):
    b, l, _ = x.shape
    flat = lambda t: t.reshape(b * l, t.shape[-1])
    seq = lambda t: t.reshape(b, l, t.shape[-1])

    (r, lw, k, v, kk, bb, g, bonus), shift = _rwkv_pre_prompt(x, nm[0], rw)
    y, wkv = _rwkv_rec_prompt(r, lw, k, v, kk, bb)
    h = _rwkv_post_ffn(flat(x), flat(y), flat(bonus), flat(g), rw, mlps[0])

    q, k, v, g, gk = _gla_pre(h, nm[1], gl)
    o, gla_s = _gla_rec_prompt(seq(q), seq(k), seq(v), seq(gk))
    h = _gla_post_ffn(h, flat(o), g, gl, mlps[1])

    h, conv_s = _conv_prompt(seq(h), nm[2], cv)
    h = _ffn(flat(h), mlps[2])

    y, pool_s = _pool_prompt_ffn(seq(h), nm[3], po, mlps[3], nfin)
    return y, wkv[None], shift[None], gla_s[None], conv_s[None], pool_s[None]


def _trunk_decode(x, st_wkv, st_shift, st_gla, st_conv, st_pool, nm, nf, nfin, w_up, w_down, rw, gl, cv, po):
    m = x.shape[0]
    h = x.reshape(m, D_MODEL)
    mlps = []

    def mlp(h, i, final_g=None):
        h, wub, wdb = _ffn_cast(h, nf[i], w_up, w_down, i, final_g)
        mlps.append((nf[i], wub, wdb))
        return h

    (r, lw, k, v, kk, bb, g, bonus), shift = _rwkv_pre_decode(h, st_shift[0], nm[0], rw)
    y_t, wkv_t = _rwkv_rec_decode(r, lw, k, v, kk, bb, jnp.transpose(st_wkv[0], (1, 2, 3, 0)))
    wkv = jnp.transpose(wkv_t, (3, 0, 1, 2))
    h = mlp(_rwkv_post_ffn(h, y_t.T, bonus, g, rw), 0)

    q, k, v, g, gk = _gla_pre(h, nm[1], gl)
    o, gla_s = _gla_rec_decode(q, k, v, gk, st_gla[0])
    h = mlp(_gla_post_ffn(h, o, g, gl), 1)

    h, conv_s = _conv_decode(h, st_conv[0], nm[2], cv)
    h = mlp(h, 2)

    h, pool_t = _pool_decode(h, jnp.transpose(st_pool[0], (1, 0, 2)), nm[3], po)
    pool_s = jnp.transpose(pool_t, (1, 0, 2))
    y = mlp(h, 3, final_g=nfin)
    return (y.reshape(m, 1, D_MODEL), wkv[None], shift[None], gla_s[None], conv_s[None], pool_s[None]), mlps


def kernel(x_prompt, x_sample, state_rwkv_wkv, state_rwkv_shift, state_gla, state_conv, state_pool, norm_mix, norm_ffn, norm_final, ffn_up, ffn_down, rwkv_mu, rwkv_w_rkv, rwkv_w0, rwkv_w1, rwkv_w2, rwkv_a0, rwkv_a1, rwkv_a2, rwkv_g1, rwkv_g2, rwkv_k_k, rwkv_k_a, rwkv_r_k, rwkv_ln_w, rwkv_ln_b, rwkv_wo, gla_w_in, gla_w_gk2, gla_b_gk, gla_norm, gla_wo, conv_w_in, conv_w, conv_wo, pool_w, pool_scale):
    assert x_prompt.shape[1] % CHUNK == 0 and x_sample.shape[1] == 1
    rw = dict(mu=rwkv_mu[0], w_rkv=rwkv_w_rkv[0], w0=rwkv_w0[0], w1=rwkv_w1[0], w2=rwkv_w2[0], a0=rwkv_a0[0],
              a1=rwkv_a1[0], a2=rwkv_a2[0], g1=rwkv_g1[0], g2=rwkv_g2[0], k_k=rwkv_k_k[0], k_a=rwkv_k_a[0],
              r_k=rwkv_r_k[0], ln_w=rwkv_ln_w[0], ln_b=rwkv_ln_b[0], wo=rwkv_wo[0])
    gl = dict(w_in=gla_w_in[0], w_gk2=gla_w_gk2[0], b_gk=gla_b_gk[0], norm=gla_norm[0], wo=gla_wo[0])
    cv = dict(w_in=conv_w_in[0], w=conv_w[0], wo=conv_wo[0])
    po = dict(w=pool_w[0], scale=pool_scale[0])
    (y_s, wkv_s, sh_s, gla_s, conv_s, pool_s), mlps = _trunk_decode(
        x_sample, state_rwkv_wkv, state_rwkv_shift, state_gla, state_conv, state_pool,
        norm_mix, norm_ffn, norm_final, ffn_up, ffn_down, rw, gl, cv, po)
    y_p, wkv_p, sh_p, gla_p, conv_p, pool_p = _trunk_prompt(x_prompt, norm_mix, norm_final, mlps, rw, gl, cv, po)
    return (y_p, y_s, wkv_p, wkv_s, sh_p, sh_s, gla_p, gla_s, conv_p, conv_s, pool_p, pool_s)
```

```python
import functools
import math

import jax
import jax.numpy as jnp
from jax import lax
from jax.experimental import pallas as pl
from jax.experimental.pallas import tpu as pltpu

F32 = jnp.float32
BF16 = jnp.bfloat16

D_MODEL = 1024
D_FF = 4 * D_MODEL
NORM_EPS = 1e-6
PAST_LEN = 16384

RWKV_HEAD = 64
RWKV_H = D_MODEL // RWKV_HEAD
RWKV_GN_EPS = 64e-5
RWKV_QUAD = 4

GLA_H = 4
GLA_DK_TOT = D_MODEL // 2
GLA_DV_TOT = D_MODEL
GLA_DK = GLA_DK_TOT // GLA_H
GLA_DV = GLA_DV_TOT // GLA_H
GLA_LR = 16
GLA_NORMALIZER = 16.0
GLA_NORM_EPS = 1e-5
GLA_SUB = 16

CONV_W = 3
POOL_WINDOWS = (2, 4, 8, 16)
POOL_G = D_MODEL // len(POOL_WINDOWS)
POOL_BUF = max(POOL_WINDOWS) - 1

LANES = 128
SUBLANES = 8
MXU_DIM = 256
VMEM_LIMIT_BYTES = 56 * 1024 * 1024

CHUNK = 64
FFN_COLS = 1024
SUBTILES = 2


def _const_spec(shape):
    nd = len(shape)
    return pl.BlockSpec(shape, lambda *_: (0,) * nd, pipeline_mode=pl.Buffered(1))


def _params(sem):
    return pltpu.CompilerParams(dimension_semantics=sem, vmem_limit_bytes=VMEM_LIMIT_BYTES)


def _rms(x, g, eps):
    return x * lax.rsqrt(jnp.mean(x * x, axis=-1, keepdims=True) + eps) * g


def _bdot(a, w):
    return jnp.dot(a.astype(BF16), w, preferred_element_type=F32)


def _mm(a, b, prec=None):
    return jnp.dot(a, b, preferred_element_type=F32, precision=prec)


def _nt(a, b, prec=None):
    return lax.dot_general(a, b, (((1,), (1,)), ((), ())), preferred_element_type=F32, precision=prec)


def _tn(a, b, prec=None):
    return lax.dot_general(a, b, (((0,), (0,)), ((), ())), preferred_element_type=F32, precision=prec)


def _iota(shape, dim):
    return lax.broadcasted_iota(jnp.int32, shape, dim)


def _group_sum(x, group, split=False):
    shift = group.bit_length() - 1
    bd = (jnp.right_shift(_iota((MXU_DIM, MXU_DIM), 0), shift)
          == jnp.right_shift(_iota((MXU_DIM, MXU_DIM), 1), shift)).astype(BF16)
    terms = [x.astype(BF16)]
    if split:
        terms.append((x - terms[0].astype(F32)).astype(BF16))
    cols = [sum(_mm(t[:, c * MXU_DIM:(c + 1) * MXU_DIM], bd) for t in terms)
            for c in range(x.shape[1] // MXU_DIM)]
    return jnp.concatenate(cols, axis=1)


def _chunk_cumsum(x):
    c = x.shape[0]
    ltri = (_iota((c, c), 0) >= _iota((c, c), 1)).astype(BF16)
    t0 = x.astype(BF16)
    r1 = x - t0.astype(F32)
    t1 = r1.astype(BF16)
    t2 = (r1 - t1.astype(F32)).astype(BF16)
    return _mm(ltri, t0) + _mm(ltri, t1) + _mm(ltri, t2)


def _softplus(x):
    return jnp.maximum(x, 0.0) + jnp.log(1.0 + jnp.exp(-jnp.abs(x)))


def _row_tile(m, want):
    t = min(want, m)
    assert m % t == 0, (m, t)
    return t


def _mlp(h, g_ref, wu_ref, wd_ref, gf_ref=None):
    x = _rms(h, g_ref[...], NORM_EPS).astype(BF16)
    acc = h
    for c in range(D_FF // FFN_COLS):
        sl = slice(c * FFN_COLS, (c + 1) * FFN_COLS)
        a = jnp.dot(x, wu_ref[:, sl], preferred_element_type=F32)
        a = jnp.square(jnp.maximum(a, 0.0)).astype(BF16)
        acc = acc + jnp.dot(a, wd_ref[sl, :], preferred_element_type=F32)
    if gf_ref is not None:
        acc = _rms(acc, gf_ref[...], NORM_EPS)
    return acc


def _mlp_operands(mlp, final_g=None):
    nf, wu, wd = mlp
    ins = [nf.reshape(1, D_MODEL), wu, wd]
    specs = [_const_spec((1, D_MODEL)), _const_spec(wu.shape), _const_spec(wd.shape)]
    if final_g is not None:
        ins.append(final_g.reshape(1, D_MODEL))
        specs.append(_const_spec((1, D_MODEL)))
    return ins, specs


def _ffn_kernel(h_ref, g_ref, wu_ref, wd_ref, *rest):
    *gf_ref, o_ref = rest
    o_ref[...] = _mlp(h_ref[...], g_ref, wu_ref, wd_ref, *gf_ref)


def _ffn(h, mlp, final_g=None, tm=1024):
    m = h.shape[0]
    tm = _row_tile(m, tm)
    row = pl.BlockSpec((tm, D_MODEL), lambda i: (i, 0))
    mlp_ins, mlp_specs = _mlp_operands(mlp, final_g)
    return pl.pallas_call(
        _ffn_kernel,
        grid=(m // tm,),
        in_specs=[row] + mlp_specs,
        out_specs=row,
        out_shape=jax.ShapeDtypeStruct((m, D_MODEL), F32),
        compiler_params=_params(("parallel",)),
        name="ffn",
    )(h, *mlp_ins)


def _ffn_cast_kernel(h_ref, g_ref, wu_ref, wd_ref, *rest):
    *gf_ref, o_ref, wub_ref, wdb_ref, x_scr, acc_scr = rest
    c = pl.program_id(0)

    @pl.when(c == 0)
    def _():
        h = h_ref[...]
        x_scr[...] = _rms(h, g_ref[...], NORM_EPS).astype(BF16)
        acc_scr[...] = h

    wu = wu_ref[...].astype(BF16)
    wd = wd_ref[...].astype(BF16)
    wub_ref[...] = wu
    wdb_ref[...] = wd
    a = jnp.dot(x_scr[...], wu, preferred_element_type=F32)
    a = jnp.square(jnp.maximum(a, 0.0)).astype(BF16)
    acc_scr[...] += jnp.dot(a, wd, preferred_element_type=F32)

    @pl.when(c == pl.num_programs(0) - 1)
    def _():
        acc = acc_scr[...]
        o_ref[...] = _rms(acc, gf_ref[0][...], NORM_EPS) if gf_ref else acc


def _ffn_cast(h, nf, w_up, w_down, layer, final_g=None, cols=FFN_COLS):
    m = h.shape[0]
    full = pl.BlockSpec((m, D_MODEL), lambda c: (0, 0))
    vec = pl.BlockSpec((1, D_MODEL), lambda c: (0, 0))
    ins = [h, nf.reshape(1, D_MODEL), w_up, w_down]
    specs = [full, vec, pl.BlockSpec((None, D_MODEL, cols), lambda c: (layer, 0, c)),
             pl.BlockSpec((None, cols, D_MODEL), lambda c: (layer, c, 0))]
    if final_g is not None:
        ins.append(final_g.reshape(1, D_MODEL))
        specs.append(vec)
    return pl.pallas_call(
        _ffn_cast_kernel,
        grid=(D_FF // cols,),
        in_specs=specs,
        out_specs=[full, pl.BlockSpec((D_MODEL, cols), lambda c: (0, c)),
                   pl.BlockSpec((cols, D_MODEL), lambda c: (c, 0))],
        out_shape=[jax.ShapeDtypeStruct((m, D_MODEL), F32), jax.ShapeDtypeStruct((D_MODEL, D_FF), BF16),
                   jax.ShapeDtypeStruct((D_FF, D_MODEL), BF16)],
        scratch_shapes=[pltpu.VMEM((m, D_MODEL), BF16), pltpu.VMEM((m, D_MODEL), F32)],
        compiler_params=_params(("arbitrary",)),
        name="ffn_cast",
    )(*ins)


def _rwkv_pre_math(u, prev, mu_ref, wrkv_ref, w0_ref, w1_ref, w2_ref, a0_ref, a1_ref, a2_ref,
                   g1_ref, g2_ref, kk_ref, ka_ref, rk_ref):
    dx = prev - u

    def mix(i):
        return (u + dx * mu_ref[i:i + 1, :]).astype(BF16)

    r = _bdot(mix(0), wrkv_ref[0])
    w_mid = _bdot(mix(1), w1_ref[...])
    a_mid = _bdot(mix(4), a1_ref[...])
    g_mid = _bdot(mix(5), g1_ref[...])
    k = _bdot(mix(2), wrkv_ref[1])
    wl = w0_ref[...] + _bdot(jnp.tanh(w_mid), w2_ref[...])
    a = jax.nn.sigmoid(a0_ref[...] + _bdot(a_mid, a2_ref[...]))
    g = _bdot(jax.nn.sigmoid(g_mid), g2_ref[...])
    lw = jax.nn.sigmoid(wl) * (-math.exp(-0.5))
    kk = k * kk_ref[...]
    kk_ss = _group_sum(kk * kk, RWKV_HEAD)
    v = _bdot(mix(3), wrkv_ref[2])
    k = k * (1.0 + (a - 1.0) * ka_ref[...])
    rk_sum = _group_sum(r * k * rk_ref[...], RWKV_HEAD)
    kk = kk * lax.rsqrt(jnp.maximum(kk_ss, 1e-24))
    return r, lw, k, v, kk, kk * a, g, rk_sum * v


_N_RWKV_W = 13


def _rwkv_pre_prompt_kernel(h_ref, nm_ref, *rest):
    w = rest[:_N_RWKV_W]
    outs = rest[_N_RWKV_W:_N_RWKV_W + 8]
    sh_ref, carry_ref = rest[_N_RWKV_W + 8:]
    tm = h_ref.shape[0]

    @pl.when(pl.program_id(1) == 0)
    def _():
        carry_ref[...] = jnp.zeros_like(carry_ref)

    u = _rms(h_ref[...], nm_ref[...], NORM_EPS)
    prev = jnp.where(_iota(u.shape, 0) == 0, carry_ref[0:1, :], pltpu.roll(u, 1, axis=0))
    last = u[tm - 1:tm, :]
    carry_ref[0:1, :] = last
    sh_ref[...] = last
    for o_ref, val in zip(outs, _rwkv_pre_math(u, prev, *w)):
        o_ref[...] = val


def _rwkv_pre_decode_kernel(h_ref, prev_ref, nm_ref, *rest):
    w = rest[:_N_RWKV_W]
    outs = rest[_N_RWKV_W:_N_RWKV_W + 8]
    (sh_ref,) = rest[_N_RWKV_W + 8:]
    u = _rms(h_ref[...], nm_ref[...], NORM_EPS)
    sh_ref[...] = u
    vals = _rwkv_pre_math(u, prev_ref[...], *w)
    for o_ref, val in zip(outs[:6], vals[:6]):
        o_ref[...] = val.T
    for o_ref, val in zip(outs[6:], vals[6:]):
        o_ref[...] = val


def _rwkv_weights(p):
    row = lambda x: x.reshape(1, D_MODEL)
    return [p["mu"], p["w_rkv"].astype(BF16), row(p["w0"]), p["w1"].astype(BF16), p["w2"].astype(BF16),
            row(p["a0"]), p["a1"].astype(BF16), p["a2"].astype(BF16), p["g1"].astype(BF16),
            p["g2"].astype(BF16), row(p["k_k"]), row(p["k_a"]), row(p["r_k"])]


def _rwkv_pre_prompt(h, nm, p, tm=512):
    b, l, _ = h.shape
    tm = _row_tile(l, tm)
    ws = _rwkv_weights(p)
    tile = pl.BlockSpec((None, tm, D_MODEL), lambda i, j: (i, j, 0))
    outs = pl.pallas_call(
        _rwkv_pre_prompt_kernel,
        grid=(b, l // tm),
        in_specs=[tile, _const_spec((1, D_MODEL))] + [_const_spec(w.shape) for w in ws],
        out_specs=[tile] * 8 + [pl.BlockSpec((None, 1, D_MODEL), lambda i, j: (i, 0, 0))],
        out_shape=[jax.ShapeDtypeStruct((b, l, D_MODEL), F32)] * 8
        + [jax.ShapeDtypeStruct((b, 1, D_MODEL), F32)],
        scratch_shapes=[pltpu.VMEM((SUBLANES, D_MODEL), F32)],
        compiler_params=_params(("parallel", "arbitrary")),
        name="rwkv_pre_prompt",
    )(h, nm.reshape(1, D_MODEL), *ws)
    return outs[:8], outs[8].reshape(b, D_MODEL)


def _rwkv_pre_decode(h, prev, nm, p):
    m = h.shape[0]
    ws = _rwkv_weights(p)
    full = pl.BlockSpec((m, D_MODEL), lambda i: (0, 0))
    full_t = pl.BlockSpec((D_MODEL, m), lambda i: (0, 0))
    outs = pl.pallas_call(
        _rwkv_pre_decode_kernel,
        grid=(1,),
        in_specs=[full, full, _const_spec((1, D_MODEL))] + [_const_spec(w.shape) for w in ws],
        out_specs=[full_t] * 6 + [full] * 3,
        out_shape=[jax.ShapeDtypeStruct((D_MODEL, m), F32)] * 6 + [jax.ShapeDtypeStruct((m, D_MODEL), F32)] * 3,
        compiler_params=_params(("arbitrary",)),
        name="rwkv_pre_decode",
    )(h, prev, nm.reshape(1, D_MODEL), *ws)
    return outs[:8], outs[8]


def _rwkv_rec_prompt_kernel(r_ref, lw_ref, k_ref, v_ref, kk_ref, bb_ref, y_ref, so_ref, s_scr):
    tc = r_ref.shape[0]
    qw = RWKV_QUAD * RWKV_HEAD
    n_quads = D_MODEL // qw
    assert RWKV_QUAD * CHUNK == qw

    @pl.when(pl.program_id(1) == 0)
    def _():
        s_scr[...] = jnp.zeros_like(s_scr)

    t_idx, s_idx = _iota((CHUNK, qw), 0), _iota((CHUNK, qw), 1) & (CHUNK - 1)
    strict = t_idx > s_idx
    incl = t_idx >= s_idx
    eye = (t_idx == s_idx).astype(F32)
    blocks = jnp.right_shift(_iota((qw, qw), 0), 6) == jnp.right_shift(_iota((qw, qw), 1), 6)

    def bdiag(z):
        zb = z.astype(BF16)
        return jnp.where(blocks, jnp.concatenate([zb] * RWKV_QUAD, axis=0), jnp.zeros((), BF16))

    quads = range(n_quads)
    sls = [slice(q * qw, (q + 1) * qw) for q in quads]
    cc = CHUNK

    n_chunks = tc // CHUNK
    items = [(c, q) for c in range(n_chunks) for q in quads]
    every = range(len(items))

    scaled = []
    for c in range(n_chunks):
        rows = slice(c * CHUNK, (c + 1) * CHUNK)
        lw = lw_ref[rows, :]
        cum = _chunk_cumsum(lw)
        g_in = jnp.exp(cum)
        g_inv = jnp.exp(-cum)
        g_last = g_in[CHUNK - 1:CHUNK, :]
        scaled.append(dict(
            rt=r_ref[rows, :] * g_in, at=-kk_ref[rows, :] * jnp.exp(cum - lw),
            bt=bb_ref[rows, :] * g_inv, kt=k_ref[rows, :] * g_inv, v=v_ref[rows, :], g_last=g_last))

    def part(name, i):
        c, q = items[i]
        return scaled[c][name][:, sls[q]]

    ar = [jnp.concatenate([part("at", i), part("rt", i)], axis=0).astype(BF16) for i in every]
    bk4 = [jnp.concatenate([bdiag(part("bt", i)), bdiag(part("kt", i))], axis=0) for i in every]
    gram = [_nt(ar[i], bk4[i]) for i in every]
    a_ab = [jnp.where(strict, gram[i][:cc, :qw], 0.0) for i in every]
    a_k = [jnp.concatenate([jnp.where(strict, gram[i][:cc, qw:], 0.0),
                            jnp.where(incl, gram[i][cc:, qw:], 0.0)], axis=0).astype(BF16) for i in every]
    a_rb = [jnp.where(incl, gram[i][cc:, :qw], 0.0).astype(BF16) for i in every]
    akv = [_mm(a_k[i], bdiag(part("v", i))) for i in every]
    inv = [eye + a_ab[i] for i in every]
    pw = [_mm(a_ab[i].astype(BF16), bdiag(a_ab[i])) for i in every]
    for _ in range(1, CHUNK.bit_length() - 2):
        both = [_mm(jnp.concatenate([pw[i], inv[i]], axis=0).astype(BF16), bdiag(pw[i])) for i in every]
        inv = [inv[i] + both[i][cc:] for i in every]
        pw = [both[i][:cc] for i in every]
    inv = [(inv[i] + _mm(inv[i].astype(BF16), bdiag(pw[i]))).astype(BF16) for i in every]
    tt = [_mm(inv[i], jnp.concatenate([bdiag(part("at", i)), bdiag(akv[i][:cc])], axis=1)) for i in every]
    ta = [tt[i][:, :qw].astype(BF16) for i in every]
    tav = [tt[i][:, qw:] for i in every]
    bkg = [(jnp.concatenate([part("bt", i), part("kt", i)], axis=0) * part("g_last", i)).astype(BF16)
           for i in every]
    v_b = [part("v", i).astype(BF16) for i in every]

    xs, u = [], []

    def emit_outputs(c):
        for i in range(c * n_quads, (c + 1) * n_quads):
            y_ref[c * CHUNK:(c + 1) * CHUNK, sls[items[i][1]]] = xs[i][cc:] + akv[i][cc:] + _mm(a_rb[i], bdiag(u[i]))

    state = [s_scr[q] for q in quads]
    for c in range(n_chunks):
        mine = [c * n_quads + q for q in quads]
        xs += [_nt(jnp.concatenate([ta[i], ar[i][cc:]], axis=0), state[q].astype(BF16))
               for q, i in enumerate(mine)]
        u += [xs[i][:cc] + tav[i] for i in mine]
        if c > 0:
            emit_outputs(c - 1)
        state = [state[q] * scaled[c]["g_last"][:, sls[q]]
                 + jnp.where(blocks, _tn(jnp.concatenate([u[i].astype(BF16), v_b[i]], axis=0), bkg[i]), 0.0)
                 for q, i in enumerate(mine)]
    emit_outputs(n_chunks - 1)
    for q in quads:
        s_scr[q] = state[q]

    @pl.when(pl.program_id(1) == pl.num_programs(1) - 1)
    def _():
        for hd in range(RWKV_H):
            q, o = divmod(hd, RWKV_QUAD)
            so_ref[hd] = s_scr[q][o * RWKV_HEAD:(o + 1) * RWKV_HEAD, o * RWKV_HEAD:(o + 1) * RWKV_HEAD]


def _rwkv_rec_prompt(r, lw, k, v, kk, bb, tc=512):
    b, l, _ = r.shape
    tc = _row_tile(l, tc)
    tile = pl.BlockSpec((None, tc, D_MODEL), lambda i, j: (i, j, 0))
    st = (RWKV_H, RWKV_HEAD, RWKV_HEAD)
    return pl.pallas_call(
        _rwkv_rec_prompt_kernel,
        grid=(b, l // tc),
        in_specs=[tile] * 6,
        out_specs=[tile, pl.BlockSpec((None,) + st, lambda i, j: (i, 0, 0, 0))],
        out_shape=[jax.ShapeDtypeStruct((b, l, D_MODEL), F32), jax.ShapeDtypeStruct((b,) + st, F32)],
        scratch_shapes=[pltpu.VMEM((RWKV_H // RWKV_QUAD, RWKV_QUAD * RWKV_HEAD, RWKV_QUAD * RWKV_HEAD), F32)],
        compiler_params=_params(("parallel", "arbitrary")),
        name="rwkv_rec_prompt",
    )(r, lw, k, v, kk, bb)


def _rwkv_rec_decode_kernel(r_ref, lw_ref, k_ref, v_ref, kk_ref, bb_ref, s_ref, y_ref, so_ref):
    w = jnp.exp(lw_ref[...])
    a = -kk_ref[...]
    b, k, r = bb_ref[...], k_ref[...], r_ref[...]

    def row(vi, carry):
        s0 = s_ref[vi]
        sa = jnp.sum(s0 * a, axis=0, keepdims=True)
        s1 = s0 * w + sa * b + v_ref[pl.ds(vi, 1), :] * k
        so_ref[vi] = s1
        y_ref[pl.ds(vi, 1), :] = jnp.sum(s1 * r, axis=0, keepdims=True)
        return carry

    lax.fori_loop(0, RWKV_HEAD, row, 0, unroll=8)


def _rwkv_rec_decode(r, lw, k, v, kk, bb, s):
    m = r.shape[1]
    vec = pl.BlockSpec((RWKV_HEAD, m), lambda i: (i, 0))
    st = pl.BlockSpec((None, RWKV_HEAD, RWKV_HEAD, m), lambda i: (i, 0, 0, 0))
    return pl.pallas_call(
        _rwkv_rec_decode_kernel,
        grid=(RWKV_H,),
        in_specs=[vec] * 6 + [st],
        out_specs=[vec, st],
        out_shape=[jax.ShapeDtypeStruct((D_MODEL, m), F32), jax.ShapeDtypeStruct(s.shape, F32)],
        compiler_params=_params(("parallel",)),
        name="rwkv_rec_decode",
    )(r, lw, k, v, kk, bb, s)


def _rwkv_post_kernel(h_ref, y_ref, bo_ref, g_ref, lnw_ref, lnb_ref, wo_ref, *rest):
    *mlp_refs, o_ref = rest
    y = y_ref[...]
    d = y - _group_sum(y, RWKV_HEAD, split=True) * (1.0 / RWKV_HEAD)
    var = _group_sum(d * d, RWKV_HEAD) * (1.0 / RWKV_HEAD)
    yn = d * lax.rsqrt(var + RWKV_GN_EPS) * lnw_ref[...] + lnb_ref[...]
    h = h_ref[...] + _bdot((yn + bo_ref[...]) * g_ref[...], wo_ref[...])
    o_ref[...] = _mlp(h, *mlp_refs) if mlp_refs else h


def _rwkv_post_ffn(h, y, bonus, g, p, mlp=None, tm=512):
    m = h.shape[0]
    tm = _row_tile(m, tm)
    row = pl.BlockSpec((tm, D_MODEL), lambda i: (i, 0))
    vec = _const_spec((1, D_MODEL))
    mlp_ins, mlp_specs = _mlp_operands(mlp) if mlp else ([], [])
    return pl.pallas_call(
        _rwkv_post_kernel,
        grid=(m // tm,),
        in_specs=[row] * 4 + [vec, vec, _const_spec((D_MODEL, D_MODEL))] + mlp_specs,
        out_specs=row,
        out_shape=jax.ShapeDtypeStruct((m, D_MODEL), F32),
        compiler_params=_params(("parallel",)),
        name="rwkv_post_ffn",
    )(h, y, bonus, g, p["ln_w"].reshape(1, D_MODEL), p["ln_b"].reshape(1, D_MODEL), p["wo"].astype(BF16),
      *mlp_ins)


def _gla_pre_kernel(h_ref, nm_ref, win_ref, wgl_ref, wgk_ref, bgk_ref, q_ref, k_ref, v_ref, g_ref, gk_ref):
    u = _rms(h_ref[...], nm_ref[...], NORM_EPS).astype(BF16)
    z = _nt(u, win_ref[...])
    q_ref[...] = z[:, :GLA_DK_TOT] * (GLA_DK ** -0.5)
    k_ref[...] = z[:, GLA_DK_TOT:2 * GLA_DK_TOT]
    v_ref[...] = z[:, 2 * GLA_DK_TOT:2 * GLA_DK_TOT + GLA_DV_TOT]
    g_ref[...] = z[:, 2 * GLA_DK_TOT + GLA_DV_TOT:]
    gl = _nt(u, wgl_ref[...])
    pre = _bdot(gl, wgk_ref[...]) + bgk_ref[...]
    gk_ref[...] = -_softplus(-pre) * (1.0 / GLA_NORMALIZER)


def _gla_pre(h, nm, p, tm=1024):
    m = h.shape[0]
    tm = _row_tile(m, tm)
    n_main = 2 * GLA_DK_TOT + 2 * GLA_DV_TOT
    w_t = p["w_in"].T
    w_main = w_t[:n_main].astype(BF16)
    w_gl = jnp.pad(w_t[n_main:].astype(BF16), ((0, LANES - GLA_LR), (0, 0)))
    w_gk = jnp.pad(p["w_gk2"].astype(BF16), ((0, LANES - GLA_LR), (0, 0)))
    row = lambda n: pl.BlockSpec((tm, n), lambda i: (i, 0))
    widths = (GLA_DK_TOT, GLA_DK_TOT, GLA_DV_TOT, GLA_DV_TOT, GLA_DK_TOT)
    return pl.pallas_call(
        _gla_pre_kernel,
        grid=(m // tm,),
        in_specs=[row(D_MODEL), _const_spec((1, D_MODEL)), _const_spec(w_main.shape), _const_spec(w_gl.shape),
                  _const_spec(w_gk.shape), _const_spec((1, GLA_DK_TOT))],
        out_specs=[row(n) for n in widths],
        out_shape=[jax.ShapeDtypeStruct((m, n), F32) for n in widths],
        compiler_params=_params(("parallel",)),
        name="gla_pre",
    )(h, nm.reshape(1, D_MODEL), w_main, w_gl, w_gk, p["b_gk"].reshape(1, GLA_DK_TOT))


def _gla_rec_prompt_kernel(q_ref, k_ref, v_ref, gk_ref, o_ref, so_ref, s_scr):
    tc = q_ref.shape[0]

    @pl.when(pl.program_id(1) == 0)
    def _():
        s_scr[...] = jnp.zeros_like(s_scr)

    ri, ci = _iota((CHUNK, CHUNK), 0), _iota((CHUNK, CHUNK), 1)
    incl = ri >= ci
    n_sub = CHUNK // GLA_SUB
    sub_of_row = jnp.right_shift(_iota((CHUNK, 1), 0), GLA_SUB.bit_length() - 1)

    heads = range(GLA_H)
    ks = [slice(hd * GLA_DK, (hd + 1) * GLA_DK) for hd in heads]
    vs = [slice(hd * GLA_DV, (hd + 1) * GLA_DV) for hd in heads]
    n_chunks = tc // CHUNK
    items = [(c, hd) for c in range(n_chunks) for hd in heads]

    prep = []
    for c in range(n_chunks):
        rows = slice(c * CHUNK, (c + 1) * CHUNK)
        cum = _chunk_cumsum(gk_ref[rows, :])
        q, k = q_ref[rows, :], k_ref[rows, :]
        last = cum[CHUNK - 1:CHUNK, :]
        refs = [jnp.zeros_like(last)] + [cum[i * GLA_SUB - 1:i * GLA_SUB, :] for i in range(1, n_sub)]
        ref_of_row = refs[0]
        for i in range(1, n_sub):
            ref_of_row = jnp.where(sub_of_row >= i, refs[i], ref_of_row)
        q_in = (q * jnp.exp(cum - ref_of_row)).astype(BF16)
        k_in = []
        for i in range(n_sub):
            n = (i + 1) * GLA_SUB
            part = (k[:n] * jnp.exp(refs[i] - cum[:n])).astype(BF16)
            k_in.append(part if n == CHUNK else
                        jnp.concatenate([part, jnp.zeros((CHUNK - n, part.shape[1]), BF16)], axis=0))
        prep.append(dict(q_in=q_in, k_in=k_in, qe=(q * jnp.exp(cum)).astype(BF16),
                         kh=(k * jnp.exp(last - cum)).astype(BF16), g_last=jnp.exp(last),
                         v=v_ref[rows, :].astype(BF16)))

    def scores(c, hd):
        p = prep[c]
        blocks = [_nt(p["q_in"][i * GLA_SUB:(i + 1) * GLA_SUB, ks[hd]], p["k_in"][i][:, ks[hd]])
                  for i in range(n_sub)]
        return jnp.where(incl, jnp.concatenate(blocks, axis=0), 0.0).astype(BF16)

    att = [scores(c, hd) for c, hd in items]
    intra = [_mm(att[i], prep[c]["v"][:, vs[hd]]) for i, (c, hd) in enumerate(items)]
    kv = [_tn(prep[c]["v"][:, vs[hd]], prep[c]["kh"][:, ks[hd]]) for c, hd in items]

    s_in = []
    state = [s_scr[hd] for hd in heads]
    for c in range(n_chunks):
        s_in.extend(state[hd].astype(BF16) for hd in heads)
        state = [state[hd] * prep[c]["g_last"][:, ks[hd]] + kv[c * GLA_H + hd] for hd in heads]
    for hd in heads:
        s_scr[hd] = state[hd]

    for i, (c, hd) in enumerate(items):
        o_ref[c * CHUNK:(c + 1) * CHUNK, vs[hd]] = _nt(prep[c]["qe"][:, ks[hd]], s_in[i]) + intra[i]

    @pl.when(pl.program_id(1) == pl.num_programs(1) - 1)
    def _():
        for hd in range(GLA_H):
            so_ref[hd] = s_scr[hd].T


def _gla_rec_prompt(q, k, v, gk, tc=1024):
    b, l, _ = q.shape
    tc = _row_tile(l, tc)
    tile = lambda n: pl.BlockSpec((None, tc, n), lambda i, j: (i, j, 0))
    st = (GLA_H, GLA_DK, GLA_DV)
    return pl.pallas_call(
        _gla_rec_prompt_kernel,
        grid=(b, l // tc),
        in_specs=[tile(GLA_DK_TOT), tile(GLA_DK_TOT), tile(GLA_DV_TOT), tile(GLA_DK_TOT)],
        out_specs=[tile(GLA_DV_TOT), pl.BlockSpec((None,) + st, lambda i, j: (i, 0, 0, 0))],
        out_shape=[jax.ShapeDtypeStruct((b, l, GLA_DV_TOT), F32), jax.ShapeDtypeStruct((b,) + st, F32)],
        scratch_shapes=[pltpu.VMEM((GLA_H, GLA_DV, GLA_DK), F32)],
        compiler_params=_params(("parallel", "arbitrary")),
        name="gla_rec_prompt",
    )(q, k, v, gk)


def _gla_rec_decode_kernel(q_ref, k_ref, v_ref, gk_ref, s_ref, o_ref, so_ref):
    eye = (_iota((GLA_DK, GLA_DK), 0) == _iota((GLA_DK, GLA_DK), 1)).astype(F32)[None]

    def col(x):
        return jnp.sum(eye * x, axis=-1, keepdims=True)

    for hd in range(GLA_H):
        hs = slice(hd, hd + 1)
        s1 = s_ref[:, hd] * col(jnp.exp(gk_ref[:, hs, :])) + col(k_ref[:, hs, :]) * v_ref[:, hs, :]
        so_ref[:, hd] = s1
        o_ref[:, hs, :] = jnp.sum(col(q_ref[:, hs, :]) * s1, axis=1, keepdims=True)


def _gla_rec_decode(q, k, v, gk, s, bblk=16):
    m = q.shape[0]
    kvec = pl.BlockSpec((bblk, GLA_H, GLA_DK), lambda i: (i, 0, 0))
    vvec = pl.BlockSpec((bblk, GLA_H, GLA_DV), lambda i: (i, 0, 0))
    st = pl.BlockSpec((bblk, GLA_H, GLA_DK, GLA_DV), lambda i: (i, 0, 0, 0))
    hk = lambda x: x.reshape(m, GLA_H, GLA_DK)
    o, s_new = pl.pallas_call(
        _gla_rec_decode_kernel,
        grid=(m // bblk,),
        in_specs=[kvec, kvec, vvec, kvec, st],
        out_specs=[vvec, st],
        out_shape=[jax.ShapeDtypeStruct((m, GLA_H, GLA_DV), F32), jax.ShapeDtypeStruct(s.shape, F32)],
        compiler_params=_params(("parallel",)),
        name="gla_rec_decode",
    )(hk(q), hk(k), v.reshape(m, GLA_H, GLA_DV), hk(gk), s)
    return o.reshape(m, GLA_DV_TOT), s_new


def _gla_post_kernel(h_ref, o_ref_in, g_ref, nw_ref, wo_ref, *rest):
    *mlp_refs, out_ref = rest
    o = o_ref_in[...]
    parts = []
    for hd in range(GLA_H):
        oh = o[:, hd * GLA_DV:(hd + 1) * GLA_DV]
        parts.append(oh * lax.rsqrt(jnp.mean(oh * oh, axis=-1, keepdims=True) + GLA_NORM_EPS))
    on = jnp.concatenate(parts, axis=1) * nw_ref[...]
    g = g_ref[...]
    h = h_ref[...] + _bdot(on * (g * jax.nn.sigmoid(g)), wo_ref[...])
    out_ref[...] = _mlp(h, *mlp_refs) if mlp_refs else h


def _gla_post_ffn(h, o, g, p, mlp=None, tm=512):
    m = h.shape[0]
    tm = _row_tile(m, tm)
    row = pl.BlockSpec((tm, D_MODEL), lambda i: (i, 0))
    mlp_ins, mlp_specs = _mlp_operands(mlp) if mlp else ([], [])
    return pl.pallas_call(
        _gla_post_kernel,
        grid=(m // tm,),
        in_specs=[row] * 3 + [_const_spec((1, D_MODEL)), _const_spec((D_MODEL, D_MODEL))] + mlp_specs,
        out_specs=row,
        out_shape=jax.ShapeDtypeStruct((m, D_MODEL), F32),
        compiler_params=_params(("parallel",)),
        name="gla_post_ffn",
    )(h, o, g, jnp.tile(p["norm"], GLA_H).reshape(1, D_MODEL), p["wo"].astype(BF16), *mlp_ins)


def _conv_gates(h_ref, nm_ref, win_ref):
    u = _rms(h_ref[...], nm_ref[...], NORM_EPS).astype(BF16)
    z = jnp.dot(u, win_ref[...], preferred_element_type=F32)
    return z[:, :D_MODEL], z[:, D_MODEL:2 * D_MODEL] * z[:, 2 * D_MODEL:]


def _conv_prompt_kernel(h_ref, nm_ref, win_ref, cw_ref, wo_ref, o_ref, st_ref, carry_ref):
    tm = h_ref.shape[0]

    @pl.when(pl.program_id(1) == 0)
    def _():
        carry_ref[...] = jnp.zeros_like(carry_ref)

    ts = tm // SUBTILES
    subs = [slice(i * ts, (i + 1) * ts) for i in range(SUBTILES)]
    gates = [_conv_gates(h_ref.at[sl], nm_ref, win_ref) for sl in subs]
    row = _iota((ts, D_MODEL), 0)
    prev2, prev1 = carry_ref[0:1, :], carry_ref[1:2, :]
    gated = []
    for g_b, zc in gates:
        z1 = jnp.where(row == 0, prev1, pltpu.roll(zc, 1, axis=0))
        z2 = jnp.where(row == 0, prev2, jnp.where(row == 1, prev1, pltpu.roll(zc, 2, axis=0)))
        gated.append((g_b * (cw_ref[0:1, :] * z2 + cw_ref[1:2, :] * z1 + cw_ref[2:3, :] * zc)).astype(BF16))
        prev2, prev1 = zc[ts - 2:ts - 1, :], zc[ts - 1:ts, :]
    for sl, x in zip(subs, gated):
        o_ref[sl, :] = h_ref[sl, :] + jnp.dot(x, wo_ref[...], preferred_element_type=F32)
    tail = gates[-1][1][ts - (CONV_W - 1):, :]
    carry_ref[0:CONV_W - 1, :] = tail
    st_ref[...] = tail


def _conv_prompt(h, nm, p, tm=1024):
    b, l, _ = h.shape
    tm = _row_tile(l, tm)
    tile = pl.BlockSpec((None, tm, D_MODEL), lambda i, j: (i, j, 0))
    return pl.pallas_call(
        _conv_prompt_kernel,
        grid=(b, l // tm),
        in_specs=[tile, _const_spec((1, D_MODEL)), _const_spec((D_MODEL, 3 * D_MODEL)),
                  _const_spec((CONV_W, D_MODEL)), _const_spec((D_MODEL, D_MODEL))],
        out_specs=[tile, pl.BlockSpec((None, CONV_W - 1, D_MODEL), lambda i, j: (i, 0, 0))],
        out_shape=[jax.ShapeDtypeStruct((b, l, D_MODEL), F32),
                   jax.ShapeDtypeStruct((b, CONV_W - 1, D_MODEL), F32)],
        scratch_shapes=[pltpu.VMEM((SUBLANES, D_MODEL), F32)],
        compiler_params=_params(("parallel", "arbitrary")),
        name="conv_prompt",
    )(h, nm.reshape(1, D_MODEL), p["w_in"].astype(BF16), p["w"], p["wo"].astype(BF16))


def _conv_decode_kernel(h_ref, buf_ref, nm_ref, win_ref, cw_ref, wo_ref, o_ref, st_ref):
    g_b, zc = _conv_gates(h_ref, nm_ref, win_ref)
    z2 = buf_ref[:, 0, :]
    z1 = buf_ref[:, 1, :]
    conv = cw_ref[0:1, :] * z2 + cw_ref[1:2, :] * z1 + cw_ref[2:3, :] * zc
    o_ref[...] = h_ref[...] + _bdot(g_b * conv, wo_ref[...])
    st_ref[:, 0, :] = z1
    st_ref[:, 1, :] = zc


def _conv_decode(h, buf, nm, p):
    m = h.shape[0]
    full = pl.BlockSpec((m, D_MODEL), lambda i: (0, 0))
    bufs = pl.BlockSpec(buf.shape, lambda i: (0, 0, 0))
    return pl.pallas_call(
        _conv_decode_kernel,
        grid=(1,),
        in_specs=[full, bufs, _const_spec((1, D_MODEL)), _const_spec((D_MODEL, 3 * D_MODEL)),
                  _const_spec((CONV_W, D_MODEL)), _const_spec((D_MODEL, D_MODEL))],
        out_specs=[full, bufs],
        out_shape=[jax.ShapeDtypeStruct((m, D_MODEL), F32), jax.ShapeDtypeStruct(buf.shape, F32)],
        compiler_params=_params(("arbitrary",)),
        name="conv_decode",
    )(h, buf, nm.reshape(1, D_MODEL), p["w_in"].astype(BF16), p["w"], p["wo"].astype(BF16))


def _pool_project(d_groups, pw_ref, sc_ref):
    ys = [_bdot(d, pw_ref[gi]) for gi, d in enumerate(d_groups)]
    return jnp.concatenate(ys, axis=1) * sc_ref[...]


def _pool_prompt_kernel(h_ref, nm_ref, pw_ref, sc_ref, nf_ref, wu_ref, wd_ref, gf_ref, o_ref, st_ref, carry_ref):
    tm = h_ref.shape[0]
    hist = carry_ref.shape[0]
    j = pl.program_id(1)

    @pl.when(j == 0)
    def _():
        carry_ref[...] = jnp.zeros_like(carry_ref)

    u = _rms(h_ref[...], nm_ref[...], NORM_EPS)
    ext = jnp.concatenate([carry_ref[...], u], axis=0)
    pos = j * tm + _iota((tm, 1), 0)
    d_groups = []
    for gi, w in enumerate(POOL_WINDOWS):
        sl = slice(gi * POOL_G, (gi + 1) * POOL_G)
        s = ext[:, sl]
        span = 1
        while span < w:
            s = s + pltpu.roll(s, span, axis=0)
            span *= 2
        cnt = jnp.minimum(w, pos + 1).astype(F32)
        d_groups.append(s[hist:, :] / cnt - u[:, sl])
    o_ref[...] = _mlp(h_ref[...] + _pool_project(d_groups, pw_ref, sc_ref), nf_ref, wu_ref, wd_ref, gf_ref)
    tail = u[tm - hist:, :]
    carry_ref[...] = tail
    st_ref[...] = tail


def _pool_prompt_ffn(h, nm, p, mlp, final_g, tm=512):
    b, l, _ = h.shape
    tm = _row_tile(l, tm)
    hist = POOL_BUF + 1
    tile = pl.BlockSpec((None, tm, D_MODEL), lambda i, j: (i, j, 0))
    mlp_ins, mlp_specs = _mlp_operands(mlp, final_g)
    out, st = pl.pallas_call(
        _pool_prompt_kernel,
        grid=(b, l // tm),
        in_specs=[tile, _const_spec((1, D_MODEL)), _const_spec(p["w"].shape), _const_spec((1, D_MODEL))]
        + mlp_specs,
        out_specs=[tile, pl.BlockSpec((None, hist, D_MODEL), lambda i, j: (i, 0, 0))],
        out_shape=[jax.ShapeDtypeStruct((b, l, D_MODEL), F32), jax.ShapeDtypeStruct((b, hist, D_MODEL), F32)],
        scratch_shapes=[pltpu.VMEM((hist, D_MODEL), F32)],
        compiler_params=_params(("parallel", "arbitrary")),
        name="pool_prompt_ffn",
    )(h, nm.reshape(1, D_MODEL), p["w"].astype(BF16), p["scale"].reshape(1, D_MODEL), *mlp_ins)
    return out, st[:, 1:]


def _pool_decode_kernel(h_ref, buf_ref, nm_ref, pw_ref, sc_ref, o_ref, st_ref):
    u = _rms(h_ref[...], nm_ref[...], NORM_EPS)
    d_groups = []
    for gi, w in enumerate(POOL_WINDOWS):
        sl = slice(gi * POOL_G, (gi + 1) * POOL_G)
        s = u[:, sl]
        for i in range(1, w):
            s = s + buf_ref[POOL_BUF - i, :, sl]
        cnt = float(min(w, PAST_LEN + 1))
        d_groups.append(s / cnt - u[:, sl])
    o_ref[...] = h_ref[...] + _pool_project(d_groups, pw_ref, sc_ref)
    st_ref[0:POOL_BUF - 1] = buf_ref[1:POOL_BUF]
    st_ref[POOL_BUF - 1] = u


def _pool_decode(h, buf, nm, p):
    m = h.shape[0]
    full = pl.BlockSpec((m, D_MODEL), lambda i: (0, 0))
    bufs = pl.BlockSpec(buf.shape, lambda i: (0, 0, 0))
    return pl.pallas_call(
        _pool_decode_kernel,
        grid=(1,),
        in_specs=[full, bufs, _const_spec((1, D_MODEL)), _const_spec(p["w"].shape), _const_spec((1, D_MODEL))],
        out_specs=[full, bufs],
        out_shape=[jax.ShapeDtypeStruct((m, D_MODEL), F32), jax.ShapeDtypeStruct(buf.shape, F32)],
        compiler_params=_params(("arbitrary",)),
        name="pool_decode",
    )(h, buf, nm.reshape(1, D_MODEL), p["w"].astype(BF16), p["scale"].reshape(1, D_MODEL))


def _trunk_prompt(x, nm, nfin, mlps, rw, gl, cv, po):
    b, l, _ = x.shape
    flat = lambda t: t.reshape(b * l, t.shape[-1])
    seq = lambda t: t.reshape(b, l, t.shape[-1])

    (r, lw, k, v, kk, bb, g, bonus), shift = _rwkv_pre_prompt(x, nm[0], rw)
    y, wkv = _rwkv_rec_prompt(r, lw, k, v, kk, bb)
    h = _rwkv_post_ffn(flat(x), flat(y), flat(bonus), flat(g), rw, mlps[0])

    q, k, v, g, gk = _gla_pre(h, nm[1], gl)
    o, gla_s = _gla_rec_prompt(seq(q), seq(k), seq(v), seq(gk))
    h = _gla_post_ffn(h, flat(o), g, gl, mlps[1])

    h, conv_s = _conv_prompt(seq(h), nm[2], cv)
    h = _ffn(flat(h), mlps[2])

    y, pool_s = _pool_prompt_ffn(seq(h), nm[3], po, mlps[3], nfin)
    return y, wkv[None], shift[None], gla_s[None], conv_s[None], pool_s[None]


def _trunk_decode(x, st_wkv, st_shift, st_gla, st_conv, st_pool, nm, nf, nfin, w_up, w_down, rw, gl, cv, po):
    m = x.shape[0]
    h = x.reshape(m, D_MODEL)
    mlps = []

    def mlp(h, i, final_g=None):
        h, wub, wdb = _ffn_cast(h, nf[i], w_up, w_down, i, final_g)
        mlps.append((nf[i], wub, wdb))
        return h

    (r, lw, k, v, kk, bb, g, bonus), shift = _rwkv_pre_decode(h, st_shift[0], nm[0], rw)
    y_t, wkv_t = _rwkv_rec_decode(r, lw, k, v, kk, bb, jnp.transpose(st_wkv[0], (1, 2, 3, 0)))
    wkv = jnp.transpose(wkv_t, (3, 0, 1, 2))
    h = mlp(_rwkv_post_ffn(h, y_t.T, bonus, g, rw), 0)

    q, k, v, g, gk = _gla_pre(h, nm[1], gl)
    o, gla_s = _gla_rec_decode(q, k, v, gk, st_gla[0])
    h = mlp(_gla_post_ffn(h, o, g, gl), 1)

    h, conv_s = _conv_decode(h, st_conv[0], nm[2], cv)
    h = mlp(h, 2)

    h, pool_t = _pool_decode(h, jnp.transpose(st_pool[0], (1, 0, 2)), nm[3], po)
    pool_s = jnp.transpose(pool_t, (1, 0, 2))
    y = mlp(h, 3, final_g=nfin)
    return (y.reshape(m, 1, D_MODEL), wkv[None], shift[None], gla_s[None], conv_s[None], pool_s[None]), mlps


def kernel(x_prompt, x_sample, state_rwkv_wkv, state_rwkv_shift, state_gla, state_conv, state_pool, norm_mix, norm_ffn, norm_final, ffn_up, ffn_down, rwkv_mu, rwkv_w_rkv, rwkv_w0, rwkv_w1, rwkv_w2, rwkv_a0, rwkv_a1, rwkv_a2, rwkv_g1, rwkv_g2, rwkv_k_k, rwkv_k_a, rwkv_r_k, rwkv_ln_w, rwkv_ln_b, rwkv_wo, gla_w_in, gla_w_gk2, gla_b_gk, gla_norm, gla_wo, conv_w_in, conv_w, conv_wo, pool_w, pool_scale):
    assert x_prompt.shape[1] % CHUNK == 0 and x_sample.shape[1] == 1
    rw = dict(mu=rwkv_mu[0], w_rkv=rwkv_w_rkv[0], w0=rwkv_w0[0], w1=rwkv_w1[0], w2=rwkv_w2[0], a0=rwkv_a0[0],
              a1=rwkv_a1[0], a2=rwkv_a2[0], g1=rwkv_g1[0], g2=rwkv_g2[0], k_k=rwkv_k_k[0], k_a=rwkv_k_a[0],
              r_k=rwkv_r_k[0], ln_w=rwkv_ln_w[0], ln_b=rwkv_ln_b[0], wo=rwkv_wo[0])
    gl = dict(w_in=gla_w_in[0], w_gk2=gla_w_gk2[0], b_gk=gla_b_gk[0], norm=gla_norm[0], wo=gla_wo[0])
    cv = dict(w_in=conv_w_in[0], w=conv_w[0], wo=conv_wo[0])
    po = dict(w=pool_w[0], scale=pool_scale[0])
    (y_s, wkv_s, sh_s, gla_s, conv_s, pool_s), mlps = _trunk_decode(
        x_sample, state_rwkv_wkv, state_rwkv_shift, state_gla, state_conv, state_pool,
        norm_mix, norm_ffn, norm_final, ffn_up, ffn_down, rw, gl, cv, po)
    y_p, wkv_p, sh_p, gla_p, conv_p, pool_p = _trunk_prompt(x_prompt, norm_mix, norm_final, mlps, rw, gl, cv, po)
    return (y_p, y_s, wkv_p, wkv_s, sh_p, sh_s, gla_p, gla_s, conv_p, conv_s, pool_p, pool_s)
```
